```python
import math
import jax, jax.numpy as jnp
from jax import lax
import numpy as np

D_MODEL = 1024
BATCH = 8
SEQ = 2048
DEPTH = 2

D_MIX = D_MODEL
D_SSM = D_MIX // 2
SSM_GROUP = 16
N_SSM_HEADS = D_SSM // SSM_GROUP
SSM_STATE = 64
DT_MIN = 0.001
DT_MAX = 0.1
D_POOL = D_MIX - D_SSM
POOL_WINDOWS = (2, 4, 8, 16)
N_POOL_GROUPS = len(POOL_WINDOWS)
POOL_GROUP_DIM = D_POOL // N_POOL_GROUPS
D_FF = 2816
N_EXPERTS = 8
TOP_K = 2
D_FF_EXPERT = 2816
N_DENSE = (DEPTH + 1) // 2
N_MOE = DEPTH // 2
RMS_EPS = 1e-6

kernel_name = "hybrid_s5_pool_moe_trunk"


def rmsnorm(x, g):
    xf = x.astype(jnp.float32)
    inv = lax.rsqrt(jnp.mean(xf * xf, axis=-1, keepdims=True) + RMS_EPS)
    return (xf * inv * g.astype(jnp.float32)).astype(x.dtype)


def _complex_affine_combine(left, right):
    a1r, a1i, b1r, b1i = left
    a2r, a2i, b2r, b2i = right
    ar = a2r * a1r - a2i * a1i
    ai = a2r * a1i + a2i * a1r
    br = a2r * b1r - a2i * b1i + b2r
    bi = a2r * b1i + a2i * b1r + b2i
    return (ar, ai, br, bi)


def s5_mixer(u, log_dt, a_re, a_im, b_re, b_im, c_re, c_im, d, w_glu, b_glu):
    bsz, L, _ = u.shape
    uf = u.astype(jnp.float32)
    ug = uf.reshape(bsz, L, N_SSM_HEADS, SSM_GROUP)
    dt = jnp.exp(log_dt.astype(jnp.float32))[:, None]
    ar = jnp.minimum(a_re.astype(jnp.float32), -1e-4)
    ai = a_im.astype(jnp.float32)
    mag = jnp.exp(ar * dt)
    lam_re = mag * jnp.cos(ai * dt)
    lam_im = mag * jnp.sin(ai * dt)
    den = ar * ar + ai * ai
    nr = lam_re - 1.0
    ni = lam_im
    coef_re = (nr * ar + ni * ai) / den
    coef_im = (ni * ar - nr * ai) / den
    br = b_re.astype(jnp.float32)
    bi = b_im.astype(jnp.float32)
    bb_re = coef_re[..., None] * br - coef_im[..., None] * bi
    bb_im = coef_re[..., None] * bi + coef_im[..., None] * br
    bu_re = jnp.einsum('blhg,hpg->blhp', ug, bb_re)
    bu_im = jnp.einsum('blhg,hpg->blhp', ug, bb_im)
    shape = (1, L, N_SSM_HEADS, SSM_STATE)
    a_seq_re = jnp.broadcast_to(lam_re[None, None], shape)
    a_seq_im = jnp.broadcast_to(lam_im[None, None], shape)
    _, _, s_re, s_im = lax.associative_scan(
        _complex_affine_combine, (a_seq_re, a_seq_im, bu_re, bu_im), axis=1)
    y = (jnp.einsum('blhp,hgp->blhg', s_re, c_re.astype(jnp.float32))
         - jnp.einsum('blhp,hgp->blhg', s_im, c_im.astype(jnp.float32)))
    y = y.reshape(bsz, L, D_SSM) + d.astype(jnp.float32) * uf
    h = jax.nn.gelu(y)
    out = h * jax.nn.sigmoid(h @ w_glu.astype(jnp.float32) + b_glu.astype(jnp.float32))
    return out.astype(u.dtype)


def pool_mixer(xp, pool_w, pool_scale):
    bsz, L, _ = xp.shape
    xg = xp.astype(jnp.float32).reshape(bsz, L, N_POOL_GROUPS, POOL_GROUP_DIM)
    pos = jnp.arange(L)
    outs = []
    for g, w in enumerate(POOL_WINDOWS):
        xc = xg[:, :, g]
        cs0 = jnp.pad(jnp.cumsum(xc, axis=1), ((0, 0), (1, 0), (0, 0)))
        lag = jnp.pad(cs0, ((0, 0), (w - 1, 0), (0, 0)))[:, :L]
        count = jnp.minimum(pos + 1, w).astype(jnp.float32)[None, :, None]
        outs.append((cs0[:, 1:] - lag) / count - xc)
    pooled = jnp.stack(outs, axis=2)
    mixed = jnp.einsum('blgc,gcd->blgd', pooled, pool_w.astype(jnp.float32))
    out = mixed.reshape(bsz, L, D_POOL) * pool_scale.astype(jnp.float32)
    return out.astype(xp.dtype)


def swiglu(h, w_gate, w_up, w_down):
    return (jax.nn.silu(h @ w_gate) * (h @ w_up)) @ w_down


def moe_swiglu(h, router_w, w_gate, w_up, w_down):
    logits = (h @ router_w).astype(jnp.float32)
    top_v, top_i = lax.top_k(logits, TOP_K)
    top_g = jax.nn.softmax(top_v, axis=-1)
    gates = jnp.sum(jax.nn.one_hot(top_i, N_EXPERTS, dtype=jnp.float32) * top_g[..., None], axis=-2)
    out = jnp.zeros(h.shape, jnp.float32)
    for e in range(N_EXPERTS):
        y_e = swiglu(h, w_gate[e], w_up[e], w_down[e]).astype(jnp.float32)
        out = out + gates[..., e:e + 1] * y_e
    return out.astype(h.dtype)


def setup_inputs(seed: int = 0) -> dict:
    key = jax.random.key(seed)
    ks = jax.random.split(key, 24)
    f32 = jnp.float32

    def nrm(k, shape, scale):
        return jax.random.normal(k, shape, f32) * scale

    H, P, G = N_SSM_HEADS, SSM_STATE, SSM_GROUP
    x = jax.random.normal(ks[0], (BATCH, SEQ, D_MODEL), f32)
    norm_mix_g = 1.0 + nrm(ks[1], (DEPTH, D_MODEL), 0.02)
    w_in = nrm(ks[2], (DEPTH, D_MODEL, D_MIX), D_MODEL ** -0.5)
    ssm_log_dt = jax.random.uniform(ks[3], (DEPTH, H), f32, math.log(DT_MIN), math.log(DT_MAX))
    ssm_a_re = -0.5 + nrm(ks[4], (DEPTH, H, P), 0.01)
    ssm_a_im = jnp.pi * jnp.arange(P, dtype=f32)[None, None, :] + nrm(ks[5], (DEPTH, H, P), 0.01)
    ssm_b_re = nrm(ks[6], (DEPTH, H, P, G), (2.0 * G) ** -0.5)
    ssm_b_im = nrm(ks[7], (DEPTH, H, P, G), (2.0 * G) ** -0.5)
    ssm_c_re = nrm(ks[8], (DEPTH, H, G, P), (2.0 * P) ** -0.5 * 4.0)
    ssm_c_im = nrm(ks[9], (DEPTH, H, G, P), (2.0 * P) ** -0.5 * 4.0)
    ssm_d = nrm(ks[10], (DEPTH, D_SSM), 1.0)
    ssm_w_glu = nrm(ks[11], (DEPTH, D_SSM, D_SSM), D_SSM ** -0.5)
    ssm_b_glu = nrm(ks[12], (DEPTH, D_SSM), 0.02)
    pool_w = nrm(ks[13], (DEPTH, N_POOL_GROUPS, POOL_GROUP_DIM, POOL_GROUP_DIM), POOL_GROUP_DIM ** -0.5)
    pool_scale = 1.0 + nrm(ks[14], (DEPTH, D_POOL), 0.02)
    w_out = nrm(ks[15], (DEPTH, D_MIX, D_MODEL), D_MIX ** -0.5)
    norm_ffn_g = 1.0 + nrm(ks[16], (DEPTH, D_MODEL), 0.02)
    ffn_w_gate = nrm(ks[17], (N_DENSE, D_MODEL, D_FF), D_MODEL ** -0.5)
    ffn_w_up = nrm(ks[18], (N_DENSE, D_MODEL, D_FF), D_MODEL ** -0.5)
    ffn_w_down = nrm(ks[19], (N_DENSE, D_FF, D_MODEL), D_FF ** -0.5)
    router_w = nrm(ks[20], (N_MOE, D_MODEL, N_EXPERTS), D_MODEL ** -0.5)
    moe_w_gate = nrm(ks[21], (N_MOE, N_EXPERTS, D_MODEL, D_FF_EXPERT), D_MODEL ** -0.5)
    moe_w_up = nrm(ks[22], (N_MOE, N_EXPERTS, D_MODEL, D_FF_EXPERT), D_MODEL ** -0.5)
    k_down, k_fin = jax.random.split(ks[23])
    moe_w_down = nrm(k_down, (N_MOE, N_EXPERTS, D_FF_EXPERT, D_MODEL), D_FF_EXPERT ** -0.5)
    final_norm_g = 1.0 + nrm(k_fin, (D_MODEL,), 0.02)
    return {
        "x": x, "norm_mix_g": norm_mix_g, "w_in": w_in,
        "ssm_log_dt": ssm_log_dt, "ssm_a_re": ssm_a_re, "ssm_a_im": ssm_a_im,
        "ssm_b_re": ssm_b_re, "ssm_b_im": ssm_b_im, "ssm_c_re": ssm_c_re, "ssm_c_im": ssm_c_im,
        "ssm_d": ssm_d, "ssm_w_glu": ssm_w_glu, "ssm_b_glu": ssm_b_glu,
        "pool_w": pool_w, "pool_scale": pool_scale, "w_out": w_out,
        "norm_ffn_g": norm_ffn_g, "ffn_w_gate": ffn_w_gate, "ffn_w_up": ffn_w_up, "ffn_w_down": ffn_w_down,
        "router_w": router_w, "moe_w_gate": moe_w_gate, "moe_w_up": moe_w_up, "moe_w_down": moe_w_down,
        "final_norm_g": final_norm_g,
    }


def reference(x, norm_mix_g, w_in, ssm_log_dt, ssm_a_re, ssm_a_im, ssm_b_re, ssm_b_im,
              ssm_c_re, ssm_c_im, ssm_d, ssm_w_glu, ssm_b_glu, pool_w, pool_scale, w_out,
              norm_ffn_g, ffn_w_gate, ffn_w_up, ffn_w_down, router_w, moe_w_gate, moe_w_up,
              moe_w_down, final_norm_g):
    h_res = x
    for i in range(DEPTH):
        h = rmsnorm(h_res, norm_mix_g[i])
        proj = h @ w_in[i]
        u_ssm = proj[..., :D_SSM]
        u_pool = proj[..., D_SSM:]
        y_ssm = s5_mixer(u_ssm, ssm_log_dt[i], ssm_a_re[i], ssm_a_im[i], ssm_b_re[i], ssm_b_im[i],
                         ssm_c_re[i], ssm_c_im[i], ssm_d[i], ssm_w_glu[i], ssm_b_glu[i])
        y_pool = pool_mixer(u_pool, pool_w[i], pool_scale[i])
        mixed = jnp.concatenate([y_ssm, y_pool], axis=-1)
        h_res = h_res + mixed @ w_out[i]
        h = rmsnorm(h_res, norm_ffn_g[i])
        j = i // 2
        if i % 2 == 0:
            y = swiglu(h, ffn_w_gate[j], ffn_w_up[j], ffn_w_down[j])
        else:
            y = moe_swiglu(h, router_w[j], moe_w_gate[j], moe_w_up[j], moe_w_down[j])
        h_res = h_res + y
    return rmsnorm(h_res, final_norm_g)
```

```python
import functools
import math

import jax
import jax.numpy as jnp
from jax import lax
from jax.experimental import pallas as pl
from jax.experimental.pallas import tpu as pltpu

RMS_EPS = 1e-6
POOL_WINDOWS = (2, 4, 8, 16)
A_RE_MAX = -1e-4
GELU_C0 = math.sqrt(2.0 / math.pi)
GELU_C1 = 0.044715

SUBLANES = 8
LANES = 128
MXU_DIM = 256

HEADS_PER_GROUP = 16
TIME_TILE = 64
SCAN_COLS = 512
FFN_ROW_TILE = 512
FFN_COL_CHUNK = 256
VMEM_LIMIT = 48 * 1024 * 1024

_F32 = jnp.float32
_BF16 = jnp.bfloat16


def _dot(a, b):
    return jnp.dot(a, b, preferred_element_type=_F32)


def _rmsnorm(x, g):
    inv = lax.rsqrt(jnp.mean(x * x, axis=-1, keepdims=True) + RMS_EPS)
    return x * inv * g


def _sigmoid(x):
    return 1.0 / (1.0 + jnp.exp(-x))


def _discretize_kernel(log_dt_ref, a_re_ref, a_im_ref, b_re_ref, b_im_ref,
                       lam_re_ref, lam_im_ref, bb_re_ref, bb_im_ref):
    dt = jnp.exp(log_dt_ref[...])
    ar = jnp.minimum(a_re_ref[...], A_RE_MAX)
    ai = a_im_ref[...]
    mag = jnp.exp(ar * dt)
    lam_re = mag * jnp.cos(ai * dt)
    lam_im = mag * jnp.sin(ai * dt)
    den = ar * ar + ai * ai
    nr = lam_re - 1.0
    ni = lam_im
    coef_re = (nr * ar + ni * ai) / den
    coef_im = (ni * ar - nr * ai) / den
    lam_re_ref[...] = lam_re
    lam_im_ref[...] = lam_im
    br = b_re_ref[...]
    bi = b_im_ref[...]
    bb_re_ref[...] = coef_re * br - coef_im * bi
    bb_im_ref[...] = coef_re * bi + coef_im * br


def _discretize(log_dt, a_re, a_im, b_re_t, b_im_t):
    n, g, p = b_re_t.shape
    full3 = lambda s: pl.BlockSpec(s, lambda: (0, 0, 0))
    return pl.pallas_call(
        _discretize_kernel,
        out_shape=(jax.ShapeDtypeStruct((n, 1, p), _F32),
                   jax.ShapeDtypeStruct((n, 1, p), _F32),
                   jax.ShapeDtypeStruct((n, g, p), _F32),
                   jax.ShapeDtypeStruct((n, g, p), _F32)),
        in_specs=[full3((n, 1, 1)), full3((n, 1, p)), full3((n, 1, p)),
                  full3((n, g, p)), full3((n, g, p))],
        out_specs=(full3((n, 1, p)), full3((n, 1, p)),
                   full3((n, g, p)), full3((n, g, p))),
        name="ssm_discretize",
    )(log_dt.reshape(n, 1, 1), a_re.reshape(n, 1, p), a_im.reshape(n, 1, p),
      b_re_t, b_im_t)


def _mix_kernel(*refs, time_tile, with_router):
    if with_router:
        (h_ref, gmix_ref, win_ref, bw_ref, lam_ref, cw_ref, dskip_ref, wglu_ref,
         bglu_ref, poolw_ref, pscale_ref, wout_ref, gffn_ref, rwt_ref,
         hout_ref, hn_ref, logits_ref, s_scr, state_scr, ext_scr, mixed_scr) = refs
    else:
        (h_ref, gmix_ref, win_ref, bw_ref, lam_ref, cw_ref, dskip_ref, wglu_ref,
         bglu_ref, poolw_ref, pscale_ref, wout_ref, gffn_ref,
         hout_ref, hn_ref, s_scr, state_scr, ext_scr, mixed_scr) = refs
        rwt_ref = logits_ref = None

    step = pl.program_id(0)
    rows = time_tile * SUBLANES
    d_ssm = dskip_ref.shape[1]
    n_groups = bw_ref.shape[0]
    gin = bw_ref.shape[1]
    gstate = bw_ref.shape[2] // 2
    hist = ext_scr.shape[0] - rows

    @pl.when(step == 0)
    def _():
        state_scr[...] = jnp.zeros_like(state_scr)
        ext_scr[0:hist, :] = jnp.zeros((hist, ext_scr.shape[1]), _F32)

    h = h_ref[...]
    hn = _rmsnorm(h, gmix_ref[...]).astype(_BF16)
    proj = _dot(hn, win_ref[...])
    u_ssm = proj[:, :d_ssm]
    ext_scr[hist:, :] = proj[:, d_ssm:]

    for q in range(n_groups):
        ug = u_ssm[:, q * gin:(q + 1) * gin].astype(_BF16)
        s_scr[...] = _dot(ug, bw_ref[q])
        for c in range(gstate // SCAN_COLS):
            re0 = c * SCAN_COLS
            im0 = gstate + c * SCAN_COLS
            lr = jnp.broadcast_to(lam_ref[2 * q:2 * q + 1, re0:re0 + SCAN_COLS],
                                  (SUBLANES, SCAN_COLS))
            li = jnp.broadcast_to(lam_ref[2 * q + 1:2 * q + 2, re0:re0 + SCAN_COLS],
                                  (SUBLANES, SCAN_COLS))

            def body(t, carry, re0=re0, im0=im0, lr=lr, li=li):
                sre, sim = carry
                r0 = pl.multiple_of(t * SUBLANES, SUBLANES)
                bre = s_scr[pl.ds(r0, SUBLANES), re0:re0 + SCAN_COLS]
                bim = s_scr[pl.ds(r0, SUBLANES), im0:im0 + SCAN_COLS]
                nre = lr * sre - li * sim + bre
                nim = lr * sim + li * sre + bim
                s_scr[pl.ds(r0, SUBLANES), re0:re0 + SCAN_COLS] = nre
                s_scr[pl.ds(r0, SUBLANES), im0:im0 + SCAN_COLS] = nim
                return nre, nim

            init = (state_scr[q, :, re0:re0 + SCAN_COLS],
                    state_scr[q, :, im0:im0 + SCAN_COLS])
            sre, sim = lax.fori_loop(0, time_tile, body, init, unroll=8)
            state_scr[q, :, re0:re0 + SCAN_COLS] = sre
            state_scr[q, :, im0:im0 + SCAN_COLS] = sim
        yq = _dot(s_scr[...].astype(_BF16), cw_ref[q])
        yq = yq + dskip_ref[:, q * gin:(q + 1) * gin] * u_ssm[:, q * gin:(q + 1) * gin]
        hq = 0.5 * yq * (1.0 + jnp.tanh(GELU_C0 * (yq + GELU_C1 * (yq * yq * yq))))
        mixed_scr[:, q * gin:(q + 1) * gin] = hq.astype(_BF16)
    hg = mixed_scr[:, :d_ssm]
    gate = _sigmoid(_dot(hg, wglu_ref[...]) + bglu_ref[...])
    mixed_scr[:, :d_ssm] = (hg.astype(_F32) * gate).astype(_BF16)

    n_ext = rows + hist
    pos = step * time_tile + (lax.broadcasted_iota(jnp.int32, (rows, LANES), 0) // SUBLANES)
    for g, w in enumerate(POOL_WINDOWS):
        c0 = g * LANES
        e = ext_scr[:, c0:c0 + LANES]
        acc = e
        n_acc = n_ext
        span = 1
        while span < w:
            sh = span * SUBLANES
            acc = acc[sh:, :] + acc[:n_acc - sh, :]
            n_acc -= sh
            span *= 2
        wsum = acc[n_acc - rows:, :]
        cnt = jnp.minimum(pos + 1, w).astype(_F32)
        pooled = wsum / cnt - e[hist:, :]
        mg = _dot(pooled.astype(_BF16), poolw_ref[g]) * pscale_ref[:, c0:c0 + LANES]
        mixed_scr[:, d_ssm + c0:d_ssm + c0 + LANES] = mg.astype(_BF16)
    ext_scr[0:hist, :] = ext_scr[rows:rows + hist, :]

    hout = h + _dot(mixed_scr[...], wout_ref[...])
    hout_ref[...] = hout
    hn2 = _rmsnorm(hout, gffn_ref[...])
    hn_ref[...] = hn2.astype(_BF16)
    if with_router:
        logits_ref[...] = lax.dot_general(
            rwt_ref[...], hn2, (((1,), (1,)), ((), ())),
            precision=lax.Precision.HIGHEST, preferred_element_type=_F32)


def _mix_layer(h, gmix, w_in, bw, lam, cw, dskip, wglu, bglu, poolw, pscale, wout,
               gffn, rwt):
    t_rows, d = h.shape
    rows = TIME_TILE * SUBLANES
    n_steps = t_rows // rows
    d_ssm = dskip.shape[1]
    d_pool = pscale.shape[1]
    hist = max(POOL_WINDOWS) * SUBLANES
    with_router = rwt is not None

    def const(a):
        nd = a.ndim
        return pl.BlockSpec(a.shape, lambda i, nd=nd: (0,) * nd)

    row_blk = lambda width: pl.BlockSpec((rows, width), lambda i: (i, 0))
    ins = [h, gmix, w_in, bw, lam, cw, dskip, wglu, bglu, poolw, pscale, wout, gffn]
    in_specs = [row_blk(d)] + [const(a) for a in ins[1:]]
    out_shape = [jax.ShapeDtypeStruct((t_rows, d), _F32),
                 jax.ShapeDtypeStruct((t_rows, d), _BF16)]
    out_specs = [row_blk(d), row_blk(d)]
    if with_router:
        ins.append(rwt)
        in_specs.append(const(rwt))
        out_shape.append(jax.ShapeDtypeStruct((rwt.shape[0], t_rows), _F32))
        out_specs.append(pl.BlockSpec((rwt.shape[0], rows), lambda i: (0, i)))
    return pl.pallas_call(
        functools.partial(_mix_kernel, time_tile=TIME_TILE, with_router=with_router),
        grid=(n_steps,),
        in_specs=in_specs,
        out_specs=out_specs,
        out_shape=out_shape,
        scratch_shapes=[
            pltpu.VMEM((rows, bw.shape[2]), _F32),
            pltpu.VMEM((bw.shape[0], SUBLANES, bw.shape[2]), _F32),
            pltpu.VMEM((rows + hist, d_pool), _F32),
            pltpu.VMEM((rows, d_ssm + d_pool), _BF16),
        ],
        compiler_params=pltpu.CompilerParams(
            dimension_semantics=("arbitrary",), vmem_limit_bytes=VMEM_LIMIT),
        name="mix_router" if with_router else "mix",
    )(*ins)


def _route_kernel(lt_ref, gates_ref):
    l = lt_ref[...]
    n_e = l.shape[0]
    ie = lax.broadcasted_iota(jnp.int32, l.shape, 0)
    m1 = jnp.max(l, axis=0, keepdims=True)
    i1 = jnp.min(jnp.where(l == m1, ie, n_e), axis=0, keepdims=True)
    l2 = jnp.where(ie == i1, -jnp.inf, l)
    m2 = jnp.max(l2, axis=0, keepdims=True)
    i2 = jnp.min(jnp.where(l2 == m2, ie, n_e), axis=0, keepdims=True)
    e2 = jnp.exp(m2 - m1)
    den = 1.0 + e2
    gates_ref[...] = (jnp.where(ie == i1, 1.0 / den, 0.0)
                      + jnp.where(ie == i2, e2 / den, 0.0))


def _route(logits_t):
    n_e, t_rows = logits_t.shape
    blk = 2048
    return pl.pallas_call(
        _route_kernel,
        grid=(t_rows // blk,),
        in_specs=[pl.BlockSpec((n_e, blk), lambda i: (0, i))],
        out_specs=pl.BlockSpec((n_e, blk), lambda i: (0, i)),
        out_shape=jax.ShapeDtypeStruct((n_e, t_rows), _F32),
        name="route_top2",
    )(logits_t)


def _ffn_kernel(*refs, gated, final_norm):
    if gated:
        x_ref, wg_ref, wu_ref, wd_ref, res_ref, gates_ref, gfin_ref, out_ref, a_scr = refs
    else:
        x_ref, wg_ref, wu_ref, wd_ref, res_ref, gfin_ref, out_ref, a_scr = refs
        gates_ref = None
    e = pl.program_id(1)
    j = pl.program_id(2)
    first = jnp.logical_and(e == 0, j == 0)
    last = jnp.logical_and(e == pl.num_programs(1) - 1, j == pl.num_programs(2) - 1)

    @pl.when(first)
    def _():
        out_ref[...] = res_ref[...]

    x = x_ref[...]
    tf = wg_ref.shape[2]
    c0 = 0
    while c0 < tf:
        cw = min(FFN_COL_CHUNK, tf - c0)
        g = _dot(x, wg_ref[0, :, c0:c0 + cw])
        u = _dot(x, wu_ref[0, :, c0:c0 + cw])
        a_scr[:, c0:c0 + cw] = (g * _sigmoid(g) * u).astype(_BF16)
        c0 += cw
    y = _dot(a_scr[...], wd_ref[0])
    if gated:
        n_e = gates_ref.shape[1]
        lane = lax.broadcasted_iota(jnp.int32, gates_ref.shape, 1)
        gate = jnp.sum(jnp.where(lane == e, gates_ref[...], 0.0), axis=1, keepdims=True)
        y = y * gate
    out_ref[...] += y

    if final_norm:
        @pl.when(last)
        def _():
            out_ref[...] = _rmsnorm(out_ref[...], gfin_ref[...])


def _ffn(x, wg, wu, wd, res, gates, gfin, final_norm):
    t_rows, d = x.shape
    n_e, _, f = wg.shape
    tm = FFN_ROW_TILE
    nj = 2
    tf = f // nj
    gated = gates is not None
    ins = [x, wg, wu, wd, res]
    in_specs = [
        pl.BlockSpec((tm, d), lambda i, e, j: (i, 0)),
        pl.BlockSpec((1, d, tf), lambda i, e, j: (e, 0, j)),
        pl.BlockSpec((1, d, tf), lambda i, e, j: (e, 0, j)),
        pl.BlockSpec((1, tf, d), lambda i, e, j: (e, j, 0)),
        pl.BlockSpec((tm, d), lambda i, e, j: (i, 0)),
    ]
    if gated:
        ins.append(gates)
        in_specs.append(pl.BlockSpec((tm, gates.shape[1]), lambda i, e, j: (i, 0)))
    ins.append(gfin)
    in_specs.append(pl.BlockSpec(gfin.shape, lambda i, e, j: (0, 0)))
    return pl.pallas_call(
        functools.partial(_ffn_kernel, gated=gated, final_norm=final_norm),
        grid=(t_rows // tm, n_e, nj),
        in_specs=in_specs,
        out_specs=pl.BlockSpec((tm, d), lambda i, e, j: (i, 0)),
        out_shape=jax.ShapeDtypeStruct((t_rows, d), _F32),
        scratch_shapes=[pltpu.VMEM((tm, tf), _BF16)],
        compiler_params=pltpu.CompilerParams(
            dimension_semantics=("arbitrary", "arbitrary", "arbitrary"),
            vmem_limit_bytes=VMEM_LIMIT),
        name="moe_dense" if gated else "ffn_dense",
    )(*ins)


def _block_diag_groups(w, groups):
    h, k, n = w.shape
    hpg = h // groups
    eye = jnp.eye(hpg, dtype=w.dtype)
    wq = w.reshape(groups, hpg, k, n)
    out = jnp.einsum("qhkn,hj->qhkjn", wq, eye)
    return out.reshape(groups, hpg * k, hpg * n)


def kernel(x, norm_mix_g, w_in, ssm_log_dt, ssm_a_re, ssm_a_im, ssm_b_re, ssm_b_im, ssm_c_re, ssm_c_im, ssm_d, ssm_w_glu, ssm_b_glu, pool_w, pool_scale, w_out, norm_ffn_g, ffn_w_gate, ffn_w_up, ffn_w_down, router_w, moe_w_gate, moe_w_up, moe_w_down, final_norm_g):
    bsz, seq, d = x.shape
    depth, n_heads, n_state, n_grp_ch = ssm_b_re.shape
    t_rows = bsz * seq
    assert bsz == SUBLANES
    groups = n_heads // HEADS_PER_GROUP

    nh = depth * n_heads
    lam_re, lam_im, bb_re, bb_im = _discretize(
        ssm_log_dt.reshape(nh), ssm_a_re.reshape(nh, n_state), ssm_a_im.reshape(nh, n_state),
        jnp.swapaxes(ssm_b_re, 2, 3).reshape(nh, n_grp_ch, n_state),
        jnp.swapaxes(ssm_b_im, 2, 3).reshape(nh, n_grp_ch, n_state))
    lam_re = lam_re.reshape(depth, groups, 1, HEADS_PER_GROUP * n_state)
    lam_im = lam_im.reshape(depth, groups, 1, HEADS_PER_GROUP * n_state)
    lam = jnp.concatenate([lam_re, lam_im], axis=2).reshape(depth, 2 * groups, -1)
    bb_re = bb_re.reshape(depth, n_heads, n_grp_ch, n_state)
    bb_im = bb_im.reshape(depth, n_heads, n_grp_ch, n_state)

    h = jnp.transpose(x, (1, 0, 2)).reshape(t_rows, d)
    row = lambda v: v.reshape(1, -1)
    for i in range(depth):
        bw = jnp.concatenate([_block_diag_groups(bb_re[i], groups),
                              _block_diag_groups(bb_im[i], groups)], axis=2).astype(_BF16)
        cw = jnp.concatenate(
            [_block_diag_groups(jnp.swapaxes(ssm_c_re[i], 1, 2), groups),
             _block_diag_groups(-jnp.swapaxes(ssm_c_im[i], 1, 2), groups)], axis=1).astype(_BF16)
        is_moe = i % 2 == 1
        j = i // 2
        rwt = jnp.transpose(router_w[j]) if is_moe else None
        outs = _mix_layer(
            h, row(norm_mix_g[i]), w_in[i].astype(_BF16), bw, lam[i], cw, row(ssm_d[i]),
            ssm_w_glu[i].astype(_BF16), row(ssm_b_glu[i]), pool_w[i].astype(_BF16),
            row(pool_scale[i]), w_out[i].astype(_BF16), row(norm_ffn_g[i]), rwt)
        final = i == depth - 1
        gfin = row(final_norm_g)
        if is_moe:
            h, hn, logits_t = outs
            gates = jnp.transpose(_route(logits_t))
            h = _ffn(hn, moe_w_gate[j].astype(_BF16), moe_w_up[j].astype(_BF16),
                     moe_w_down[j].astype(_BF16), h, gates, gfin, final)
        else:
            h, hn = outs
            h = _ffn(hn, ffn_w_gate[j][None].astype(_BF16), ffn_w_up[j][None].astype(_BF16),
                     ffn_w_down[j][None].astype(_BF16), h, None, gfin, final)
    return jnp.transpose(h.reshape(seq, bsz, d), (1, 0, 2))
```

```python
import functools
import math

import jax
import jax.numpy as jnp
from jax import lax
from jax.experimental import pallas as pl
from jax.experimental.pallas import tpu as pltpu

RMS_EPS = 1e-6
POOL_WINDOWS = (2, 4, 8, 16)
A_RE_MAX = -1e-4
GELU_C0 = math.sqrt(2.0 / math.pi)
GELU_C1 = 0.044715

SUBLANES = 8
LANES = 128
MXU_DIM = 256

HEADS_PER_GROUP = 16
TIME_TILE = 64
SCAN_COLS = 512
FFN_ROW_TILE = 512
FFN_COL_CHUNK = 256
FFN_COL_SPLIT = 2
TOKEN_CHUNK = MXU_DIM
SUB_ROWS = LANES
VMEM_LIMIT = 48 * 1024 * 1024

_F32 = jnp.float32
_BF16 = jnp.bfloat16
_I32 = jnp.int32


def _dot(a, b):
    return jnp.dot(a, b, preferred_element_type=_F32)


def _dot_nt(a, b):
    return lax.dot_general(a, b, (((1,), (1,)), ((), ())), preferred_element_type=_F32)


def _rmsnorm(x, g):
    inv = lax.rsqrt(jnp.mean(x * x, axis=-1, keepdims=True) + RMS_EPS)
    return x * inv * g


def _sigmoid(x):
    return 1.0 / (1.0 + jnp.exp(-x))


def _ceil_to(x, m):
    return jnp.floor((x + (m - 1.0)) * (1.0 / m)) * m


def _discretize_kernel(log_dt_ref, a_re_ref, a_im_ref, b_re_ref, b_im_ref,
                       lam_re_ref, lam_im_ref, bb_re_ref, bb_im_ref):
    dt = jnp.exp(log_dt_ref[...])
    ar = jnp.minimum(a_re_ref[...], A_RE_MAX)
    ai = a_im_ref[...]
    mag = jnp.exp(ar * dt)
    lam_re = mag * jnp.cos(ai * dt)
    lam_im = mag * jnp.sin(ai * dt)
    den = ar * ar + ai * ai
    nr = lam_re - 1.0
    ni = lam_im
    coef_re = (nr * ar + ni * ai) / den
    coef_im = (ni * ar - nr * ai) / den
    lam_re_ref[...] = lam_re
    lam_im_ref[...] = lam_im
    br = b_re_ref[...]
    bi = b_im_ref[...]
    bb_re_ref[...] = coef_re * br - coef_im * bi
    bb_im_ref[...] = coef_re * bi + coef_im * br


def _discretize(log_dt, a_re, a_im, b_re_t, b_im_t):
    n, g, p = b_re_t.shape
    full3 = lambda s: pl.BlockSpec(s, lambda: (0, 0, 0))
    return pl.pallas_call(
        _discretize_kernel,
        out_shape=(jax.ShapeDtypeStruct((n, 1, p), _F32),
                   jax.ShapeDtypeStruct((n, 1, p), _F32),
                   jax.ShapeDtypeStruct((n, g, p), _F32),
                   jax.ShapeDtypeStruct((n, g, p), _F32)),
        in_specs=[full3((n, 1, 1)), full3((n, 1, p)), full3((n, 1, p)),
                  full3((n, g, p)), full3((n, g, p))],
        out_specs=(full3((n, 1, p)), full3((n, 1, p)),
                   full3((n, g, p)), full3((n, g, p))),
        name="ssm_discretize",
    )(log_dt.reshape(n, 1, 1), a_re.reshape(n, 1, p), a_im.reshape(n, 1, p),
      b_re_t, b_im_t)


def _mix_kernel(*refs, time_tile, with_router):
    if with_router:
        (h_ref, gmix_ref, win_ref, bw_ref, lam_ref, cw_ref, dskip_ref, wglu_ref,
         bglu_ref, poolw_ref, pscale_ref, wout_ref, gffn_ref, rwt_ref,
         hout_ref, hn_ref, logits_ref, s_scr, state_scr, ext_scr, mixed_scr) = refs
    else:
        (h_ref, gmix_ref, win_ref, bw_ref, lam_ref, cw_ref, dskip_ref, wglu_ref,
         bglu_ref, poolw_ref, pscale_ref, wout_ref, gffn_ref,
         hout_ref, hn_ref, s_scr, state_scr, ext_scr, mixed_scr) = refs
        rwt_ref = logits_ref = None

    step = pl.program_id(0)
    rows = time_tile * SUBLANES
    d_ssm = dskip_ref.shape[1]
    n_groups = bw_ref.shape[0]
    gin = bw_ref.shape[1]
    gstate = bw_ref.shape[2] // 2
    hist = ext_scr.shape[0] - rows

    @pl.when(step == 0)
    def _():
        state_scr[...] = jnp.zeros_like(state_scr)
        ext_scr[0:hist, :] = jnp.zeros((hist, ext_scr.shape[1]), _F32)

    h = h_ref[...]
    hn = _rmsnorm(h, gmix_ref[...]).astype(_BF16)
    proj = _dot(hn, win_ref[...])
    u_ssm = proj[:, :d_ssm]
    ext_scr[hist:, :] = proj[:, d_ssm:]

    for q in range(n_groups):
        ug = u_ssm[:, q * gin:(q + 1) * gin].astype(_BF16)
        s_scr[...] = _dot(ug, bw_ref[q])
        for c in range(gstate // SCAN_COLS):
            re0 = c * SCAN_COLS
            im0 = gstate + c * SCAN_COLS
            lr = jnp.broadcast_to(lam_ref[2 * q:2 * q + 1, re0:re0 + SCAN_COLS],
                                  (SUBLANES, SCAN_COLS))
            li = jnp.broadcast_to(lam_ref[2 * q + 1:2 * q + 2, re0:re0 + SCAN_COLS],
                                  (SUBLANES, SCAN_COLS))

            def body(t, carry, re0=re0, im0=im0, lr=lr, li=li):
                sre, sim = carry
                r0 = pl.multiple_of(t * SUBLANES, SUBLANES)
                bre = s_scr[pl.ds(r0, SUBLANES), re0:re0 + SCAN_COLS]
                bim = s_scr[pl.ds(r0, SUBLANES), im0:im0 + SCAN_COLS]
                nre = lr * sre - li * sim + bre
                nim = lr * sim + li * sre + bim
                s_scr[pl.ds(r0, SUBLANES), re0:re0 + SCAN_COLS] = nre
                s_scr[pl.ds(r0, SUBLANES), im0:im0 + SCAN_COLS] = nim
                return nre, nim

            init = (state_scr[q, :, re0:re0 + SCAN_COLS],
                    state_scr[q, :, im0:im0 + SCAN_COLS])
            sre, sim = lax.fori_loop(0, time_tile, body, init, unroll=8)
            state_scr[q, :, re0:re0 + SCAN_COLS] = sre
            state_scr[q, :, im0:im0 + SCAN_COLS] = sim
        yq = _dot(s_scr[...].astype(_BF16), cw_ref[q])
        yq = yq + dskip_ref[:, q * gin:(q + 1) * gin] * u_ssm[:, q * gin:(q + 1) * gin]
        hq = 0.5 * yq * (1.0 + jnp.tanh(GELU_C0 * (yq + GELU_C1 * (yq * yq * yq))))
        mixed_scr[:, q * gin:(q + 1) * gin] = hq.astype(_BF16)
    hg = mixed_scr[:, :d_ssm]
    gate = _sigmoid(_dot(hg, wglu_ref[...]) + bglu_ref[...])
    mixed_scr[:, :d_ssm] = (hg.astype(_F32) * gate).astype(_BF16)

    n_ext = rows + hist
    pos = step * time_tile + (lax.broadcasted_iota(_I32, (rows, LANES), 0) // SUBLANES)
    for g, w in enumerate(POOL_WINDOWS):
        c0 = g * LANES
        e = ext_scr[:, c0:c0 + LANES]
        acc = e
        n_acc = n_ext
        span = 1
        while span < w:
            sh = span * SUBLANES
            acc = acc[sh:, :] + acc[:n_acc - sh, :]
            n_acc -= sh
            span *= 2
        wsum = acc[n_acc - rows:, :]
        cnt = jnp.minimum(pos + 1, w).astype(_F32)
        pooled = wsum / cnt - e[hist:, :]
        mg = _dot(pooled.astype(_BF16), poolw_ref[g]) * pscale_ref[:, c0:c0 + LANES]
        mixed_scr[:, d_ssm + c0:d_ssm + c0 + LANES] = mg.astype(_BF16)
    ext_scr[0:hist, :] = ext_scr[rows:rows + hist, :]

    hout = h + _dot(mixed_scr[...], wout_ref[...])
    hout_ref[...] = hout
    hn2 = _rmsnorm(hout, gffn_ref[...])
    hn_ref[...] = hn2.astype(_BF16)
    if with_router:
        logits_ref[...] = lax.dot_general(
            rwt_ref[...], hn2, (((1,), (1,)), ((), ())),
            precision=lax.Precision.HIGHEST, preferred_element_type=_F32)


def _mix_layer(h, gmix, w_in, bw, lam, cw, dskip, wglu, bglu, poolw, pscale, wout,
               gffn, rwt):
    t_rows, d = h.shape
    rows = TIME_TILE * SUBLANES
    n_steps = t_rows // rows
    d_ssm = dskip.shape[1]
    d_pool = pscale.shape[1]
    hist = max(POOL_WINDOWS) * SUBLANES
    with_router = rwt is not None

    def const(a):
        nd = a.ndim
        return pl.BlockSpec(a.shape, lambda i, nd=nd: (0,) * nd)

    row_blk = lambda width: pl.BlockSpec((rows, width), lambda i: (i, 0))
    ins = [h, gmix, w_in, bw, lam, cw, dskip, wglu, bglu, poolw, pscale, wout, gffn]
    in_specs = [row_blk(d)] + [const(a) for a in ins[1:]]
    out_shape = [jax.ShapeDtypeStruct((t_rows, d), _F32),
                 jax.ShapeDtypeStruct((t_rows, d), _BF16)]
    out_specs = [row_blk(d), row_blk(d)]
    if with_router:
        ins.append(rwt)
        in_specs.append(const(rwt))
        out_shape.append(jax.ShapeDtypeStruct((rwt.shape[0], t_rows), _F32))
        out_specs.append(pl.BlockSpec((rwt.shape[0], rows), lambda i: (0, i)))
    return pl.pallas_call(
        functools.partial(_mix_kernel, time_tile=TIME_TILE, with_router=with_router),
        grid=(n_steps,),
        in_specs=in_specs,
        out_specs=out_specs,
        out_shape=out_shape,
        scratch_shapes=[
            pltpu.VMEM((rows, bw.shape[2]), _F32),
            pltpu.VMEM((bw.shape[0], SUBLANES, bw.shape[2]), _F32),
            pltpu.VMEM((rows + hist, d_pool), _F32),
            pltpu.VMEM((rows, d_ssm + d_pool), _BF16),
        ],
        compiler_params=pltpu.CompilerParams(
            dimension_semantics=("arbitrary",), vmem_limit_bytes=VMEM_LIMIT),
        name="mix_router" if with_router else "mix",
    )(*ins)


def _swiglu_partial(x, wg_ref, wu_ref, wd_ref, a_scr):
    tf = wg_ref.shape[2]
    c0 = 0
    while c0 < tf:
        cw = min(FFN_COL_CHUNK, tf - c0)
        g = _dot(x, wg_ref[0, :, c0:c0 + cw])
        u = _dot(x, wu_ref[0, :, c0:c0 + cw])
        a_scr[:, c0:c0 + cw] = (g * _sigmoid(g) * u).astype(_BF16)
        c0 += cw
    return _dot(a_scr[...], wd_ref[0])


def _ffn_kernel(x_ref, wg_ref, wu_ref, wd_ref, res_ref, out_ref, a_scr):
    j = pl.program_id(1)

    @pl.when(j == 0)
    def _():
        out_ref[...] = res_ref[...]

    out_ref[...] += _swiglu_partial(x_ref[...], wg_ref, wu_ref, wd_ref, a_scr)


def _ffn(x, wg, wu, wd, res):
    t_rows, d = x.shape
    f = wg.shape[2]
    tm = FFN_ROW_TILE
    nj = FFN_COL_SPLIT
    tf = f // nj
    return pl.pallas_call(
        _ffn_kernel,
        grid=(t_rows // tm, nj),
        in_specs=[
            pl.BlockSpec((tm, d), lambda i, j: (i, 0)),
            pl.BlockSpec((1, d, tf), lambda i, j: (0, 0, j)),
            pl.BlockSpec((1, d, tf), lambda i, j: (0, 0, j)),
            pl.BlockSpec((1, tf, d), lambda i, j: (0, j, 0)),
            pl.BlockSpec((tm, d), lambda i, j: (i, 0)),
        ],
        out_specs=pl.BlockSpec((tm, d), lambda i, j: (i, 0)),
        out_shape=jax.ShapeDtypeStruct((t_rows, d), _F32),
        scratch_shapes=[pltpu.VMEM((tm, tf), _BF16)],
        compiler_params=pltpu.CompilerParams(
            dimension_semantics=("arbitrary", "arbitrary"), vmem_limit_bytes=VMEM_LIMIT),
        name="ffn_dense",
    )(x, wg, wu, wd, res)


def _route_kernel(lt_ref, pos_ref, gates_ref, segtab_ref, subtab_ref, *, n_sub, row_tile):
    l = lt_ref[...]
    n_e, t_rows = l.shape
    ch = TOKEN_CHUNK
    n_ch = t_rows // ch
    ie = lax.broadcasted_iota(_I32, l.shape, 0)
    m1 = jnp.max(l, axis=0, keepdims=True)
    i1 = jnp.min(jnp.where(l == m1, ie, n_e), axis=0, keepdims=True)
    l2 = jnp.where(ie == i1, -jnp.inf, l)
    m2 = jnp.max(l2, axis=0, keepdims=True)
    i2 = jnp.min(jnp.where(l2 == m2, ie, n_e), axis=0, keepdims=True)
    e2 = jnp.exp(m2 - m1)
    den = 1.0 + e2
    gates_ref[0:1, :] = 1.0 / den
    gates_ref[1:2, :] = e2 / den
    sel1 = ie == i1
    sel2 = ie == i2
    chosen = jnp.where(sel1 | sel2, 1.0, 0.0).astype(_BF16)

    chunk_of_t = lax.broadcasted_iota(_I32, (n_ch, t_rows), 1) // ch
    in_chunk = jnp.where(chunk_of_t == lax.broadcasted_iota(_I32, (n_ch, t_rows), 0),
                         1.0, 0.0).astype(_BF16)
    n_ec = _dot_nt(chosen, in_chunk)
    n_ce = _dot_nt(in_chunk, chosen)
    pad_ec = _ceil_to(n_ec, float(SUBLANES))
    pad_ce = _ceil_to(n_ce, float(SUBLANES))
    cc_r = lax.broadcasted_iota(_I32, (n_ch, n_ch), 0)
    cc_c = lax.broadcasted_iota(_I32, (n_ch, n_ch), 1)
    before = jnp.where(cc_r < cc_c, 1.0, 0.0).astype(_BF16)
    after = jnp.where(cc_c < cc_r, 1.0, 0.0).astype(_BF16)
    segoff_ec = _dot(pad_ec.astype(_BF16), before)
    segoff_ce = _dot(after, pad_ce.astype(_BF16))
    tot_col = _ceil_to(jnp.sum(pad_ec, axis=1, keepdims=True), float(row_tile))
    tot_row = _ceil_to(jnp.sum(pad_ce, axis=0, keepdims=True), float(row_tile))
    ee_r = lax.broadcasted_iota(_I32, (n_e, n_e), 0)
    ee_c = lax.broadcasted_iota(_I32, (n_e, n_e), 1)
    base_col = jnp.sum(jnp.where(ee_c < ee_r, tot_row, 0.0), axis=1, keepdims=True)
    base_row = jnp.sum(jnp.where(ee_r < ee_c, tot_col, 0.0), axis=0, keepdims=True)
    w0_ec = base_col + segoff_ec
    w0_ce = base_row + segoff_ce
    segtab_ref[0] = w0_ec.astype(_I32)
    segtab_ref[1] = pad_ec.astype(_I32)

    tt_r = lax.broadcasted_iota(_I32, (ch, ch), 0)
    tt_c = lax.broadcasted_iota(_I32, (ch, ch), 1)
    earlier = jnp.where(tt_r < tt_c, 1.0, 0.0).astype(_BF16)
    for c in range(n_ch):
        cols = slice(c * ch, (c + 1) * ch)
        rank = _dot(chosen[:, cols], earlier)
        offs = w0_ec[:, c:c + 1] + rank
        p1 = jnp.sum(jnp.where(sel1[:, cols], offs, 0.0), axis=0, keepdims=True)
        p2 = jnp.sum(jnp.where(sel2[:, cols], offs, 0.0), axis=0, keepdims=True)
        pos_ref[0:1, cols] = p1.astype(_I32)
        pos_ref[1:2, cols] = p2.astype(_I32)

    start = (lax.broadcasted_iota(_I32, (1, n_sub), 1) * SUB_ROWS).astype(_F32)
    end_col = base_col + tot_col
    owner = jnp.sum(jnp.where(end_col <= start, 1.0, 0.0), axis=0, keepdims=True)
    segend_ce = w0_ce + pad_ce
    c_lo = jnp.zeros((1, n_sub), _F32)
    c_hi = jnp.where(owner >= n_e, -1.0, 0.0)
    for e in range(n_e):
        lo_e = jnp.sum(jnp.where(segend_ce[:, e:e + 1] <= start, 1.0, 0.0), axis=0, keepdims=True)
        hi_e = jnp.sum(jnp.where(w0_ce[:, e:e + 1] < start + SUB_ROWS, 1.0, 0.0),
                       axis=0, keepdims=True) - 1.0
        mine = owner == float(e)
        c_lo = c_lo + jnp.where(mine, lo_e, 0.0)
        c_hi = c_hi + jnp.where(mine, hi_e, 0.0)
    n_valid = jnp.sum(tot_col, axis=0, keepdims=True) * (1.0 / row_tile)
    subtab_ref[0:1, :] = jnp.minimum(owner, n_e - 1.0).astype(_I32)
    subtab_ref[1:2, :] = c_lo.astype(_I32)
    subtab_ref[2:3, :] = c_hi.astype(_I32)
    subtab_ref[3:4, :] = jnp.broadcast_to(n_valid, (1, n_sub)).astype(_I32)


def _route(logits_t, n_sub, row_tile):
    n_e, t_rows = logits_t.shape
    n_ch = t_rows // TOKEN_CHUNK
    full = lambda s: pl.BlockSpec(s, lambda: (0,) * len(s))
    return pl.pallas_call(
        functools.partial(_route_kernel, n_sub=n_sub, row_tile=row_tile),
        in_specs=[full((n_e, t_rows))],
        out_specs=(full((2, t_rows)), full((2, t_rows)), full((2, n_e, n_ch)), full((4, n_sub))),
        out_shape=(jax.ShapeDtypeStruct((2, t_rows), _I32),
                   jax.ShapeDtypeStruct((2, t_rows), _F32),
                   jax.ShapeDtypeStruct((2, n_e, n_ch), _I32),
                   jax.ShapeDtypeStruct((4, n_sub), _I32)),
        compiler_params=pltpu.CompilerParams(vmem_limit_bytes=VMEM_LIMIT),
        name="route_sort",
    )(logits_t)


def _dispatch_kernel(clo_ref, chi_ref, hn_ref, posl_ref, x_ref, acc_scr):
    b = pl.program_id(0)
    ch = posl_ref.shape[2]
    subs = x_ref.shape[0] // SUB_ROWS
    for j in range(subs):
        s = b * subs + j
        row_id = s * SUB_ROWS + lax.broadcasted_iota(_I32, (SUB_ROWS, ch), 0)
        acc_scr[...] = jnp.zeros_like(acc_scr)

        def body(c, carry, row_id=row_id):
            pp = posl_ref[c]
            hit = (pp[0:1, :] == row_id) | (pp[1:2, :] == row_id)
            onehot = jnp.where(hit, 1.0, 0.0).astype(_BF16)
            r0 = pl.multiple_of(c * ch, ch)
            acc_scr[...] += _dot(onehot, hn_ref[pl.ds(r0, ch), :])
            return carry

        lax.fori_loop(clo_ref[s], chi_ref[s] + 1, body, 0)
        x_ref[j * SUB_ROWS:(j + 1) * SUB_ROWS, :] = acc_scr[...].astype(_BF16)


def _dispatch(hn, pos_l, c_lo, c_hi, n_blocks, row_tile):
    t_rows, d = hn.shape
    return pl.pallas_call(
        _dispatch_kernel,
        grid_spec=pltpu.PrefetchScalarGridSpec(
            num_scalar_prefetch=2,
            grid=(n_blocks,),
            in_specs=[pl.BlockSpec((t_rows, d), lambda b, lo, hi: (0, 0),
                                   pipeline_mode=pl.Buffered(1)),
                      pl.BlockSpec(pos_l.shape, lambda b, lo, hi: (0, 0, 0))],
            out_specs=pl.BlockSpec((row_tile, d), lambda b, lo, hi: (b, 0)),
            scratch_shapes=[pltpu.VMEM((SUB_ROWS, d), _F32)]),
        out_shape=jax.ShapeDtypeStruct((n_blocks * row_tile, d), _BF16),
        compiler_params=pltpu.CompilerParams(
            dimension_semantics=("arbitrary",), vmem_limit_bytes=VMEM_LIMIT),
        name="moe_dispatch",
    )(c_lo, c_hi, hn, pos_l)


def _ffn_grouped_kernel(be_ref, nv_ref, x_ref, wg_ref, wu_ref, wd_ref, y_ref, a_scr, acc_scr):
    b = pl.program_id(0)
    j = pl.program_id(1)
    last = j == pl.num_programs(1) - 1
    valid = b < nv_ref[0]

    @pl.when(valid)
    def _():
        y = _swiglu_partial(x_ref[...], wg_ref, wu_ref, wd_ref, a_scr)

        @pl.when(j == 0)
        def _():
            acc_scr[...] = y

        @pl.when(j > 0)
        def _():
            acc_scr[...] += y

        @pl.when(last)
        def _():
            y_ref[...] = acc_scr[...]

    @pl.when(jnp.logical_and(jnp.logical_not(valid), last))
    def _():
        y_ref[...] = jnp.zeros_like(y_ref)


def _ffn_grouped(x_sorted, wg, wu, wd, block_expert, n_valid, row_tile):
    rows, d = x_sorted.shape
    f = wg.shape[2]
    nj = FFN_COL_SPLIT
    tf = f // nj
    n_blocks = rows // row_tile

    def w_idx(b, j, be, nv):
        ok = b < nv[0]
        e = be[jnp.minimum(b, nv[0] - 1)]
        return e, jnp.where(ok, j, nj - 1)

    def wcol(b, j, be, nv):
        e, jj = w_idx(b, j, be, nv)
        return (e, 0, jj)

    def wrow(b, j, be, nv):
        e, jj = w_idx(b, j, be, nv)
        return (e, jj, 0)

    return pl.pallas_call(
        _ffn_grouped_kernel,
        grid_spec=pltpu.PrefetchScalarGridSpec(
            num_scalar_prefetch=2,
            grid=(n_blocks, nj),
            in_specs=[pl.BlockSpec((row_tile, d), lambda b, j, be, nv: (b, 0)),
                      pl.BlockSpec((1, d, tf), wcol),
                      pl.BlockSpec((1, d, tf), wcol),
                      pl.BlockSpec((1, tf, d), wrow)],
            out_specs=pl.BlockSpec((row_tile, d), lambda b, j, be, nv: (b, 0)),
            scratch_shapes=[pltpu.VMEM((row_tile, tf), _BF16),
                            pltpu.VMEM((row_tile, d), _F32)]),
        out_shape=jax.ShapeDtypeStruct((rows, d), _F32),
        compiler_params=pltpu.CompilerParams(
            dimension_semantics=("arbitrary", "arbitrary"), vmem_limit_bytes=VMEM_LIMIT),
        name="moe_experts",
    )(block_expert, n_valid, x_sorted, wg, wu, wd)


def _combine_kernel(w0_ref, sl_ref, res_ref, pos_ref, gate_ref, gfin_ref, y_hbm,
                    out_ref, ybuf, sems, tmp_scr, *, n_e, n_ch):
    c = pl.program_id(0)
    ch = res_ref.shape[0]

    def window_copies(cc, slot):
        return [pltpu.make_async_copy(
            y_hbm.at[pl.ds(pl.multiple_of(w0_ref[e * n_ch + cc], SUBLANES), ch)],
            ybuf.at[slot, e], sems.at[slot, e]) for e in range(n_e)]

    @pl.when(c == 0)
    def _():
        for cp in window_copies(0, 0):
            cp.start()

    @pl.when(c + 1 < n_ch)
    def _():
        for cp in window_copies(c + 1, (c + 1) % 2):
            cp.start()

    slot = c % 2
    for cp in window_copies(c, slot):
        cp.wait()

    lane = lax.broadcasted_iota(_I32, (ch, ch), 1)
    p1 = pos_ref[:, 0:1]
    p2 = pos_ref[:, 1:2]
    g1 = gate_ref[:, 0:1]
    g2 = gate_ref[:, 1:2]
    acc = res_ref[...]
    for e in range(n_e):
        w0 = w0_ref[e * n_ch + c]
        seg = sl_ref[e * n_ch + c]
        r1 = p1 - w0
        r2 = p2 - w0
        gm1 = jnp.where((r1 >= 0) & (r1 < seg), g1, 0.0)
        gm2 = jnp.where((r2 >= 0) & (r2 < seg), g2, 0.0)
        q = (jnp.where(r1 == lane, gm1, 0.0) + jnp.where(r2 == lane, gm2, 0.0)).astype(_BF16)
        acc = acc + _dot(q, ybuf[slot, e].astype(_BF16))
    normed = _rmsnorm(acc, gfin_ref[...])
    steps = ch // SUBLANES
    for k in range(tmp_scr.shape[0]):
        tmp_scr[k] = normed[:, k * LANES:(k + 1) * LANES]
    for bb in range(SUBLANES):
        for k in range(tmp_scr.shape[0]):
            out_ref[bb, :, k * LANES:(k + 1) * LANES] = (
                tmp_scr[k, pl.ds(bb, steps, stride=SUBLANES), :])


def _combine(res, pos_c, gates_c, gfin, y_sorted, w0_flat, seglen_flat, n_e, bsz, seq):
    t_rows, d = res.shape
    ch = TOKEN_CHUNK
    n_ch = t_rows // ch
    steps = ch // bsz
    return pl.pallas_call(
        functools.partial(_combine_kernel, n_e=n_e, n_ch=n_ch),
        grid_spec=pltpu.PrefetchScalarGridSpec(
            num_scalar_prefetch=2,
            grid=(n_ch,),
            in_specs=[pl.BlockSpec((ch, d), lambda c, w0, sl: (c, 0)),
                      pl.BlockSpec((ch, 2), lambda c, w0, sl: (c, 0)),
                      pl.BlockSpec((ch, 2), lambda c, w0, sl: (c, 0)),
                      pl.BlockSpec((1, d), lambda c, w0, sl: (0, 0)),
                      pl.BlockSpec(memory_space=pl.ANY)],
            out_specs=pl.BlockSpec((bsz, steps, d), lambda c, w0, sl: (0, c, 0)),
            scratch_shapes=[pltpu.VMEM((2, n_e, ch, d), _F32),
                            pltpu.SemaphoreType.DMA((2, n_e)),
                            pltpu.VMEM((d // LANES, ch, LANES), _F32)]),
        out_shape=jax.ShapeDtypeStruct((bsz, seq, d), _F32),
        compiler_params=pltpu.CompilerParams(
            dimension_semantics=("arbitrary",), vmem_limit_bytes=VMEM_LIMIT),
        name="moe_combine",
    )(w0_flat, seglen_flat, res, pos_c, gates_c, gfin, y_sorted)


def _moe_layer(h, hn, logits_t, wg, wu, wd, gfin, bsz, seq):
    t_rows, d = h.shape
    n_e = logits_t.shape[0]
    tm = FFN_ROW_TILE
    n_ch = t_rows // TOKEN_CHUNK
    max_rows = 2 * t_rows + (SUBLANES - 1) * n_e * n_ch + n_e * tm
    n_blocks = -(-max_rows // tm) + 1
    n_sub = n_blocks * (tm // SUB_ROWS)
    n_sub_pad = -(-n_sub // LANES) * LANES

    pos, gates, segtab, subtab = _route(logits_t, n_sub_pad, tm)
    pos_l = jnp.transpose(pos.reshape(2, n_ch, TOKEN_CHUNK), (1, 0, 2))
    x_sorted = _dispatch(hn, pos_l, subtab[1], subtab[2], n_blocks, tm)
    y_sorted = _ffn_grouped(x_sorted, wg, wu, wd, subtab[0, ::tm // SUB_ROWS],
                            subtab[3, :1], tm)
    return _combine(h, jnp.transpose(pos), jnp.transpose(gates), gfin, y_sorted,
                    segtab[0].reshape(-1), segtab[1].reshape(-1), n_e, bsz, seq)


def _block_diag_groups(w, groups):
    h, k, n = w.shape
    hpg = h // groups
    eye = jnp.eye(hpg, dtype=w.dtype)
    wq = w.reshape(groups, hpg, k, n)
    out = jnp.einsum("qhkn,hj->qhkjn", wq, eye)
    return out.reshape(groups, hpg * k, hpg * n)


def kernel(x, norm_mix_g, w_in, ssm_log_dt, ssm_a_re, ssm_a_im, ssm_b_re, ssm_b_im, ssm_c_re, ssm_c_im, ssm_d, ssm_w_glu, ssm_b_glu, pool_w, pool_scale, w_out, norm_ffn_g, ffn_w_gate, ffn_w_up, ffn_w_down, router_w, moe_w_gate, moe_w_up, moe_w_down, final_norm_g):
    bsz, seq, d = x.shape
    depth, n_heads, n_state, n_grp_ch = ssm_b_re.shape
    t_rows = bsz * seq
    assert bsz == SUBLANES and depth % 2 == 0
    groups = n_heads // HEADS_PER_GROUP

    nh = depth * n_heads
    lam_re, lam_im, bb_re, bb_im = _discretize(
        ssm_log_dt.reshape(nh), ssm_a_re.reshape(nh, n_state), ssm_a_im.reshape(nh, n_state),
        jnp.swapaxes(ssm_b_re, 2, 3).reshape(nh, n_grp_ch, n_state),
        jnp.swapaxes(ssm_b_im, 2, 3).reshape(nh, n_grp_ch, n_state))
    lam_re = lam_re.reshape(depth, groups, 1, HEADS_PER_GROUP * n_state)
    lam_im = lam_im.reshape(depth, groups, 1, HEADS_PER_GROUP * n_state)
    lam = jnp.concatenate([lam_re, lam_im], axis=2).reshape(depth, 2 * groups, -1)
    bb_re = bb_re.reshape(depth, n_heads, n_grp_ch, n_state)
    bb_im = bb_im.reshape(depth, n_heads, n_grp_ch, n_state)

    h = jnp.transpose(x, (1, 0, 2)).reshape(t_rows, d)
    row = lambda v: v.reshape(1, -1)
    out = None
    for i in range(depth):
        bw = jnp.concatenate([_block_diag_groups(bb_re[i], groups),
                              _block_diag_groups(bb_im[i], groups)], axis=2).astype(_BF16)
        cw = jnp.concatenate(
            [_block_diag_groups(jnp.swapaxes(ssm_c_re[i], 1, 2), groups),
             _block_diag_groups(-jnp.swapaxes(ssm_c_im[i], 1, 2), groups)], axis=1).astype(_BF16)
        is_moe = i % 2 == 1
        j = i // 2
        rwt = jnp.transpose(router_w[j]) if is_moe else None
        outs = _mix_layer(
            h, row(norm_mix_g[i]), w_in[i].astype(_BF16), bw, lam[i], cw, row(ssm_d[i]),
            ssm_w_glu[i].astype(_BF16), row(ssm_b_glu[i]), pool_w[i].astype(_BF16),
            row(pool_scale[i]), w_out[i].astype(_BF16), row(norm_ffn_g[i]), rwt)
        if is_moe:
            assert i == depth - 1
            h, hn, logits_t = outs
            out = _moe_layer(h, hn, logits_t, moe_w_gate[j].astype(_BF16),
                             moe_w_up[j].astype(_BF16), moe_w_down[j].astype(_BF16),
                             row(final_norm_g), bsz, seq)
        else:
            h, hn = outs
            h = _ffn(hn, ffn_w_gate[j][None].astype(_BF16), ffn_w_up[j][None].astype(_BF16),
                     ffn_w_down[j][None].astype(_BF16), h)
    return out
```

```python
import functools
import math

import jax
import jax.numpy as jnp
from jax import lax
from jax.experimental import pallas as pl
from jax.experimental.pallas import tpu as pltpu

RMS_EPS = 1e-6
POOL_WINDOWS = (2, 4, 8, 16)
A_RE_MAX = -1e-4
GELU_C0 = math.sqrt(2.0 / math.pi)
GELU_C1 = 0.044715

SUBLANES = 8
LANES = 128
MXU_DIM = 256

HEADS_PER_GROUP = 16
TIME_TILE = 64
SCAN_COLS = 512
FFN_ROW_TILE = 512
FFN_COL_CHUNK = 256
FFN_COL_SPLIT = 2
TOKEN_CHUNK = MXU_DIM
SUB_ROWS = MXU_DIM
WINDOW_PIECES = 2
VMEM_LIMIT = 48 * 1024 * 1024

_F32 = jnp.float32
_BF16 = jnp.bfloat16
_I32 = jnp.int32


def _dot(a, b):
    return jnp.dot(a, b, preferred_element_type=_F32)


def _dot_nt(a, b):
    return lax.dot_general(a, b, (((1,), (1,)), ((), ())), preferred_element_type=_F32)


def _rmsnorm(x, g):
    inv = lax.rsqrt(jnp.mean(x * x, axis=-1, keepdims=True) + RMS_EPS)
    return x * inv * g


def _sigmoid(x):
    return 1.0 / (1.0 + jnp.exp(-x))


def _ceil_to(x, m):
    return jnp.floor((x + (m - 1.0)) * (1.0 / m)) * m


def _discretize_kernel(log_dt_ref, a_re_ref, a_im_ref, b_re_ref, b_im_ref,
                       lam_re_ref, lam_im_ref, bb_re_ref, bb_im_ref):
    dt = jnp.exp(log_dt_ref[...])
    ar = jnp.minimum(a_re_ref[...], A_RE_MAX)
    ai = a_im_ref[...]
    mag = jnp.exp(ar * dt)
    lam_re = mag * jnp.cos(ai * dt)
    lam_im = mag * jnp.sin(ai * dt)
    den = ar * ar + ai * ai
    nr = lam_re - 1.0
    ni = lam_im
    coef_re = (nr * ar + ni * ai) / den
    coef_im = (ni * ar - nr * ai) / den
    lam_re_ref[...] = lam_re
    lam_im_ref[...] = lam_im
    br = b_re_ref[...]
    bi = b_im_ref[...]
    bb_re_ref[...] = coef_re * br - coef_im * bi
    bb_im_ref[...] = coef_re * bi + coef_im * br


def _discretize(log_dt, a_re, a_im, b_re_t, b_im_t):
    n, g, p = b_re_t.shape
    full3 = lambda s: pl.BlockSpec(s, lambda: (0, 0, 0))
    return pl.pallas_call(
        _discretize_kernel,
        out_shape=(jax.ShapeDtypeStruct((n, 1, p), _F32),
                   jax.ShapeDtypeStruct((n, 1, p), _F32),
                   jax.ShapeDtypeStruct((n, g, p), _F32),
                   jax.ShapeDtypeStruct((n, g, p), _F32)),
        in_specs=[full3((n, 1, 1)), full3((n, 1, p)), full3((n, 1, p)),
                  full3((n, g, p)), full3((n, g, p))],
        out_specs=(full3((n, 1, p)), full3((n, 1, p)),
                   full3((n, g, p)), full3((n, g, p))),
        name="ssm_discretize",
    )(log_dt.reshape(n, 1, 1), a_re.reshape(n, 1, p), a_im.reshape(n, 1, p),
      b_re_t, b_im_t)


def _mix_kernel(*refs, time_tile, with_router, batch_major_in):
    (h_ref, gmix_ref, win_ref, bw_ref, lam_ref, cw_ref, dskip_ref, wglu_ref,
     bglu_ref, poolw_ref, pscale_ref, wout_ref, gffn_ref) = refs[:13]
    refs = refs[13:]
    rwt_ref = logits_ref = xin_scr = None
    if with_router:
        rwt_ref, hout_ref, hn_ref, logits_ref = refs[:4]
        refs = refs[4:]
    else:
        hout_ref, hn_ref = refs[:2]
        refs = refs[2:]
    s_scr, state_scr, ext_scr, mixed_scr = refs[:4]
    if batch_major_in:
        xin_scr = refs[4]

    step = pl.program_id(0)
    rows = time_tile * SUBLANES
    d_ssm = dskip_ref.shape[1]
    n_groups = bw_ref.shape[0]
    gin = bw_ref.shape[1]
    gstate = bw_ref.shape[2] // 2
    hist = ext_scr.shape[0] - rows

    @pl.when(step == 0)
    def _():
        state_scr[...] = jnp.zeros_like(state_scr)
        ext_scr[0:hist, :] = jnp.zeros((hist, ext_scr.shape[1]), _F32)

    if batch_major_in:
        for bb in range(SUBLANES):
            for k in range(xin_scr.shape[0]):
                xin_scr[k, pl.ds(bb, time_tile, stride=SUBLANES), :] = (
                    h_ref[bb, :, k * LANES:(k + 1) * LANES])
        h = jnp.concatenate([xin_scr[k] for k in range(xin_scr.shape[0])], axis=1)
    else:
        h = h_ref[...]
    hn = _rmsnorm(h, gmix_ref[...]).astype(_BF16)
    proj = _dot(hn, win_ref[...])
    u_ssm = proj[:, :d_ssm]
    ext_scr[hist:, :] = proj[:, d_ssm:]

    for q in range(n_groups):
        ug = u_ssm[:, q * gin:(q + 1) * gin].astype(_BF16)
        s_scr[q] = _dot(ug, bw_ref[q])
        for c in range(gstate // SCAN_COLS):
            re0 = c * SCAN_COLS
            im0 = gstate + c * SCAN_COLS
            lr = jnp.broadcast_to(lam_ref[2 * q:2 * q + 1, re0:re0 + SCAN_COLS],
                                  (SUBLANES, SCAN_COLS))
            li = jnp.broadcast_to(lam_ref[2 * q + 1:2 * q + 2, re0:re0 + SCAN_COLS],
                                  (SUBLANES, SCAN_COLS))
            sre = state_scr[q, :, re0:re0 + SCAN_COLS]
            sim = state_scr[q, :, im0:im0 + SCAN_COLS]
            for t in range(time_tile):
                r0 = t * SUBLANES
                bre = s_scr[q, r0:r0 + SUBLANES, re0:re0 + SCAN_COLS]
                bim = s_scr[q, r0:r0 + SUBLANES, im0:im0 + SCAN_COLS]
                sre, sim = (lr * sre - li * sim + bre, lr * sim + li * sre + bim)
                s_scr[q, r0:r0 + SUBLANES, re0:re0 + SCAN_COLS] = sre
                s_scr[q, r0:r0 + SUBLANES, im0:im0 + SCAN_COLS] = sim
            state_scr[q, :, re0:re0 + SCAN_COLS] = sre
            state_scr[q, :, im0:im0 + SCAN_COLS] = sim
        yq = _dot(s_scr[q].astype(_BF16), cw_ref[q])
        yq = yq + dskip_ref[:, q * gin:(q + 1) * gin] * u_ssm[:, q * gin:(q + 1) * gin]
        hq = 0.5 * yq * (1.0 + jnp.tanh(GELU_C0 * (yq + GELU_C1 * (yq * yq * yq))))
        mixed_scr[:, q * gin:(q + 1) * gin] = hq.astype(_BF16)
    hg = mixed_scr[:, :d_ssm]
    gate = _sigmoid(_dot(hg, wglu_ref[...]) + bglu_ref[...])
    mixed_scr[:, :d_ssm] = (hg.astype(_F32) * gate).astype(_BF16)

    n_ext = rows + hist
    pos = step * time_tile + (lax.broadcasted_iota(_I32, (rows, LANES), 0) // SUBLANES)
    for g, w in enumerate(POOL_WINDOWS):
        c0 = g * LANES
        e = ext_scr[:, c0:c0 + LANES]
        acc = e
        n_acc = n_ext
        span = 1
        while span < w:
            sh = span * SUBLANES
            acc = acc[sh:, :] + acc[:n_acc - sh, :]
            n_acc -= sh
            span *= 2
        wsum = acc[n_acc - rows:, :]
        cnt = jnp.minimum(pos + 1, w).astype(_F32)
        pooled = wsum / cnt - e[hist:, :]
        mg = _dot(pooled.astype(_BF16), poolw_ref[g]) * pscale_ref[:, c0:c0 + LANES]
        mixed_scr[:, d_ssm + c0:d_ssm + c0 + LANES] = mg.astype(_BF16)
    ext_scr[0:hist, :] = ext_scr[rows:rows + hist, :]

    hout = h + _dot(mixed_scr[...], wout_ref[...])
    hout_ref[...] = hout
    hn2 = _rmsnorm(hout, gffn_ref[...])
    hn_ref[...] = hn2.astype(_BF16)
    if with_router:
        logits_ref[...] = lax.dot_general(
            rwt_ref[...], hn2, (((1,), (1,)), ((), ())),
            precision=lax.Precision.HIGHEST, preferred_element_type=_F32)


def _mix_layer(h, gmix, w_in, bw, lam, cw, dskip, wglu, bglu, poolw, pscale, wout,
               gffn, rwt):
    batch_major_in = h.ndim == 3
    d = h.shape[-1]
    t_rows = h.size // d
    rows = TIME_TILE * SUBLANES
    n_steps = t_rows // rows
    d_ssm = dskip.shape[1]
    d_pool = pscale.shape[1]
    hist = max(POOL_WINDOWS) * SUBLANES
    with_router = rwt is not None

    def const(a):
        nd = a.ndim
        return pl.BlockSpec(a.shape, lambda i, nd=nd: (0,) * nd)

    row_blk = lambda width: pl.BlockSpec((rows, width), lambda i: (i, 0))
    ins = [h, gmix, w_in, bw, lam, cw, dskip, wglu, bglu, poolw, pscale, wout, gffn]
    h_spec = (pl.BlockSpec((h.shape[0], TIME_TILE, d), lambda i: (0, i, 0))
              if batch_major_in else row_blk(d))
    in_specs = [h_spec] + [const(a) for a in ins[1:]]
    out_shape = [jax.ShapeDtypeStruct((t_rows, d), _F32),
                 jax.ShapeDtypeStruct((t_rows, d), _BF16)]
    out_specs = [row_blk(d), row_blk(d)]
    if with_router:
        ins.append(rwt)
        in_specs.append(const(rwt))
        out_shape.append(jax.ShapeDtypeStruct((rwt.shape[0], t_rows), _F32))
        out_specs.append(pl.BlockSpec((rwt.shape[0], rows), lambda i: (0, i)))
    scratch = [
        pltpu.VMEM((bw.shape[0], rows, bw.shape[2]), _F32),
        pltpu.VMEM((bw.shape[0], SUBLANES, bw.shape[2]), _F32),
        pltpu.VMEM((rows + hist, d_pool), _F32),
        pltpu.VMEM((rows, d_ssm + d_pool), _BF16),
    ]
    if batch_major_in:
        scratch.append(pltpu.VMEM((d // LANES, rows, LANES), _F32))
    return pl.pallas_call(
        functools.partial(_mix_kernel, time_tile=TIME_TILE, with_router=with_router,
                          batch_major_in=batch_major_in),
        grid=(n_steps,),
        in_specs=in_specs,
        out_specs=out_specs,
        out_shape=out_shape,
        scratch_shapes=scratch,
        compiler_params=pltpu.CompilerParams(
            dimension_semantics=("arbitrary",), vmem_limit_bytes=VMEM_LIMIT),
        name="mix_router" if with_router else "mix",
    )(*ins)


def _swiglu_partial(x, wg_ref, wu_ref, wd_ref, a_scr):
    tf = wg_ref.shape[2]
    c0 = 0
    while c0 < tf:
        cw = min(FFN_COL_CHUNK, tf - c0)
        g = _dot(x, wg_ref[0, :, c0:c0 + cw])
        u = _dot(x, wu_ref[0, :, c0:c0 + cw])
        a_scr[:, c0:c0 + cw] = (g * _sigmoid(g) * u).astype(_BF16)
        c0 += cw
    return _dot(a_scr[...], wd_ref[0])


def _ffn_kernel(x_ref, wg_ref, wu_ref, wd_ref, res_ref, out_ref, a_scr):
    j = pl.program_id(1)

    @pl.when(j == 0)
    def _():
        out_ref[...] = res_ref[...]

    out_ref[...] += _swiglu_partial(x_ref[...], wg_ref, wu_ref, wd_ref, a_scr)


def _ffn(x, wg, wu, wd, res):
    t_rows, d = x.shape
    f = wg.shape[2]
    tm = FFN_ROW_TILE
    nj = FFN_COL_SPLIT
    tf = f // nj
    return pl.pallas_call(
        _ffn_kernel,
        grid=(t_rows // tm, nj),
        in_specs=[
            pl.BlockSpec((tm, d), lambda i, j: (i, 0)),
            pl.BlockSpec((1, d, tf), lambda i, j: (0, 0, j)),
            pl.BlockSpec((1, d, tf), lambda i, j: (0, 0, j)),
            pl.BlockSpec((1, tf, d), lambda i, j: (0, j, 0)),
            pl.BlockSpec((tm, d), lambda i, j: (i, 0)),
        ],
        out_specs=pl.BlockSpec((tm, d), lambda i, j: (i, 0)),
        out_shape=jax.ShapeDtypeStruct((t_rows, d), _F32),
        scratch_shapes=[pltpu.VMEM((tm, tf), _BF16)],
        compiler_params=pltpu.CompilerParams(
            dimension_semantics=("arbitrary", "arbitrary"), vmem_limit_bytes=VMEM_LIMIT),
        name="ffn_dense",
    )(x, wg, wu, wd, res)


def _route_kernel(lt_ref, pos_ref, gates_ref, segtab_ref, subtab_ref, *, n_sub, row_tile):
    l = lt_ref[...]
    n_e, t_rows = l.shape
    ch = TOKEN_CHUNK
    n_ch = t_rows // ch
    ie = lax.broadcasted_iota(_I32, l.shape, 0)
    m1 = jnp.max(l, axis=0, keepdims=True)
    i1 = jnp.min(jnp.where(l == m1, ie, n_e), axis=0, keepdims=True)
    l2 = jnp.where(ie == i1, -jnp.inf, l)
    m2 = jnp.max(l2, axis=0, keepdims=True)
    i2 = jnp.min(jnp.where(l2 == m2, ie, n_e), axis=0, keepdims=True)
    e2 = jnp.exp(m2 - m1)
    den = 1.0 + e2
    gates_ref[0:1, :] = 1.0 / den
    gates_ref[1:2, :] = e2 / den
    sel1 = ie == i1
    sel2 = ie == i2
    chosen = jnp.where(sel1 | sel2, 1.0, 0.0).astype(_BF16)

    chunk_of_t = lax.broadcasted_iota(_I32, (n_ch, t_rows), 1) // ch
    in_chunk = jnp.where(chunk_of_t == lax.broadcasted_iota(_I32, (n_ch, t_rows), 0),
                         1.0, 0.0).astype(_BF16)
    n_ec = _dot_nt(chosen, in_chunk)
    n_ce = _dot_nt(in_chunk, chosen)
    pad_ec = _ceil_to(n_ec, float(SUBLANES))
    pad_ce = _ceil_to(n_ce, float(SUBLANES))
    cc_r = lax.broadcasted_iota(_I32, (n_ch, n_ch), 0)
    cc_c = lax.broadcasted_iota(_I32, (n_ch, n_ch), 1)
    before = jnp.where(cc_r < cc_c, 1.0, 0.0).astype(_BF16)
    after = jnp.where(cc_c < cc_r, 1.0, 0.0).astype(_BF16)
    segoff_ec = _dot(pad_ec.astype(_BF16), before)
    segoff_ce = _dot(after, pad_ce.astype(_BF16))
    tot_col = _ceil_to(jnp.sum(pad_ec, axis=1, keepdims=True), float(row_tile))
    tot_row = _ceil_to(jnp.sum(pad_ce, axis=0, keepdims=True), float(row_tile))
    ee_r = lax.broadcasted_iota(_I32, (n_e, n_e), 0)
    ee_c = lax.broadcasted_iota(_I32, (n_e, n_e), 1)
    base_col = jnp.sum(jnp.where(ee_c < ee_r, tot_row, 0.0), axis=1, keepdims=True)
    base_row = jnp.sum(jnp.where(ee_r < ee_c, tot_col, 0.0), axis=0, keepdims=True)
    w0_ec = base_col + segoff_ec
    w0_ce = base_row + segoff_ce
    segtab_ref[0] = w0_ec.astype(_I32)
    segtab_ref[1] = pad_ec.astype(_I32)

    tt_r = lax.broadcasted_iota(_I32, (ch, ch), 0)
    tt_c = lax.broadcasted_iota(_I32, (ch, ch), 1)
    earlier = jnp.where(tt_r < tt_c, 1.0, 0.0).astype(_BF16)
    for c in range(n_ch):
        cols = slice(c * ch, (c + 1) * ch)
        rank = _dot(chosen[:, cols], earlier)
        offs = w0_ec[:, c:c + 1] + rank
        p1 = jnp.sum(jnp.where(sel1[:, cols], offs, 0.0), axis=0, keepdims=True)
        p2 = jnp.sum(jnp.where(sel2[:, cols], offs, 0.0), axis=0, keepdims=True)
        pos_ref[0:1, cols] = p1.astype(_I32)
        pos_ref[1:2, cols] = p2.astype(_I32)

    start = (lax.broadcasted_iota(_I32, (1, n_sub), 1) * SUB_ROWS).astype(_F32)
    end_col = base_col + tot_col
    owner = jnp.sum(jnp.where(end_col <= start, 1.0, 0.0), axis=0, keepdims=True)
    segend_ce = w0_ce + pad_ce
    c_lo = jnp.zeros((1, n_sub), _F32)
    c_hi = jnp.where(owner >= n_e, -1.0, 0.0)
    for e in range(n_e):
        lo_e = jnp.sum(jnp.where(segend_ce[:, e:e + 1] <= start, 1.0, 0.0), axis=0, keepdims=True)
        hi_e = jnp.sum(jnp.where(w0_ce[:, e:e + 1] < start + SUB_ROWS, 1.0, 0.0),
                       axis=0, keepdims=True) - 1.0
        mine = owner == float(e)
        c_lo = c_lo + jnp.where(mine, lo_e, 0.0)
        c_hi = c_hi + jnp.where(mine, hi_e, 0.0)
    n_valid = jnp.sum(tot_col, axis=0, keepdims=True) * (1.0 / row_tile)
    subtab_ref[0:1, :] = jnp.minimum(owner, n_e - 1.0).astype(_I32)
    subtab_ref[1:2, :] = c_lo.astype(_I32)
    subtab_ref[2:3, :] = c_hi.astype(_I32)
    subtab_ref[3:4, :] = jnp.broadcast_to(n_valid, (1, n_sub)).astype(_I32)


def _route(logits_t, n_sub, row_tile):
    n_e, t_rows = logits_t.shape
    n_ch = t_rows // TOKEN_CHUNK
    full = lambda s: pl.BlockSpec(s, lambda: (0,) * len(s))
    return pl.pallas_call(
        functools.partial(_route_kernel, n_sub=n_sub, row_tile=row_tile),
        in_specs=[full((n_e, t_rows))],
        out_specs=(full((2, t_rows)), full((2, t_rows)), full((2, n_e, n_ch)), full((4, n_sub))),
        out_shape=(jax.ShapeDtypeStruct((2, t_rows), _I32),
                   jax.ShapeDtypeStruct((2, t_rows), _F32),
                   jax.ShapeDtypeStruct((2, n_e, n_ch), _I32),
                   jax.ShapeDtypeStruct((4, n_sub), _I32)),
        compiler_params=pltpu.CompilerParams(vmem_limit_bytes=VMEM_LIMIT),
        name="route_sort",
    )(logits_t)


def _dispatch_kernel(clo_ref, chi_ref, hn_ref, posl_ref, x_ref, acc_scr):
    b = pl.program_id(0)
    ch = posl_ref.shape[2]
    subs = x_ref.shape[0] // SUB_ROWS
    for j in range(subs):
        s = b * subs + j
        row_id = s * SUB_ROWS + lax.broadcasted_iota(_I32, (SUB_ROWS, ch), 0)
        acc_scr[...] = jnp.zeros_like(acc_scr)

        def body(c, carry, row_id=row_id):
            pp = posl_ref[c]
            hit = (pp[0:1, :] == row_id) | (pp[1:2, :] == row_id)
            onehot = jnp.where(hit, 1.0, 0.0).astype(_BF16)
            r0 = pl.multiple_of(c * ch, ch)
            acc_scr[...] += _dot(onehot, hn_ref[pl.ds(r0, ch), :])
            return carry

        lax.fori_loop(clo_ref[s], chi_ref[s] + 1, body, 0)
        x_ref[j * SUB_ROWS:(j + 1) * SUB_ROWS, :] = acc_scr[...].astype(_BF16)


def _dispatch(hn, pos_l, c_lo, c_hi, n_blocks, row_tile):
    t_rows, d = hn.shape
    return pl.pallas_call(
        _dispatch_kernel,
        grid_spec=pltpu.PrefetchScalarGridSpec(
            num_scalar_prefetch=2,
            grid=(n_blocks,),
            in_specs=[pl.BlockSpec((t_rows, d), lambda b, lo, hi: (0, 0),
                                   pipeline_mode=pl.Buffered(1)),
                      pl.BlockSpec(pos_l.shape, lambda b, lo, hi: (0, 0, 0))],
            out_specs=pl.BlockSpec((row_tile, d), lambda b, lo, hi: (b, 0)),
            scratch_shapes=[pltpu.VMEM((SUB_ROWS, d), _F32)]),
        out_shape=jax.ShapeDtypeStruct((n_blocks * row_tile, d), _BF16),
        compiler_params=pltpu.CompilerParams(
            dimension_semantics=("arbitrary",), vmem_limit_bytes=VMEM_LIMIT),
        name="moe_dispatch",
    )(c_lo, c_hi, hn, pos_l)


def _ffn_grouped_kernel(be_ref, nv_ref, x_ref, wg_ref, wu_ref, wd_ref, y_ref, a_scr, acc_scr):
    b = pl.program_id(0)
    j = pl.program_id(1)
    last = j == pl.num_programs(1) - 1
    valid = b < nv_ref[0]

    @pl.when(valid)
    def _():
        y = _swiglu_partial(x_ref[...], wg_ref, wu_ref, wd_ref, a_scr)

        @pl.when(j == 0)
        def _():
            acc_scr[...] = y

        @pl.when(j > 0)
        def _():
            acc_scr[...] += y

        @pl.when(last)
        def _():
            y_ref[...] = acc_scr[...]

    @pl.when(jnp.logical_and(jnp.logical_not(valid), last))
    def _():
        y_ref[...] = jnp.zeros_like(y_ref)


def _ffn_grouped(x_sorted, wg, wu, wd, block_expert, n_valid, row_tile):
    rows, d = x_sorted.shape
    f = wg.shape[2]
    nj = FFN_COL_SPLIT
    tf = f // nj
    n_blocks = rows // row_tile

    def w_idx(b, j, be, nv):
        ok = b < nv[0]
        e = be[jnp.minimum(b, nv[0] - 1)]
        return e, jnp.where(ok, j, nj - 1)

    def wcol(b, j, be, nv):
        e, jj = w_idx(b, j, be, nv)
        return (e, 0, jj)

    def wrow(b, j, be, nv):
        e, jj = w_idx(b, j, be, nv)
        return (e, jj, 0)

    return pl.pallas_call(
        _ffn_grouped_kernel,
        grid_spec=pltpu.PrefetchScalarGridSpec(
            num_scalar_prefetch=2,
            grid=(n_blocks, nj),
            in_specs=[pl.BlockSpec((row_tile, d), lambda b, j, be, nv: (b, 0)),
                      pl.BlockSpec((1, d, tf), wcol),
                      pl.BlockSpec((1, d, tf), wcol),
                      pl.BlockSpec((1, tf, d), wrow)],
            out_specs=pl.BlockSpec((row_tile, d), lambda b, j, be, nv: (b, 0)),
            scratch_shapes=[pltpu.VMEM((row_tile, tf), _BF16),
                            pltpu.VMEM((row_tile, d), _F32)]),
        out_shape=jax.ShapeDtypeStruct((rows, d), _F32),
        compiler_params=pltpu.CompilerParams(
            dimension_semantics=("arbitrary", "arbitrary"), vmem_limit_bytes=VMEM_LIMIT),
        name="moe_experts",
    )(block_expert, n_valid, x_sorted, wg, wu, wd)


def _combine_kernel(w0_ref, sl_ref, res_ref, pos_ref, gate_ref, gfin_ref, y_hbm,
                    out_ref, ybuf, sems, tmp_scr, acc_scr, *, n_e, n_ch):
    c = pl.program_id(0)
    ch = res_ref.shape[0]
    piece = ch // WINDOW_PIECES

    def piece_copy(cc, slot, e, k):
        row0 = pl.multiple_of(w0_ref[e * n_ch + cc] + k * piece, SUBLANES)
        return pltpu.make_async_copy(
            y_hbm.at[pl.ds(row0, piece)],
            ybuf.at[slot, k, pl.ds(e * piece, piece)], sems.at[slot, e, k])

    def for_each_piece(cc, slot, fn):
        for e in range(n_e):
            fn(piece_copy(cc, slot, e, 0))
            for k in range(1, WINDOW_PIECES):
                @pl.when(sl_ref[e * n_ch + cc] > k * piece)
                def _(e=e, k=k):
                    fn(piece_copy(cc, slot, e, k))

    @pl.when(c == 0)
    def _():
        ybuf[...] = jnp.zeros_like(ybuf)
        for_each_piece(0, 0, lambda cp: cp.start())

    @pl.when(c + 1 < n_ch)
    def _():
        for_each_piece(c + 1, (c + 1) % 2, lambda cp: cp.start())

    slot = c % 2
    for_each_piece(c, slot, lambda cp: cp.wait())

    per_dot = MXU_DIM // piece
    depth = per_dot * piece
    lane = lax.broadcasted_iota(_I32, (1, depth), 1)
    p1b = jnp.broadcast_to(pos_ref[:, 0:1], (ch, depth))
    p2b = jnp.broadcast_to(pos_ref[:, 1:2], (ch, depth))
    g1b = jnp.broadcast_to(gate_ref[:, 0:1], (ch, depth))
    g2b = jnp.broadcast_to(gate_ref[:, 1:2], (ch, depth))

    def scatter_back(m, k):
        tgt = jnp.full((1, depth), -1, _I32)
        for i in range(per_dot):
            e = m * per_dot + i
            row = lane - i * piece + k * piece
            mine = (lane >= i * piece) & (lane < (i + 1) * piece) & (row < sl_ref[e * n_ch + c])
            tgt = jnp.where(mine, row + w0_ref[e * n_ch + c], tgt)
        q = (jnp.where(p1b == tgt, g1b, 0.0) + jnp.where(p2b == tgt, g2b, 0.0)).astype(_BF16)
        return _dot(q, ybuf[slot, k, m * depth:(m + 1) * depth, :].astype(_BF16))

    def any_reaches(m, k):
        hit = sl_ref[(m * per_dot) * n_ch + c] > k * piece
        for i in range(1, per_dot):
            hit = jnp.logical_or(hit, sl_ref[(m * per_dot + i) * n_ch + c] > k * piece)
        return hit

    acc = res_ref[...]
    for m in range(n_e // per_dot):
        acc = acc + scatter_back(m, 0)
    acc_scr[...] = acc
    for m in range(n_e // per_dot):
        for k in range(1, WINDOW_PIECES):
            @pl.when(any_reaches(m, k))
            def _(m=m, k=k):
                acc_scr[...] += scatter_back(m, k)
    normed = _rmsnorm(acc_scr[...], gfin_ref[...])
    steps = ch // SUBLANES
    for k in range(tmp_scr.shape[0]):
        tmp_scr[k] = normed[:, k * LANES:(k + 1) * LANES]
    for bb in range(SUBLANES):
        for k in range(tmp_scr.shape[0]):
            out_ref[bb, :, k * LANES:(k + 1) * LANES] = (
                tmp_scr[k, pl.ds(bb, steps, stride=SUBLANES), :])


def _combine(res, pos_c, gates_c, gfin, y_sorted, w0_flat, seglen_flat, n_e, bsz, seq):
    t_rows, d = res.shape
    ch = TOKEN_CHUNK
    n_ch = t_rows // ch
    steps = ch // bsz
    return pl.pallas_call(
        functools.partial(_combine_kernel, n_e=n_e, n_ch=n_ch),
        grid_spec=pltpu.PrefetchScalarGridSpec(
            num_scalar_prefetch=2,
            grid=(n_ch,),
            in_specs=[pl.BlockSpec((ch, d), lambda c, w0, sl: (c, 0)),
                      pl.BlockSpec((ch, 2), lambda c, w0, sl: (c, 0)),
                      pl.BlockSpec((ch, 2), lambda c, w0, sl: (c, 0)),
                      pl.BlockSpec((1, d), lambda c, w0, sl: (0, 0)),
                      pl.BlockSpec(memory_space=pl.ANY)],
            out_specs=pl.BlockSpec((bsz, steps, d), lambda c, w0, sl: (0, c, 0)),
            scratch_shapes=[pltpu.VMEM((2, WINDOW_PIECES, n_e * (ch // WINDOW_PIECES), d), _F32),
                            pltpu.SemaphoreType.DMA((2, n_e, WINDOW_PIECES)),
                            pltpu.VMEM((d // LANES, ch, LANES), _F32),
                            pltpu.VMEM((ch, d), _F32)]),
        out_shape=jax.ShapeDtypeStruct((bsz, seq, d), _F32),
        compiler_params=pltpu.CompilerParams(
            dimension_semantics=("arbitrary",), vmem_limit_bytes=VMEM_LIMIT),
        name="moe_combine",
    )(w0_flat, seglen_flat, res, pos_c, gates_c, gfin, y_sorted)


def _moe_layer(h, hn, logits_t, wg, wu, wd, gfin, bsz, seq):
    t_rows, d = h.shape
    n_e = logits_t.shape[0]
    tm = FFN_ROW_TILE
    n_ch = t_rows // TOKEN_CHUNK
    max_rows = 2 * t_rows + (SUBLANES - 1) * n_e * n_ch + n_e * tm
    n_blocks = -(-max_rows // tm) + 1
    n_sub = n_blocks * (tm // SUB_ROWS)
    n_sub_pad = -(-n_sub // LANES) * LANES

    pos, gates, segtab, subtab = _route(logits_t, n_sub_pad, tm)
    pos_l = jnp.transpose(pos.reshape(2, n_ch, TOKEN_CHUNK), (1, 0, 2))
    x_sorted = _dispatch(hn, pos_l, subtab[1], subtab[2], n_blocks, tm)
    y_sorted = _ffn_grouped(x_sorted, wg, wu, wd, subtab[0, ::tm // SUB_ROWS],
                            subtab[3, :1], tm)
    return _combine(h, jnp.transpose(pos), jnp.transpose(gates), gfin, y_sorted,
                    segtab[0].reshape(-1), segtab[1].reshape(-1), n_e, bsz, seq)


def _block_diag_groups(w, groups):
    h, k, n = w.shape
    hpg = h // groups
    eye = jnp.eye(hpg, dtype=w.dtype)
    wq = w.reshape(groups, hpg, k, n)
    out = jnp.einsum("qhkn,hj->qhkjn", wq, eye)
    return out.reshape(groups, hpg * k, hpg * n)


def kernel(x, norm_mix_g, w_in, ssm_log_dt, ssm_a_re, ssm_a_im, ssm_b_re, ssm_b_im, ssm_c_re, ssm_c_im, ssm_d, ssm_w_glu, ssm_b_glu, pool_w, pool_scale, w_out, norm_ffn_g, ffn_w_gate, ffn_w_up, ffn_w_down, router_w, moe_w_gate, moe_w_up, moe_w_down, final_norm_g):
    bsz, seq, d = x.shape
    depth, n_heads, n_state, n_grp_ch = ssm_b_re.shape
    t_rows = bsz * seq
    assert bsz == SUBLANES and depth % 2 == 0
    groups = n_heads // HEADS_PER_GROUP

    nh = depth * n_heads
    lam_re, lam_im, bb_re, bb_im = _discretize(
        ssm_log_dt.reshape(nh), ssm_a_re.reshape(nh, n_state), ssm_a_im.reshape(nh, n_state),
        jnp.swapaxes(ssm_b_re, 2, 3).reshape(nh, n_grp_ch, n_state),
        jnp.swapaxes(ssm_b_im, 2, 3).reshape(nh, n_grp_ch, n_state))
    lam_re = lam_re.reshape(depth, groups, 1, HEADS_PER_GROUP * n_state)
    lam_im = lam_im.reshape(depth, groups, 1, HEADS_PER_GROUP * n_state)
    lam = jnp.concatenate([lam_re, lam_im], axis=2).reshape(depth, 2 * groups, -1)
    bb_re = bb_re.reshape(depth, n_heads, n_grp_ch, n_state)
    bb_im = bb_im.reshape(depth, n_heads, n_grp_ch, n_state)

    h = x
    row = lambda v: v.reshape(1, -1)
    out = None
    for i in range(depth):
        bw = jnp.concatenate([_block_diag_groups(bb_re[i], groups),
                              _block_diag_groups(bb_im[i], groups)], axis=2).astype(_BF16)
        cw = jnp.concatenate(
            [_block_diag_groups(jnp.swapaxes(ssm_c_re[i], 1, 2), groups),
             _block_diag_groups(-jnp.swapaxes(ssm_c_im[i], 1, 2), groups)], axis=1).astype(_BF16)
        is_moe = i % 2 == 1
        j = i // 2
        rwt = jnp.transpose(router_w[j]) if is_moe else None
        outs = _mix_layer(
            h, row(norm_mix_g[i]), w_in[i].astype(_BF16), bw, lam[i], cw, row(ssm_d[i]),
            ssm_w_glu[i].astype(_BF16), row(ssm_b_glu[i]), pool_w[i].astype(_BF16),
            row(pool_scale[i]), w_out[i].astype(_BF16), row(norm_ffn_g[i]), rwt)
        if is_moe:
            assert i == depth - 1
            h, hn, logits_t = outs
            out = _moe_layer(h, hn, logits_t, moe_w_gate[j].astype(_BF16),
                             moe_w_up[j].astype(_BF16), moe_w_down[j].astype(_BF16),
                             row(final_norm_g), bsz, seq)
        else:
            h, hn = outs
            h = _ffn(hn, ffn_w_gate[j][None].astype(_BF16), ffn_w_up[j][None].astype(_BF16),
                     ffn_w_down[j][None].astype(_BF16), h)
    return out
```

```python
import functools
import math

import jax
import jax.numpy as jnp
from jax import lax
from jax.experimental import pallas as pl
from jax.experimental.pallas import tpu as pltpu

RMS_EPS = 1e-6
POOL_WINDOWS = (2, 4, 8, 16)
A_RE_MAX = -1e-4
GELU_C0 = math.sqrt(2.0 / math.pi)
GELU_C1 = 0.044715

SUBLANES = 8
LANES = 128
MXU_DIM = 256

HEADS_PER_GROUP = 16
TIME_TILE = 64
SCAN_COLS = 512
FFN_ROW_TILE = 512
FFN_COL_CHUNK = 256
FFN_COL_SPLIT = 2
TOKEN_CHUNK = MXU_DIM
SEG_ALIGN = 2 * SUBLANES
WINDOW_PIECE = LANES
VMEM_LIMIT = 48 * 1024 * 1024

_F32 = jnp.float32
_BF16 = jnp.bfloat16
_I32 = jnp.int32


def _dot(a, b):
    return jnp.dot(a, b, preferred_element_type=_F32)


def _dot_nt(a, b):
    return lax.dot_general(a, b, (((1,), (1,)), ((), ())), preferred_element_type=_F32)


def _rmsnorm(x, g):
    inv = lax.rsqrt(jnp.mean(x * x, axis=-1, keepdims=True) + RMS_EPS)
    return x * inv * g


def _sigmoid(x):
    return 1.0 / (1.0 + jnp.exp(-x))


def _ceil_to(x, m):
    return jnp.floor((x + (m - 1.0)) * (1.0 / m)) * m


def _discretize_kernel(log_dt_ref, a_re_ref, a_im_ref, b_re_ref, b_im_ref,
                       lam_re_ref, lam_im_ref, bb_re_ref, bb_im_ref):
    dt = jnp.exp(log_dt_ref[...])
    ar = jnp.minimum(a_re_ref[...], A_RE_MAX)
    ai = a_im_ref[...]
    mag = jnp.exp(ar * dt)
    lam_re = mag * jnp.cos(ai * dt)
    lam_im = mag * jnp.sin(ai * dt)
    den = ar * ar + ai * ai
    nr = lam_re - 1.0
    ni = lam_im
    coef_re = (nr * ar + ni * ai) / den
    coef_im = (ni * ar - nr * ai) / den
    lam_re_ref[...] = lam_re
    lam_im_ref[...] = lam_im
    br = b_re_ref[...]
    bi = b_im_ref[...]
    bb_re_ref[...] = coef_re * br - coef_im * bi
    bb_im_ref[...] = coef_re * bi + coef_im * br


def _discretize(log_dt, a_re, a_im, b_re_t, b_im_t):
    n, g, p = b_re_t.shape
    full3 = lambda s: pl.BlockSpec(s, lambda: (0, 0, 0))
    return pl.pallas_call(
        _discretize_kernel,
        out_shape=(jax.ShapeDtypeStruct((n, 1, p), _F32),
                   jax.ShapeDtypeStruct((n, 1, p), _F32),
                   jax.ShapeDtypeStruct((n, g, p), _F32),
                   jax.ShapeDtypeStruct((n, g, p), _F32)),
        in_specs=[full3((n, 1, 1)), full3((n, 1, p)), full3((n, 1, p)),
                  full3((n, g, p)), full3((n, g, p))],
        out_specs=(full3((n, 1, p)), full3((n, 1, p)),
                   full3((n, g, p)), full3((n, g, p))),
        name="ssm_discretize",
    )(log_dt.reshape(n, 1, 1), a_re.reshape(n, 1, p), a_im.reshape(n, 1, p),
      b_re_t, b_im_t)


def _mix_kernel(*refs, time_tile, with_router, batch_major_in):
    (h_ref, gmix_ref, win_ref, bw_ref, lam_ref, cw_ref, dskip_ref, wglu_ref,
     bglu_ref, poolw_ref, pscale_ref, wout_ref, gffn_ref) = refs[:13]
    refs = refs[13:]
    rwt_ref = logits_ref = xin_scr = None
    if with_router:
        rwt_ref, hout_ref, hn_ref, logits_ref = refs[:4]
        refs = refs[4:]
    else:
        hout_ref, hn_ref = refs[:2]
        refs = refs[2:]
    s_scr, state_scr, ext_scr, mixed_scr = refs[:4]
    if batch_major_in:
        xin_scr = refs[4]

    step = pl.program_id(0)
    rows = time_tile * SUBLANES
    d_ssm = dskip_ref.shape[1]
    n_groups = bw_ref.shape[0]
    gin = bw_ref.shape[1]
    gstate = bw_ref.shape[2] // 2
    hist = ext_scr.shape[0] - rows

    @pl.when(step == 0)
    def _():
        state_scr[...] = jnp.zeros_like(state_scr)
        ext_scr[0:hist, :] = jnp.zeros((hist, ext_scr.shape[1]), _F32)

    if batch_major_in:
        for bb in range(SUBLANES):
            for k in range(xin_scr.shape[0]):
                xin_scr[k, pl.ds(bb, time_tile, stride=SUBLANES), :] = (
                    h_ref[bb, :, k * LANES:(k + 1) * LANES])
        h = jnp.concatenate([xin_scr[k] for k in range(xin_scr.shape[0])], axis=1)
    else:
        h = h_ref[...]
    hn = _rmsnorm(h, gmix_ref[...]).astype(_BF16)
    proj = _dot(hn, win_ref[...])
    u_ssm = proj[:, :d_ssm]
    ext_scr[hist:, :] = proj[:, d_ssm:]

    for q in range(n_groups):
        ug = u_ssm[:, q * gin:(q + 1) * gin].astype(_BF16)
        s_scr[q] = _dot(ug, bw_ref[q])
        for c in range(gstate // SCAN_COLS):
            re0 = c * SCAN_COLS
            im0 = gstate + c * SCAN_COLS
            lr = jnp.broadcast_to(lam_ref[2 * q:2 * q + 1, re0:re0 + SCAN_COLS],
                                  (SUBLANES, SCAN_COLS))
            li = jnp.broadcast_to(lam_ref[2 * q + 1:2 * q + 2, re0:re0 + SCAN_COLS],
                                  (SUBLANES, SCAN_COLS))
            sre = state_scr[q, :, re0:re0 + SCAN_COLS]
            sim = state_scr[q, :, im0:im0 + SCAN_COLS]
            for t in range(time_tile):
                r0 = t * SUBLANES
                bre = s_scr[q, r0:r0 + SUBLANES, re0:re0 + SCAN_COLS]
                bim = s_scr[q, r0:r0 + SUBLANES, im0:im0 + SCAN_COLS]
                sre, sim = (lr * sre - li * sim + bre, lr * sim + li * sre + bim)
                s_scr[q, r0:r0 + SUBLANES, re0:re0 + SCAN_COLS] = sre
                s_scr[q, r0:r0 + SUBLANES, im0:im0 + SCAN_COLS] = sim
            state_scr[q, :, re0:re0 + SCAN_COLS] = sre
            state_scr[q, :, im0:im0 + SCAN_COLS] = sim
        yq = _dot(s_scr[q].astype(_BF16), cw_ref[q])
        yq = yq + dskip_ref[:, q * gin:(q + 1) * gin] * u_ssm[:, q * gin:(q + 1) * gin]
        hq = 0.5 * yq * (1.0 + jnp.tanh(GELU_C0 * (yq + GELU_C1 * (yq * yq * yq))))
        mixed_scr[:, q * gin:(q + 1) * gin] = hq.astype(_BF16)
    hg = mixed_scr[:, :d_ssm]
    gate = _sigmoid(_dot(hg, wglu_ref[...]) + bglu_ref[...])
    mixed_scr[:, :d_ssm] = (hg.astype(_F32) * gate).astype(_BF16)

    n_ext = rows + hist
    pos = step * time_tile + (lax.broadcasted_iota(_I32, (rows, LANES), 0) // SUBLANES)
    for g, w in enumerate(POOL_WINDOWS):
        c0 = g * LANES
        e = ext_scr[:, c0:c0 + LANES]
        acc = e
        n_acc = n_ext
        span = 1
        while span < w:
            sh = span * SUBLANES
            acc = acc[sh:, :] + acc[:n_acc - sh, :]
            n_acc -= sh
            span *= 2
        wsum = acc[n_acc - rows:, :]
        cnt = jnp.minimum(pos + 1, w).astype(_F32)
        pooled = wsum / cnt - e[hist:, :]
        mg = _dot(pooled.astype(_BF16), poolw_ref[g]) * pscale_ref[:, c0:c0 + LANES]
        mixed_scr[:, d_ssm + c0:d_ssm + c0 + LANES] = mg.astype(_BF16)
    ext_scr[0:hist, :] = ext_scr[rows:rows + hist, :]

    hout = h + _dot(mixed_scr[...], wout_ref[...])
    hout_ref[...] = hout
    hn2 = _rmsnorm(hout, gffn_ref[...])
    hn_ref[...] = hn2.astype(_BF16)
    if with_router:
        logits_ref[...] = lax.dot_general(
            rwt_ref[...], hn2, (((1,), (1,)), ((), ())),
            precision=lax.Precision.HIGHEST, preferred_element_type=_F32)


def _mix_layer(h, gmix, w_in, bw, lam, cw, dskip, wglu, bglu, poolw, pscale, wout,
               gffn, rwt):
    batch_major_in = h.ndim == 3
    d = h.shape[-1]
    t_rows = h.size // d
    rows = TIME_TILE * SUBLANES
    n_steps = t_rows // rows
    d_ssm = dskip.shape[1]
    d_pool = pscale.shape[1]
    hist = max(POOL_WINDOWS) * SUBLANES
    with_router = rwt is not None

    def const(a):
        nd = a.ndim
        return pl.BlockSpec(a.shape, lambda i, nd=nd: (0,) * nd)

    row_blk = lambda width: pl.BlockSpec((rows, width), lambda i: (i, 0))
    ins = [h, gmix, w_in, bw, lam, cw, dskip, wglu, bglu, poolw, pscale, wout, gffn]
    h_spec = (pl.BlockSpec((h.shape[0], TIME_TILE, d), lambda i: (0, i, 0))
              if batch_major_in else row_blk(d))
    in_specs = [h_spec] + [const(a) for a in ins[1:]]
    out_shape = [jax.ShapeDtypeStruct((t_rows, d), _F32),
                 jax.ShapeDtypeStruct((t_rows, d), _BF16)]
    out_specs = [row_blk(d), row_blk(d)]
    if with_router:
        ins.append(rwt)
        in_specs.append(const(rwt))
        out_shape.append(jax.ShapeDtypeStruct((rwt.shape[0], t_rows), _F32))
        out_specs.append(pl.BlockSpec((rwt.shape[0], rows), lambda i: (0, i)))
    scratch = [
        pltpu.VMEM((bw.shape[0], rows, bw.shape[2]), _F32),
        pltpu.VMEM((bw.shape[0], SUBLANES, bw.shape[2]), _F32),
        pltpu.VMEM((rows + hist, d_pool), _F32),
        pltpu.VMEM((rows, d_ssm + d_pool), _BF16),
    ]
    if batch_major_in:
        scratch.append(pltpu.VMEM((d // LANES, rows, LANES), _F32))
    return pl.pallas_call(
        functools.partial(_mix_kernel, time_tile=TIME_TILE, with_router=with_router,
                          batch_major_in=batch_major_in),
        grid=(n_steps,),
        in_specs=in_specs,
        out_specs=out_specs,
        out_shape=out_shape,
        scratch_shapes=scratch,
        compiler_params=pltpu.CompilerParams(
            dimension_semantics=("arbitrary",), vmem_limit_bytes=VMEM_LIMIT),
        name="mix_router" if with_router else "mix",
    )(*ins)


def _swiglu_partial(x, wg_ref, wu_ref, wd_ref, a_scr):
    tf = wg_ref.shape[2]
    c0 = 0
    while c0 < tf:
        cw = min(FFN_COL_CHUNK, tf - c0)
        g = _dot(x, wg_ref[0, :, c0:c0 + cw])
        u = _dot(x, wu_ref[0, :, c0:c0 + cw])
        a_scr[:, c0:c0 + cw] = (g * _sigmoid(g) * u).astype(_BF16)
        c0 += cw
    return _dot(a_scr[...], wd_ref[0])


def _ffn_kernel(x_ref, wg_ref, wu_ref, wd_ref, res_ref, out_ref, a_scr):
    j = pl.program_id(1)

    @pl.when(j == 0)
    def _():
        out_ref[...] = res_ref[...]

    out_ref[...] += _swiglu_partial(x_ref[...], wg_ref, wu_ref, wd_ref, a_scr)


def _ffn(x, wg, wu, wd, res):
    t_rows, d = x.shape
    f = wg.shape[2]
    tm = FFN_ROW_TILE
    nj = FFN_COL_SPLIT
    tf = f // nj
    return pl.pallas_call(
        _ffn_kernel,
        grid=(t_rows // tm, nj),
        in_specs=[
            pl.BlockSpec((tm, d), lambda i, j: (i, 0)),
            pl.BlockSpec((1, d, tf), lambda i, j: (0, 0, j)),
            pl.BlockSpec((1, d, tf), lambda i, j: (0, 0, j)),
            pl.BlockSpec((1, tf, d), lambda i, j: (0, j, 0)),
            pl.BlockSpec((tm, d), lambda i, j: (i, 0)),
        ],
        out_specs=pl.BlockSpec((tm, d), lambda i, j: (i, 0)),
        out_shape=jax.ShapeDtypeStruct((t_rows, d), _F32),
        scratch_shapes=[pltpu.VMEM((tm, tf), _BF16)],
        compiler_params=pltpu.CompilerParams(
            dimension_semantics=("arbitrary", "arbitrary"), vmem_limit_bytes=VMEM_LIMIT),
        name="ffn_dense",
    )(x, wg, wu, wd, res)


def _route_kernel(lt_ref, pos_ref, gates_ref, segtab_ref, blktab_ref, *, n_blk, row_tile):
    l = lt_ref[...]
    n_e, t_rows = l.shape
    ch = TOKEN_CHUNK
    n_ch = t_rows // ch
    ie = lax.broadcasted_iota(_I32, l.shape, 0)
    m1 = jnp.max(l, axis=0, keepdims=True)
    i1 = jnp.min(jnp.where(l == m1, ie, n_e), axis=0, keepdims=True)
    l2 = jnp.where(ie == i1, -jnp.inf, l)
    m2 = jnp.max(l2, axis=0, keepdims=True)
    i2 = jnp.min(jnp.where(l2 == m2, ie, n_e), axis=0, keepdims=True)
    e2 = jnp.exp(m2 - m1)
    den = 1.0 + e2
    gates_ref[0:1, :] = 1.0 / den
    gates_ref[1:2, :] = e2 / den
    sel1 = ie == i1
    sel2 = ie == i2
    chosen = jnp.where(sel1 | sel2, 1.0, 0.0).astype(_BF16)

    chunk_of_t = lax.broadcasted_iota(_I32, (n_ch, t_rows), 1) // ch
    in_chunk = jnp.where(chunk_of_t == lax.broadcasted_iota(_I32, (n_ch, t_rows), 0),
                         1.0, 0.0).astype(_BF16)
    n_ec = _dot_nt(chosen, in_chunk)
    n_ce = _dot_nt(in_chunk, chosen)
    pad_ec = _ceil_to(n_ec, float(SEG_ALIGN))
    pad_ce = _ceil_to(n_ce, float(SEG_ALIGN))
    cc_r = lax.broadcasted_iota(_I32, (n_ch, n_ch), 0)
    cc_c = lax.broadcasted_iota(_I32, (n_ch, n_ch), 1)
    before = jnp.where(cc_r < cc_c, 1.0, 0.0).astype(_BF16)
    after = jnp.where(cc_c < cc_r, 1.0, 0.0).astype(_BF16)
    segoff_ec = _dot(pad_ec.astype(_BF16), before)
    segoff_ce = _dot(after, pad_ce.astype(_BF16))
    win_ec = _ceil_to(jnp.maximum(pad_ec, 1.0), float(WINDOW_PIECE))
    win_ce = _ceil_to(jnp.maximum(pad_ce, 1.0), float(WINDOW_PIECE))
    tot_col = _ceil_to(jnp.max(segoff_ec + win_ec, axis=1, keepdims=True), float(row_tile))
    tot_row = _ceil_to(jnp.max(segoff_ce + win_ce, axis=0, keepdims=True), float(row_tile))
    ee_r = lax.broadcasted_iota(_I32, (n_e, n_e), 0)
    ee_c = lax.broadcasted_iota(_I32, (n_e, n_e), 1)
    base_col = jnp.sum(jnp.where(ee_c < ee_r, tot_row, 0.0), axis=1, keepdims=True)
    w0_ec = base_col + segoff_ec
    end_col = base_col + tot_col
    segtab_ref[0] = w0_ec.astype(_I32)
    segtab_ref[1] = pad_ec.astype(_I32)
    segtab_ref[2] = jnp.broadcast_to(end_col, (n_e, n_ch)).astype(_I32)

    tt_r = lax.broadcasted_iota(_I32, (ch, ch), 0)
    tt_c = lax.broadcasted_iota(_I32, (ch, ch), 1)
    earlier = jnp.where(tt_r < tt_c, 1.0, 0.0).astype(_BF16)
    for c in range(n_ch):
        cols = slice(c * ch, (c + 1) * ch)
        rank = _dot(chosen[:, cols], earlier)
        offs = w0_ec[:, c:c + 1] + rank
        p1 = jnp.sum(jnp.where(sel1[:, cols], offs, 0.0), axis=0, keepdims=True)
        p2 = jnp.sum(jnp.where(sel2[:, cols], offs, 0.0), axis=0, keepdims=True)
        pos_ref[0:1, cols] = p1.astype(_I32)
        pos_ref[1:2, cols] = p2.astype(_I32)

    start = (lax.broadcasted_iota(_I32, (1, n_blk), 1) * row_tile).astype(_F32)
    owner = jnp.sum(jnp.where(end_col <= start, 1.0, 0.0), axis=0, keepdims=True)
    n_valid = jnp.sum(tot_col, axis=0, keepdims=True) * (1.0 / row_tile)
    blktab_ref[0:1, :] = jnp.minimum(owner, n_e - 1.0).astype(_I32)
    blktab_ref[1:2, :] = jnp.broadcast_to(n_valid, (1, n_blk)).astype(_I32)


def _route(logits_t, n_blk, row_tile):
    n_e, t_rows = logits_t.shape
    n_ch = t_rows // TOKEN_CHUNK
    full = lambda s: pl.BlockSpec(s, lambda: (0,) * len(s))
    return pl.pallas_call(
        functools.partial(_route_kernel, n_blk=n_blk, row_tile=row_tile),
        in_specs=[full((n_e, t_rows))],
        out_specs=(full((2, t_rows)), full((2, t_rows)), full((3, n_e, n_ch)), full((2, n_blk))),
        out_shape=(jax.ShapeDtypeStruct((2, t_rows), _I32),
                   jax.ShapeDtypeStruct((2, t_rows), _F32),
                   jax.ShapeDtypeStruct((3, n_e, n_ch), _I32),
                   jax.ShapeDtypeStruct((2, n_blk), _I32)),
        compiler_params=pltpu.CompilerParams(vmem_limit_bytes=VMEM_LIMIT),
        name="route_sort",
    )(logits_t)


def _dispatch_kernel(w0_ref, sl_ref, end_ref, nv_ref, hn_ref, posl_ref, x_hbm,
                     stage, sems, zero_scr, zsems, tsems, *, n_e, n_ch, row_tile, min_blocks):
    c = pl.program_id(0)
    ch = hn_ref.shape[0]
    piece = WINDOW_PIECE
    n_pieces = ch // piece
    slot = c % 2
    n_blocks = x_hbm.shape[0] // row_tile

    def for_each_unused_block(fn):
        for b in range(min_blocks, n_blocks):
            @pl.when(b >= nv_ref[0])
            def _(b=b):
                fn(pltpu.make_async_copy(zero_scr, x_hbm.at[pl.ds(b * row_tile, row_tile)],
                                         tsems.at[b - min_blocks]))

    def piece_copy(cc, sl, e, k):
        row0 = pl.multiple_of(w0_ref[e * n_ch + cc] + k * piece, SEG_ALIGN)
        return pltpu.make_async_copy(
            stage.at[sl, k, pl.ds(e * piece, piece)],
            x_hbm.at[pl.ds(row0, piece)], sems.at[sl, e, k])

    def for_each_piece(cc, sl, fn):
        for e in range(n_e):
            fn(piece_copy(cc, sl, e, 0))
            for k in range(1, n_pieces):
                @pl.when(sl_ref[e * n_ch + cc] > k * piece)
                def _(e=e, k=k):
                    fn(piece_copy(cc, sl, e, k))

    @pl.when(c == 0)
    def _():
        zero_scr[...] = jnp.zeros_like(zero_scr)
        fills = [pltpu.make_async_copy(
            zero_scr,
            x_hbm.at[pl.ds(pl.multiple_of(end_ref[e * n_ch] - row_tile, SEG_ALIGN), row_tile)],
            zsems.at[e]) for e in range(n_e)]
        for cp in fills:
            cp.start()
        for_each_unused_block(lambda cp: cp.start())
        for cp in fills:
            cp.wait()

    pp = posl_ref[0]
    hn = hn_ref[...]

    def gather_piece(k):
        row_j = lax.broadcasted_iota(_I32, (piece, ch), 0) + k * piece
        parts = []
        for e in range(n_e):
            row_id = row_j + w0_ref[e * n_ch + c]
            hit = (pp[0:1, :] == row_id) | (pp[1:2, :] == row_id)
            parts.append(jnp.where(hit, 1.0, 0.0).astype(_BF16))
        onehot = jnp.concatenate(parts, axis=0)
        stage[slot, k] = _dot(onehot, hn).astype(_BF16)

    gather_piece(0)
    for k in range(1, n_pieces):
        reaches = sl_ref[c] > k * piece
        for e in range(1, n_e):
            reaches = jnp.logical_or(reaches, sl_ref[e * n_ch + c] > k * piece)

        @pl.when(reaches)
        def _(k=k):
            gather_piece(k)

    @pl.when(c > 0)
    def _():
        for_each_piece(c - 1, 1 - slot, lambda cp: cp.wait())

    for_each_piece(c, slot, lambda cp: cp.start())

    @pl.when(c == n_ch - 1)
    def _():
        for_each_piece(c, slot, lambda cp: cp.wait())
        for_each_unused_block(lambda cp: cp.wait())


def _dispatch(hn, pos_l, w0_flat, seglen_flat, end_flat, n_valid, n_e, n_rows, row_tile):
    t_rows, d = hn.shape
    ch = TOKEN_CHUNK
    n_ch = t_rows // ch
    n_pieces = ch // WINDOW_PIECE
    min_blocks = 2 * t_rows // row_tile
    return pl.pallas_call(
        functools.partial(_dispatch_kernel, n_e=n_e, n_ch=n_ch, row_tile=row_tile,
                          min_blocks=min_blocks),
        grid_spec=pltpu.PrefetchScalarGridSpec(
            num_scalar_prefetch=4,
            grid=(n_ch,),
            in_specs=[pl.BlockSpec((ch, d), lambda c, *_: (c, 0)),
                      pl.BlockSpec((1, 2, ch), lambda c, *_: (c, 0, 0))],
            out_specs=pl.BlockSpec(memory_space=pl.ANY),
            scratch_shapes=[pltpu.VMEM((2, n_pieces, n_e * WINDOW_PIECE, d), _BF16),
                            pltpu.SemaphoreType.DMA((2, n_e, n_pieces)),
                            pltpu.VMEM((row_tile, d), _BF16),
                            pltpu.SemaphoreType.DMA((n_e,)),
                            pltpu.SemaphoreType.DMA((n_rows // row_tile - min_blocks,))]),
        out_shape=jax.ShapeDtypeStruct((n_rows, d), _BF16),
        compiler_params=pltpu.CompilerParams(
            dimension_semantics=("arbitrary",), vmem_limit_bytes=VMEM_LIMIT),
        name="moe_dispatch",
    )(w0_flat, seglen_flat, end_flat, n_valid, hn, pos_l)


def _ffn_grouped_kernel(be_ref, nv_ref, x_ref, wg_ref, wu_ref, wd_ref, y_ref, a_scr, acc_scr):
    b = pl.program_id(0)
    j = pl.program_id(1)
    last = j == pl.num_programs(1) - 1
    valid = b < nv_ref[0]

    @pl.when(valid)
    def _():
        y = _swiglu_partial(x_ref[...], wg_ref, wu_ref, wd_ref, a_scr)

        @pl.when(j == 0)
        def _():
            acc_scr[...] = y

        @pl.when(j > 0)
        def _():
            acc_scr[...] += y

        @pl.when(last)
        def _():
            y_ref[...] = acc_scr[...].astype(_BF16)

    @pl.when(jnp.logical_and(jnp.logical_not(valid), last))
    def _():
        y_ref[...] = jnp.zeros_like(y_ref)


def _ffn_grouped(x_sorted, wg, wu, wd, block_expert, n_valid, row_tile):
    rows, d = x_sorted.shape
    f = wg.shape[2]
    nj = FFN_COL_SPLIT
    tf = f // nj
    n_blocks = rows // row_tile

    def w_idx(b, j, be, nv):
        ok = b < nv[0]
        e = be[jnp.minimum(b, nv[0] - 1)]
        return e, jnp.where(ok, j, nj - 1)

    def wcol(b, j, be, nv):
        e, jj = w_idx(b, j, be, nv)
        return (e, 0, jj)

    def wrow(b, j, be, nv):
        e, jj = w_idx(b, j, be, nv)
        return (e, jj, 0)

    return pl.pallas_call(
        _ffn_grouped_kernel,
        grid_spec=pltpu.PrefetchScalarGridSpec(
            num_scalar_prefetch=2,
            grid=(n_blocks, nj),
            in_specs=[pl.BlockSpec((row_tile, d),
                                   lambda b, j, be, nv: (jnp.minimum(b, nv[0] - 1), 0)),
                      pl.BlockSpec((1, d, tf), wcol),
                      pl.BlockSpec((1, d, tf), wcol),
                      pl.BlockSpec((1, tf, d), wrow)],
            out_specs=pl.BlockSpec((row_tile, d), lambda b, j, be, nv: (b, 0)),
            scratch_shapes=[pltpu.VMEM((row_tile, tf), _BF16),
                            pltpu.VMEM((row_tile, d), _F32)]),
        out_shape=jax.ShapeDtypeStruct((rows, d), _BF16),
        compiler_params=pltpu.CompilerParams(
            dimension_semantics=("arbitrary", "arbitrary"), vmem_limit_bytes=VMEM_LIMIT),
        name="moe_experts",
    )(block_expert, n_valid, x_sorted, wg, wu, wd)


def _combine_kernel(w0_ref, sl_ref, res_ref, pos_ref, gate_ref, gfin_ref, y_hbm,
                    out_ref, ybuf, sems, tmp_scr, acc_scr, *, n_e, n_ch):
    c = pl.program_id(0)
    ch = res_ref.shape[0]
    piece = WINDOW_PIECE
    n_pieces = ch // piece

    def piece_copy(cc, slot, e, k):
        row0 = pl.multiple_of(w0_ref[e * n_ch + cc] + k * piece, SEG_ALIGN)
        return pltpu.make_async_copy(
            y_hbm.at[pl.ds(row0, piece)],
            ybuf.at[slot, k, pl.ds(e * piece, piece)], sems.at[slot, e, k])

    def for_each_piece(cc, slot, fn):
        for e in range(n_e):
            fn(piece_copy(cc, slot, e, 0))
            for k in range(1, n_pieces):
                @pl.when(sl_ref[e * n_ch + cc] > k * piece)
                def _(e=e, k=k):
                    fn(piece_copy(cc, slot, e, k))

    @pl.when(c == 0)
    def _():
        ybuf[...] = jnp.zeros_like(ybuf)
        for_each_piece(0, 0, lambda cp: cp.start())

    @pl.when(c + 1 < n_ch)
    def _():
        for_each_piece(c + 1, (c + 1) % 2, lambda cp: cp.start())

    slot = c % 2
    for_each_piece(c, slot, lambda cp: cp.wait())

    per_dot = MXU_DIM // piece
    depth = per_dot * piece
    lane = lax.broadcasted_iota(_I32, (1, depth), 1)
    p1b = jnp.broadcast_to(pos_ref[:, 0:1], (ch, depth))
    p2b = jnp.broadcast_to(pos_ref[:, 1:2], (ch, depth))
    g1b = jnp.broadcast_to(gate_ref[:, 0:1], (ch, depth))
    g2b = jnp.broadcast_to(gate_ref[:, 1:2], (ch, depth))

    def scatter_back(m, k):
        tgt = jnp.full((1, depth), -1, _I32)
        for i in range(per_dot):
            e = m * per_dot + i
            row = lane - i * piece + k * piece
            mine = (lane >= i * piece) & (lane < (i + 1) * piece) & (row < sl_ref[e * n_ch + c])
            tgt = jnp.where(mine, row + w0_ref[e * n_ch + c], tgt)
        q = (jnp.where(p1b == tgt, g1b, 0.0) + jnp.where(p2b == tgt, g2b, 0.0)).astype(_BF16)
        return _dot(q, ybuf[slot, k, m * depth:(m + 1) * depth, :])

    def any_reaches(m, k):
        hit = sl_ref[(m * per_dot) * n_ch + c] > k * piece
        for i in range(1, per_dot):
            hit = jnp.logical_or(hit, sl_ref[(m * per_dot + i) * n_ch + c] > k * piece)
        return hit

    acc = res_ref[...]
    for m in range(n_e // per_dot):
        acc = acc + scatter_back(m, 0)
    acc_scr[...] = acc
    for m in range(n_e // per_dot):
        for k in range(1, n_pieces):
            @pl.when(any_reaches(m, k))
            def _(m=m, k=k):
                acc_scr[...] += scatter_back(m, k)
    normed = _rmsnorm(acc_scr[...], gfin_ref[...])
    steps = ch // SUBLANES
    for k in range(tmp_scr.shape[0]):
        tmp_scr[k] = normed[:, k * LANES:(k + 1) * LANES]
    for bb in range(SUBLANES):
        for k in range(tmp_scr.shape[0]):
            out_ref[bb, :, k * LANES:(k + 1) * LANES] = (
                tmp_scr[k, pl.ds(bb, steps, stride=SUBLANES), :])


def _combine(res, pos_c, gates_c, gfin, y_sorted, w0_flat, seglen_flat, n_e, bsz, seq):
    t_rows, d = res.shape
    ch = TOKEN_CHUNK
    n_ch = t_rows // ch
    steps = ch // bsz
    return pl.pallas_call(
        functools.partial(_combine_kernel, n_e=n_e, n_ch=n_ch),
        grid_spec=pltpu.PrefetchScalarGridSpec(
            num_scalar_prefetch=2,
            grid=(n_ch,),
            in_specs=[pl.BlockSpec((ch, d), lambda c, w0, sl: (c, 0)),
                      pl.BlockSpec((ch, 2), lambda c, w0, sl: (c, 0)),
                      pl.BlockSpec((ch, 2), lambda c, w0, sl: (c, 0)),
                      pl.BlockSpec((1, d), lambda c, w0, sl: (0, 0)),
                      pl.BlockSpec(memory_space=pl.ANY)],
            out_specs=pl.BlockSpec((bsz, steps, d), lambda c, w0, sl: (0, c, 0)),
            scratch_shapes=[pltpu.VMEM((2, ch // WINDOW_PIECE, n_e * WINDOW_PIECE, d), _BF16),
                            pltpu.SemaphoreType.DMA((2, n_e, ch // WINDOW_PIECE)),
                            pltpu.VMEM((d // LANES, ch, LANES), _F32),
                            pltpu.VMEM((ch, d), _F32)]),
        out_shape=jax.ShapeDtypeStruct((bsz, seq, d), _F32),
        compiler_params=pltpu.CompilerParams(
            dimension_semantics=("arbitrary",), vmem_limit_bytes=VMEM_LIMIT),
        name="moe_combine",
    )(w0_flat, seglen_flat, res, pos_c, gates_c, gfin, y_sorted)


def _moe_layer(h, hn, logits_t, wg, wu, wd, gfin, bsz, seq):
    t_rows, d = h.shape
    n_e = logits_t.shape[0]
    tm = FFN_ROW_TILE
    n_ch = t_rows // TOKEN_CHUNK
    max_rows = 2 * t_rows + (SEG_ALIGN - 1) * n_e * n_ch + n_e * (TOKEN_CHUNK + tm)
    n_blocks = -(-max_rows // tm) + 1
    n_blk_pad = -(-n_blocks // LANES) * LANES

    pos, gates, segtab, blktab = _route(logits_t, n_blk_pad, tm)
    pos_l = jnp.transpose(pos.reshape(2, n_ch, TOKEN_CHUNK), (1, 0, 2))
    w0_flat = segtab[0].reshape(-1)
    seglen_flat = segtab[1].reshape(-1)
    n_valid = blktab[1, :1]
    x_sorted = _dispatch(hn, pos_l, w0_flat, seglen_flat, segtab[2].reshape(-1), n_valid,
                         n_e, n_blocks * tm, tm)
    y_sorted = _ffn_grouped(x_sorted, wg, wu, wd, blktab[0], n_valid, tm)
    return _combine(h, jnp.transpose(pos), jnp.transpose(gates), gfin, y_sorted,
                    w0_flat, seglen_flat, n_e, bsz, seq)


def _block_diag_groups(w, groups):
    h, k, n = w.shape
    hpg = h // groups
    eye = jnp.eye(hpg, dtype=w.dtype)
    wq = w.reshape(groups, hpg, k, n)
    out = jnp.einsum("qhkn,hj->qhkjn", wq, eye)
    return out.reshape(groups, hpg * k, hpg * n)


def kernel(x, norm_mix_g, w_in, ssm_log_dt, ssm_a_re, ssm_a_im, ssm_b_re, ssm_b_im, ssm_c_re, ssm_c_im, ssm_d, ssm_w_glu, ssm_b_glu, pool_w, pool_scale, w_out, norm_ffn_g, ffn_w_gate, ffn_w_up, ffn_w_down, router_w, moe_w_gate, moe_w_up, moe_w_down, final_norm_g):
    bsz, seq, d = x.shape
    depth, n_heads, n_state, n_grp_ch = ssm_b_re.shape
    t_rows = bsz * seq
    assert bsz == SUBLANES and depth % 2 == 0
    groups = n_heads // HEADS_PER_GROUP

    nh = depth * n_heads
    lam_re, lam_im, bb_re, bb_im = _discretize(
        ssm_log_dt.reshape(nh), ssm_a_re.reshape(nh, n_state), ssm_a_im.reshape(nh, n_state),
        jnp.swapaxes(ssm_b_re, 2, 3).reshape(nh, n_grp_ch, n_state),
        jnp.swapaxes(ssm_b_im, 2, 3).reshape(nh, n_grp_ch, n_state))
    lam_re = lam_re.reshape(depth, groups, 1, HEADS_PER_GROUP * n_state)
    lam_im = lam_im.reshape(depth, groups, 1, HEADS_PER_GROUP * n_state)
    lam = jnp.concatenate([lam_re, lam_im], axis=2).reshape(depth, 2 * groups, -1)
    bb_re = bb_re.reshape(depth, n_heads, n_grp_ch, n_state)
    bb_im = bb_im.reshape(depth, n_heads, n_grp_ch, n_state)

    h = x
    row = lambda v: v.reshape(1, -1)
    out = None
    for i in range(depth):
        bw = jnp.concatenate([_block_diag_groups(bb_re[i], groups),
                              _block_diag_groups(bb_im[i], groups)], axis=2).astype(_BF16)
        cw = jnp.concatenate(
            [_block_diag_groups(jnp.swapaxes(ssm_c_re[i], 1, 2), groups),
             _block_diag_groups(-jnp.swapaxes(ssm_c_im[i], 1, 2), groups)], axis=1).astype(_BF16)
        is_moe = i % 2 == 1
        j = i // 2
        rwt = jnp.transpose(router_w[j]) if is_moe else None
        outs = _mix_layer(
            h, row(norm_mix_g[i]), w_in[i].astype(_BF16), bw, lam[i], cw, row(ssm_d[i]),
            ssm_w_glu[i].astype(_BF16), row(ssm_b_glu[i]), pool_w[i].astype(_BF16),
            row(pool_scale[i]), w_out[i].astype(_BF16), row(norm_ffn_g[i]), rwt)
        if is_moe:
            assert i == depth - 1
            h, hn, logits_t = outs
            out = _moe_layer(h, hn, logits_t, moe_w_gate[j].astype(_BF16),
                             moe_w_up[j].astype(_BF16), moe_w_down[j].astype(_BF16),
                             row(final_norm_g), bsz, seq)
        else:
            h, hn = outs
            h = _ffn(hn, ffn_w_gate[j][None].astype(_BF16), ffn_w_up[j][None].astype(_BF16),
                     ffn_w_down[j][None].astype(_BF16), h)
    return out
```

```python
import functools
import math

import jax
import jax.numpy as jnp
from jax import lax
from jax.experimental import pallas as pl
from jax.experimental.pallas import tpu as pltpu

RMS_EPS = 1e-6
POOL_WINDOWS = (2, 4, 8, 16)
A_RE_MAX = -1e-4
GELU_C0 = math.sqrt(2.0 / math.pi)
GELU_C1 = 0.044715

SUBLANES = 8
LANES = 128
MXU_DIM = 256

HEADS_PER_GROUP = 16
TIME_TILE = 128
SCAN_COLS = 512
FFN_ROW_TILE = 512
FFN_COL_CHUNK = 256
TOKEN_CHUNK = MXU_DIM
SEG_ALIGN = 2 * SUBLANES
WINDOW_PIECE = LANES
VMEM_LIMIT = 56 * 1024 * 1024

_F32 = jnp.float32
_BF16 = jnp.bfloat16
_I32 = jnp.int32


def _dot(a, b):
    return jnp.dot(a, b, preferred_element_type=_F32)


def _dot_nt(a, b):
    return lax.dot_general(a, b, (((1,), (1,)), ((), ())), preferred_element_type=_F32)


def _rmsnorm(x, g):
    inv = lax.rsqrt(jnp.mean(x * x, axis=-1, keepdims=True) + RMS_EPS)
    return x * inv * g


def _sigmoid(x):
    return 1.0 / (1.0 + jnp.exp(-x))


def _ceil_to(x, m):
    return jnp.floor((x + (m - 1.0)) * (1.0 / m)) * m


def _discretize_kernel(log_dt_ref, a_re_ref, a_im_ref, b_re_ref, b_im_ref,
                       lam_re_ref, lam_im_ref, bb_re_ref, bb_im_ref):
    dt = jnp.exp(log_dt_ref[...])
    ar = jnp.minimum(a_re_ref[...], A_RE_MAX)
    ai = a_im_ref[...]
    mag = jnp.exp(ar * dt)
    lam_re = mag * jnp.cos(ai * dt)
    lam_im = mag * jnp.sin(ai * dt)
    den = ar * ar + ai * ai
    nr = lam_re - 1.0
    ni = lam_im
    coef_re = (nr * ar + ni * ai) / den
    coef_im = (ni * ar - nr * ai) / den
    lam_re_ref[...] = lam_re
    lam_im_ref[...] = lam_im
    br = b_re_ref[...]
    bi = b_im_ref[...]
    bb_re_ref[...] = coef_re * br - coef_im * bi
    bb_im_ref[...] = coef_re * bi + coef_im * br


def _discretize(log_dt, a_re, a_im, b_re_t, b_im_t):
    n, g, p = b_re_t.shape
    full3 = lambda s: pl.BlockSpec(s, lambda: (0, 0, 0))
    return pl.pallas_call(
        _discretize_kernel,
        out_shape=(jax.ShapeDtypeStruct((n, 1, p), _F32),
                   jax.ShapeDtypeStruct((n, 1, p), _F32),
                   jax.ShapeDtypeStruct((n, g, p), _F32),
                   jax.ShapeDtypeStruct((n, g, p), _F32)),
        in_specs=[full3((n, 1, 1)), full3((n, 1, p)), full3((n, 1, p)),
                  full3((n, g, p)), full3((n, g, p))],
        out_specs=(full3((n, 1, p)), full3((n, 1, p)),
                   full3((n, g, p)), full3((n, g, p))),
        name="ssm_discretize",
    )(log_dt.reshape(n, 1, 1), a_re.reshape(n, 1, p), a_im.reshape(n, 1, p),
      b_re_t, b_im_t)


def _mix_kernel(*refs, time_tile, with_router, batch_major_in):
    (h_ref, gmix_ref, win_ref, bw_ref, lam_ref, cw_ref, dskip_ref, wglu_ref,
     bglu_ref, poolw_ref, pscale_ref, wout_ref, gffn_ref) = refs[:13]
    refs = refs[13:]
    rw_ref = logits_ref = xin_scr = None
    if with_router:
        rw_ref, hout_ref, hn_ref, logits_ref = refs[:4]
        refs = refs[4:]
    else:
        hout_ref, hn_ref = refs[:2]
        refs = refs[2:]
    s_scr, state_scr, ext_scr, mixed_scr = refs[:4]
    if batch_major_in:
        xin_scr = refs[4]

    step = pl.program_id(0)
    rows = time_tile * SUBLANES
    d_ssm = dskip_ref.shape[1]
    n_groups = bw_ref.shape[0]
    gin = bw_ref.shape[1]
    gstate = bw_ref.shape[2] // 2
    hist = ext_scr.shape[0] - rows

    @pl.when(step == 0)
    def _():
        state_scr[...] = jnp.zeros_like(state_scr)
        ext_scr[0:hist, :] = jnp.zeros((hist, ext_scr.shape[1]), _F32)

    if batch_major_in:
        for bb in range(SUBLANES):
            for k in range(xin_scr.shape[0]):
                xin_scr[k, pl.ds(bb, time_tile, stride=SUBLANES), :] = (
                    h_ref[bb, :, k * LANES:(k + 1) * LANES])
        h = jnp.concatenate([xin_scr[k] for k in range(xin_scr.shape[0])], axis=1)
    else:
        h = h_ref[...]
    hn = _rmsnorm(h, gmix_ref[...]).astype(_BF16)
    proj = _dot(hn, win_ref[...])
    u_ssm = proj[:, :d_ssm]
    ext_scr[hist:, :] = proj[:, d_ssm:]

    for q in range(n_groups):
        ug = u_ssm[:, q * gin:(q + 1) * gin].astype(_BF16)
        s_scr[q] = _dot(ug, bw_ref[q])
        for c in range(gstate // SCAN_COLS):
            re0 = c * SCAN_COLS
            im0 = gstate + c * SCAN_COLS
            lr = jnp.broadcast_to(lam_ref[2 * q:2 * q + 1, re0:re0 + SCAN_COLS],
                                  (SUBLANES, SCAN_COLS))
            li = jnp.broadcast_to(lam_ref[2 * q + 1:2 * q + 2, re0:re0 + SCAN_COLS],
                                  (SUBLANES, SCAN_COLS))
            sre = state_scr[q, :, re0:re0 + SCAN_COLS]
            sim = state_scr[q, :, im0:im0 + SCAN_COLS]
            for t in range(time_tile):
                r0 = t * SUBLANES
                bre = s_scr[q, r0:r0 + SUBLANES, re0:re0 + SCAN_COLS]
                bim = s_scr[q, r0:r0 + SUBLANES, im0:im0 + SCAN_COLS]
                sre, sim = (lr * sre - li * sim + bre, lr * sim + li * sre + bim)
                s_scr[q, r0:r0 + SUBLANES, re0:re0 + SCAN_COLS] = sre
                s_scr[q, r0:r0 + SUBLANES, im0:im0 + SCAN_COLS] = sim
            state_scr[q, :, re0:re0 + SCAN_COLS] = sre
            state_scr[q, :, im0:im0 + SCAN_COLS] = sim
        yq = _dot(s_scr[q].astype(_BF16), cw_ref[q])
        yq = yq + dskip_ref[:, q * gin:(q + 1) * gin] * u_ssm[:, q * gin:(q + 1) * gin]
        hq = 0.5 * yq * (1.0 + jnp.tanh(GELU_C0 * (yq + GELU_C1 * (yq * yq * yq))))
        mixed_scr[:, q * gin:(q + 1) * gin] = hq.astype(_BF16)
    hg = mixed_scr[:, :d_ssm]
    gate = _sigmoid(_dot(hg, wglu_ref[...]) + bglu_ref[...])
    mixed_scr[:, :d_ssm] = (hg.astype(_F32) * gate).astype(_BF16)

    n_ext = rows + hist
    pos = step * time_tile + (lax.broadcasted_iota(_I32, (rows, LANES), 0) // SUBLANES)
    for g, w in enumerate(POOL_WINDOWS):
        c0 = g * LANES
        e = ext_scr[:, c0:c0 + LANES]
        acc = e
        n_acc = n_ext
        span = 1
        while span < w:
            sh = span * SUBLANES
            acc = acc[sh:, :] + acc[:n_acc - sh, :]
            n_acc -= sh
            span *= 2
        wsum = acc[n_acc - rows:, :]
        cnt = jnp.minimum(pos + 1, w).astype(_F32)
        pooled = wsum / cnt - e[hist:, :]
        mg = _dot(pooled.astype(_BF16), poolw_ref[g]) * pscale_ref[:, c0:c0 + LANES]
        mixed_scr[:, d_ssm + c0:d_ssm + c0 + LANES] = mg.astype(_BF16)
    ext_scr[0:hist, :] = ext_scr[rows:rows + hist, :]

    hout = h + _dot(mixed_scr[...], wout_ref[...])
    hout_ref[...] = hout
    hn2 = _rmsnorm(hout, gffn_ref[...])
    hn_hi = hn2.astype(_BF16)
    hn_ref[...] = hn_hi
    if with_router:
        hn_lo = (hn2 - hn_hi.astype(_F32)).astype(_BF16)
        rw = rw_ref[...]
        rw_hi = rw.astype(_BF16)
        rw_lo = (rw - rw_hi.astype(_F32)).astype(_BF16)
        logits_ref[...] = (_dot_nt(rw_hi, hn_hi) + _dot_nt(rw_hi, hn_lo)
                           + _dot_nt(rw_lo, hn_hi))


def _mix_layer(h, gmix, w_in, bw, lam, cw, dskip, wglu, bglu, poolw, pscale, wout,
               gffn, router_w):
    batch_major_in = h.ndim == 3
    d = h.shape[-1]
    t_rows = h.size // d
    rows = TIME_TILE * SUBLANES
    n_steps = t_rows // rows
    d_ssm = dskip.shape[1]
    d_pool = pscale.shape[1]
    hist = max(POOL_WINDOWS) * SUBLANES
    with_router = router_w is not None

    def const(a):
        nd = a.ndim
        return pl.BlockSpec(a.shape, lambda i, nd=nd: (0,) * nd)

    row_blk = lambda width: pl.BlockSpec((rows, width), lambda i: (i, 0))
    ins = [h, gmix, w_in, bw, lam, cw, dskip, wglu, bglu, poolw, pscale, wout, gffn]
    h_spec = (pl.BlockSpec((h.shape[0], TIME_TILE, d), lambda i: (0, i, 0))
              if batch_major_in else row_blk(d))
    in_specs = [h_spec] + [const(a) for a in ins[1:]]
    out_shape = [jax.ShapeDtypeStruct((t_rows, d), _F32),
                 jax.ShapeDtypeStruct((t_rows, d), _BF16)]
    out_specs = [row_blk(d), row_blk(d)]
    if with_router:
        n_e = router_w.shape[1]
        rw = jnp.transpose(router_w)
        ins.append(rw)
        in_specs.append(const(rw))
        out_shape.append(jax.ShapeDtypeStruct((n_e, t_rows), _F32))
        out_specs.append(pl.BlockSpec((n_e, rows), lambda i: (0, i)))
    scratch = [
        pltpu.VMEM((bw.shape[0], rows, bw.shape[2]), _F32),
        pltpu.VMEM((bw.shape[0], SUBLANES, bw.shape[2]), _F32),
        pltpu.VMEM((rows + hist, d_pool), _F32),
        pltpu.VMEM((rows, d_ssm + d_pool), _BF16),
    ]
    if batch_major_in:
        scratch.append(pltpu.VMEM((d // LANES, rows, LANES), _F32))
    return pl.pallas_call(
        functools.partial(_mix_kernel, time_tile=TIME_TILE, with_router=with_router,
                          batch_major_in=batch_major_in),
        grid=(n_steps,),
        in_specs=in_specs,
        out_specs=out_specs,
        out_shape=out_shape,
        scratch_shapes=scratch,
        compiler_params=pltpu.CompilerParams(
            dimension_semantics=("arbitrary",), vmem_limit_bytes=VMEM_LIMIT),
        name="mix_router" if with_router else "mix",
    )(*ins)


def _swiglu_partial(x, wg_ref, wu_ref, wd_ref, a_scr):
    tf = wg_ref.shape[2]
    c0 = 0
    while c0 < tf:
        cw = min(FFN_COL_CHUNK, tf - c0)
        g = _dot(x, wg_ref[0, :, c0:c0 + cw])
        u = _dot(x, wu_ref[0, :, c0:c0 + cw])
        a_scr[:, c0:c0 + cw] = (g * _sigmoid(g) * u).astype(_BF16)
        c0 += cw
    return _dot(a_scr[...], wd_ref[0])


def _ffn_kernel(x_ref, wg_ref, wu_ref, wd_ref, res_ref, out_ref, a_scr):
    out_ref[...] = res_ref[...] + _swiglu_partial(x_ref[...], wg_ref, wu_ref, wd_ref, a_scr)


def _ffn(x, wg, wu, wd, res):
    t_rows, d = x.shape
    f = wg.shape[2]
    tm = FFN_ROW_TILE
    held = lambda shape: pl.BlockSpec(shape, lambda i: (0, 0, 0), pipeline_mode=pl.Buffered(1))
    return pl.pallas_call(
        _ffn_kernel,
        grid=(t_rows // tm,),
        in_specs=[
            pl.BlockSpec((tm, d), lambda i: (i, 0)),
            held((1, d, f)), held((1, d, f)), held((1, f, d)),
            pl.BlockSpec((tm, d), lambda i: (i, 0)),
        ],
        out_specs=pl.BlockSpec((tm, d), lambda i: (i, 0)),
        out_shape=jax.ShapeDtypeStruct((t_rows, d), _F32),
        scratch_shapes=[pltpu.VMEM((tm, f), _BF16)],
        compiler_params=pltpu.CompilerParams(
            dimension_semantics=("arbitrary",), vmem_limit_bytes=VMEM_LIMIT),
        name="ffn_dense",
    )(x, wg, wu, wd, res)


def _route_kernel(lt_ref, pos_ref, gates_ref, segtab_ref, blktab_ref, *, n_blk, row_tile):
    l = lt_ref[...]
    n_e, t_rows = l.shape
    ch = TOKEN_CHUNK
    n_ch = t_rows // ch
    ie = lax.broadcasted_iota(_I32, l.shape, 0)
    m1 = jnp.max(l, axis=0, keepdims=True)
    i1 = jnp.min(jnp.where(l == m1, ie, n_e), axis=0, keepdims=True)
    l2 = jnp.where(ie == i1, -jnp.inf, l)
    m2 = jnp.max(l2, axis=0, keepdims=True)
    i2 = jnp.min(jnp.where(l2 == m2, ie, n_e), axis=0, keepdims=True)
    e2 = jnp.exp(m2 - m1)
    den = 1.0 + e2
    gates_ref[0:1, :] = 1.0 / den
    gates_ref[1:2, :] = e2 / den
    sel1 = ie == i1
    sel2 = ie == i2
    chosen = jnp.where(sel1 | sel2, 1.0, 0.0).astype(_BF16)

    chunk_of_t = lax.broadcasted_iota(_I32, (n_ch, t_rows), 1) // ch
    in_chunk = jnp.where(chunk_of_t == lax.broadcasted_iota(_I32, (n_ch, t_rows), 0),
                         1.0, 0.0).astype(_BF16)
    n_ec = _dot_nt(chosen, in_chunk)
    n_ce = _dot_nt(in_chunk, chosen)
    pad_ec = _ceil_to(n_ec, float(SEG_ALIGN))
    pad_ce = _ceil_to(n_ce, float(SEG_ALIGN))
    cc_r = lax.broadcasted_iota(_I32, (n_ch, n_ch), 0)
    cc_c = lax.broadcasted_iota(_I32, (n_ch, n_ch), 1)
    before = jnp.where(cc_r < cc_c, 1.0, 0.0).astype(_BF16)
    after = jnp.where(cc_c < cc_r, 1.0, 0.0).astype(_BF16)
    segoff_ec = _dot(pad_ec.astype(_BF16), before)
    segoff_ce = _dot(after, pad_ce.astype(_BF16))
    win_ec = _ceil_to(jnp.maximum(pad_ec, 1.0), float(WINDOW_PIECE))
    win_ce = _ceil_to(jnp.maximum(pad_ce, 1.0), float(WINDOW_PIECE))
    tot_col = _ceil_to(jnp.max(segoff_ec + win_ec, axis=1, keepdims=True), float(row_tile))
    tot_row = _ceil_to(jnp.max(segoff_ce + win_ce, axis=0, keepdims=True), float(row_tile))
    ee_r = lax.broadcasted_iota(_I32, (n_e, n_e), 0)
    ee_c = lax.broadcasted_iota(_I32, (n_e, n_e), 1)
    base_col = jnp.sum(jnp.where(ee_c < ee_r, tot_row, 0.0), axis=1, keepdims=True)
    w0_ec = base_col + segoff_ec
    end_col = base_col + tot_col
    segtab_ref[0] = w0_ec.astype(_I32)
    segtab_ref[1] = pad_ec.astype(_I32)
    segtab_ref[2] = jnp.broadcast_to(end_col, (n_e, n_ch)).astype(_I32)

    tt_r = lax.broadcasted_iota(_I32, (ch, ch), 0)
    tt_c = lax.broadcasted_iota(_I32, (ch, ch), 1)
    earlier = jnp.where(tt_r < tt_c, 1.0, 0.0).astype(_BF16)
    for c in range(n_ch):
        cols = slice(c * ch, (c + 1) * ch)
        rank = _dot(chosen[:, cols], earlier)
        offs = w0_ec[:, c:c + 1] + rank
        p1 = jnp.sum(jnp.where(sel1[:, cols], offs, 0.0), axis=0, keepdims=True)
        p2 = jnp.sum(jnp.where(sel2[:, cols], offs, 0.0), axis=0, keepdims=True)
        pos_ref[0:1, cols] = p1.astype(_I32)
        pos_ref[1:2, cols] = p2.astype(_I32)

    start = (lax.broadcasted_iota(_I32, (1, n_blk), 1) * row_tile).astype(_F32)
    owner = jnp.sum(jnp.where(end_col <= start, 1.0, 0.0), axis=0, keepdims=True)
    n_valid = jnp.sum(tot_col, axis=0, keepdims=True) * (1.0 / row_tile)
    blktab_ref[0:1, :] = jnp.minimum(owner, n_e - 1.0).astype(_I32)
    blktab_ref[1:2, :] = jnp.broadcast_to(n_valid, (1, n_blk)).astype(_I32)


def _route(logits_t, n_blk, row_tile):
    n_e, t_rows = logits_t.shape
    n_ch = t_rows // TOKEN_CHUNK
    full = lambda s: pl.BlockSpec(s, lambda: (0,) * len(s))
    return pl.pallas_call(
        functools.partial(_route_kernel, n_blk=n_blk, row_tile=row_tile),
        in_specs=[full((n_e, t_rows))],
        out_specs=(full((2, t_rows)), full((2, t_rows)), full((3, n_e, n_ch)), full((2, n_blk))),
        out_shape=(jax.ShapeDtypeStruct((2, t_rows), _I32),
                   jax.ShapeDtypeStruct((2, t_rows), _F32),
                   jax.ShapeDtypeStruct((3, n_e, n_ch), _I32),
                   jax.ShapeDtypeStruct((2, n_blk), _I32)),
        compiler_params=pltpu.CompilerParams(vmem_limit_bytes=VMEM_LIMIT),
        name="route_sort",
    )(logits_t)


def _dispatch_kernel(w0_ref, sl_ref, end_ref, nv_ref, hn_ref, posl_ref, x_hbm,
                     stage, sems, zero_scr, zsems, tsems, *, n_e, n_ch, row_tile, min_blocks):
    c = pl.program_id(0)
    ch = hn_ref.shape[0]
    piece = WINDOW_PIECE
    n_pieces = ch // piece
    slot = c % 2
    n_blocks = x_hbm.shape[0] // row_tile

    def for_each_unused_block(fn):
        for b in range(min_blocks, n_blocks):
            @pl.when(b >= nv_ref[0])
            def _(b=b):
                fn(pltpu.make_async_copy(zero_scr, x_hbm.at[pl.ds(b * row_tile, row_tile)],
                                         tsems.at[b - min_blocks]))

    def piece_copy(cc, sl, e, k):
        row0 = pl.multiple_of(w0_ref[e * n_ch + cc] + k * piece, SEG_ALIGN)
        return pltpu.make_async_copy(
            stage.at[sl, k, pl.ds(e * piece, piece)],
            x_hbm.at[pl.ds(row0, piece)], sems.at[sl, e, k])

    def for_each_piece(cc, sl, fn):
        for e in range(n_e):
            fn(piece_copy(cc, sl, e, 0))
            for k in range(1, n_pieces):
                @pl.when(sl_ref[e * n_ch + cc] > k * piece)
                def _(e=e, k=k):
                    fn(piece_copy(cc, sl, e, k))

    @pl.when(c == 0)
    def _():
        zero_scr[...] = jnp.zeros_like(zero_scr)
        fills = [pltpu.make_async_copy(
            zero_scr,
            x_hbm.at[pl.ds(pl.multiple_of(end_ref[e * n_ch] - row_tile, SEG_ALIGN), row_tile)],
            zsems.at[e]) for e in range(n_e)]
        for cp in fills:
            cp.start()
        for_each_unused_block(lambda cp: cp.start())
        for cp in fills:
            cp.wait()

    pp = posl_ref[0]
    hn = hn_ref[...]

    def gather_piece(k):
        row_j = lax.broadcasted_iota(_I32, (piece, ch), 0) + k * piece
        parts = []
        for e in range(n_e):
            row_id = row_j + w0_ref[e * n_ch + c]
            hit = (pp[0:1, :] == row_id) | (pp[1:2, :] == row_id)
            parts.append(jnp.where(hit, 1.0, 0.0).astype(_BF16))
        onehot = jnp.concatenate(parts, axis=0)
        stage[slot, k] = _dot(onehot, hn).astype(_BF16)

    gather_piece(0)
    for k in range(1, n_pieces):
        reaches = sl_ref[c] > k * piece
        for e in range(1, n_e):
            reaches = jnp.logical_or(reaches, sl_ref[e * n_ch + c] > k * piece)

        @pl.when(reaches)
        def _(k=k):
            gather_piece(k)

    @pl.when(c > 0)
    def _():
        for_each_piece(c - 1, 1 - slot, lambda cp: cp.wait())

    for_each_piece(c, slot, lambda cp: cp.start())

    @pl.when(c == n_ch - 1)
    def _():
        for_each_piece(c, slot, lambda cp: cp.wait())
        for_each_unused_block(lambda cp: cp.wait())


def _dispatch(hn, pos_l, w0_flat, seglen_flat, end_flat, n_valid, n_e, n_rows, row_tile):
    t_rows, d = hn.shape
    ch = TOKEN_CHUNK
    n_ch = t_rows // ch
    n_pieces = ch // WINDOW_PIECE
    min_blocks = 2 * t_rows // row_tile
    return pl.pallas_call(
        functools.partial(_dispatch_kernel, n_e=n_e, n_ch=n_ch, row_tile=row_tile,
                          min_blocks=min_blocks),
        grid_spec=pltpu.PrefetchScalarGridSpec(
            num_scalar_prefetch=4,
            grid=(n_ch,),
            in_specs=[pl.BlockSpec((ch, d), lambda c, *_: (c, 0)),
                      pl.BlockSpec((1, 2, ch), lambda c, *_: (c, 0, 0))],
            out_specs=pl.BlockSpec(memory_space=pl.ANY),
            scratch_shapes=[pltpu.VMEM((2, n_pieces, n_e * WINDOW_PIECE, d), _BF16),
                            pltpu.SemaphoreType.DMA((2, n_e, n_pieces)),
                            pltpu.VMEM((row_tile, d), _BF16),
                            pltpu.SemaphoreType.DMA((n_e,)),
                            pltpu.SemaphoreType.DMA((n_rows // row_tile - min_blocks,))]),
        out_shape=jax.ShapeDtypeStruct((n_rows, d), _BF16),
        compiler_params=pltpu.CompilerParams(
            dimension_semantics=("arbitrary",), vmem_limit_bytes=VMEM_LIMIT),
        name="moe_dispatch",
    )(w0_flat, seglen_flat, end_flat, n_valid, hn, pos_l)


def _ffn_grouped_kernel(be_ref, nv_ref, x_ref, wg_ref, wu_ref, wd_ref, y_ref, a_scr):
    valid = pl.program_id(0) < nv_ref[0]

    @pl.when(valid)
    def _():
        y_ref[...] = _swiglu_partial(x_ref[...], wg_ref, wu_ref, wd_ref, a_scr).astype(_BF16)

    @pl.when(jnp.logical_not(valid))
    def _():
        y_ref[...] = jnp.zeros_like(y_ref)


def _ffn_grouped(x_sorted, wg, wu, wd, block_expert, n_valid, row_tile):
    rows, d = x_sorted.shape
    f = wg.shape[2]
    n_blocks = rows // row_tile

    def in_use(b, nv):
        return jnp.minimum(b, nv[0] - 1)

    w_spec = lambda shape: pl.BlockSpec(shape, lambda b, be, nv: (be[in_use(b, nv)], 0, 0))
    return pl.pallas_call(
        _ffn_grouped_kernel,
        grid_spec=pltpu.PrefetchScalarGridSpec(
            num_scalar_prefetch=2,
            grid=(n_blocks,),
            in_specs=[pl.BlockSpec((row_tile, d), lambda b, be, nv: (in_use(b, nv), 0)),
                      w_spec((1, d, f)), w_spec((1, d, f)), w_spec((1, f, d))],
            out_specs=pl.BlockSpec((row_tile, d), lambda b, be, nv: (b, 0)),
            scratch_shapes=[pltpu.VMEM((row_tile, f), _BF16)]),
        out_shape=jax.ShapeDtypeStruct((rows, d), _BF16),
        compiler_params=pltpu.CompilerParams(
            dimension_semantics=("arbitrary",), vmem_limit_bytes=VMEM_LIMIT),
        name="moe_experts",
    )(block_expert, n_valid, x_sorted, wg, wu, wd)


def _combine_kernel(w0_ref, sl_ref, res_ref, pos_ref, gate_ref, gfin_ref, y_hbm,
                    out_ref, ybuf, sems, tmp_scr, acc_scr, *, n_e, n_ch):
    c = pl.program_id(0)
    ch = res_ref.shape[0]
    piece = WINDOW_PIECE
    n_pieces = ch // piece

    def piece_copy(cc, slot, e, k):
        row0 = pl.multiple_of(w0_ref[e * n_ch + cc] + k * piece, SEG_ALIGN)
        return pltpu.make_async_copy(
            y_hbm.at[pl.ds(row0, piece)],
            ybuf.at[slot, k, pl.ds(e * piece, piece)], sems.at[slot, e, k])

    def for_each_piece(cc, slot, fn):
        for e in range(n_e):
            fn(piece_copy(cc, slot, e, 0))
            for k in range(1, n_pieces):
                @pl.when(sl_ref[e * n_ch + cc] > k * piece)
                def _(e=e, k=k):
                    fn(piece_copy(cc, slot, e, k))

    @pl.when(c == 0)
    def _():
        ybuf[...] = jnp.zeros_like(ybuf)
        for_each_piece(0, 0, lambda cp: cp.start())

    @pl.when(c + 1 < n_ch)
    def _():
        for_each_piece(c + 1, (c + 1) % 2, lambda cp: cp.start())

    slot = c % 2
    for_each_piece(c, slot, lambda cp: cp.wait())

    per_dot = MXU_DIM // piece
    depth = per_dot * piece
    lane = lax.broadcasted_iota(_I32, (1, depth), 1)
    p1b = jnp.broadcast_to(pos_ref[:, 0:1], (ch, depth))
    p2b = jnp.broadcast_to(pos_ref[:, 1:2], (ch, depth))
    g1b = jnp.broadcast_to(gate_ref[:, 0:1], (ch, depth))
    g2b = jnp.broadcast_to(gate_ref[:, 1:2], (ch, depth))

    def scatter_back(m, k):
        tgt = jnp.full((1, depth), -1, _I32)
        for i in range(per_dot):
            e = m * per_dot + i
            row = lane - i * piece + k * piece
            mine = (lane >= i * piece) & (lane < (i + 1) * piece) & (row < sl_ref[e * n_ch + c])
            tgt = jnp.where(mine, row + w0_ref[e * n_ch + c], tgt)
        q = (jnp.where(p1b == tgt, g1b, 0.0) + jnp.where(p2b == tgt, g2b, 0.0)).astype(_BF16)
        return _dot(q, ybuf[slot, k, m * depth:(m + 1) * depth, :])

    def any_reaches(m, k):
        hit = sl_ref[(m * per_dot) * n_ch + c] > k * piece
        for i in range(1, per_dot):
            hit = jnp.logical_or(hit, sl_ref[(m * per_dot + i) * n_ch + c] > k * piece)
        return hit

    acc = res_ref[...]
    for m in range(n_e // per_dot):
        acc = acc + scatter_back(m, 0)
    acc_scr[...] = acc
    for m in range(n_e // per_dot):
        for k in range(1, n_pieces):
            @pl.when(any_reaches(m, k))
            def _(m=m, k=k):
                acc_scr[...] += scatter_back(m, k)
    normed = _rmsnorm(acc_scr[...], gfin_ref[...])
    steps = ch // SUBLANES
    for k in range(tmp_scr.shape[0]):
        tmp_scr[k] = normed[:, k * LANES:(k + 1) * LANES]
    for bb in range(SUBLANES):
        for k in range(tmp_scr.shape[0]):
            out_ref[bb, :, k * LANES:(k + 1) * LANES] = (
                tmp_scr[k, pl.ds(bb, steps, stride=SUBLANES), :])


def _combine(res, pos_c, gates_c, gfin, y_sorted, w0_flat, seglen_flat, n_e, bsz, seq):
    t_rows, d = res.shape
    ch = TOKEN_CHUNK
    n_ch = t_rows // ch
    steps = ch // bsz
    return pl.pallas_call(
        functools.partial(_combine_kernel, n_e=n_e, n_ch=n_ch),
        grid_spec=pltpu.PrefetchScalarGridSpec(
            num_scalar_prefetch=2,
            grid=(n_ch,),
            in_specs=[pl.BlockSpec((ch, d), lambda c, w0, sl: (c, 0)),
                      pl.BlockSpec((ch, 2), lambda c, w0, sl: (c, 0)),
                      pl.BlockSpec((ch, 2), lambda c, w0, sl: (c, 0)),
                      pl.BlockSpec((1, d), lambda c, w0, sl: (0, 0)),
                      pl.BlockSpec(memory_space=pl.ANY)],
            out_specs=pl.BlockSpec((bsz, steps, d), lambda c, w0, sl: (0, c, 0)),
            scratch_shapes=[pltpu.VMEM((2, ch // WINDOW_PIECE, n_e * WINDOW_PIECE, d), _BF16),
                            pltpu.SemaphoreType.DMA((2, n_e, ch // WINDOW_PIECE)),
                            pltpu.VMEM((d // LANES, ch, LANES), _F32),
                            pltpu.VMEM((ch, d), _F32)]),
        out_shape=jax.ShapeDtypeStruct((bsz, seq, d), _F32),
        compiler_params=pltpu.CompilerParams(
            dimension_semantics=("arbitrary",), vmem_limit_bytes=VMEM_LIMIT),
        name="moe_combine",
    )(w0_flat, seglen_flat, res, pos_c, gates_c, gfin, y_sorted)


def _moe_layer(h, hn, logits_t, wg, wu, wd, gfin, bsz, seq):
    t_rows, d = h.shape
    n_e = logits_t.shape[0]
    tm = FFN_ROW_TILE
    n_ch = t_rows // TOKEN_CHUNK
    max_rows = 2 * t_rows + (SEG_ALIGN - 1) * n_e * n_ch + n_e * (TOKEN_CHUNK + tm)
    n_blocks = -(-max_rows // tm) + 1
    n_blk_pad = -(-n_blocks // LANES) * LANES

    pos, gates, segtab, blktab = _route(logits_t, n_blk_pad, tm)
    pos_l = jnp.transpose(pos.reshape(2, n_ch, TOKEN_CHUNK), (1, 0, 2))
    w0_flat = segtab[0].reshape(-1)
    seglen_flat = segtab[1].reshape(-1)
    n_valid = blktab[1, :1]
    x_sorted = _dispatch(hn, pos_l, w0_flat, seglen_flat, segtab[2].reshape(-1), n_valid,
                         n_e, n_blocks * tm, tm)
    y_sorted = _ffn_grouped(x_sorted, wg, wu, wd, blktab[0], n_valid, tm)
    return _combine(h, jnp.transpose(pos), jnp.transpose(gates), gfin, y_sorted,
                    w0_flat, seglen_flat, n_e, bsz, seq)


def _block_diag_groups(w, groups):
    h, k, n = w.shape
    hpg = h // groups
    eye = jnp.eye(hpg, dtype=w.dtype)
    wq = w.reshape(groups, hpg, k, n)
    out = jnp.einsum("qhkn,hj->qhkjn", wq, eye)
    return out.reshape(groups, hpg * k, hpg * n)


def kernel(x, norm_mix_g, w_in, ssm_log_dt, ssm_a_re, ssm_a_im, ssm_b_re, ssm_b_im, ssm_c_re, ssm_c_im, ssm_d, ssm_w_glu, ssm_b_glu, pool_w, pool_scale, w_out, norm_ffn_g, ffn_w_gate, ffn_w_up, ffn_w_down, router_w, moe_w_gate, moe_w_up, moe_w_down, final_norm_g):
    bsz, seq, d = x.shape
    depth, n_heads, n_state, n_grp_ch = ssm_b_re.shape
    t_rows = bsz * seq
    assert bsz == SUBLANES and depth % 2 == 0
    groups = n_heads // HEADS_PER_GROUP

    nh = depth * n_heads
    lam_re, lam_im, bb_re, bb_im = _discretize(
        ssm_log_dt.reshape(nh), ssm_a_re.reshape(nh, n_state), ssm_a_im.reshape(nh, n_state),
        jnp.swapaxes(ssm_b_re, 2, 3).reshape(nh, n_grp_ch, n_state),
        jnp.swapaxes(ssm_b_im, 2, 3).reshape(nh, n_grp_ch, n_state))
    lam_re = lam_re.reshape(depth, groups, 1, HEADS_PER_GROUP * n_state)
    lam_im = lam_im.reshape(depth, groups, 1, HEADS_PER_GROUP * n_state)
    lam = jnp.concatenate([lam_re, lam_im], axis=2).reshape(depth, 2 * groups, -1)
    bb_re = bb_re.reshape(depth, n_heads, n_grp_ch, n_state)
    bb_im = bb_im.reshape(depth, n_heads, n_grp_ch, n_state)

    h = x
    row = lambda v: v.reshape(1, -1)
    out = None
    for i in range(depth):
        bw = jnp.concatenate([_block_diag_groups(bb_re[i], groups),
                              _block_diag_groups(bb_im[i], groups)], axis=2).astype(_BF16)
        cw = jnp.concatenate(
            [_block_diag_groups(jnp.swapaxes(ssm_c_re[i], 1, 2), groups),
             _block_diag_groups(-jnp.swapaxes(ssm_c_im[i], 1, 2), groups)], axis=1).astype(_BF16)
        is_moe = i % 2 == 1
        j = i // 2
        rw = router_w[j] if is_moe else None
        outs = _mix_layer(
            h, row(norm_mix_g[i]), w_in[i].astype(_BF16), bw, lam[i], cw, row(ssm_d[i]),
            ssm_w_glu[i].astype(_BF16), row(ssm_b_glu[i]), pool_w[i].astype(_BF16),
            row(pool_scale[i]), w_out[i].astype(_BF16), row(norm_ffn_g[i]), rw)
        if is_moe:
            assert i == depth - 1
            h, hn, logits_t = outs
            out = _moe_layer(h, hn, logits_t, moe_w_gate[j].astype(_BF16),
                             moe_w_up[j].astype(_BF16), moe_w_down[j].astype(_BF16),
                             row(final_norm_g), bsz, seq)
        else:
            h, hn = outs
            h = _ffn(hn, ffn_w_gate[j][None].astype(_BF16), ffn_w_up[j][None].astype(_BF16),
                     ffn_w_down[j][None].astype(_BF16), h)
    return out
```

```python
import functools
import math

import jax
import jax.numpy as jnp
from jax import lax
from jax.experimental import pallas as pl
from jax.experimental.pallas import tpu as pltpu

RMS_EPS = 1e-6
POOL_WINDOWS = (2, 4, 8, 16)
A_RE_MAX = -1e-4
GELU_C0 = math.sqrt(2.0 / math.pi)
GELU_C1 = 0.044715

SUBLANES = 8
LANES = 128
MXU_DIM = 256

HEADS_PER_GROUP = 16
TIME_TILE = 64
SCAN_COLS = 512
FFN_ROW_TILE = 512
EXPERT_ROW_TILE = 512
FFN_COL_CHUNK = 256
WEIGHT_UNITS = 8
TOKEN_CHUNK = MXU_DIM
SEG_ALIGN = 2 * SUBLANES
WINDOW_PIECE = LANES
VMEM_LIMIT = 56 * 1024 * 1024

_F32 = jnp.float32
_BF16 = jnp.bfloat16
_I32 = jnp.int32


def _dot(a, b):
    return jnp.dot(a, b, preferred_element_type=_F32)


def _dot_nt(a, b):
    return lax.dot_general(a, b, (((1,), (1,)), ((), ())), preferred_element_type=_F32)


def _rmsnorm(x, g):
    inv = lax.rsqrt(jnp.mean(x * x, axis=-1, keepdims=True) + RMS_EPS)
    return x * inv * g


def _sigmoid(x):
    return 1.0 / (1.0 + jnp.exp(-x))


def _ceil_to(x, m):
    return jnp.floor((x + (m - 1.0)) * (1.0 / m)) * m


def _discretize_kernel(log_dt_ref, a_re_ref, a_im_ref, b_re_ref, b_im_ref,
                       lam_re_ref, lam_im_ref, bb_re_ref, bb_im_ref):
    dt = jnp.exp(log_dt_ref[...])
    ar = jnp.minimum(a_re_ref[...], A_RE_MAX)
    ai = a_im_ref[...]
    mag = jnp.exp(ar * dt)
    lam_re = mag * jnp.cos(ai * dt)
    lam_im = mag * jnp.sin(ai * dt)
    den = ar * ar + ai * ai
    nr = lam_re - 1.0
    ni = lam_im
    coef_re = (nr * ar + ni * ai) / den
    coef_im = (ni * ar - nr * ai) / den
    lam_re_ref[...] = lam_re
    lam_im_ref[...] = lam_im
    br = b_re_ref[...]
    bi = b_im_ref[...]
    bb_re_ref[...] = coef_re * br - coef_im * bi
    bb_im_ref[...] = coef_re * bi + coef_im * br


def _discretize(log_dt, a_re, a_im, b_re_t, b_im_t):
    n, g, p = b_re_t.shape
    full3 = lambda s: pl.BlockSpec(s, lambda: (0, 0, 0))
    return pl.pallas_call(
        _discretize_kernel,
        out_shape=(jax.ShapeDtypeStruct((n, 1, p), _F32),
                   jax.ShapeDtypeStruct((n, 1, p), _F32),
                   jax.ShapeDtypeStruct((n, g, p), _F32),
                   jax.ShapeDtypeStruct((n, g, p), _F32)),
        in_specs=[full3((n, 1, 1)), full3((n, 1, p)), full3((n, 1, p)),
                  full3((n, g, p)), full3((n, g, p))],
        out_specs=(full3((n, 1, p)), full3((n, 1, p)),
                   full3((n, g, p)), full3((n, g, p))),
        name="ssm_discretize",
    )(log_dt.reshape(n, 1, 1), a_re.reshape(n, 1, p), a_im.reshape(n, 1, p),
      b_re_t, b_im_t)


def _mix_kernel(*refs, time_tile, with_router, batch_major_in):
    (h_ref, gmix_ref, win_ref, bw_ref, lam_ref, cw_ref, dskip_ref, wglu_ref,
     bglu_ref, poolw_ref, pscale_ref, wout_ref, gffn_ref) = refs[:13]
    refs = refs[13:]
    rw_ref = logits_ref = xin_scr = None
    if with_router:
        rw_ref, hout_ref, hn_ref, logits_ref = refs[:4]
        refs = refs[4:]
    else:
        hout_ref, hn_ref = refs[:2]
        refs = refs[2:]
    s_scr, state_scr, ext_scr, mixed_scr = refs[:4]
    if batch_major_in:
        xin_scr = refs[4]

    step = pl.program_id(0)
    rows = time_tile * SUBLANES
    d_ssm = dskip_ref.shape[1]
    n_groups = bw_ref.shape[0]
    gin = bw_ref.shape[1]
    gstate = bw_ref.shape[2] // 2
    hist = ext_scr.shape[0] - rows

    @pl.when(step == 0)
    def _():
        state_scr[...] = jnp.zeros_like(state_scr)
        ext_scr[0:hist, :] = jnp.zeros((hist, ext_scr.shape[1]), _F32)

    if batch_major_in:
        for bb in range(SUBLANES):
            for k in range(xin_scr.shape[0]):
                xin_scr[k, pl.ds(bb, time_tile, stride=SUBLANES), :] = (
                    h_ref[bb, :, k * LANES:(k + 1) * LANES])
        h = jnp.concatenate([xin_scr[k] for k in range(xin_scr.shape[0])], axis=1)
    else:
        h = h_ref[...]
    hn = _rmsnorm(h, gmix_ref[...]).astype(_BF16)
    proj = _dot(hn, win_ref[...])
    u_ssm = proj[:, :d_ssm]
    ext_scr[hist:, :] = proj[:, d_ssm:]

    for q in range(n_groups):
        ug = u_ssm[:, q * gin:(q + 1) * gin].astype(_BF16)
        s_scr[q] = _dot(ug, bw_ref[q])
        for c in range(gstate // SCAN_COLS):
            re0 = c * SCAN_COLS
            im0 = gstate + c * SCAN_COLS
            lr = jnp.broadcast_to(lam_ref[2 * q:2 * q + 1, re0:re0 + SCAN_COLS],
                                  (SUBLANES, SCAN_COLS))
            li = jnp.broadcast_to(lam_ref[2 * q + 1:2 * q + 2, re0:re0 + SCAN_COLS],
                                  (SUBLANES, SCAN_COLS))
            sre = state_scr[q, :, re0:re0 + SCAN_COLS]
            sim = state_scr[q, :, im0:im0 + SCAN_COLS]
            for t in range(time_tile):
                r0 = t * SUBLANES
                bre = s_scr[q, r0:r0 + SUBLANES, re0:re0 + SCAN_COLS]
                bim = s_scr[q, r0:r0 + SUBLANES, im0:im0 + SCAN_COLS]
                sre, sim = (lr * sre - li * sim + bre, lr * sim + li * sre + bim)
                s_scr[q, r0:r0 + SUBLANES, re0:re0 + SCAN_COLS] = sre
                s_scr[q, r0:r0 + SUBLANES, im0:im0 + SCAN_COLS] = sim
            state_scr[q, :, re0:re0 + SCAN_COLS] = sre
            state_scr[q, :, im0:im0 + SCAN_COLS] = sim
        yq = _dot(s_scr[q].astype(_BF16), cw_ref[q])
        yq = yq + dskip_ref[:, q * gin:(q + 1) * gin] * u_ssm[:, q * gin:(q + 1) * gin]
        hq = 0.5 * yq * (1.0 + jnp.tanh(GELU_C0 * (yq + GELU_C1 * (yq * yq * yq))))
        mixed_scr[:, q * gin:(q + 1) * gin] = hq.astype(_BF16)
    hg = mixed_scr[:, :d_ssm]
    gate = _sigmoid(_dot(hg, wglu_ref[...]) + bglu_ref[...])
    mixed_scr[:, :d_ssm] = (hg.astype(_F32) * gate).astype(_BF16)

    n_ext = rows + hist
    pos = step * time_tile + (lax.broadcasted_iota(_I32, (rows, LANES), 0) // SUBLANES)
    for g, w in enumerate(POOL_WINDOWS):
        c0 = g * LANES
        e = ext_scr[:, c0:c0 + LANES]
        acc = e
        n_acc = n_ext
        span = 1
        while span < w:
            sh = span * SUBLANES
            acc = acc[sh:, :] + acc[:n_acc - sh, :]
            n_acc -= sh
            span *= 2
        wsum = acc[n_acc - rows:, :]
        cnt = jnp.minimum(pos + 1, w).astype(_F32)
        pooled = wsum / cnt - e[hist:, :]
        mg = _dot(pooled.astype(_BF16), poolw_ref[g]) * pscale_ref[:, c0:c0 + LANES]
        mixed_scr[:, d_ssm + c0:d_ssm + c0 + LANES] = mg.astype(_BF16)
    ext_scr[0:hist, :] = ext_scr[rows:rows + hist, :]

    hout = h + _dot(mixed_scr[...], wout_ref[...])
    hout_ref[...] = hout
    hn2 = _rmsnorm(hout, gffn_ref[...])
    hn_hi = hn2.astype(_BF16)
    hn_ref[...] = hn_hi
    if with_router:
        hn_lo = (hn2 - hn_hi.astype(_F32)).astype(_BF16)
        rw = rw_ref[...]
        rw_hi = rw.astype(_BF16)
        rw_lo = (rw - rw_hi.astype(_F32)).astype(_BF16)
        logits_ref[...] = (_dot_nt(rw_hi, hn_hi) + _dot_nt(rw_hi, hn_lo)
                           + _dot_nt(rw_lo, hn_hi))


def _mix_layer(h, gmix, w_in, bw, lam, cw, dskip, wglu, bglu, poolw, pscale, wout,
               gffn, router_w):
    batch_major_in = h.ndim == 3
    d = h.shape[-1]
    t_rows = h.size // d
    rows = TIME_TILE * SUBLANES
    n_steps = t_rows // rows
    d_ssm = dskip.shape[1]
    d_pool = pscale.shape[1]
    hist = max(POOL_WINDOWS) * SUBLANES
    with_router = router_w is not None

    def const(a):
        nd = a.ndim
        return pl.BlockSpec(a.shape, lambda i, nd=nd: (0,) * nd)

    row_blk = lambda width: pl.BlockSpec((rows, width), lambda i: (i, 0))
    ins = [h, gmix, w_in, bw, lam, cw, dskip, wglu, bglu, poolw, pscale, wout, gffn]
    h_spec = (pl.BlockSpec((h.shape[0], TIME_TILE, d), lambda i: (0, i, 0))
              if batch_major_in else row_blk(d))
    in_specs = [h_spec] + [const(a) for a in ins[1:]]
    out_shape = [jax.ShapeDtypeStruct((t_rows, d), _F32),
                 jax.ShapeDtypeStruct((t_rows, d), _BF16)]
    out_specs = [row_blk(d), row_blk(d)]
    if with_router:
        n_e = router_w.shape[1]
        rw = jnp.transpose(router_w)
        ins.append(rw)
        in_specs.append(const(rw))
        out_shape.append(jax.ShapeDtypeStruct((n_e, t_rows), _F32))
        out_specs.append(pl.BlockSpec((n_e, rows), lambda i: (0, i)))
    scratch = [
        pltpu.VMEM((bw.shape[0], rows, bw.shape[2]), _F32),
        pltpu.VMEM((bw.shape[0], SUBLANES, bw.shape[2]), _F32),
        pltpu.VMEM((rows + hist, d_pool), _F32),
        pltpu.VMEM((rows, d_ssm + d_pool), _BF16),
    ]
    if batch_major_in:
        scratch.append(pltpu.VMEM((d // LANES, rows, LANES), _F32))
    return pl.pallas_call(
        functools.partial(_mix_kernel, time_tile=TIME_TILE, with_router=with_router,
                          batch_major_in=batch_major_in),
        grid=(n_steps,),
        in_specs=in_specs,
        out_specs=out_specs,
        out_shape=out_shape,
        scratch_shapes=scratch,
        compiler_params=pltpu.CompilerParams(
            dimension_semantics=("arbitrary",), vmem_limit_bytes=VMEM_LIMIT),
        name="mix_router" if with_router else "mix",
    )(*ins)


def _swiglu(x, wg_ref, wu_ref, wd_ref, a_scr, slot):
    tf = wg_ref.shape[2]
    c0 = 0
    while c0 < tf:
        cw = min(FFN_COL_CHUNK, tf - c0)
        g = _dot(x, wg_ref[slot, :, c0:c0 + cw])
        u = _dot(x, wu_ref[slot, :, c0:c0 + cw])
        a_scr[:, c0:c0 + cw] = (g * _sigmoid(g) * u).astype(_BF16)
        c0 += cw
    return _dot(a_scr[...], wd_ref[slot])


def _ffn_kernel(x_ref, wg_ref, wu_ref, wd_ref, res_ref, out_ref, a_scr):
    out_ref[...] = res_ref[...] + _swiglu(x_ref[...], wg_ref, wu_ref, wd_ref, a_scr, 0)


def _ffn(x, wg, wu, wd, res):
    t_rows, d = x.shape
    f = wg.shape[2]
    tm = FFN_ROW_TILE
    held = lambda shape: pl.BlockSpec(shape, lambda i: (0, 0, 0), pipeline_mode=pl.Buffered(1))
    return pl.pallas_call(
        _ffn_kernel,
        grid=(t_rows // tm,),
        in_specs=[
            pl.BlockSpec((tm, d), lambda i: (i, 0)),
            held((1, d, f)), held((1, d, f)), held((1, f, d)),
            pl.BlockSpec((tm, d), lambda i: (i, 0)),
        ],
        out_specs=pl.BlockSpec((tm, d), lambda i: (i, 0)),
        out_shape=jax.ShapeDtypeStruct((t_rows, d), _F32),
        scratch_shapes=[pltpu.VMEM((tm, f), _BF16)],
        compiler_params=pltpu.CompilerParams(
            dimension_semantics=("arbitrary",), vmem_limit_bytes=VMEM_LIMIT),
        name="ffn_dense",
    )(x, wg, wu, wd, res)


def _route_kernel(lt_ref, pos_ref, gates_ref, segtab_ref, blktab_ref, *, n_blk, row_tile):
    l = lt_ref[...]
    n_e, t_rows = l.shape
    ch = TOKEN_CHUNK
    n_ch = t_rows // ch
    ie = lax.broadcasted_iota(_I32, l.shape, 0)
    m1 = jnp.max(l, axis=0, keepdims=True)
    i1 = jnp.min(jnp.where(l == m1, ie, n_e), axis=0, keepdims=True)
    l2 = jnp.where(ie == i1, -jnp.inf, l)
    m2 = jnp.max(l2, axis=0, keepdims=True)
    i2 = jnp.min(jnp.where(l2 == m2, ie, n_e), axis=0, keepdims=True)
    e2 = jnp.exp(m2 - m1)
    den = 1.0 + e2
    gates_ref[0:1, :] = 1.0 / den
    gates_ref[1:2, :] = e2 / den
    sel1 = ie == i1
    sel2 = ie == i2
    chosen = jnp.where(sel1 | sel2, 1.0, 0.0).astype(_BF16)

    chunk_of_t = lax.broadcasted_iota(_I32, (n_ch, t_rows), 1) // ch
    in_chunk = jnp.where(chunk_of_t == lax.broadcasted_iota(_I32, (n_ch, t_rows), 0),
                         1.0, 0.0).astype(_BF16)
    n_ec = _dot_nt(chosen, in_chunk)
    n_ce = _dot_nt(in_chunk, chosen)
    pad_ec = _ceil_to(n_ec, float(SEG_ALIGN))
    pad_ce = _ceil_to(n_ce, float(SEG_ALIGN))
    cc_r = lax.broadcasted_iota(_I32, (n_ch, n_ch), 0)
    cc_c = lax.broadcasted_iota(_I32, (n_ch, n_ch), 1)
    before = jnp.where(cc_r < cc_c, 1.0, 0.0).astype(_BF16)
    after = jnp.where(cc_c < cc_r, 1.0, 0.0).astype(_BF16)
    segoff_ec = _dot(pad_ec.astype(_BF16), before)
    segoff_ce = _dot(after, pad_ce.astype(_BF16))
    win_ec = _ceil_to(jnp.maximum(pad_ec, 1.0), float(WINDOW_PIECE))
    win_ce = _ceil_to(jnp.maximum(pad_ce, 1.0), float(WINDOW_PIECE))
    tot_col = _ceil_to(jnp.max(segoff_ec + win_ec, axis=1, keepdims=True), float(row_tile))
    tot_row = _ceil_to(jnp.max(segoff_ce + win_ce, axis=0, keepdims=True), float(row_tile))
    ee_r = lax.broadcasted_iota(_I32, (n_e, n_e), 0)
    ee_c = lax.broadcasted_iota(_I32, (n_e, n_e), 1)
    base_col = jnp.sum(jnp.where(ee_c < ee_r, tot_row, 0.0), axis=1, keepdims=True)
    w0_ec = base_col + segoff_ec
    end_col = base_col + tot_col
    segtab_ref[0] = w0_ec.astype(_I32)
    segtab_ref[1] = pad_ec.astype(_I32)
    segtab_ref[2] = jnp.broadcast_to(end_col, (n_e, n_ch)).astype(_I32)

    tt_r = lax.broadcasted_iota(_I32, (ch, ch), 0)
    tt_c = lax.broadcasted_iota(_I32, (ch, ch), 1)
    earlier = jnp.where(tt_r < tt_c, 1.0, 0.0).astype(_BF16)
    for c in range(n_ch):
        cols = slice(c * ch, (c + 1) * ch)
        rank = _dot(chosen[:, cols], earlier)
        offs = w0_ec[:, c:c + 1] + rank
        p1 = jnp.sum(jnp.where(sel1[:, cols], offs, 0.0), axis=0, keepdims=True)
        p2 = jnp.sum(jnp.where(sel2[:, cols], offs, 0.0), axis=0, keepdims=True)
        pos_ref[0:1, cols] = p1.astype(_I32)
        pos_ref[1:2, cols] = p2.astype(_I32)

    start = (lax.broadcasted_iota(_I32, (1, n_blk), 1) * row_tile).astype(_F32)
    owner = jnp.sum(jnp.where(end_col <= start, 1.0, 0.0), axis=0, keepdims=True)
    n_valid = jnp.sum(tot_col, axis=0, keepdims=True) * (1.0 / row_tile)
    blktab_ref[0:1, :] = jnp.minimum(owner, n_e - 1.0).astype(_I32)
    blktab_ref[1:2, :] = jnp.broadcast_to(n_valid, (1, n_blk)).astype(_I32)


def _route(logits_t, n_blk, row_tile):
    n_e, t_rows = logits_t.shape
    n_ch = t_rows // TOKEN_CHUNK
    full = lambda s: pl.BlockSpec(s, lambda: (0,) * len(s))
    return pl.pallas_call(
        functools.partial(_route_kernel, n_blk=n_blk, row_tile=row_tile),
        in_specs=[full((n_e, t_rows))],
        out_specs=(full((2, t_rows)), full((2, t_rows)), full((3, n_e, n_ch)), full((2, n_blk))),
        out_shape=(jax.ShapeDtypeStruct((2, t_rows), _I32),
                   jax.ShapeDtypeStruct((2, t_rows), _F32),
                   jax.ShapeDtypeStruct((3, n_e, n_ch), _I32),
                   jax.ShapeDtypeStruct((2, n_blk), _I32)),
        compiler_params=pltpu.CompilerParams(vmem_limit_bytes=VMEM_LIMIT),
        name="route_sort",
    )(logits_t)


def _dispatch_kernel(w0_ref, sl_ref, end_ref, nv_ref, hn_ref, posl_ref, x_hbm,
                     stage, sems, zero_scr, zsems, tsems, *, n_e, n_ch, row_tile, min_blocks):
    c = pl.program_id(0)
    ch = hn_ref.shape[0]
    piece = WINDOW_PIECE
    n_pieces = ch // piece
    slot = c % 2
    n_blocks = x_hbm.shape[0] // row_tile

    def for_each_unused_block(fn):
        for b in range(min_blocks, n_blocks):
            @pl.when(b >= nv_ref[0])
            def _(b=b):
                fn(pltpu.make_async_copy(zero_scr, x_hbm.at[pl.ds(b * row_tile, row_tile)],
                                         tsems.at[b - min_blocks]))

    def piece_copy(cc, sl, e, k):
        row0 = pl.multiple_of(w0_ref[e * n_ch + cc] + k * piece, SEG_ALIGN)
        return pltpu.make_async_copy(
            stage.at[sl, k, pl.ds(e * piece, piece)],
            x_hbm.at[pl.ds(row0, piece)], sems.at[sl, e, k])

    def for_each_piece(cc, sl, fn):
        for e in range(n_e):
            fn(piece_copy(cc, sl, e, 0))
            for k in range(1, n_pieces):
                @pl.when(sl_ref[e * n_ch + cc] > k * piece)
                def _(e=e, k=k):
                    fn(piece_copy(cc, sl, e, k))

    @pl.when(c == 0)
    def _():
        zero_scr[...] = jnp.zeros_like(zero_scr)
        fills = [pltpu.make_async_copy(
            zero_scr,
            x_hbm.at[pl.ds(pl.multiple_of(end_ref[e * n_ch] - row_tile, SEG_ALIGN), row_tile)],
            zsems.at[e]) for e in range(n_e)]
        for cp in fills:
            cp.start()
        for_each_unused_block(lambda cp: cp.start())
        for cp in fills:
            cp.wait()

    pp = posl_ref[0]
    hn = hn_ref[...]

    def gather_piece(k):
        row_j = lax.broadcasted_iota(_I32, (piece, ch), 0) + k * piece
        parts = []
        for e in range(n_e):
            row_id = row_j + w0_ref[e * n_ch + c]
            hit = (pp[0:1, :] == row_id) | (pp[1:2, :] == row_id)
            parts.append(jnp.where(hit, 1.0, 0.0).astype(_BF16))
        onehot = jnp.concatenate(parts, axis=0)
        stage[slot, k] = _dot(onehot, hn).astype(_BF16)

    gather_piece(0)
    for k in range(1, n_pieces):
        reaches = sl_ref[c] > k * piece
        for e in range(1, n_e):
            reaches = jnp.logical_or(reaches, sl_ref[e * n_ch + c] > k * piece)

        @pl.when(reaches)
        def _(k=k):
            gather_piece(k)

    @pl.when(c > 0)
    def _():
        for_each_piece(c - 1, 1 - slot, lambda cp: cp.wait())

    for_each_piece(c, slot, lambda cp: cp.start())

    @pl.when(c == n_ch - 1)
    def _():
        for_each_piece(c, slot, lambda cp: cp.wait())
        for_each_unused_block(lambda cp: cp.wait())


def _dispatch(hn, pos_l, w0_flat, seglen_flat, end_flat, n_valid, n_e, n_rows, row_tile):
    t_rows, d = hn.shape
    ch = TOKEN_CHUNK
    n_ch = t_rows // ch
    n_pieces = ch // WINDOW_PIECE
    min_blocks = 2 * t_rows // row_tile
    return pl.pallas_call(
        functools.partial(_dispatch_kernel, n_e=n_e, n_ch=n_ch, row_tile=row_tile,
                          min_blocks=min_blocks),
        grid_spec=pltpu.PrefetchScalarGridSpec(
            num_scalar_prefetch=4,
            grid=(n_ch,),
            in_specs=[pl.BlockSpec((ch, d), lambda c, *_: (c, 0)),
                      pl.BlockSpec((1, 2, ch), lambda c, *_: (c, 0, 0))],
            out_specs=pl.BlockSpec(memory_space=pl.ANY),
            scratch_shapes=[pltpu.VMEM((2, n_pieces, n_e * WINDOW_PIECE, d), _BF16),
                            pltpu.SemaphoreType.DMA((2, n_e, n_pieces)),
                            pltpu.VMEM((row_tile, d), _BF16),
                            pltpu.SemaphoreType.DMA((n_e,)),
                            pltpu.SemaphoreType.DMA((n_rows // row_tile - min_blocks,))]),
        out_shape=jax.ShapeDtypeStruct((n_rows, d), _BF16),
        compiler_params=pltpu.CompilerParams(
            dimension_semantics=("arbitrary",), vmem_limit_bytes=VMEM_LIMIT),
        name="moe_dispatch",
    )(w0_flat, seglen_flat, end_flat, n_valid, hn, pos_l)


def _ffn_grouped_kernel(be_ref, nv_ref, end_ref, x_ref, wg_hbm, wu_hbm, wd_hbm, y_ref,
                        wg_buf, wu_buf, wd_buf, stg_g, stg_u, stg_d, sems, state, a_scr,
                        *, n_ch, expert0):
    b = pl.program_id(0)
    n_valid = nv_ref[0]
    valid = b < n_valid
    row_tile = x_ref.shape[0]
    units = WEIGHT_UNITS
    gu_rows = wg_buf.shape[1] // units
    d_rows = wd_buf.shape[1] // units
    ST_SLOT, ST_RESIDENT, ST_NEXT, ST_DONE, ST_STAGE = range(5)

    def unit_copies(e, k, st):
        g0 = pl.multiple_of(k * gu_rows, gu_rows)
        d0 = pl.multiple_of(k * d_rows, SEG_ALIGN)
        return (pltpu.make_async_copy(wg_hbm.at[expert0 + e, pl.ds(g0, gu_rows)], stg_g.at[st],
                                      sems.at[st, 0]),
                pltpu.make_async_copy(wu_hbm.at[expert0 + e, pl.ds(g0, gu_rows)], stg_u.at[st],
                                      sems.at[st, 1]),
                pltpu.make_async_copy(wd_hbm.at[expert0 + e, pl.ds(d0, d_rows)], stg_d.at[st],
                                      sems.at[st, 2]))

    def start(e, k, st):
        for cp in unit_copies(e, k, st):
            cp.start()

    def wait(e, k, st):
        for cp in unit_copies(e, k, st):
            cp.wait()

    def cast_unit(slot, k, st):
        g0 = pl.multiple_of(k * gu_rows, gu_rows)
        d0 = pl.multiple_of(k * d_rows, SEG_ALIGN)
        wg_buf[slot, pl.ds(g0, gu_rows), :] = stg_g[st].astype(_BF16)
        wu_buf[slot, pl.ds(g0, gu_rows), :] = stg_u[st].astype(_BF16)
        wd_buf[slot, pl.ds(d0, d_rows), :] = stg_d[st].astype(_BF16)

    def expert_after(e):
        blk = end_ref[e * n_ch] // row_tile
        return jnp.where(blk < n_valid, be_ref[jnp.minimum(blk, n_valid - 1)], -1)

    def become_resident(slot, e):
        state[ST_SLOT] = slot
        state[ST_RESIDENT] = e
        nxt = expert_after(e)
        state[ST_NEXT] = nxt
        state[ST_DONE] = 0

        @pl.when(nxt >= 0)
        def _():
            start(nxt, 0, 0)
        state[ST_STAGE] = 0

    e_b = be_ref[jnp.minimum(b, n_valid - 1)]

    @pl.when(b == 0)
    def _():
        state[ST_SLOT] = 1
        state[ST_RESIDENT] = -1
        state[ST_NEXT] = e_b
        state[ST_DONE] = 0
        state[ST_STAGE] = 0
        start(e_b, 0, 0)

    s_slot = state[ST_SLOT]
    s_next = state[ST_NEXT]
    s_done = state[ST_DONE]
    s_stage = state[ST_STAGE]
    change = valid & (e_b != state[ST_RESIDENT])
    steady = valid & jnp.logical_not(change) & (s_next >= 0) & (s_done < units)

    @pl.when(change)
    def _():
        other = 1 - s_slot

        @pl.when(s_done < units)
        def _():
            wait(e_b, s_done, s_stage)
            cast_unit(other, s_done, s_stage)

            def body(k, carry):
                start(e_b, k, 0)
                wait(e_b, k, 0)
                cast_unit(other, k, 0)
                return carry
            lax.fori_loop(s_done + 1, units, body, 0)
        become_resident(other, e_b)

    def run_block(slot):
        y_ref[...] = _swiglu(x_ref[...], wg_buf, wu_buf, wd_buf, a_scr, slot).astype(_BF16)

    @pl.when(steady)
    def _():
        wait(s_next, s_done, s_stage)

        @pl.when(s_done + 1 < units)
        def _():
            start(s_next, s_done + 1, 1 - s_stage)
        state[ST_DONE] = s_done + 1
        state[ST_STAGE] = 1 - s_stage
        cast_unit(1 - s_slot, s_done, s_stage)
        run_block(s_slot)

    @pl.when(valid & jnp.logical_not(steady))
    def _():
        run_block(state[ST_SLOT])

    @pl.when(jnp.logical_not(valid))
    def _():
        y_ref[...] = jnp.zeros_like(y_ref)


def _ffn_grouped(x_sorted, wg, wu, wd, expert0, block_expert, n_valid, end_flat, n_ch, row_tile):
    rows, d = x_sorted.shape
    f = wg.shape[2]
    n_blocks = rows // row_tile
    hbm = pl.BlockSpec(memory_space=pl.ANY)
    return pl.pallas_call(
        functools.partial(_ffn_grouped_kernel, n_ch=n_ch, expert0=expert0),
        grid_spec=pltpu.PrefetchScalarGridSpec(
            num_scalar_prefetch=3,
            grid=(n_blocks,),
            in_specs=[pl.BlockSpec((row_tile, d),
                                   lambda b, be, nv, en: (jnp.minimum(b, nv[0] - 1), 0)),
                      hbm, hbm, hbm],
            out_specs=pl.BlockSpec((row_tile, d), lambda b, be, nv, en: (b, 0)),
            scratch_shapes=[pltpu.VMEM((2, d, f), _BF16),
                            pltpu.VMEM((2, d, f), _BF16),
                            pltpu.VMEM((2, f, d), _BF16),
                            pltpu.VMEM((2, d // WEIGHT_UNITS, f), _F32),
                            pltpu.VMEM((2, d // WEIGHT_UNITS, f), _F32),
                            pltpu.VMEM((2, f // WEIGHT_UNITS, d), _F32),
                            pltpu.SemaphoreType.DMA((2, 3)),
                            pltpu.SMEM((5,), _I32),
                            pltpu.VMEM((row_tile, f), _BF16)]),
        out_shape=jax.ShapeDtypeStruct((rows, d), _BF16),
        compiler_params=pltpu.CompilerParams(
            dimension_semantics=("arbitrary",), vmem_limit_bytes=VMEM_LIMIT),
        name="moe_experts",
    )(block_expert, n_valid, end_flat, x_sorted, wg, wu, wd)


def _combine_kernel(w0_ref, sl_ref, res_ref, pos_ref, gate_ref, gfin_ref, y_hbm,
                    out_ref, ybuf, sems, tmp_scr, acc_scr, *, n_e, n_ch):
    c = pl.program_id(0)
    ch = res_ref.shape[0]
    piece = WINDOW_PIECE
    n_pieces = ch // piece

    def piece_copy(cc, slot, e, k):
        row0 = pl.multiple_of(w0_ref[e * n_ch + cc] + k * piece, SEG_ALIGN)
        return pltpu.make_async_copy(
            y_hbm.at[pl.ds(row0, piece)],
            ybuf.at[slot, k, pl.ds(e * piece, piece)], sems.at[slot, e, k])

    def for_each_piece(cc, slot, fn):
        for e in range(n_e):
            fn(piece_copy(cc, slot, e, 0))
            for k in range(1, n_pieces):
                @pl.when(sl_ref[e * n_ch + cc] > k * piece)
                def _(e=e, k=k):
                    fn(piece_copy(cc, slot, e, k))

    @pl.when(c == 0)
    def _():
        ybuf[...] = jnp.zeros_like(ybuf)
        for_each_piece(0, 0, lambda cp: cp.start())

    @pl.when(c + 1 < n_ch)
    def _():
        for_each_piece(c + 1, (c + 1) % 2, lambda cp: cp.start())

    slot = c % 2
    for_each_piece(c, slot, lambda cp: cp.wait())

    per_dot = MXU_DIM // piece
    depth = per_dot * piece
    lane = lax.broadcasted_iota(_I32, (1, depth), 1)
    p1b = jnp.broadcast_to(pos_ref[:, 0:1], (ch, depth))
    p2b = jnp.broadcast_to(pos_ref[:, 1:2], (ch, depth))
    g1b = jnp.broadcast_to(gate_ref[:, 0:1], (ch, depth))
    g2b = jnp.broadcast_to(gate_ref[:, 1:2], (ch, depth))

    def scatter_back(m, k):
        tgt = jnp.full((1, depth), -1, _I32)
        for i in range(per_dot):
            e = m * per_dot + i
            row = lane - i * piece + k * piece
            mine = (lane >= i * piece) & (lane < (i + 1) * piece) & (row < sl_ref[e * n_ch + c])
            tgt = jnp.where(mine, row + w0_ref[e * n_ch + c], tgt)
        q = (jnp.where(p1b == tgt, g1b, 0.0) + jnp.where(p2b == tgt, g2b, 0.0)).astype(_BF16)
        return _dot(q, ybuf[slot, k, m * depth:(m + 1) * depth, :])

    def any_reaches(m, k):
        hit = sl_ref[(m * per_dot) * n_ch + c] > k * piece
        for i in range(1, per_dot):
            hit = jnp.logical_or(hit, sl_ref[(m * per_dot + i) * n_ch + c] > k * piece)
        return hit

    acc = res_ref[...]
    for m in range(n_e // per_dot):
        acc = acc + scatter_back(m, 0)
    acc_scr[...] = acc
    for m in range(n_e // per_dot):
        for k in range(1, n_pieces):
            @pl.when(any_reaches(m, k))
            def _(m=m, k=k):
                acc_scr[...] += scatter_back(m, k)
    normed = _rmsnorm(acc_scr[...], gfin_ref[...])
    steps = ch // SUBLANES
    for k in range(tmp_scr.shape[0]):
        tmp_scr[k] = normed[:, k * LANES:(k + 1) * LANES]
    for bb in range(SUBLANES):
        for k in range(tmp_scr.shape[0]):
            out_ref[bb, :, k * LANES:(k + 1) * LANES] = (
                tmp_scr[k, pl.ds(bb, steps, stride=SUBLANES), :])


def _combine(res, pos_c, gates_c, gfin, y_sorted, w0_flat, seglen_flat, n_e, bsz, seq):
    t_rows, d = res.shape
    ch = TOKEN_CHUNK
    n_ch = t_rows // ch
    steps = ch // bsz
    return pl.pallas_call(
        functools.partial(_combine_kernel, n_e=n_e, n_ch=n_ch),
        grid_spec=pltpu.PrefetchScalarGridSpec(
            num_scalar_prefetch=2,
            grid=(n_ch,),
            in_specs=[pl.BlockSpec((ch, d), lambda c, w0, sl: (c, 0)),
                      pl.BlockSpec((ch, 2), lambda c, w0, sl: (c, 0)),
                      pl.BlockSpec((ch, 2), lambda c, w0, sl: (c, 0)),
                      pl.BlockSpec((1, d), lambda c, w0, sl: (0, 0)),
                      pl.BlockSpec(memory_space=pl.ANY)],
            out_specs=pl.BlockSpec((bsz, steps, d), lambda c, w0, sl: (0, c, 0)),
            scratch_shapes=[pltpu.VMEM((2, ch // WINDOW_PIECE, n_e * WINDOW_PIECE, d), _BF16),
                            pltpu.SemaphoreType.DMA((2, n_e, ch // WINDOW_PIECE)),
                            pltpu.VMEM((d // LANES, ch, LANES), _F32),
                            pltpu.VMEM((ch, d), _F32)]),
        out_shape=jax.ShapeDtypeStruct((bsz, seq, d), _F32),
        compiler_params=pltpu.CompilerParams(
            dimension_semantics=("arbitrary",), vmem_limit_bytes=VMEM_LIMIT),
        name="moe_combine",
    )(w0_flat, seglen_flat, res, pos_c, gates_c, gfin, y_sorted)


def _moe_layer(h, hn, logits_t, wg, wu, wd, expert0, gfin, bsz, seq):
    t_rows, d = h.shape
    n_e = logits_t.shape[0]
    tm = EXPERT_ROW_TILE
    n_ch = t_rows // TOKEN_CHUNK
    max_rows = 2 * t_rows + (SEG_ALIGN - 1) * n_e * n_ch + n_e * (TOKEN_CHUNK + tm)
    n_blocks = -(-max_rows // tm) + 1
    n_blk_pad = -(-n_blocks // LANES) * LANES

    pos, gates, segtab, blktab = _route(logits_t, n_blk_pad, tm)
    pos_l = jnp.transpose(pos.reshape(2, n_ch, TOKEN_CHUNK), (1, 0, 2))
    w0_flat = segtab[0].reshape(-1)
    seglen_flat = segtab[1].reshape(-1)
    n_valid = blktab[1, :1]
    end_flat = segtab[2].reshape(-1)
    x_sorted = _dispatch(hn, pos_l, w0_flat, seglen_flat, end_flat, n_valid,
                         n_e, n_blocks * tm, tm)
    y_sorted = _ffn_grouped(x_sorted, wg, wu, wd, expert0, blktab[0], n_valid, end_flat,
                            n_ch, tm)
    return _combine(h, jnp.transpose(pos), jnp.transpose(gates), gfin, y_sorted,
                    w0_flat, seglen_flat, n_e, bsz, seq)


def _block_diag_groups(w, groups):
    h, k, n = w.shape
    hpg = h // groups
    eye = jnp.eye(hpg, dtype=w.dtype)
    wq = w.reshape(groups, hpg, k, n)
    out = jnp.einsum("qhkn,hj->qhkjn", wq, eye)
    return out.reshape(groups, hpg * k, hpg * n)


def kernel(x, norm_mix_g, w_in, ssm_log_dt, ssm_a_re, ssm_a_im, ssm_b_re, ssm_b_im, ssm_c_re, ssm_c_im, ssm_d, ssm_w_glu, ssm_b_glu, pool_w, pool_scale, w_out, norm_ffn_g, ffn_w_gate, ffn_w_up, ffn_w_down, router_w, moe_w_gate, moe_w_up, moe_w_down, final_norm_g):
    bsz, seq, d = x.shape
    depth, n_heads, n_state, n_grp_ch = ssm_b_re.shape
    t_rows = bsz * seq
    assert bsz == SUBLANES and depth % 2 == 0
    groups = n_heads // HEADS_PER_GROUP

    nh = depth * n_heads
    lam_re, lam_im, bb_re, bb_im = _discretize(
        ssm_log_dt.reshape(nh), ssm_a_re.reshape(nh, n_state), ssm_a_im.reshape(nh, n_state),
        jnp.swapaxes(ssm_b_re, 2, 3).reshape(nh, n_grp_ch, n_state),
        jnp.swapaxes(ssm_b_im, 2, 3).reshape(nh, n_grp_ch, n_state))
    lam_re = lam_re.reshape(depth, groups, 1, HEADS_PER_GROUP * n_state)
    lam_im = lam_im.reshape(depth, groups, 1, HEADS_PER_GROUP * n_state)
    lam = jnp.concatenate([lam_re, lam_im], axis=2).reshape(depth, 2 * groups, -1)
    bb_re = bb_re.reshape(depth, n_heads, n_grp_ch, n_state)
    bb_im = bb_im.reshape(depth, n_heads, n_grp_ch, n_state)

    h = x
    row = lambda v: v.reshape(1, -1)
    out = None
    for i in range(depth):
        bw = jnp.concatenate([_block_diag_groups(bb_re[i], groups),
                              _block_diag_groups(bb_im[i], groups)], axis=2).astype(_BF16)
        cw = jnp.concatenate(
            [_block_diag_groups(jnp.swapaxes(ssm_c_re[i], 1, 2), groups),
             _block_diag_groups(-jnp.swapaxes(ssm_c_im[i], 1, 2), groups)], axis=1).astype(_BF16)
        is_moe = i % 2 == 1
        j = i // 2
        rw = router_w[j] if is_moe else None
        outs = _mix_layer(
            h, row(norm_mix_g[i]), w_in[i].astype(_BF16), bw, lam[i], cw, row(ssm_d[i]),
            ssm_w_glu[i].astype(_BF16), row(ssm_b_glu[i]), pool_w[i].astype(_BF16),
            row(pool_scale[i]), w_out[i].astype(_BF16), row(norm_ffn_g[i]), rw)
        if is_moe:
            assert i == depth - 1
            h, hn, logits_t = outs
            n_e = moe_w_gate.shape[1]
            stack = lambda w: w.reshape((-1,) + w.shape[2:])
            out = _moe_layer(h, hn, logits_t, stack(moe_w_gate), stack(moe_w_up),
                             stack(moe_w_down), j * n_e, row(final_norm_g), bsz, seq)
        else:
            h, hn = outs
            h = _ffn(hn, ffn_w_gate[j][None].astype(_BF16), ffn_w_up[j][None].astype(_BF16),
                     ffn_w_down[j][None].astype(_BF16), h)
    return out
```

```python
import functools
import math

import jax
import jax.numpy as jnp
from jax import lax
from jax.experimental import pallas as pl
from jax.experimental.pallas import tpu as pltpu

RMS_EPS = 1e-6
POOL_WINDOWS = (2, 4, 8, 16)
A_RE_MAX = -1e-4
GELU_C0 = math.sqrt(2.0 / math.pi)
GELU_C1 = 0.044715

SUBLANES = 8
LANES = 128
MXU_DIM = 256

HEADS_PER_GROUP = 16
TIME_TILE = 64
SCAN_COLS = 512
FFN_ROW_TILE = 512
EXPERT_ROW_TILE = 512
FFN_COL_CHUNK = 256
WEIGHT_UNITS = 8
TOKEN_CHUNK = MXU_DIM
SEG_ALIGN = 2 * SUBLANES
WINDOW_PIECE = LANES
VMEM_LIMIT = 56 * 1024 * 1024

_F32 = jnp.float32
_BF16 = jnp.bfloat16
_I32 = jnp.int32


def _dot(a, b):
    return jnp.dot(a, b, preferred_element_type=_F32)


def _dot_nt(a, b):
    return lax.dot_general(a, b, (((1,), (1,)), ((), ())), preferred_element_type=_F32)


def _rmsnorm(x, g):
    inv = lax.rsqrt(jnp.mean(x * x, axis=-1, keepdims=True) + RMS_EPS)
    return x * inv * g


def _sigmoid(x):
    return 1.0 / (1.0 + jnp.exp(-x))


def _ceil_to(x, m):
    return jnp.floor((x + (m - 1.0)) * (1.0 / m)) * m


def _discretize_kernel(log_dt_ref, a_re_ref, a_im_ref, b_re_ref, b_im_ref,
                       lam_re_ref, lam_im_ref, bb_re_ref, bb_im_ref):
    dt = jnp.exp(log_dt_ref[...])
    ar = jnp.minimum(a_re_ref[...], A_RE_MAX)
    ai = a_im_ref[...]
    mag = jnp.exp(ar * dt)
    lam_re = mag * jnp.cos(ai * dt)
    lam_im = mag * jnp.sin(ai * dt)
    den = ar * ar + ai * ai
    nr = lam_re - 1.0
    ni = lam_im
    coef_re = (nr * ar + ni * ai) / den
    coef_im = (ni * ar - nr * ai) / den
    lam_re_ref[...] = lam_re
    lam_im_ref[...] = lam_im
    br = b_re_ref[...]
    bi = b_im_ref[...]
    bb_re_ref[...] = coef_re * br - coef_im * bi
    bb_im_ref[...] = coef_re * bi + coef_im * br


def _discretize(log_dt, a_re, a_im, b_re_t, b_im_t):
    n, g, p = b_re_t.shape
    full3 = lambda s: pl.BlockSpec(s, lambda: (0, 0, 0))
    return pl.pallas_call(
        _discretize_kernel,
        out_shape=(jax.ShapeDtypeStruct((n, 1, p), _F32),
                   jax.ShapeDtypeStruct((n, 1, p), _F32),
                   jax.ShapeDtypeStruct((n, g, p), _F32),
                   jax.ShapeDtypeStruct((n, g, p), _F32)),
        in_specs=[full3((n, 1, 1)), full3((n, 1, p)), full3((n, 1, p)),
                  full3((n, g, p)), full3((n, g, p))],
        out_specs=(full3((n, 1, p)), full3((n, 1, p)),
                   full3((n, g, p)), full3((n, g, p))),
        name="ssm_discretize",
    )(log_dt.reshape(n, 1, 1), a_re.reshape(n, 1, p), a_im.reshape(n, 1, p),
      b_re_t, b_im_t)


def _mix_kernel(*refs, time_tile, with_router, batch_major_in):
    (h_ref, gmix_ref, win_ref, bw_ref, lam_ref, cw_ref, dskip_ref, wglu_ref,
     bglu_ref, poolw_ref, pscale_ref, wout_ref, gffn_ref) = refs[:13]
    refs = refs[13:]
    rw_ref = logits_ref = xin_scr = None
    if with_router:
        rw_ref, hout_ref, hn_ref, logits_ref = refs[:4]
        refs = refs[4:]
    else:
        hout_ref, hn_ref = refs[:2]
        refs = refs[2:]
    s_scr, state_scr, ext_scr, mixed_scr = refs[:4]
    if batch_major_in:
        xin_scr = refs[4]

    step = pl.program_id(0)
    rows = time_tile * SUBLANES
    d_ssm = dskip_ref.shape[1]
    n_groups = bw_ref.shape[0]
    gin = bw_ref.shape[1]
    gstate = bw_ref.shape[2] // 2
    hist = ext_scr.shape[0] - rows

    @pl.when(step == 0)
    def _():
        state_scr[...] = jnp.zeros_like(state_scr)
        ext_scr[0:hist, :] = jnp.zeros((hist, ext_scr.shape[1]), _F32)

    if batch_major_in:
        for bb in range(SUBLANES):
            for k in range(xin_scr.shape[0]):
                xin_scr[k, pl.ds(bb, time_tile, stride=SUBLANES), :] = (
                    h_ref[bb, :, k * LANES:(k + 1) * LANES])
        h = jnp.concatenate([xin_scr[k] for k in range(xin_scr.shape[0])], axis=1)
    else:
        h = h_ref[...]
    hn = _rmsnorm(h, gmix_ref[...]).astype(_BF16)
    proj = _dot(hn, win_ref[...])
    u_ssm = proj[:, :d_ssm]
    ext_scr[hist:, :] = proj[:, d_ssm:]

    for q in range(n_groups):
        ug = u_ssm[:, q * gin:(q + 1) * gin].astype(_BF16)
        s_scr[q] = _dot(ug, bw_ref[q])
        for c in range(gstate // SCAN_COLS):
            re0 = c * SCAN_COLS
            im0 = gstate + c * SCAN_COLS
            lr = jnp.broadcast_to(lam_ref[2 * q:2 * q + 1, re0:re0 + SCAN_COLS],
                                  (SUBLANES, SCAN_COLS))
            li = jnp.broadcast_to(lam_ref[2 * q + 1:2 * q + 2, re0:re0 + SCAN_COLS],
                                  (SUBLANES, SCAN_COLS))
            sre = state_scr[q, :, re0:re0 + SCAN_COLS]
            sim = state_scr[q, :, im0:im0 + SCAN_COLS]
            for t in range(time_tile):
                r0 = t * SUBLANES
                bre = s_scr[q, r0:r0 + SUBLANES, re0:re0 + SCAN_COLS]
                bim = s_scr[q, r0:r0 + SUBLANES, im0:im0 + SCAN_COLS]
                sre, sim = (lr * sre - li * sim + bre, lr * sim + li * sre + bim)
                s_scr[q, r0:r0 + SUBLANES, re0:re0 + SCAN_COLS] = sre
                s_scr[q, r0:r0 + SUBLANES, im0:im0 + SCAN_COLS] = sim
            state_scr[q, :, re0:re0 + SCAN_COLS] = sre
            state_scr[q, :, im0:im0 + SCAN_COLS] = sim
        yq = _dot(s_scr[q].astype(_BF16), cw_ref[q])
        yq = yq + dskip_ref[:, q * gin:(q + 1) * gin] * u_ssm[:, q * gin:(q + 1) * gin]
        hq = 0.5 * yq * (1.0 + jnp.tanh(GELU_C0 * (yq + GELU_C1 * (yq * yq * yq))))
        mixed_scr[:, q * gin:(q + 1) * gin] = hq.astype(_BF16)
    hg = mixed_scr[:, :d_ssm]
    gate = _sigmoid(_dot(hg, wglu_ref[...]) + bglu_ref[...])
    mixed_scr[:, :d_ssm] = (hg.astype(_F32) * gate).astype(_BF16)

    n_ext = rows + hist
    pos = step * time_tile + (lax.broadcasted_iota(_I32, (rows, LANES), 0) // SUBLANES)
    for g, w in enumerate(POOL_WINDOWS):
        c0 = g * LANES
        e = ext_scr[:, c0:c0 + LANES]
        acc = e
        n_acc = n_ext
        span = 1
        while span < w:
            sh = span * SUBLANES
            acc = acc[sh:, :] + acc[:n_acc - sh, :]
            n_acc -= sh
            span *= 2
        wsum = acc[n_acc - rows:, :]
        cnt = jnp.minimum(pos + 1, w).astype(_F32)
        pooled = wsum / cnt - e[hist:, :]
        mg = _dot(pooled.astype(_BF16), poolw_ref[g]) * pscale_ref[:, c0:c0 + LANES]
        mixed_scr[:, d_ssm + c0:d_ssm + c0 + LANES] = mg.astype(_BF16)
    ext_scr[0:hist, :] = ext_scr[rows:rows + hist, :]

    hout = h + _dot(mixed_scr[...], wout_ref[...])
    hout_ref[...] = hout
    hn2 = _rmsnorm(hout, gffn_ref[...])
    hn_hi = hn2.astype(_BF16)
    hn_ref[...] = hn_hi
    if with_router:
        hn_lo = (hn2 - hn_hi.astype(_F32)).astype(_BF16)
        rw = rw_ref[...]
        rw_hi = rw.astype(_BF16)
        rw_lo = (rw - rw_hi.astype(_F32)).astype(_BF16)
        logits_ref[...] = (_dot_nt(rw_hi, hn_hi) + _dot_nt(rw_hi, hn_lo)
                           + _dot_nt(rw_lo, hn_hi))


def _mix_layer(h, gmix, w_in, bw, lam, cw, dskip, wglu, bglu, poolw, pscale, wout,
               gffn, router_w):
    batch_major_in = h.ndim == 3
    d = h.shape[-1]
    t_rows = h.size // d
    rows = TIME_TILE * SUBLANES
    n_steps = t_rows // rows
    d_ssm = dskip.shape[1]
    d_pool = pscale.shape[1]
    hist = max(POOL_WINDOWS) * SUBLANES
    with_router = router_w is not None

    def const(a):
        nd = a.ndim
        return pl.BlockSpec(a.shape, lambda i, nd=nd: (0,) * nd)

    row_blk = lambda width: pl.BlockSpec((rows, width), lambda i: (i, 0))
    ins = [h, gmix, w_in, bw, lam, cw, dskip, wglu, bglu, poolw, pscale, wout, gffn]
    h_spec = (pl.BlockSpec((h.shape[0], TIME_TILE, d), lambda i: (0, i, 0))
              if batch_major_in else row_blk(d))
    in_specs = [h_spec] + [const(a) for a in ins[1:]]
    out_shape = [jax.ShapeDtypeStruct((t_rows, d), _F32),
                 jax.ShapeDtypeStruct((t_rows, d), _BF16)]
    out_specs = [row_blk(d), row_blk(d)]
    if with_router:
        n_e = router_w.shape[1]
        rw = jnp.transpose(router_w)
        ins.append(rw)
        in_specs.append(const(rw))
        out_shape.append(jax.ShapeDtypeStruct((n_e, t_rows), _F32))
        out_specs.append(pl.BlockSpec((n_e, rows), lambda i: (0, i)))
    scratch = [
        pltpu.VMEM((bw.shape[0], rows, bw.shape[2]), _F32),
        pltpu.VMEM((bw.shape[0], SUBLANES, bw.shape[2]), _F32),
        pltpu.VMEM((rows + hist, d_pool), _F32),
        pltpu.VMEM((rows, d_ssm + d_pool), _BF16),
    ]
    if batch_major_in:
        scratch.append(pltpu.VMEM((d // LANES, rows, LANES), _F32))
    return pl.pallas_call(
        functools.partial(_mix_kernel, time_tile=TIME_TILE, with_router=with_router,
                          batch_major_in=batch_major_in),
        grid=(n_steps,),
        in_specs=in_specs,
        out_specs=out_specs,
        out_shape=out_shape,
        scratch_shapes=scratch,
        compiler_params=pltpu.CompilerParams(
            dimension_semantics=("arbitrary",), vmem_limit_bytes=VMEM_LIMIT),
        name="mix_router" if with_router else "mix",
    )(*ins)


def _swiglu(x, wg_ref, wu_ref, wd_ref, a_scr, slot):
    tf = wg_ref.shape[2]
    c0 = 0
    while c0 < tf:
        cw = min(FFN_COL_CHUNK, tf - c0)
        g = _dot(x, wg_ref[slot, :, c0:c0 + cw])
        u = _dot(x, wu_ref[slot, :, c0:c0 + cw])
        a_scr[:, c0:c0 + cw] = (g * _sigmoid(g) * u).astype(_BF16)
        c0 += cw
    return _dot(a_scr[...], wd_ref[slot])


def _ffn_kernel(x_ref, wg_ref, wu_ref, wd_ref, res_ref, out_ref, a_scr):
    out_ref[...] = res_ref[...] + _swiglu(x_ref[...], wg_ref, wu_ref, wd_ref, a_scr, 0)


def _ffn(x, wg, wu, wd, res):
    t_rows, d = x.shape
    f = wg.shape[2]
    tm = FFN_ROW_TILE
    held = lambda shape: pl.BlockSpec(shape, lambda i: (0, 0, 0), pipeline_mode=pl.Buffered(1))
    return pl.pallas_call(
        _ffn_kernel,
        grid=(t_rows // tm,),
        in_specs=[
            pl.BlockSpec((tm, d), lambda i: (i, 0)),
            held((1, d, f)), held((1, d, f)), held((1, f, d)),
            pl.BlockSpec((tm, d), lambda i: (i, 0)),
        ],
        out_specs=pl.BlockSpec((tm, d), lambda i: (i, 0)),
        out_shape=jax.ShapeDtypeStruct((t_rows, d), _F32),
        scratch_shapes=[pltpu.VMEM((tm, f), _BF16)],
        compiler_params=pltpu.CompilerParams(
            dimension_semantics=("arbitrary",), vmem_limit_bytes=VMEM_LIMIT),
        name="ffn_dense",
    )(x, wg, wu, wd, res)


def _route_kernel(lt_ref, pos_ref, gates_ref, segtab_ref, blktab_ref, *, n_blk, row_tile):
    l = lt_ref[...]
    n_e, t_rows = l.shape
    ch = TOKEN_CHUNK
    n_ch = t_rows // ch
    ie = lax.broadcasted_iota(_I32, l.shape, 0)
    m1 = jnp.max(l, axis=0, keepdims=True)
    i1 = jnp.min(jnp.where(l == m1, ie, n_e), axis=0, keepdims=True)
    l2 = jnp.where(ie == i1, -jnp.inf, l)
    m2 = jnp.max(l2, axis=0, keepdims=True)
    i2 = jnp.min(jnp.where(l2 == m2, ie, n_e), axis=0, keepdims=True)
    e2 = jnp.exp(m2 - m1)
    den = 1.0 + e2
    gates_ref[0:1, :] = 1.0 / den
    gates_ref[1:2, :] = e2 / den
    sel1 = ie == i1
    sel2 = ie == i2
    chosen = jnp.where(sel1 | sel2, 1.0, 0.0).astype(_BF16)

    chunk_of_t = lax.broadcasted_iota(_I32, (n_ch, t_rows), 1) // ch
    in_chunk = jnp.where(chunk_of_t == lax.broadcasted_iota(_I32, (n_ch, t_rows), 0),
                         1.0, 0.0).astype(_BF16)
    n_ec = _dot_nt(chosen, in_chunk)
    n_ce = _dot_nt(in_chunk, chosen)
    cc_r = lax.broadcasted_iota(_I32, (n_ch, n_ch), 0)
    cc_c = lax.broadcasted_iota(_I32, (n_ch, n_ch), 1)
    before = jnp.where(cc_r < cc_c, 1.0, 0.0).astype(_BF16)
    after = jnp.where(cc_c < cc_r, 1.0, 0.0).astype(_BF16)
    segoff_ec = _dot(n_ec.astype(_BF16), before)
    segoff_ce = _dot(after, n_ce.astype(_BF16))
    def window(segoff, n):
        start = jnp.floor(segoff * (1.0 / SEG_ALIGN)) * SEG_ALIGN
        used = segoff - start + n
        return start, used, start + _ceil_to(jnp.maximum(used, 1.0), float(WINDOW_PIECE))

    wstart_ec, used_ec, wend_ec = window(segoff_ec, n_ec)
    _, _, wend_ce = window(segoff_ce, n_ce)
    tot_col = _ceil_to(jnp.max(wend_ec, axis=1, keepdims=True), float(row_tile))
    tot_row = _ceil_to(jnp.max(wend_ce, axis=0, keepdims=True), float(row_tile))
    ee_r = lax.broadcasted_iota(_I32, (n_e, n_e), 0)
    ee_c = lax.broadcasted_iota(_I32, (n_e, n_e), 1)
    base_col = jnp.sum(jnp.where(ee_c < ee_r, tot_row, 0.0), axis=1, keepdims=True)
    w0_ec = base_col + segoff_ec
    end_col = base_col + tot_col
    segtab_ref[0] = (base_col + wstart_ec).astype(_I32)
    segtab_ref[1] = used_ec.astype(_I32)
    segtab_ref[2] = jnp.broadcast_to(end_col, (n_e, n_ch)).astype(_I32)

    tt_r = lax.broadcasted_iota(_I32, (ch, ch), 0)
    tt_c = lax.broadcasted_iota(_I32, (ch, ch), 1)
    earlier = jnp.where(tt_r < tt_c, 1.0, 0.0).astype(_BF16)
    for c in range(n_ch):
        cols = slice(c * ch, (c + 1) * ch)
        rank = _dot(chosen[:, cols], earlier)
        offs = w0_ec[:, c:c + 1] + rank
        p1 = jnp.sum(jnp.where(sel1[:, cols], offs, 0.0), axis=0, keepdims=True)
        p2 = jnp.sum(jnp.where(sel2[:, cols], offs, 0.0), axis=0, keepdims=True)
        pos_ref[0:1, cols] = p1.astype(_I32)
        pos_ref[1:2, cols] = p2.astype(_I32)

    start = (lax.broadcasted_iota(_I32, (1, n_blk), 1) * row_tile).astype(_F32)
    owner = jnp.sum(jnp.where(end_col <= start, 1.0, 0.0), axis=0, keepdims=True)
    n_valid = jnp.sum(tot_col, axis=0, keepdims=True) * (1.0 / row_tile)
    blktab_ref[0:1, :] = jnp.minimum(owner, n_e - 1.0).astype(_I32)
    blktab_ref[1:2, :] = jnp.broadcast_to(n_valid, (1, n_blk)).astype(_I32)


def _route(logits_t, n_blk, row_tile):
    n_e, t_rows = logits_t.shape
    n_ch = t_rows // TOKEN_CHUNK
    full = lambda s: pl.BlockSpec(s, lambda: (0,) * len(s))
    return pl.pallas_call(
        functools.partial(_route_kernel, n_blk=n_blk, row_tile=row_tile),
        in_specs=[full((n_e, t_rows))],
        out_specs=(full((2, t_rows)), full((2, t_rows)), full((3, n_e, n_ch)), full((2, n_blk))),
        out_shape=(jax.ShapeDtypeStruct((2, t_rows), _I32),
                   jax.ShapeDtypeStruct((2, t_rows), _F32),
                   jax.ShapeDtypeStruct((3, n_e, n_ch), _I32),
                   jax.ShapeDtypeStruct((2, n_blk), _I32)),
        compiler_params=pltpu.CompilerParams(vmem_limit_bytes=VMEM_LIMIT),
        name="route_sort",
    )(logits_t)


def _dispatch_kernel(w0_ref, sl_ref, end_ref, nv_ref, hn_ref, posl_ref, x_hbm,
                     stage, sems, zero_scr, zsems, tsems, carry, *, n_e, n_ch, row_tile, min_blocks):
    c = pl.program_id(0)
    ch = hn_ref.shape[0]
    piece = WINDOW_PIECE
    n_pieces = ch // piece
    slot = c % 2
    n_blocks = x_hbm.shape[0] // row_tile

    def for_each_unused_block(fn):
        for b in range(min_blocks, n_blocks):
            @pl.when(b >= nv_ref[0])
            def _(b=b):
                fn(pltpu.make_async_copy(zero_scr, x_hbm.at[pl.ds(b * row_tile, row_tile)],
                                         tsems.at[b - min_blocks]))

    def piece_copy(cc, sl, e, k):
        row0 = pl.multiple_of(w0_ref[e * n_ch + cc] + k * piece, SEG_ALIGN)
        return pltpu.make_async_copy(
            stage.at[sl, k, pl.ds(e * piece, piece)],
            x_hbm.at[pl.ds(row0, piece)], sems.at[sl, e, k])

    def for_each_piece(cc, sl, fn):
        for e in range(n_e):
            fn(piece_copy(cc, sl, e, 0))
            for k in range(1, n_pieces):
                @pl.when(sl_ref[e * n_ch + cc] > k * piece)
                def _(e=e, k=k):
                    fn(piece_copy(cc, sl, e, k))

    @pl.when(c == 0)
    def _():
        zero_scr[...] = jnp.zeros_like(zero_scr)
        carry[...] = jnp.zeros_like(carry)
        fills = [pltpu.make_async_copy(
            zero_scr,
            x_hbm.at[pl.ds(pl.multiple_of(end_ref[e * n_ch] - row_tile, SEG_ALIGN), row_tile)],
            zsems.at[e]) for e in range(n_e)]
        for cp in fills:
            cp.start()
        for_each_unused_block(lambda cp: cp.start())
        for cp in fills:
            cp.wait()

    pp = posl_ref[0]
    hn = hn_ref[...]

    def gather_piece(k):
        row_j = lax.broadcasted_iota(_I32, (piece, ch), 0) + k * piece
        parts = []
        for e in range(n_e):
            row_id = row_j + w0_ref[e * n_ch + c]
            hit = (pp[0:1, :] == row_id) | (pp[1:2, :] == row_id)
            parts.append(jnp.where(hit, 1.0, 0.0).astype(_BF16))
        onehot = jnp.concatenate(parts, axis=0)
        stage[slot, k] = _dot(onehot, hn).astype(_BF16)

    gather_piece(0)
    for e in range(n_e):
        head = slice(e * piece, e * piece + SEG_ALIGN)
        stage[slot, 0, head, :] = stage[slot, 0, head, :] + carry[e]
    for k in range(1, n_pieces):
        reaches = sl_ref[c] > k * piece
        for e in range(1, n_e):
            reaches = jnp.logical_or(reaches, sl_ref[e * n_ch + c] > k * piece)

        @pl.when(reaches)
        def _(k=k):
            gather_piece(k)
    for e in range(n_e):
        used = sl_ref[e * n_ch + c]
        gathered = ((jnp.maximum(used, 1) + piece - 1) // piece) * piece
        off = (used // SEG_ALIGN) * SEG_ALIGN
        src = jnp.minimum(off, gathered - SEG_ALIGN)
        grp = stage[slot, src // piece,
                    pl.ds(pl.multiple_of(e * piece + src % piece, SEG_ALIGN), SEG_ALIGN), :]
        carry[e] = jnp.where(off < gathered, grp, jnp.zeros_like(grp))

    @pl.when(c > 0)
    def _():
        for_each_piece(c - 1, 1 - slot, lambda cp: cp.wait())

    for_each_piece(c, slot, lambda cp: cp.start())

    @pl.when(c == n_ch - 1)
    def _():
        for_each_piece(c, slot, lambda cp: cp.wait())
        for_each_unused_block(lambda cp: cp.wait())


def _dispatch(hn, pos_l, w0_flat, seglen_flat, end_flat, n_valid, n_e, n_rows, row_tile):
    t_rows, d = hn.shape
    ch = TOKEN_CHUNK
    n_ch = t_rows // ch
    n_pieces = ch // WINDOW_PIECE
    min_blocks = 2 * t_rows // row_tile
    return pl.pallas_call(
        functools.partial(_dispatch_kernel, n_e=n_e, n_ch=n_ch, row_tile=row_tile,
                          min_blocks=min_blocks),
        grid_spec=pltpu.PrefetchScalarGridSpec(
            num_scalar_prefetch=4,
            grid=(n_ch,),
            in_specs=[pl.BlockSpec((ch, d), lambda c, *_: (c, 0)),
                      pl.BlockSpec((1, 2, ch), lambda c, *_: (c, 0, 0))],
            out_specs=pl.BlockSpec(memory_space=pl.ANY),
            scratch_shapes=[pltpu.VMEM((2, n_pieces, n_e * WINDOW_PIECE, d), _BF16),
                            pltpu.SemaphoreType.DMA((2, n_e, n_pieces)),
                            pltpu.VMEM((row_tile, d), _BF16),
                            pltpu.SemaphoreType.DMA((n_e,)),
                            pltpu.SemaphoreType.DMA((n_rows // row_tile - min_blocks,)),
                            pltpu.VMEM((n_e, SEG_ALIGN, d), _BF16)]),
        out_shape=jax.ShapeDtypeStruct((n_rows, d), _BF16),
        compiler_params=pltpu.CompilerParams(
            dimension_semantics=("arbitrary",), vmem_limit_bytes=VMEM_LIMIT),
        name="moe_dispatch",
    )(w0_flat, seglen_flat, end_flat, n_valid, hn, pos_l)


def _ffn_grouped_kernel(be_ref, nv_ref, end_ref, x_ref, wg_hbm, wu_hbm, wd_hbm, y_ref,
                        wg_buf, wu_buf, wd_buf, stg_g, stg_u, stg_d, sems, state, a_scr,
                        *, n_ch, expert0):
    b = pl.program_id(0)
    n_valid = nv_ref[0]
    valid = b < n_valid
    row_tile = x_ref.shape[0]
    units = WEIGHT_UNITS
    gu_rows = wg_buf.shape[1] // units
    d_rows = wd_buf.shape[1] // units
    ST_SLOT, ST_RESIDENT, ST_NEXT, ST_DONE, ST_STAGE = range(5)

    def unit_copies(e, k, st):
        g0 = pl.multiple_of(k * gu_rows, gu_rows)
        d0 = pl.multiple_of(k * d_rows, SEG_ALIGN)
        return (pltpu.make_async_copy(wg_hbm.at[expert0 + e, pl.ds(g0, gu_rows)], stg_g.at[st],
                                      sems.at[st, 0]),
                pltpu.make_async_copy(wu_hbm.at[expert0 + e, pl.ds(g0, gu_rows)], stg_u.at[st],
                                      sems.at[st, 1]),
                pltpu.make_async_copy(wd_hbm.at[expert0 + e, pl.ds(d0, d_rows)], stg_d.at[st],
                                      sems.at[st, 2]))

    def start(e, k, st):
        for cp in unit_copies(e, k, st):
            cp.start()

    def wait(e, k, st):
        for cp in unit_copies(e, k, st):
            cp.wait()

    def cast_unit(slot, k, st):
        g0 = pl.multiple_of(k * gu_rows, gu_rows)
        d0 = pl.multiple_of(k * d_rows, SEG_ALIGN)
        wg_buf[slot, pl.ds(g0, gu_rows), :] = stg_g[st].astype(_BF16)
        wu_buf[slot, pl.ds(g0, gu_rows), :] = stg_u[st].astype(_BF16)
        wd_buf[slot, pl.ds(d0, d_rows), :] = stg_d[st].astype(_BF16)

    def expert_after(e):
        blk = end_ref[e * n_ch] // row_tile
        return jnp.where(blk < n_valid, be_ref[jnp.minimum(blk, n_valid - 1)], -1)

    def become_resident(slot, e):
        state[ST_SLOT] = slot
        state[ST_RESIDENT] = e
        nxt = expert_after(e)
        state[ST_NEXT] = nxt
        state[ST_DONE] = 0

        @pl.when(nxt >= 0)
        def _():
            start(nxt, 0, 0)
        state[ST_STAGE] = 0

    e_b = be_ref[jnp.minimum(b, n_valid - 1)]

    @pl.when(b == 0)
    def _():
        state[ST_SLOT] = 1
        state[ST_RESIDENT] = -1
        state[ST_NEXT] = e_b
        state[ST_DONE] = 0
        state[ST_STAGE] = 0
        start(e_b, 0, 0)

    s_slot = state[ST_SLOT]
    s_next = state[ST_NEXT]
    s_done = state[ST_DONE]
    s_stage = state[ST_STAGE]
    change = valid & (e_b != state[ST_RESIDENT])
    steady = valid & jnp.logical_not(change) & (s_next >= 0) & (s_done < units)

    @pl.when(change)
    def _():
        other = 1 - s_slot

        @pl.when(s_done < units)
        def _():
            wait(e_b, s_done, s_stage)
            cast_unit(other, s_done, s_stage)

            def body(k, carry):
                start(e_b, k, 0)
                wait(e_b, k, 0)
                cast_unit(other, k, 0)
                return carry
            lax.fori_loop(s_done + 1, units, body, 0)
        become_resident(other, e_b)

    def run_block(slot):
        y_ref[...] = _swiglu(x_ref[...], wg_buf, wu_buf, wd_buf, a_scr, slot).astype(_BF16)

    @pl.when(steady)
    def _():
        wait(s_next, s_done, s_stage)

        @pl.when(s_done + 1 < units)
        def _():
            start(s_next, s_done + 1, 1 - s_stage)
        state[ST_DONE] = s_done + 1
        state[ST_STAGE] = 1 - s_stage
        cast_unit(1 - s_slot, s_done, s_stage)
        run_block(s_slot)

    @pl.when(valid & jnp.logical_not(steady))
    def _():
        run_block(state[ST_SLOT])

    @pl.when(jnp.logical_not(valid))
    def _():
        y_ref[...] = jnp.zeros_like(y_ref)


def _ffn_grouped(x_sorted, wg, wu, wd, expert0, block_expert, n_valid, end_flat, n_ch, row_tile):
    rows, d = x_sorted.shape
    f = wg.shape[2]
    n_blocks = rows // row_tile
    hbm = pl.BlockSpec(memory_space=pl.ANY)
    return pl.pallas_call(
        functools.partial(_ffn_grouped_kernel, n_ch=n_ch, expert0=expert0),
        grid_spec=pltpu.PrefetchScalarGridSpec(
            num_scalar_prefetch=3,
            grid=(n_blocks,),
            in_specs=[pl.BlockSpec((row_tile, d),
                                   lambda b, be, nv, en: (jnp.minimum(b, nv[0] - 1), 0)),
                      hbm, hbm, hbm],
            out_specs=pl.BlockSpec((row_tile, d), lambda b, be, nv, en: (b, 0)),
            scratch_shapes=[pltpu.VMEM((2, d, f), _BF16),
                            pltpu.VMEM((2, d, f), _BF16),
                            pltpu.VMEM((2, f, d), _BF16),
                            pltpu.VMEM((2, d // WEIGHT_UNITS, f), _F32),
                            pltpu.VMEM((2, d // WEIGHT_UNITS, f), _F32),
                            pltpu.VMEM((2, f // WEIGHT_UNITS, d), _F32),
                            pltpu.SemaphoreType.DMA((2, 3)),
                            pltpu.SMEM((5,), _I32),
                            pltpu.VMEM((row_tile, f), _BF16)]),
        out_shape=jax.ShapeDtypeStruct((rows, d), _BF16),
        compiler_params=pltpu.CompilerParams(
            dimension_semantics=("arbitrary",), vmem_limit_bytes=VMEM_LIMIT),
        name="moe_experts",
    )(block_expert, n_valid, end_flat, x_sorted, wg, wu, wd)


def _combine_kernel(w0_ref, sl_ref, res_ref, pos_ref, gate_ref, gfin_ref, y_hbm,
                    out_ref, ybuf, sems, tmp_scr, acc_scr, *, n_e, n_ch):
    c = pl.program_id(0)
    ch = res_ref.shape[0]
    piece = WINDOW_PIECE
    n_pieces = ch // piece

    def piece_copy(cc, slot, e, k):
        row0 = pl.multiple_of(w0_ref[e * n_ch + cc] + k * piece, SEG_ALIGN)
        return pltpu.make_async_copy(
            y_hbm.at[pl.ds(row0, piece)],
            ybuf.at[slot, k, pl.ds(e * piece, piece)], sems.at[slot, e, k])

    def for_each_piece(cc, slot, fn):
        for e in range(n_e):
            fn(piece_copy(cc, slot, e, 0))
            for k in range(1, n_pieces):
                @pl.when(sl_ref[e * n_ch + cc] > k * piece)
                def _(e=e, k=k):
                    fn(piece_copy(cc, slot, e, k))

    @pl.when(c == 0)
    def _():
        ybuf[...] = jnp.zeros_like(ybuf)
        for_each_piece(0, 0, lambda cp: cp.start())

    @pl.when(c + 1 < n_ch)
    def _():
        for_each_piece(c + 1, (c + 1) % 2, lambda cp: cp.start())

    slot = c % 2
    for_each_piece(c, slot, lambda cp: cp.wait())

    per_dot = MXU_DIM // piece
    depth = per_dot * piece
    lane = lax.broadcasted_iota(_I32, (1, depth), 1)
    p1b = jnp.broadcast_to(pos_ref[:, 0:1], (ch, depth))
    p2b = jnp.broadcast_to(pos_ref[:, 1:2], (ch, depth))
    g1b = jnp.broadcast_to(gate_ref[:, 0:1], (ch, depth))
    g2b = jnp.broadcast_to(gate_ref[:, 1:2], (ch, depth))

    def scatter_back(m, k):
        tgt = jnp.full((1, depth), -1, _I32)
        for i in range(per_dot):
            e = m * per_dot + i
            row = lane - i * piece + k * piece
            mine = (lane >= i * piece) & (lane < (i + 1) * piece) & (row < sl_ref[e * n_ch + c])
            tgt = jnp.where(mine, row + w0_ref[e * n_ch + c], tgt)
        q = (jnp.where(p1b == tgt, g1b, 0.0) + jnp.where(p2b == tgt, g2b, 0.0)).astype(_BF16)
        return _dot(q, ybuf[slot, k, m * depth:(m + 1) * depth, :])

    def any_reaches(m, k):
        hit = sl_ref[(m * per_dot) * n_ch + c] > k * piece
        for i in range(1, per_dot):
            hit = jnp.logical_or(hit, sl_ref[(m * per_dot + i) * n_ch + c] > k * piece)
        return hit

    acc = res_ref[...]
    for m in range(n_e // per_dot):
        acc = acc + scatter_back(m, 0)
    acc_scr[...] = acc
    for m in range(n_e // per_dot):
        for k in range(1, n_pieces):
            @pl.when(any_reaches(m, k))
            def _(m=m, k=k):
                acc_scr[...] += scatter_back(m, k)
    normed = _rmsnorm(acc_scr[...], gfin_ref[...])
    steps = ch // SUBLANES
    for k in range(tmp_scr.shape[0]):
        tmp_scr[k] = normed[:, k * LANES:(k + 1) * LANES]
    for bb in range(SUBLANES):
        for k in range(tmp_scr.shape[0]):
            out_ref[bb, :, k * LANES:(k + 1) * LANES] = (
                tmp_scr[k, pl.ds(bb, steps, stride=SUBLANES), :])


def _combine(res, pos_c, gates_c, gfin, y_sorted, w0_flat, seglen_flat, n_e, bsz, seq):
    t_rows, d = res.shape
    ch = TOKEN_CHUNK
    n_ch = t_rows // ch
    steps = ch // bsz
    return pl.pallas_call(
        functools.partial(_combine_kernel, n_e=n_e, n_ch=n_ch),
        grid_spec=pltpu.PrefetchScalarGridSpec(
            num_scalar_prefetch=2,
            grid=(n_ch,),
            in_specs=[pl.BlockSpec((ch, d), lambda c, w0, sl: (c, 0)),
                      pl.BlockSpec((ch, 2), lambda c, w0, sl: (c, 0)),
                      pl.BlockSpec((ch, 2), lambda c, w0, sl: (c, 0)),
                      pl.BlockSpec((1, d), lambda c, w0, sl: (0, 0)),
                      pl.BlockSpec(memory_space=pl.ANY)],
            out_specs=pl.BlockSpec((bsz, steps, d), lambda c, w0, sl: (0, c, 0)),
            scratch_shapes=[pltpu.VMEM((2, ch // WINDOW_PIECE, n_e * WINDOW_PIECE, d), _BF16),
                            pltpu.SemaphoreType.DMA((2, n_e, ch // WINDOW_PIECE)),
                            pltpu.VMEM((d // LANES, ch, LANES), _F32),
                            pltpu.VMEM((ch, d), _F32)]),
        out_shape=jax.ShapeDtypeStruct((bsz, seq, d), _F32),
        compiler_params=pltpu.CompilerParams(
            dimension_semantics=("arbitrary",), vmem_limit_bytes=VMEM_LIMIT),
        name="moe_combine",
    )(w0_flat, seglen_flat, res, pos_c, gates_c, gfin, y_sorted)


def _moe_layer(h, hn, logits_t, wg, wu, wd, expert0, gfin, bsz, seq):
    t_rows, d = h.shape
    n_e = logits_t.shape[0]
    tm = EXPERT_ROW_TILE
    n_ch = t_rows // TOKEN_CHUNK
    max_rows = 2 * t_rows + n_e * (SEG_ALIGN + TOKEN_CHUNK + tm)
    n_blocks = -(-max_rows // tm)
    n_blk_pad = -(-n_blocks // LANES) * LANES

    pos, gates, segtab, blktab = _route(logits_t, n_blk_pad, tm)
    pos_l = jnp.transpose(pos.reshape(2, n_ch, TOKEN_CHUNK), (1, 0, 2))
    w0_flat = segtab[0].reshape(-1)
    seglen_flat = segtab[1].reshape(-1)
    n_valid = blktab[1, :1]
    end_flat = segtab[2].reshape(-1)
    x_sorted = _dispatch(hn, pos_l, w0_flat, seglen_flat, end_flat, n_valid,
                         n_e, n_blocks * tm, tm)
    y_sorted = _ffn_grouped(x_sorted, wg, wu, wd, expert0, blktab[0], n_valid, end_flat,
                            n_ch, tm)
    return _combine(h, jnp.transpose(pos), jnp.transpose(gates), gfin, y_sorted,
                    w0_flat, seglen_flat, n_e, bsz, seq)


def _block_diag_groups(w, groups):
    h, k, n = w.shape
    hpg = h // groups
    eye = jnp.eye(hpg, dtype=w.dtype)
    wq = w.reshape(groups, hpg, k, n)
    out = jnp.einsum("qhkn,hj->qhkjn", wq, eye)
    return out.reshape(groups, hpg * k, hpg * n)


def kernel(x, norm_mix_g, w_in, ssm_log_dt, ssm_a_re, ssm_a_im, ssm_b_re, ssm_b_im, ssm_c_re, ssm_c_im, ssm_d, ssm_w_glu, ssm_b_glu, pool_w, pool_scale, w_out, norm_ffn_g, ffn_w_gate, ffn_w_up, ffn_w_down, router_w, moe_w_gate, moe_w_up, moe_w_down, final_norm_g):
    bsz, seq, d = x.shape
    depth, n_heads, n_state, n_grp_ch = ssm_b_re.shape
    t_rows = bsz * seq
    assert bsz == SUBLANES and depth % 2 == 0
    groups = n_heads // HEADS_PER_GROUP

    nh = depth * n_heads
    lam_re, lam_im, bb_re, bb_im = _discretize(
        ssm_log_dt.reshape(nh), ssm_a_re.reshape(nh, n_state), ssm_a_im.reshape(nh, n_state),
        jnp.swapaxes(ssm_b_re, 2, 3).reshape(nh, n_grp_ch, n_state),
        jnp.swapaxes(ssm_b_im, 2, 3).reshape(nh, n_grp_ch, n_state))
    lam_re = lam_re.reshape(depth, groups, 1, HEADS_PER_GROUP * n_state)
    lam_im = lam_im.reshape(depth, groups, 1, HEADS_PER_GROUP * n_state)
    lam = jnp.concatenate([lam_re, lam_im], axis=2).reshape(depth, 2 * groups, -1)
    bb_re = bb_re.reshape(depth, n_heads, n_grp_ch, n_state)
    bb_im = bb_im.reshape(depth, n_heads, n_grp_ch, n_state)

    h = x
    row = lambda v: v.reshape(1, -1)
    out = None
    for i in range(depth):
        bw = jnp.concatenate([_block_diag_groups(bb_re[i], groups),
                              _block_diag_groups(bb_im[i], groups)], axis=2).astype(_BF16)
        cw = jnp.concatenate(
            [_block_diag_groups(jnp.swapaxes(ssm_c_re[i], 1, 2), groups),
             _block_diag_groups(-jnp.swapaxes(ssm_c_im[i], 1, 2), groups)], axis=1).astype(_BF16)
        is_moe = i % 2 == 1
        j = i // 2
        rw = router_w[j] if is_moe else None
        outs = _mix_layer(
            h, row(norm_mix_g[i]), w_in[i].astype(_BF16), bw, lam[i], cw, row(ssm_d[i]),
            ssm_w_glu[i].astype(_BF16), row(ssm_b_glu[i]), pool_w[i].astype(_BF16),
            row(pool_scale[i]), w_out[i].astype(_BF16), row(norm_ffn_g[i]), rw)
        if is_moe:
            assert i == depth - 1
            h, hn, logits_t = outs
            n_e = moe_w_gate.shape[1]
            stack = lambda w: w.reshape((-1,) + w.shape[2:])
            out = _moe_layer(h, hn, logits_t, stack(moe_w_gate), stack(moe_w_up),
                             stack(moe_w_down), j * n_e, row(final_norm_g), bsz, seq)
        else:
            h, hn = outs
            h = _ffn(hn, ffn_w_gate[j][None].astype(_BF16), ffn_w_up[j][None].astype(_BF16),
                     ffn_w_down[j][None].astype(_BF16), h)
    return out
```

```python
import functools
import math

import jax
import jax.numpy as jnp
from jax import lax
from jax.experimental import pallas as pl
from jax.experimental.pallas import tpu as pltpu

RMS_EPS = 1e-6
POOL_WINDOWS = (2, 4, 8, 16)
A_RE_MAX = -1e-4
GELU_C0 = math.sqrt(2.0 / math.pi)
GELU_C1 = 0.044715

SUBLANES = 8
LANES = 128
MXU_DIM = 256

HEADS_PER_GROUP = 16
TIME_TILE = 64
SCAN_COLS = 512
FFN_ROW_TILE = 512
EXPERT_ROW_TILE = 512
FFN_COL_CHUNK = 256
WEIGHT_UNITS = 8
TOKEN_CHUNK = MXU_DIM
SEG_ALIGN = 2 * SUBLANES
WINDOW_PIECE = LANES
VMEM_LIMIT = 56 * 1024 * 1024

_F32 = jnp.float32
_BF16 = jnp.bfloat16
_I32 = jnp.int32


def _dot(a, b):
    return jnp.dot(a, b, preferred_element_type=_F32)


def _dot_nt(a, b):
    return lax.dot_general(a, b, (((1,), (1,)), ((), ())), preferred_element_type=_F32)


def _rmsnorm(x, g):
    inv = lax.rsqrt(jnp.mean(x * x, axis=-1, keepdims=True) + RMS_EPS)
    return x * inv * g


def _sigmoid(x):
    return 1.0 / (1.0 + jnp.exp(-x))


def _ceil_to(x, m):
    return jnp.floor((x + (m - 1.0)) * (1.0 / m)) * m


def _discretize_kernel(log_dt_ref, a_re_ref, a_im_ref, b_re_ref, b_im_ref,
                       lam_re_ref, lam_im_ref, bb_re_ref, bb_im_ref):
    dt = jnp.exp(log_dt_ref[...])
    ar = jnp.minimum(a_re_ref[...], A_RE_MAX)
    ai = a_im_ref[...]
    mag = jnp.exp(ar * dt)
    lam_re = mag * jnp.cos(ai * dt)
    lam_im = mag * jnp.sin(ai * dt)
    den = ar * ar + ai * ai
    nr = lam_re - 1.0
    ni = lam_im
    coef_re = (nr * ar + ni * ai) / den
    coef_im = (ni * ar - nr * ai) / den
    lam_re_ref[...] = lam_re
    lam_im_ref[...] = lam_im
    br = b_re_ref[...]
    bi = b_im_ref[...]
    bb_re_ref[...] = coef_re * br - coef_im * bi
    bb_im_ref[...] = coef_re * bi + coef_im * br


def _discretize(log_dt, a_re, a_im, b_re_t, b_im_t):
    n, g, p = b_re_t.shape
    full3 = lambda s: pl.BlockSpec(s, lambda: (0, 0, 0))
    return pl.pallas_call(
        _discretize_kernel,
        out_shape=(jax.ShapeDtypeStruct((n, 1, p), _F32),
                   jax.ShapeDtypeStruct((n, 1, p), _F32),
                   jax.ShapeDtypeStruct((n, g, p), _F32),
                   jax.ShapeDtypeStruct((n, g, p), _F32)),
        in_specs=[full3((n, 1, 1)), full3((n, 1, p)), full3((n, 1, p)),
                  full3((n, g, p)), full3((n, g, p))],
        out_specs=(full3((n, 1, p)), full3((n, 1, p)),
                   full3((n, g, p)), full3((n, g, p))),
        name="ssm_discretize",
    )(log_dt.reshape(n, 1, 1), a_re.reshape(n, 1, p), a_im.reshape(n, 1, p),
      b_re_t, b_im_t)


def _mix_kernel(*refs, time_tile, with_router, batch_major_in):
    (h_ref, gmix_ref, win_ref, bw_ref, lam_ref, cw_ref, dskip_ref, wglu_ref,
     bglu_ref, poolw_ref, pscale_ref, wout_ref, gffn_ref) = refs[:13]
    refs = refs[13:]
    rw_ref = logits_ref = xin_scr = None
    if with_router:
        rw_ref, hout_ref, hn_ref, logits_ref = refs[:4]
        refs = refs[4:]
    else:
        hout_ref, hn_ref = refs[:2]
        refs = refs[2:]
    s_scr, state_scr, ext_scr, mixed_scr, u_save, h_save = refs[:6]
    if batch_major_in:
        xin_scr = refs[6]

    step = pl.program_id(0)
    rows = time_tile * SUBLANES
    d_ssm = dskip_ref.shape[1]
    n_groups = bw_ref.shape[0]
    gin = bw_ref.shape[1]
    gstate = bw_ref.shape[2] // 2
    hist = ext_scr.shape[0] - rows

    @pl.when(step == 0)
    def _():
        state_scr[...] = jnp.zeros_like(state_scr)
        ext_scr[0:hist, :] = jnp.zeros((hist, ext_scr.shape[1]), _F32)

    if batch_major_in:
        for bb in range(SUBLANES):
            for k in range(xin_scr.shape[0]):
                xin_scr[k, pl.ds(bb, time_tile, stride=SUBLANES), :] = (
                    h_ref[bb, :, k * LANES:(k + 1) * LANES])
        h = jnp.concatenate([xin_scr[k] for k in range(xin_scr.shape[0])], axis=1)
    else:
        h = h_ref[...]
    h_save[...] = h
    hn = _rmsnorm(h, gmix_ref[...]).astype(_BF16)
    proj = _dot(hn, win_ref[...])
    u_ssm = proj[:, :d_ssm]
    u_save[...] = u_ssm
    ext_scr[hist:, :] = proj[:, d_ssm:]

    def drive(q):
        ug = u_ssm[:, q * gin:(q + 1) * gin].astype(_BF16)
        s_scr[q] = _dot(ug, bw_ref[q])

    def scan(q, c):
        re0 = c * SCAN_COLS
        im0 = gstate + c * SCAN_COLS
        lr = jnp.broadcast_to(lam_ref[2 * q:2 * q + 1, re0:re0 + SCAN_COLS],
                              (SUBLANES, SCAN_COLS))
        li = jnp.broadcast_to(lam_ref[2 * q + 1:2 * q + 2, re0:re0 + SCAN_COLS],
                              (SUBLANES, SCAN_COLS))
        sre = state_scr[q, :, re0:re0 + SCAN_COLS]
        sim = state_scr[q, :, im0:im0 + SCAN_COLS]
        for t in range(time_tile):
            r0 = t * SUBLANES
            bre = s_scr[q, r0:r0 + SUBLANES, re0:re0 + SCAN_COLS]
            bim = s_scr[q, r0:r0 + SUBLANES, im0:im0 + SCAN_COLS]
            sre, sim = (lr * sre - li * sim + bre, lr * sim + li * sre + bim)
            s_scr[q, r0:r0 + SUBLANES, re0:re0 + SCAN_COLS] = sre
            s_scr[q, r0:r0 + SUBLANES, im0:im0 + SCAN_COLS] = sim
        state_scr[q, :, re0:re0 + SCAN_COLS] = sre
        state_scr[q, :, im0:im0 + SCAN_COLS] = sim

    def pool():
        n_ext = rows + hist
        pos = step * time_tile + (lax.broadcasted_iota(_I32, (rows, LANES), 0) // SUBLANES)
        for g, w in enumerate(POOL_WINDOWS):
            c0 = g * LANES
            e = ext_scr[:, c0:c0 + LANES]
            acc = e
            n_acc = n_ext
            span = 1
            while span < w:
                sh = span * SUBLANES
                acc = acc[sh:, :] + acc[:n_acc - sh, :]
                n_acc -= sh
                span *= 2
            wsum = acc[n_acc - rows:, :]
            cnt = jnp.minimum(pos + 1, w).astype(_F32)
            pooled = wsum / cnt - e[hist:, :]
            mg = _dot(pooled.astype(_BF16), poolw_ref[g]) * pscale_ref[:, c0:c0 + LANES]
            mixed_scr[:, d_ssm + c0:d_ssm + c0 + LANES] = mg.astype(_BF16)
        ext_scr[0:hist, :] = ext_scr[rows:rows + hist, :]

    def readout(q):
        cols = slice(q * gin, (q + 1) * gin)
        yq = _dot(s_scr[q].astype(_BF16), cw_ref[q])
        yq = yq + dskip_ref[:, cols] * u_save[:, cols]
        hq = 0.5 * yq * (1.0 + jnp.tanh(GELU_C0 * (yq + GELU_C1 * (yq * yq * yq))))
        mixed_scr[:, cols] = hq.astype(_BF16)

    def glu_gate():
        hg = mixed_scr[:, :d_ssm]
        gate = _sigmoid(_dot(hg, wglu_ref[...]) + bglu_ref[...])
        mixed_scr[:, :d_ssm] = (hg.astype(_F32) * gate).astype(_BF16)

    def project_out():
        hout = h_save[...] + _dot(mixed_scr[...], wout_ref[...])
        hout_ref[...] = hout
        hn2 = _rmsnorm(hout, gffn_ref[...])
        hn_hi = hn2.astype(_BF16)
        hn_ref[...] = hn_hi
        if with_router:
            hn_lo = (hn2 - hn_hi.astype(_F32)).astype(_BF16)
            rw = rw_ref[...]
            rw_hi = rw.astype(_BF16)
            rw_lo = (rw - rw_hi.astype(_F32)).astype(_BF16)
            logits_ref[...] = (_dot_nt(rw_hi, hn_hi) + _dot_nt(rw_hi, hn_lo)
                               + _dot_nt(rw_lo, hn_hi))

    for q in range(n_groups):
        drive(q)
    for c in range(gstate // SCAN_COLS):
        for q in range(n_groups):
            scan(q, c)
        if c == 0:
            pool()
    for q in range(n_groups):
        readout(q)
    glu_gate()
    project_out()


def _mix_layer(h, gmix, w_in, bw, lam, cw, dskip, wglu, bglu, poolw, pscale, wout,
               gffn, router_w):
    batch_major_in = h.ndim == 3
    d = h.shape[-1]
    t_rows = h.size // d
    rows = TIME_TILE * SUBLANES
    n_steps = t_rows // rows
    d_ssm = dskip.shape[1]
    d_pool = pscale.shape[1]
    hist = max(POOL_WINDOWS) * SUBLANES
    with_router = router_w is not None

    def const(a):
        nd = a.ndim
        return pl.BlockSpec(a.shape, lambda i, nd=nd: (0,) * nd, pipeline_mode=pl.Buffered(1))

    row_blk = lambda: pl.BlockSpec((rows, d), lambda i: (i, 0))
    ins = [h, gmix, w_in, bw, lam, cw, dskip, wglu, bglu, poolw, pscale, wout, gffn]
    h_spec = (pl.BlockSpec((h.shape[0], TIME_TILE, d), lambda i: (0, i, 0))
              if batch_major_in else row_blk())
    in_specs = [h_spec] + [const(a) for a in ins[1:]]
    out_shape = [jax.ShapeDtypeStruct((t_rows, d), _F32),
                 jax.ShapeDtypeStruct((t_rows, d), _BF16)]
    out_specs = [row_blk(), row_blk()]
    if with_router:
        n_e = router_w.shape[1]
        rw = jnp.transpose(router_w)
        ins.append(rw)
        in_specs.append(const(rw))
        out_shape.append(jax.ShapeDtypeStruct((n_e, t_rows), _F32))
        out_specs.append(pl.BlockSpec((n_e, rows), lambda i: (0, i)))
    scratch = [
        pltpu.VMEM((bw.shape[0], rows, bw.shape[2]), _F32),
        pltpu.VMEM((bw.shape[0], SUBLANES, bw.shape[2]), _F32),
        pltpu.VMEM((rows + hist, d_pool), _F32),
        pltpu.VMEM((rows, d_ssm + d_pool), _BF16),
        pltpu.VMEM((rows, d_ssm), _F32),
        pltpu.VMEM((rows, d), _F32),
    ]
    if batch_major_in:
        scratch.append(pltpu.VMEM((d // LANES, rows, LANES), _F32))
    return pl.pallas_call(
        functools.partial(_mix_kernel, time_tile=TIME_TILE, with_router=with_router,
                          batch_major_in=batch_major_in),
        grid=(n_steps,),
        in_specs=in_specs,
        out_specs=out_specs,
        out_shape=out_shape,
        scratch_shapes=scratch,
        compiler_params=pltpu.CompilerParams(
            dimension_semantics=("arbitrary",), vmem_limit_bytes=VMEM_LIMIT),
        name="mix_router" if with_router else "mix",
    )(*ins)


def _swiglu(x, wg_ref, wu_ref, wd_ref, a_scr, slot):
    tf = wg_ref.shape[2]
    c0 = 0
    while c0 < tf:
        cw = min(FFN_COL_CHUNK, tf - c0)
        g = _dot(x, wg_ref[slot, :, c0:c0 + cw])
        u = _dot(x, wu_ref[slot, :, c0:c0 + cw])
        a_scr[:, c0:c0 + cw] = (g * _sigmoid(g) * u).astype(_BF16)
        c0 += cw
    return _dot(a_scr[...], wd_ref[slot])


def _ffn_kernel(x_ref, wg_hbm, wu_hbm, wd_hbm, res_ref, out_ref,
                wg_buf, wu_buf, wd_buf, stg_g, stg_u, stg_d, sems, a_scr, *, layer):
    units = WEIGHT_UNITS
    gu_rows = wg_buf.shape[1] // units
    d_rows = wd_buf.shape[1] // units

    @pl.when(pl.program_id(0) == 0)
    def _():
        def copies(k):
            st = k % 2
            return (pltpu.make_async_copy(wg_hbm.at[layer, pl.ds(k * gu_rows, gu_rows)],
                                          stg_g.at[st], sems.at[st, 0]),
                    pltpu.make_async_copy(wu_hbm.at[layer, pl.ds(k * gu_rows, gu_rows)],
                                          stg_u.at[st], sems.at[st, 1]),
                    pltpu.make_async_copy(wd_hbm.at[layer, pl.ds(k * d_rows, d_rows)],
                                          stg_d.at[st], sems.at[st, 2]))

        for cp in copies(0):
            cp.start()
        for k in range(units):
            if k + 1 < units:
                for cp in copies(k + 1):
                    cp.start()
            for cp in copies(k):
                cp.wait()
            st = k % 2
            wg_buf[0, k * gu_rows:(k + 1) * gu_rows, :] = stg_g[st].astype(_BF16)
            wu_buf[0, k * gu_rows:(k + 1) * gu_rows, :] = stg_u[st].astype(_BF16)
            wd_buf[0, k * d_rows:(k + 1) * d_rows, :] = stg_d[st].astype(_BF16)

    out_ref[...] = res_ref[...] + _swiglu(x_ref[...], wg_buf, wu_buf, wd_buf, a_scr, 0)


def _ffn(x, wg, wu, wd, layer, res):
    t_rows, d = x.shape
    f = wg.shape[2]
    tm = FFN_ROW_TILE
    hbm = pl.BlockSpec(memory_space=pl.ANY)
    return pl.pallas_call(
        functools.partial(_ffn_kernel, layer=layer),
        grid=(t_rows // tm,),
        in_specs=[pl.BlockSpec((tm, d), lambda i: (i, 0)), hbm, hbm, hbm,
                  pl.BlockSpec((tm, d), lambda i: (i, 0))],
        out_specs=pl.BlockSpec((tm, d), lambda i: (i, 0)),
        out_shape=jax.ShapeDtypeStruct((t_rows, d), _F32),
        scratch_shapes=[pltpu.VMEM((1, d, f), _BF16),
                        pltpu.VMEM((1, d, f), _BF16),
                        pltpu.VMEM((1, f, d), _BF16),
                        pltpu.VMEM((2, d // WEIGHT_UNITS, f), _F32),
                        pltpu.VMEM((2, d // WEIGHT_UNITS, f), _F32),
                        pltpu.VMEM((2, f // WEIGHT_UNITS, d), _F32),
                        pltpu.SemaphoreType.DMA((2, 3)),
                        pltpu.VMEM((tm, f), _BF16)],
        compiler_params=pltpu.CompilerParams(
            dimension_semantics=("arbitrary",), vmem_limit_bytes=VMEM_LIMIT),
        name="ffn_dense",
    )(x, wg, wu, wd, res)


def _route_kernel(lt_ref, pos_ref, gates_ref, segtab_ref, blktab_ref, *, n_blk, row_tile):
    l = lt_ref[...]
    n_e, t_rows = l.shape
    ch = TOKEN_CHUNK
    n_ch = t_rows // ch
    ie = lax.broadcasted_iota(_I32, l.shape, 0)
    m1 = jnp.max(l, axis=0, keepdims=True)
    i1 = jnp.min(jnp.where(l == m1, ie, n_e), axis=0, keepdims=True)
    l2 = jnp.where(ie == i1, -jnp.inf, l)
    m2 = jnp.max(l2, axis=0, keepdims=True)
    i2 = jnp.min(jnp.where(l2 == m2, ie, n_e), axis=0, keepdims=True)
    e2 = jnp.exp(m2 - m1)
    den = 1.0 + e2
    gates_ref[0:1, :] = 1.0 / den
    gates_ref[1:2, :] = e2 / den
    sel1 = ie == i1
    sel2 = ie == i2
    chosen = jnp.where(sel1 | sel2, 1.0, 0.0).astype(_BF16)

    chunk_of_t = lax.broadcasted_iota(_I32, (n_ch, t_rows), 1) // ch
    in_chunk = jnp.where(chunk_of_t == lax.broadcasted_iota(_I32, (n_ch, t_rows), 0),
                         1.0, 0.0).astype(_BF16)
    n_ec = _dot_nt(chosen, in_chunk)
    n_ce = _dot_nt(in_chunk, chosen)
    cc_r = lax.broadcasted_iota(_I32, (n_ch, n_ch), 0)
    cc_c = lax.broadcasted_iota(_I32, (n_ch, n_ch), 1)
    before = jnp.where(cc_r < cc_c, 1.0, 0.0).astype(_BF16)
    after = jnp.where(cc_c < cc_r, 1.0, 0.0).astype(_BF16)
    segoff_ec = _dot(n_ec.astype(_BF16), before)
    segoff_ce = _dot(after, n_ce.astype(_BF16))
    def window(segoff, n):
        start = jnp.floor(segoff * (1.0 / SEG_ALIGN)) * SEG_ALIGN
        used = segoff - start + n
        return start, used, start + _ceil_to(jnp.maximum(used, 1.0), float(WINDOW_PIECE))

    wstart_ec, used_ec, wend_ec = window(segoff_ec, n_ec)
    _, _, wend_ce = window(segoff_ce, n_ce)
    tot_col = _ceil_to(jnp.max(wend_ec, axis=1, keepdims=True), float(row_tile))
    tot_row = _ceil_to(jnp.max(wend_ce, axis=0, keepdims=True), float(row_tile))
    ee_r = lax.broadcasted_iota(_I32, (n_e, n_e), 0)
    ee_c = lax.broadcasted_iota(_I32, (n_e, n_e), 1)
    base_col = jnp.sum(jnp.where(ee_c < ee_r, tot_row, 0.0), axis=1, keepdims=True)
    w0_ec = base_col + segoff_ec
    end_col = base_col + tot_col
    segtab_ref[0] = (base_col + wstart_ec).astype(_I32)
    segtab_ref[1] = used_ec.astype(_I32)
    segtab_ref[2] = jnp.broadcast_to(end_col, (n_e, n_ch)).astype(_I32)

    tt_r = lax.broadcasted_iota(_I32, (ch, ch), 0)
    tt_c = lax.broadcasted_iota(_I32, (ch, ch), 1)
    earlier = jnp.where(tt_r < tt_c, 1.0, 0.0).astype(_BF16)
    for c in range(n_ch):
        cols = slice(c * ch, (c + 1) * ch)
        rank = _dot(chosen[:, cols], earlier)
        offs = w0_ec[:, c:c + 1] + rank
        p1 = jnp.sum(jnp.where(sel1[:, cols], offs, 0.0), axis=0, keepdims=True)
        p2 = jnp.sum(jnp.where(sel2[:, cols], offs, 0.0), axis=0, keepdims=True)
        pos_ref[0:1, cols] = p1.astype(_I32)
        pos_ref[1:2, cols] = p2.astype(_I32)

    start = (lax.broadcasted_iota(_I32, (1, n_blk), 1) * row_tile).astype(_F32)
    owner = jnp.sum(jnp.where(end_col <= start, 1.0, 0.0), axis=0, keepdims=True)
    n_valid = jnp.sum(tot_col, axis=0, keepdims=True) * (1.0 / row_tile)
    blktab_ref[0:1, :] = jnp.minimum(owner, n_e - 1.0).astype(_I32)
    blktab_ref[1:2, :] = jnp.broadcast_to(n_valid, (1, n_blk)).astype(_I32)


def _route(logits_t, n_blk, row_tile):
    n_e, t_rows = logits_t.shape
    n_ch = t_rows // TOKEN_CHUNK
    full = lambda s: pl.BlockSpec(s, lambda: (0,) * len(s))
    return pl.pallas_call(
        functools.partial(_route_kernel, n_blk=n_blk, row_tile=row_tile),
        in_specs=[full((n_e, t_rows))],
        out_specs=(full((2, t_rows)), full((2, t_rows)), full((3, n_e, n_ch)), full((2, n_blk))),
        out_shape=(jax.ShapeDtypeStruct((2, t_rows), _I32),
                   jax.ShapeDtypeStruct((2, t_rows), _F32),
                   jax.ShapeDtypeStruct((3, n_e, n_ch), _I32),
                   jax.ShapeDtypeStruct((2, n_blk), _I32)),
        compiler_params=pltpu.CompilerParams(vmem_limit_bytes=VMEM_LIMIT),
        name="route_sort",
    )(logits_t)


def _dispatch_kernel(w0_ref, sl_ref, end_ref, nv_ref, hn_ref, posl_ref, x_hbm,
                     stage, sems, zero_scr, zsems, tsems, carry, *, n_e, n_ch, row_tile, min_blocks):
    c = pl.program_id(0)
    ch = hn_ref.shape[0]
    piece = WINDOW_PIECE
    n_pieces = ch // piece
    slot = c % 2
    n_blocks = x_hbm.shape[0] // row_tile

    def for_each_unused_block(fn):
        for b in range(min_blocks, n_blocks):
            @pl.when(b >= nv_ref[0])
            def _(b=b):
                fn(pltpu.make_async_copy(zero_scr, x_hbm.at[pl.ds(b * row_tile, row_tile)],
                                         tsems.at[b - min_blocks]))

    def piece_copy(cc, sl, e, k):
        row0 = pl.multiple_of(w0_ref[e * n_ch + cc] + k * piece, SEG_ALIGN)
        return pltpu.make_async_copy(
            stage.at[sl, k, pl.ds(e * piece, piece)],
            x_hbm.at[pl.ds(row0, piece)], sems.at[sl, e, k])

    def for_each_piece(cc, sl, fn):
        for e in range(n_e):
            fn(piece_copy(cc, sl, e, 0))
            for k in range(1, n_pieces):
                @pl.when(sl_ref[e * n_ch + cc] > k * piece)
                def _(e=e, k=k):
                    fn(piece_copy(cc, sl, e, k))

    @pl.when(c == 0)
    def _():
        zero_scr[...] = jnp.zeros_like(zero_scr)
        carry[...] = jnp.zeros_like(carry)
        fills = [pltpu.make_async_copy(
            zero_scr,
            x_hbm.at[pl.ds(pl.multiple_of(end_ref[e * n_ch] - row_tile, SEG_ALIGN), row_tile)],
            zsems.at[e]) for e in range(n_e)]
        for cp in fills:
            cp.start()
        for_each_unused_block(lambda cp: cp.start())
        for cp in fills:
            cp.wait()

    pp = posl_ref[0]
    hn = hn_ref[...]

    def gather_piece(k):
        row_j = lax.broadcasted_iota(_I32, (piece, ch), 0) + k * piece
        parts = []
        for e in range(n_e):
            row_id = row_j + w0_ref[e * n_ch + c]
            hit = (pp[0:1, :] == row_id) | (pp[1:2, :] == row_id)
            parts.append(jnp.where(hit, 1.0, 0.0).astype(_BF16))
        onehot = jnp.concatenate(parts, axis=0)
        stage[slot, k] = _dot(onehot, hn).astype(_BF16)

    gather_piece(0)
    for e in range(n_e):
        head = slice(e * piece, e * piece + SEG_ALIGN)
        stage[slot, 0, head, :] = stage[slot, 0, head, :] + carry[e]
    for k in range(1, n_pieces):
        reaches = sl_ref[c] > k * piece
        for e in range(1, n_e):
            reaches = jnp.logical_or(reaches, sl_ref[e * n_ch + c] > k * piece)

        @pl.when(reaches)
        def _(k=k):
            gather_piece(k)
    for e in range(n_e):
        used = sl_ref[e * n_ch + c]
        gathered = ((jnp.maximum(used, 1) + piece - 1) // piece) * piece
        off = (used // SEG_ALIGN) * SEG_ALIGN
        src = jnp.minimum(off, gathered - SEG_ALIGN)
        grp = stage[slot, src // piece,
                    pl.ds(pl.multiple_of(e * piece + src % piece, SEG_ALIGN), SEG_ALIGN), :]
        carry[e] = jnp.where(off < gathered, grp, jnp.zeros_like(grp))

    @pl.when(c > 0)
    def _():
        for_each_piece(c - 1, 1 - slot, lambda cp: cp.wait())

    for_each_piece(c, slot, lambda cp: cp.start())

    @pl.when(c == n_ch - 1)
    def _():
        for_each_piece(c, slot, lambda cp: cp.wait())
        for_each_unused_block(lambda cp: cp.wait())


def _dispatch(hn, pos_l, w0_flat, seglen_flat, end_flat, n_valid, n_e, n_rows, row_tile):
    t_rows, d = hn.shape
    ch = TOKEN_CHUNK
    n_ch = t_rows // ch
    n_pieces = ch // WINDOW_PIECE
    min_blocks = 2 * t_rows // row_tile
    return pl.pallas_call(
        functools.partial(_dispatch_kernel, n_e=n_e, n_ch=n_ch, row_tile=row_tile,
                          min_blocks=min_blocks),
        grid_spec=pltpu.PrefetchScalarGridSpec(
            num_scalar_prefetch=4,
            grid=(n_ch,),
            in_specs=[pl.BlockSpec((ch, d), lambda c, *_: (c, 0)),
                      pl.BlockSpec((1, 2, ch), lambda c, *_: (c, 0, 0))],
            out_specs=pl.BlockSpec(memory_space=pl.ANY),
            scratch_shapes=[pltpu.VMEM((2, n_pieces, n_e * WINDOW_PIECE, d), _BF16),
                            pltpu.SemaphoreType.DMA((2, n_e, n_pieces)),
                            pltpu.VMEM((row_tile, d), _BF16),
                            pltpu.SemaphoreType.DMA((n_e,)),
                            pltpu.SemaphoreType.DMA((n_rows // row_tile - min_blocks,)),
                            pltpu.VMEM((n_e, SEG_ALIGN, d), _BF16)]),
        out_shape=jax.ShapeDtypeStruct((n_rows, d), _BF16),
        compiler_params=pltpu.CompilerParams(
            dimension_semantics=("arbitrary",), vmem_limit_bytes=VMEM_LIMIT),
        name="moe_dispatch",
    )(w0_flat, seglen_flat, end_flat, n_valid, hn, pos_l)


def _ffn_grouped_kernel(be_ref, nv_ref, end_ref, x_ref, wg_hbm, wu_hbm, wd_hbm, y_ref,
                        wg_buf, wu_buf, wd_buf, stg_g, stg_u, stg_d, sems, state, a_scr,
                        *, n_ch, expert0):
    b = pl.program_id(0)
    n_valid = nv_ref[0]
    valid = b < n_valid
    row_tile = x_ref.shape[0]
    units = WEIGHT_UNITS
    gu_rows = wg_buf.shape[1] // units
    d_rows = wd_buf.shape[1] // units
    ST_SLOT, ST_RESIDENT, ST_NEXT, ST_DONE, ST_STAGE = range(5)

    def unit_copies(e, k, st):
        g0 = pl.multiple_of(k * gu_rows, gu_rows)
        d0 = pl.multiple_of(k * d_rows, SEG_ALIGN)
        return (pltpu.make_async_copy(wg_hbm.at[expert0 + e, pl.ds(g0, gu_rows)], stg_g.at[st],
                                      sems.at[st, 0]),
                pltpu.make_async_copy(wu_hbm.at[expert0 + e, pl.ds(g0, gu_rows)], stg_u.at[st],
                                      sems.at[st, 1]),
                pltpu.make_async_copy(wd_hbm.at[expert0 + e, pl.ds(d0, d_rows)], stg_d.at[st],
                                      sems.at[st, 2]))

    def start(e, k, st):
        for cp in unit_copies(e, k, st):
            cp.start()

    def wait(e, k, st):
        for cp in unit_copies(e, k, st):
            cp.wait()

    def cast_unit(slot, k, st):
        g0 = pl.multiple_of(k * gu_rows, gu_rows)
        d0 = pl.multiple_of(k * d_rows, SEG_ALIGN)
        wg_buf[slot, pl.ds(g0, gu_rows), :] = stg_g[st].astype(_BF16)
        wu_buf[slot, pl.ds(g0, gu_rows), :] = stg_u[st].astype(_BF16)
        wd_buf[slot, pl.ds(d0, d_rows), :] = stg_d[st].astype(_BF16)

    def expert_after(e):
        blk = end_ref[e * n_ch] // row_tile
        return jnp.where(blk < n_valid, be_ref[jnp.minimum(blk, n_valid - 1)], -1)

    def become_resident(slot, e):
        state[ST_SLOT] = slot
        state[ST_RESIDENT] = e
        nxt = expert_after(e)
        state[ST_NEXT] = nxt
        state[ST_DONE] = 0

        @pl.when(nxt >= 0)
        def _():
            start(nxt, 0, 0)
        state[ST_STAGE] = 0

    e_b = be_ref[jnp.minimum(b, n_valid - 1)]

    @pl.when(b == 0)
    def _():
        state[ST_SLOT] = 1
        state[ST_RESIDENT] = -1
        state[ST_NEXT] = e_b
        state[ST_DONE] = 0
        state[ST_STAGE] = 0
        start(e_b, 0, 0)

    s_slot = state[ST_SLOT]
    s_next = state[ST_NEXT]
    s_done = state[ST_DONE]
    s_stage = state[ST_STAGE]
    change = valid & (e_b != state[ST_RESIDENT])
    steady = valid & jnp.logical_not(change) & (s_next >= 0) & (s_done < units)

    @pl.when(change)
    def _():
        other = 1 - s_slot

        @pl.when(s_done < units)
        def _():
            wait(e_b, s_done, s_stage)
            cast_unit(other, s_done, s_stage)

            def body(k, carry):
                start(e_b, k, 0)
                wait(e_b, k, 0)
                cast_unit(other, k, 0)
                return carry
            lax.fori_loop(s_done + 1, units, body, 0)
        become_resident(other, e_b)

    def run_block(slot):
        y_ref[...] = _swiglu(x_ref[...], wg_buf, wu_buf, wd_buf, a_scr, slot).astype(_BF16)

    @pl.when(steady)
    def _():
        wait(s_next, s_done, s_stage)

        @pl.when(s_done + 1 < units)
        def _():
            start(s_next, s_done + 1, 1 - s_stage)
        state[ST_DONE] = s_done + 1
        state[ST_STAGE] = 1 - s_stage
        cast_unit(1 - s_slot, s_done, s_stage)
        run_block(s_slot)

    @pl.when(valid & jnp.logical_not(steady))
    def _():
        run_block(state[ST_SLOT])

    @pl.when(jnp.logical_not(valid))
    def _():
        y_ref[...] = jnp.zeros_like(y_ref)


def _ffn_grouped(x_sorted, wg, wu, wd, expert0, block_expert, n_valid, end_flat, n_ch, row_tile):
    rows, d = x_sorted.shape
    f = wg.shape[2]
    n_blocks = rows // row_tile
    hbm = pl.BlockSpec(memory_space=pl.ANY)
    return pl.pallas_call(
        functools.partial(_ffn_grouped_kernel, n_ch=n_ch, expert0=expert0),
        grid_spec=pltpu.PrefetchScalarGridSpec(
            num_scalar_prefetch=3,
            grid=(n_blocks,),
            in_specs=[pl.BlockSpec((row_tile, d),
                                   lambda b, be, nv, en: (jnp.minimum(b, nv[0] - 1), 0)),
                      hbm, hbm, hbm],
            out_specs=pl.BlockSpec((row_tile, d), lambda b, be, nv, en: (b, 0)),
            scratch_shapes=[pltpu.VMEM((2, d, f), _BF16),
                            pltpu.VMEM((2, d, f), _BF16),
                            pltpu.VMEM((2, f, d), _BF16),
                            pltpu.VMEM((2, d // WEIGHT_UNITS, f), _F32),
                            pltpu.VMEM((2, d // WEIGHT_UNITS, f), _F32),
                            pltpu.VMEM((2, f // WEIGHT_UNITS, d), _F32),
                            pltpu.SemaphoreType.DMA((2, 3)),
                            pltpu.SMEM((5,), _I32),
                            pltpu.VMEM((row_tile, f), _BF16)]),
        out_shape=jax.ShapeDtypeStruct((rows, d), _BF16),
        compiler_params=pltpu.CompilerParams(
            dimension_semantics=("arbitrary",), vmem_limit_bytes=VMEM_LIMIT),
        name="moe_experts",
    )(block_expert, n_valid, end_flat, x_sorted, wg, wu, wd)


def _combine_kernel(w0_ref, sl_ref, res_ref, tok_ref, gfin_ref, y_hbm,
                    out_ref, ybuf, sems, tmp_scr, acc_scr, *, n_e, n_ch):
    c = pl.program_id(0)
    ch = res_ref.shape[0]
    piece = WINDOW_PIECE
    n_pieces = ch // piece

    def piece_copy(cc, slot, e, k):
        row0 = pl.multiple_of(w0_ref[e * n_ch + cc] + k * piece, SEG_ALIGN)
        return pltpu.make_async_copy(
            y_hbm.at[pl.ds(row0, piece)],
            ybuf.at[slot, k, pl.ds(e * piece, piece)], sems.at[slot, e, k])

    def for_each_piece(cc, slot, fn):
        for e in range(n_e):
            fn(piece_copy(cc, slot, e, 0))
            for k in range(1, n_pieces):
                @pl.when(sl_ref[e * n_ch + cc] > k * piece)
                def _(e=e, k=k):
                    fn(piece_copy(cc, slot, e, k))

    @pl.when(c == 0)
    def _():
        ybuf[...] = jnp.zeros_like(ybuf)
        for_each_piece(0, 0, lambda cp: cp.start())

    @pl.when(c + 1 < n_ch)
    def _():
        for_each_piece(c + 1, (c + 1) % 2, lambda cp: cp.start())

    slot = c % 2
    for_each_piece(c, slot, lambda cp: cp.wait())

    per_dot = MXU_DIM // piece
    depth = per_dot * piece
    lane = lax.broadcasted_iota(_I32, (1, depth), 1)
    p1b = jnp.broadcast_to(tok_ref[:, 0:1], (ch, depth))
    p2b = jnp.broadcast_to(tok_ref[:, 1:2], (ch, depth))
    g1b = jnp.broadcast_to(tok_ref[:, 2:3], (ch, depth))
    g2b = jnp.broadcast_to(tok_ref[:, 3:4], (ch, depth))

    def scatter_back(m, k):
        tgt = jnp.full((1, depth), -1.0, _F32)
        for i in range(per_dot):
            e = m * per_dot + i
            row = lane - i * piece + k * piece
            mine = (lane >= i * piece) & (lane < (i + 1) * piece) & (row < sl_ref[e * n_ch + c])
            tgt = jnp.where(mine, (row + w0_ref[e * n_ch + c]).astype(_F32), tgt)
        q = (jnp.where(p1b == tgt, g1b, 0.0) + jnp.where(p2b == tgt, g2b, 0.0)).astype(_BF16)
        return _dot(q, ybuf[slot, k, m * depth:(m + 1) * depth, :])

    def any_reaches(m, k):
        hit = sl_ref[(m * per_dot) * n_ch + c] > k * piece
        for i in range(1, per_dot):
            hit = jnp.logical_or(hit, sl_ref[(m * per_dot + i) * n_ch + c] > k * piece)
        return hit

    acc = res_ref[...]
    for m in range(n_e // per_dot):
        acc = acc + scatter_back(m, 0)
    acc_scr[...] = acc
    for m in range(n_e // per_dot):
        for k in range(1, n_pieces):
            @pl.when(any_reaches(m, k))
            def _(m=m, k=k):
                acc_scr[...] += scatter_back(m, k)
    normed = _rmsnorm(acc_scr[...], gfin_ref[...])
    steps = ch // SUBLANES
    for k in range(tmp_scr.shape[0]):
        tmp_scr[k] = normed[:, k * LANES:(k + 1) * LANES]
    for bb in range(SUBLANES):
        for k in range(tmp_scr.shape[0]):
            out_ref[bb, :, k * LANES:(k + 1) * LANES] = (
                tmp_scr[k, pl.ds(bb, steps, stride=SUBLANES), :])


def _combine(res, tok_tab, gfin, y_sorted, w0_flat, seglen_flat, n_e, bsz, seq):
    t_rows, d = res.shape
    ch = TOKEN_CHUNK
    n_ch = t_rows // ch
    steps = ch // bsz
    return pl.pallas_call(
        functools.partial(_combine_kernel, n_e=n_e, n_ch=n_ch),
        grid_spec=pltpu.PrefetchScalarGridSpec(
            num_scalar_prefetch=2,
            grid=(n_ch,),
            in_specs=[pl.BlockSpec((ch, d), lambda c, w0, sl: (c, 0)),
                      pl.BlockSpec((ch, tok_tab.shape[1]), lambda c, w0, sl: (c, 0)),
                      pl.BlockSpec((1, d), lambda c, w0, sl: (0, 0)),
                      pl.BlockSpec(memory_space=pl.ANY)],
            out_specs=pl.BlockSpec((bsz, steps, d), lambda c, w0, sl: (0, c, 0)),
            scratch_shapes=[pltpu.VMEM((2, ch // WINDOW_PIECE, n_e * WINDOW_PIECE, d), _BF16),
                            pltpu.SemaphoreType.DMA((2, n_e, ch // WINDOW_PIECE)),
                            pltpu.VMEM((d // LANES, ch, LANES), _F32),
                            pltpu.VMEM((ch, d), _F32)]),
        out_shape=jax.ShapeDtypeStruct((bsz, seq, d), _F32),
        compiler_params=pltpu.CompilerParams(
            dimension_semantics=("arbitrary",), vmem_limit_bytes=VMEM_LIMIT),
        name="moe_combine",
    )(w0_flat, seglen_flat, res, tok_tab, gfin, y_sorted)


def _moe_layer(h, hn, logits_t, wg, wu, wd, expert0, gfin, bsz, seq):
    t_rows, d = h.shape
    n_e = logits_t.shape[0]
    tm = EXPERT_ROW_TILE
    n_ch = t_rows // TOKEN_CHUNK
    max_rows = 2 * t_rows + n_e * (SEG_ALIGN + TOKEN_CHUNK + tm)
    n_blocks = -(-max_rows // tm)
    n_blk_pad = -(-n_blocks // LANES) * LANES

    pos, gates, segtab, blktab = _route(logits_t, n_blk_pad, tm)
    pos_l = jnp.transpose(pos.reshape(2, n_ch, TOKEN_CHUNK), (1, 0, 2))
    w0_flat = segtab[0].reshape(-1)
    seglen_flat = segtab[1].reshape(-1)
    n_valid = blktab[1, :1]
    end_flat = segtab[2].reshape(-1)
    x_sorted = _dispatch(hn, pos_l, w0_flat, seglen_flat, end_flat, n_valid,
                         n_e, n_blocks * tm, tm)
    y_sorted = _ffn_grouped(x_sorted, wg, wu, wd, expert0, blktab[0], n_valid, end_flat,
                            n_ch, tm)
    tok_tab = jnp.transpose(jnp.concatenate([pos.astype(_F32), gates], axis=0))
    return _combine(h, tok_tab, gfin, y_sorted,
                    w0_flat, seglen_flat, n_e, bsz, seq)


def _block_diag_groups(w, groups):
    h, k, n = w.shape
    hpg = h // groups
    eye = jnp.eye(hpg, dtype=w.dtype)
    wq = w.reshape(groups, hpg, k, n)
    out = jnp.einsum("qhkn,hj->qhkjn", wq, eye)
    return out.reshape(groups, hpg * k, hpg * n)


def kernel(x, norm_mix_g, w_in, ssm_log_dt, ssm_a_re, ssm_a_im, ssm_b_re, ssm_b_im, ssm_c_re, ssm_c_im, ssm_d, ssm_w_glu, ssm_b_glu, pool_w, pool_scale, w_out, norm_ffn_g, ffn_w_gate, ffn_w_up, ffn_w_down, router_w, moe_w_gate, moe_w_up, moe_w_down, final_norm_g):
    bsz, seq, d = x.shape
    depth, n_heads, n_state, n_grp_ch = ssm_b_re.shape
    t_rows = bsz * seq
    assert bsz == SUBLANES and depth % 2 == 0
    groups = n_heads // HEADS_PER_GROUP

    nh = depth * n_heads
    lam_re, lam_im, bb_re, bb_im = _discretize(
        ssm_log_dt.reshape(nh), ssm_a_re.reshape(nh, n_state), ssm_a_im.reshape(nh, n_state),
        jnp.swapaxes(ssm_b_re, 2, 3).reshape(nh, n_grp_ch, n_state),
        jnp.swapaxes(ssm_b_im, 2, 3).reshape(nh, n_grp_ch, n_state))
    lam_re = lam_re.reshape(depth, groups, 1, HEADS_PER_GROUP * n_state)
    lam_im = lam_im.reshape(depth, groups, 1, HEADS_PER_GROUP * n_state)
    lam = jnp.concatenate([lam_re, lam_im], axis=2).reshape(depth, 2 * groups, -1)
    bb_re = bb_re.reshape(depth, n_heads, n_grp_ch, n_state)
    bb_im = bb_im.reshape(depth, n_heads, n_grp_ch, n_state)

    h = x
    row = lambda v: v.reshape(1, -1)
    out = None
    for i in range(depth):
        bw = jnp.concatenate([_block_diag_groups(bb_re[i], groups),
                              _block_diag_groups(bb_im[i], groups)], axis=2).astype(_BF16)
        cw = jnp.concatenate(
            [_block_diag_groups(jnp.swapaxes(ssm_c_re[i], 1, 2), groups),
             _block_diag_groups(-jnp.swapaxes(ssm_c_im[i], 1, 2), groups)], axis=1).astype(_BF16)
        is_moe = i % 2 == 1
        j = i // 2
        rw = router_w[j] if is_moe else None
        outs = _mix_layer(
            h, row(norm_mix_g[i]), w_in[i].astype(_BF16), bw, lam[i], cw, row(ssm_d[i]),
            ssm_w_glu[i].astype(_BF16), row(ssm_b_glu[i]), pool_w[i].astype(_BF16),
            row(pool_scale[i]), w_out[i].astype(_BF16), row(norm_ffn_g[i]), rw)
        if is_moe:
            assert i == depth - 1
            h, hn, logits_t = outs
            n_e = moe_w_gate.shape[1]
            stack = lambda w: w.reshape((-1,) + w.shape[2:])
            out = _moe_layer(h, hn, logits_t, stack(moe_w_gate), stack(moe_w_up),
                             stack(moe_w_down), j * n_e, row(final_norm_g), bsz, seq)
        else:
            h, hn = outs
            h = _ffn(hn, ffn_w_gate, ffn_w_up, ffn_w_down, j, h)
    return out
```

```python
import functools
import math

import jax
import jax.numpy as jnp
from jax import lax
from jax.experimental import pallas as pl
from jax.experimental.pallas import tpu as pltpu

RMS_EPS = 1e-6
POOL_WINDOWS = (2, 4, 8, 16)
A_RE_MAX = -1e-4
GELU_C0 = math.sqrt(2.0 / math.pi)
GELU_C1 = 0.044715

SUBLANES = 8
LANES = 128
MXU_DIM = 256

HEADS_PER_GROUP = 16
TIME_TILE = 64
SCAN_COLS = 512
FFN_ROW_TILE = 512
EXPERT_ROW_TILE = 512
FFN_COL_CHUNK = 256
WEIGHT_UNITS = 8
TOKEN_CHUNK = MXU_DIM
SEG_ALIGN = 2 * SUBLANES
WINDOW_PIECE = LANES
VMEM_LIMIT = 56 * 1024 * 1024

_F32 = jnp.float32
_BF16 = jnp.bfloat16
_I32 = jnp.int32


def _dot(a, b):
    return jnp.dot(a, b, preferred_element_type=_F32)


def _dot_nt(a, b):
    return lax.dot_general(a, b, (((1,), (1,)), ((), ())), preferred_element_type=_F32)


def _rmsnorm(x, g):
    inv = lax.rsqrt(jnp.mean(x * x, axis=-1, keepdims=True) + RMS_EPS)
    return x * inv * g


def _sigmoid(x):
    return 1.0 / (1.0 + jnp.exp(-x))


def _ceil_to(x, m):
    return jnp.floor((x + (m - 1.0)) * (1.0 / m)) * m


def _discretize_kernel(log_dt_ref, a_re_ref, a_im_ref, b_re_ref, b_im_ref, c_re_ref, c_im_ref,
                       lam_ref, bw_ref, cw_ref):
    dt = jnp.exp(log_dt_ref[...])
    ar = jnp.minimum(a_re_ref[...], A_RE_MAX)
    ai = a_im_ref[...]
    mag = jnp.exp(ar * dt)
    lam_re = mag * jnp.cos(ai * dt)
    lam_im = mag * jnp.sin(ai * dt)
    den = ar * ar + ai * ai
    nr = lam_re - 1.0
    ni = lam_im
    coef_re = (nr * ar + ni * ai) / den
    coef_im = (ni * ar - nr * ai) / den
    br = b_re_ref[...]
    bi = b_im_ref[...]
    bb_re = (coef_re * br - coef_im * bi).astype(_BF16)
    bb_im = (coef_re * bi + coef_im * br).astype(_BF16)
    c_re = c_re_ref[...].astype(_BF16)
    c_im_neg = (-c_im_ref[...]).astype(_BF16)

    n, g, p = br.shape
    hpg = n // lam_ref.shape[0]
    half = hpg * p
    bw_ref[...] = jnp.zeros_like(bw_ref)
    cw_ref[...] = jnp.zeros_like(cw_ref)
    for h in range(n):
        q, hl = divmod(h, hpg)
        st = slice(hl * p, (hl + 1) * p)
        st_im = slice(half + hl * p, half + (hl + 1) * p)
        ch = slice(hl * g, (hl + 1) * g)
        lam_ref[q, 0:1, st] = lam_re[h]
        lam_ref[q, 1:2, st] = lam_im[h]
        bw_ref[q, ch, st] = bb_re[h]
        bw_ref[q, ch, st_im] = bb_im[h]
        cw_ref[q, st, ch] = c_re[h]
        cw_ref[q, st_im, ch] = c_im_neg[h]


def _discretize(log_dt, a_re, a_im, b_re_t, b_im_t, c_re_t, c_im_t):
    n, g, p = b_re_t.shape
    hpg = HEADS_PER_GROUP
    nq = n // hpg
    full3 = lambda s: pl.BlockSpec(s, lambda: (0, 0, 0))
    return pl.pallas_call(
        _discretize_kernel,
        out_shape=(jax.ShapeDtypeStruct((nq, 2, hpg * p), _F32),
                   jax.ShapeDtypeStruct((nq, hpg * g, 2 * hpg * p), _BF16),
                   jax.ShapeDtypeStruct((nq, 2 * hpg * p, hpg * g), _BF16)),
        in_specs=[full3((n, 1, 1)), full3((n, 1, p)), full3((n, 1, p)),
                  full3((n, g, p)), full3((n, g, p)), full3((n, p, g)), full3((n, p, g))],
        out_specs=(full3((nq, 2, hpg * p)), full3((nq, hpg * g, 2 * hpg * p)),
                   full3((nq, 2 * hpg * p, hpg * g))),
        name="ssm_discretize",
    )(log_dt.reshape(n, 1, 1), a_re.reshape(n, 1, p), a_im.reshape(n, 1, p),
      b_re_t, b_im_t, c_re_t, c_im_t)


def _mix_kernel(*refs, time_tile, with_router, batch_major_in):
    (h_ref, gmix_ref, win_f32, bw_ref, lam_ref, cw_ref, dskip_ref, wglu_f32,
     bglu_ref, poolw_f32, pscale_ref, wout_f32, gffn_ref) = refs[:13]
    refs = refs[13:]
    rw_ref = logits_ref = xin_scr = None
    if with_router:
        rw_ref, hout_ref, hn_ref, logits_ref = refs[:4]
        refs = refs[4:]
    else:
        hout_ref, hn_ref = refs[:2]
        refs = refs[2:]
    (s_scr, state_scr, ext_scr, mixed_scr, u_save, h_save,
     win_ref, wglu_ref, poolw_ref, wout_ref) = refs[:10]
    if batch_major_in:
        xin_scr = refs[10]

    step = pl.program_id(0)
    rows = time_tile * SUBLANES
    d_ssm = dskip_ref.shape[1]
    n_groups = bw_ref.shape[0]
    gin = bw_ref.shape[1]
    gstate = bw_ref.shape[2] // 2
    hist = ext_scr.shape[0] - rows

    @pl.when(step == 0)
    def _():
        state_scr[...] = jnp.zeros_like(state_scr)
        ext_scr[0:hist, :] = jnp.zeros((hist, ext_scr.shape[1]), _F32)
        win_ref[...] = win_f32[...].astype(_BF16)
        wglu_ref[...] = wglu_f32[...].astype(_BF16)
        poolw_ref[...] = poolw_f32[...].astype(_BF16)
        wout_ref[...] = wout_f32[...].astype(_BF16)

    if batch_major_in:
        for bb in range(SUBLANES):
            for k in range(xin_scr.shape[0]):
                xin_scr[k, pl.ds(bb, time_tile, stride=SUBLANES), :] = (
                    h_ref[bb, :, k * LANES:(k + 1) * LANES])
        h = jnp.concatenate([xin_scr[k] for k in range(xin_scr.shape[0])], axis=1)
    else:
        h = h_ref[...]
    h_save[...] = h
    hn = _rmsnorm(h, gmix_ref[...]).astype(_BF16)
    proj = _dot(hn, win_ref[...])
    u_ssm = proj[:, :d_ssm]
    u_save[...] = u_ssm
    ext_scr[hist:, :] = proj[:, d_ssm:]

    def drive(q):
        ug = u_ssm[:, q * gin:(q + 1) * gin].astype(_BF16)
        s_scr[q] = _dot(ug, bw_ref[q])

    def scan(q, c):
        re0 = c * SCAN_COLS
        im0 = gstate + c * SCAN_COLS
        lr = jnp.broadcast_to(lam_ref[2 * q:2 * q + 1, re0:re0 + SCAN_COLS],
                              (SUBLANES, SCAN_COLS))
        li = jnp.broadcast_to(lam_ref[2 * q + 1:2 * q + 2, re0:re0 + SCAN_COLS],
                              (SUBLANES, SCAN_COLS))
        sre = state_scr[q, :, re0:re0 + SCAN_COLS]
        sim = state_scr[q, :, im0:im0 + SCAN_COLS]
        for t in range(time_tile):
            r0 = t * SUBLANES
            bre = s_scr[q, r0:r0 + SUBLANES, re0:re0 + SCAN_COLS]
            bim = s_scr[q, r0:r0 + SUBLANES, im0:im0 + SCAN_COLS]
            sre, sim = (lr * sre - li * sim + bre, lr * sim + li * sre + bim)
            s_scr[q, r0:r0 + SUBLANES, re0:re0 + SCAN_COLS] = sre
            s_scr[q, r0:r0 + SUBLANES, im0:im0 + SCAN_COLS] = sim
        state_scr[q, :, re0:re0 + SCAN_COLS] = sre
        state_scr[q, :, im0:im0 + SCAN_COLS] = sim

    def pool():
        n_ext = rows + hist
        pos = step * time_tile + (lax.broadcasted_iota(_I32, (rows, LANES), 0) // SUBLANES)
        for g, w in enumerate(POOL_WINDOWS):
            c0 = g * LANES
            e = ext_scr[:, c0:c0 + LANES]
            acc = e
            n_acc = n_ext
            span = 1
            while span < w:
                sh = span * SUBLANES
                acc = acc[sh:, :] + acc[:n_acc - sh, :]
                n_acc -= sh
                span *= 2
            wsum = acc[n_acc - rows:, :]
            cnt = jnp.minimum(pos + 1, w).astype(_F32)
            pooled = wsum / cnt - e[hist:, :]
            mg = _dot(pooled.astype(_BF16), poolw_ref[g]) * pscale_ref[:, c0:c0 + LANES]
            mixed_scr[:, d_ssm + c0:d_ssm + c0 + LANES] = mg.astype(_BF16)
        ext_scr[0:hist, :] = ext_scr[rows:rows + hist, :]

    def readout(q):
        cols = slice(q * gin, (q + 1) * gin)
        yq = _dot(s_scr[q].astype(_BF16), cw_ref[q])
        yq = yq + dskip_ref[:, cols] * u_save[:, cols]
        hq = 0.5 * yq * (1.0 + jnp.tanh(GELU_C0 * (yq + GELU_C1 * (yq * yq * yq))))
        mixed_scr[:, cols] = hq.astype(_BF16)

    def glu_gate():
        hg = mixed_scr[:, :d_ssm]
        gate = _sigmoid(_dot(hg, wglu_ref[...]) + bglu_ref[...])
        mixed_scr[:, :d_ssm] = (hg.astype(_F32) * gate).astype(_BF16)

    def project_out():
        hout = h_save[...] + _dot(mixed_scr[...], wout_ref[...])
        hout_ref[...] = hout
        hn2 = _rmsnorm(hout, gffn_ref[...])
        hn_hi = hn2.astype(_BF16)
        hn_ref[...] = hn_hi
        if with_router:
            hn_lo = (hn2 - hn_hi.astype(_F32)).astype(_BF16)
            rw = rw_ref[...]
            rw_hi = rw.astype(_BF16)
            rw_lo = (rw - rw_hi.astype(_F32)).astype(_BF16)
            logits_ref[...] = (_dot_nt(rw_hi, hn_hi) + _dot_nt(rw_hi, hn_lo)
                               + _dot_nt(rw_lo, hn_hi))

    for q in range(n_groups):
        drive(q)
    for c in range(gstate // SCAN_COLS):
        for q in range(n_groups):
            scan(q, c)
        if c == 0:
            pool()
    for q in range(n_groups):
        readout(q)
    glu_gate()
    project_out()


def _mix_layer(h, layer, gmix, w_in, bw, lam, cw, dskip, wglu, bglu, poolw, pscale, wout,
               gffn, router_wt, moe_layer):
    batch_major_in = h.ndim == 3
    d = h.shape[-1]
    t_rows = h.size // d
    rows = TIME_TILE * SUBLANES
    n_steps = t_rows // rows
    d_ssm = dskip.shape[2]
    d_pool = pscale.shape[2]
    n_groups, _, two_gstate = bw.shape[1:]
    hist = max(POOL_WINDOWS) * SUBLANES
    with_router = router_wt is not None

    def const(a, idx):
        nd = a.ndim
        return pl.BlockSpec((None,) + a.shape[1:], lambda i, nd=nd, idx=idx: (idx,) + (0,) * (nd - 1),
                            pipeline_mode=pl.Buffered(1))

    row_blk = lambda: pl.BlockSpec((rows, d), lambda i: (i, 0))
    ins = [h, gmix, w_in, bw, lam, cw, dskip, wglu, bglu, poolw, pscale, wout, gffn]
    h_spec = (pl.BlockSpec((h.shape[0], TIME_TILE, d), lambda i: (0, i, 0))
              if batch_major_in else row_blk())
    in_specs = [h_spec] + [const(a, layer) for a in ins[1:]]
    out_shape = [jax.ShapeDtypeStruct((t_rows, d), _F32),
                 jax.ShapeDtypeStruct((t_rows, d), _BF16)]
    out_specs = [row_blk(), row_blk()]
    if with_router:
        n_e = router_wt.shape[1]
        ins.append(router_wt)
        in_specs.append(const(router_wt, moe_layer))
        out_shape.append(jax.ShapeDtypeStruct((n_e, t_rows), _F32))
        out_specs.append(pl.BlockSpec((n_e, rows), lambda i: (0, i)))
    scratch = [
        pltpu.VMEM((n_groups, rows, two_gstate), _F32),
        pltpu.VMEM((n_groups, SUBLANES, two_gstate), _F32),
        pltpu.VMEM((rows + hist, d_pool), _F32),
        pltpu.VMEM((rows, d_ssm + d_pool), _BF16),
        pltpu.VMEM((rows, d_ssm), _F32),
        pltpu.VMEM((rows, d), _F32),
        pltpu.VMEM(w_in.shape[1:], _BF16),
        pltpu.VMEM(wglu.shape[1:], _BF16),
        pltpu.VMEM(poolw.shape[1:], _BF16),
        pltpu.VMEM(wout.shape[1:], _BF16),
    ]
    if batch_major_in:
        scratch.append(pltpu.VMEM((d // LANES, rows, LANES), _F32))
    return pl.pallas_call(
        functools.partial(_mix_kernel, time_tile=TIME_TILE, with_router=with_router,
                          batch_major_in=batch_major_in),
        grid=(n_steps,),
        in_specs=in_specs,
        out_specs=out_specs,
        out_shape=out_shape,
        scratch_shapes=scratch,
        compiler_params=pltpu.CompilerParams(
            dimension_semantics=("arbitrary",), vmem_limit_bytes=VMEM_LIMIT),
        name="mix_router" if with_router else "mix",
    )(*ins)


def _swiglu(x, wg_ref, wu_ref, wd_ref, a_scr, slot):
    tf = wg_ref.shape[2]
    c0 = 0
    while c0 < tf:
        cw = min(FFN_COL_CHUNK, tf - c0)
        g = _dot(x, wg_ref[slot, :, c0:c0 + cw])
        u = _dot(x, wu_ref[slot, :, c0:c0 + cw])
        a_scr[:, c0:c0 + cw] = (g * _sigmoid(g) * u).astype(_BF16)
        c0 += cw
    return _dot(a_scr[...], wd_ref[slot])


def _ffn_kernel(x_ref, wg_hbm, wu_hbm, wd_hbm, res_ref, out_ref,
                wg_buf, wu_buf, wd_buf, stg_g, stg_u, stg_d, sems, a_scr, *, layer):
    units = WEIGHT_UNITS
    gu_rows = wg_buf.shape[1] // units
    d_rows = wd_buf.shape[1] // units

    @pl.when(pl.program_id(0) == 0)
    def _():
        def copies(k):
            st = k % 2
            return (pltpu.make_async_copy(wg_hbm.at[layer, pl.ds(k * gu_rows, gu_rows)],
                                          stg_g.at[st], sems.at[st, 0]),
                    pltpu.make_async_copy(wu_hbm.at[layer, pl.ds(k * gu_rows, gu_rows)],
                                          stg_u.at[st], sems.at[st, 1]),
                    pltpu.make_async_copy(wd_hbm.at[layer, pl.ds(k * d_rows, d_rows)],
                                          stg_d.at[st], sems.at[st, 2]))

        for cp in copies(0):
            cp.start()
        for k in range(units):
            if k + 1 < units:
                for cp in copies(k + 1):
                    cp.start()
            for cp in copies(k):
                cp.wait()
            st = k % 2
            wg_buf[0, k * gu_rows:(k + 1) * gu_rows, :] = stg_g[st].astype(_BF16)
            wu_buf[0, k * gu_rows:(k + 1) * gu_rows, :] = stg_u[st].astype(_BF16)
            wd_buf[0, k * d_rows:(k + 1) * d_rows, :] = stg_d[st].astype(_BF16)

    out_ref[...] = res_ref[...] + _swiglu(x_ref[...], wg_buf, wu_buf, wd_buf, a_scr, 0)


def _ffn(x, wg, wu, wd, layer, res):
    t_rows, d = x.shape
    f = wg.shape[2]
    tm = FFN_ROW_TILE
    hbm = pl.BlockSpec(memory_space=pl.ANY)
    return pl.pallas_call(
        functools.partial(_ffn_kernel, layer=layer),
        grid=(t_rows // tm,),
        in_specs=[pl.BlockSpec((tm, d), lambda i: (i, 0)), hbm, hbm, hbm,
                  pl.BlockSpec((tm, d), lambda i: (i, 0))],
        out_specs=pl.BlockSpec((tm, d), lambda i: (i, 0)),
        out_shape=jax.ShapeDtypeStruct((t_rows, d), _F32),
        scratch_shapes=[pltpu.VMEM((1, d, f), _BF16),
                        pltpu.VMEM((1, d, f), _BF16),
                        pltpu.VMEM((1, f, d), _BF16),
                        pltpu.VMEM((2, d // WEIGHT_UNITS, f), _F32),
                        pltpu.VMEM((2, d // WEIGHT_UNITS, f), _F32),
                        pltpu.VMEM((2, f // WEIGHT_UNITS, d), _F32),
                        pltpu.SemaphoreType.DMA((2, 3)),
                        pltpu.VMEM((tm, f), _BF16)],
        compiler_params=pltpu.CompilerParams(
            dimension_semantics=("arbitrary",), vmem_limit_bytes=VMEM_LIMIT),
        name="ffn_dense",
    )(x, wg, wu, wd, res)


def _route_kernel(lt_ref, pos_ref, gates_ref, segtab_ref, blktab_ref, *, n_blk, row_tile):
    l = lt_ref[...]
    n_e, t_rows = l.shape
    ch = TOKEN_CHUNK
    n_ch = t_rows // ch
    ie = lax.broadcasted_iota(_I32, l.shape, 0)
    m1 = jnp.max(l, axis=0, keepdims=True)
    i1 = jnp.min(jnp.where(l == m1, ie, n_e), axis=0, keepdims=True)
    l2 = jnp.where(ie == i1, -jnp.inf, l)
    m2 = jnp.max(l2, axis=0, keepdims=True)
    i2 = jnp.min(jnp.where(l2 == m2, ie, n_e), axis=0, keepdims=True)
    e2 = jnp.exp(m2 - m1)
    den = 1.0 + e2
    gates_ref[0:1, :] = 1.0 / den
    gates_ref[1:2, :] = e2 / den
    sel1 = ie == i1
    sel2 = ie == i2
    chosen = jnp.where(sel1 | sel2, 1.0, 0.0).astype(_BF16)

    chunk_of_t = lax.broadcasted_iota(_I32, (n_ch, t_rows), 1) // ch
    in_chunk = jnp.where(chunk_of_t == lax.broadcasted_iota(_I32, (n_ch, t_rows), 0),
                         1.0, 0.0).astype(_BF16)
    n_ec = _dot_nt(chosen, in_chunk)
    n_ce = _dot_nt(in_chunk, chosen)
    cc_r = lax.broadcasted_iota(_I32, (n_ch, n_ch), 0)
    cc_c = lax.broadcasted_iota(_I32, (n_ch, n_ch), 1)
    before = jnp.where(cc_r < cc_c, 1.0, 0.0).astype(_BF16)
    after = jnp.where(cc_c < cc_r, 1.0, 0.0).astype(_BF16)
    segoff_ec = _dot(n_ec.astype(_BF16), before)
    segoff_ce = _dot(after, n_ce.astype(_BF16))
    def window(segoff, n):
        start = jnp.floor(segoff * (1.0 / SEG_ALIGN)) * SEG_ALIGN
        used = segoff - start + n
        return start, used, start + _ceil_to(jnp.maximum(used, 1.0), float(WINDOW_PIECE))

    wstart_ec, used_ec, wend_ec = window(segoff_ec, n_ec)
    _, _, wend_ce = window(segoff_ce, n_ce)
    tot_col = _ceil_to(jnp.max(wend_ec, axis=1, keepdims=True), float(row_tile))
    tot_row = _ceil_to(jnp.max(wend_ce, axis=0, keepdims=True), float(row_tile))
    ee_r = lax.broadcasted_iota(_I32, (n_e, n_e), 0)
    ee_c = lax.broadcasted_iota(_I32, (n_e, n_e), 1)
    base_col = jnp.sum(jnp.where(ee_c < ee_r, tot_row, 0.0), axis=1, keepdims=True)
    w0_ec = base_col + segoff_ec
    end_col = base_col + tot_col
    segtab_ref[0] = (base_col + wstart_ec).astype(_I32)
    segtab_ref[1] = used_ec.astype(_I32)
    segtab_ref[2] = jnp.broadcast_to(end_col, (n_e, n_ch)).astype(_I32)

    tt_r = lax.broadcasted_iota(_I32, (ch, ch), 0)
    tt_c = lax.broadcasted_iota(_I32, (ch, ch), 1)
    earlier = jnp.where(tt_r < tt_c, 1.0, 0.0).astype(_BF16)
    for c in range(n_ch):
        cols = slice(c * ch, (c + 1) * ch)
        rank = _dot(chosen[:, cols], earlier)
        offs = w0_ec[:, c:c + 1] + rank
        p1 = jnp.sum(jnp.where(sel1[:, cols], offs, 0.0), axis=0, keepdims=True)
        p2 = jnp.sum(jnp.where(sel2[:, cols], offs, 0.0), axis=0, keepdims=True)
        pos_ref[0:1, cols] = p1.astype(_I32)
        pos_ref[1:2, cols] = p2.astype(_I32)

    start = (lax.broadcasted_iota(_I32, (1, n_blk), 1) * row_tile).astype(_F32)
    owner = jnp.sum(jnp.where(end_col <= start, 1.0, 0.0), axis=0, keepdims=True)
    n_valid = jnp.sum(tot_col, axis=0, keepdims=True) * (1.0 / row_tile)
    blktab_ref[0:1, :] = jnp.minimum(owner, n_e - 1.0).astype(_I32)
    blktab_ref[1:2, :] = jnp.broadcast_to(n_valid, (1, n_blk)).astype(_I32)


def _route(logits_t, n_blk, row_tile):
    n_e, t_rows = logits_t.shape
    n_ch = t_rows // TOKEN_CHUNK
    full = lambda s: pl.BlockSpec(s, lambda: (0,) * len(s))
    return pl.pallas_call(
        functools.partial(_route_kernel, n_blk=n_blk, row_tile=row_tile),
        in_specs=[full((n_e, t_rows))],
        out_specs=(full((2, t_rows)), full((2, t_rows)), full((3, n_e, n_ch)), full((2, n_blk))),
        out_shape=(jax.ShapeDtypeStruct((2, t_rows), _I32),
                   jax.ShapeDtypeStruct((2, t_rows), _F32),
                   jax.ShapeDtypeStruct((3, n_e, n_ch), _I32),
                   jax.ShapeDtypeStruct((2, n_blk), _I32)),
        compiler_params=pltpu.CompilerParams(vmem_limit_bytes=VMEM_LIMIT),
        name="route_sort",
    )(logits_t)


def _dispatch_kernel(w0_ref, sl_ref, end_ref, nv_ref, hn_ref, posl_ref, x_hbm,
                     stage, sems, zero_scr, zsems, tsems, carry, *, n_e, n_ch, row_tile, min_blocks):
    c = pl.program_id(0)
    ch = hn_ref.shape[0]
    piece = WINDOW_PIECE
    n_pieces = ch // piece
    slot = c % 2
    n_blocks = x_hbm.shape[0] // row_tile

    def for_each_unused_block(fn):
        for b in range(min_blocks, n_blocks):
            @pl.when(b >= nv_ref[0])
            def _(b=b):
                fn(pltpu.make_async_copy(zero_scr, x_hbm.at[pl.ds(b * row_tile, row_tile)],
                                         tsems.at[b - min_blocks]))

    def piece_copy(cc, sl, e, k):
        row0 = pl.multiple_of(w0_ref[e * n_ch + cc] + k * piece, SEG_ALIGN)
        return pltpu.make_async_copy(
            stage.at[sl, k, pl.ds(e * piece, piece)],
            x_hbm.at[pl.ds(row0, piece)], sems.at[sl, e, k])

    def for_each_piece(cc, sl, fn):
        for e in range(n_e):
            fn(piece_copy(cc, sl, e, 0))
            for k in range(1, n_pieces):
                @pl.when(sl_ref[e * n_ch + cc] > k * piece)
                def _(e=e, k=k):
                    fn(piece_copy(cc, sl, e, k))

    @pl.when(c == 0)
    def _():
        zero_scr[...] = jnp.zeros_like(zero_scr)
        carry[...] = jnp.zeros_like(carry)
        fills = [pltpu.make_async_copy(
            zero_scr,
            x_hbm.at[pl.ds(pl.multiple_of(end_ref[e * n_ch] - row_tile, SEG_ALIGN), row_tile)],
            zsems.at[e]) for e in range(n_e)]
        for cp in fills:
            cp.start()
        for_each_unused_block(lambda cp: cp.start())
        for cp in fills:
            cp.wait()

    pp = posl_ref[0]
    hn = hn_ref[...]

    def gather_piece(k):
        row_j = lax.broadcasted_iota(_I32, (piece, ch), 0) + k * piece
        parts = []
        for e in range(n_e):
            row_id = row_j + w0_ref[e * n_ch + c]
            hit = (pp[0:1, :] == row_id) | (pp[1:2, :] == row_id)
            parts.append(jnp.where(hit, 1.0, 0.0).astype(_BF16))
        onehot = jnp.concatenate(parts, axis=0)
        stage[slot, k] = _dot(onehot, hn).astype(_BF16)

    gather_piece(0)
    for e in range(n_e):
        head = slice(e * piece, e * piece + SEG_ALIGN)
        stage[slot, 0, head, :] = stage[slot, 0, head, :] + carry[e]
    for k in range(1, n_pieces):
        reaches = sl_ref[c] > k * piece
        for e in range(1, n_e):
            reaches = jnp.logical_or(reaches, sl_ref[e * n_ch + c] > k * piece)

        @pl.when(reaches)
        def _(k=k):
            gather_piece(k)
    for e in range(n_e):
        used = sl_ref[e * n_ch + c]
        gathered = ((jnp.maximum(used, 1) + piece - 1) // piece) * piece
        off = (used // SEG_ALIGN) * SEG_ALIGN
        src = jnp.minimum(off, gathered - SEG_ALIGN)
        grp = stage[slot, src // piece,
                    pl.ds(pl.multiple_of(e * piece + src % piece, SEG_ALIGN), SEG_ALIGN), :]
        carry[e] = jnp.where(off < gathered, grp, jnp.zeros_like(grp))

    @pl.when(c > 0)
    def _():
        for_each_piece(c - 1, 1 - slot, lambda cp: cp.wait())

    for_each_piece(c, slot, lambda cp: cp.start())

    @pl.when(c == n_ch - 1)
    def _():
        for_each_piece(c, slot, lambda cp: cp.wait())
        for_each_unused_block(lambda cp: cp.wait())


def _dispatch(hn, pos_l, w0_flat, seglen_flat, end_flat, n_valid, n_e, n_rows, row_tile):
    t_rows, d = hn.shape
    ch = TOKEN_CHUNK
    n_ch = t_rows // ch
    n_pieces = ch // WINDOW_PIECE
    min_blocks = 2 * t_rows // row_tile
    return pl.pallas_call(
        functools.partial(_dispatch_kernel, n_e=n_e, n_ch=n_ch, row_tile=row_tile,
                          min_blocks=min_blocks),
        grid_spec=pltpu.PrefetchScalarGridSpec(
            num_scalar_prefetch=4,
            grid=(n_ch,),
            in_specs=[pl.BlockSpec((ch, d), lambda c, *_: (c, 0)),
                      pl.BlockSpec((1, 2, ch), lambda c, *_: (c, 0, 0))],
            out_specs=pl.BlockSpec(memory_space=pl.ANY),
            scratch_shapes=[pltpu.VMEM((2, n_pieces, n_e * WINDOW_PIECE, d), _BF16),
                            pltpu.SemaphoreType.DMA((2, n_e, n_pieces)),
                            pltpu.VMEM((row_tile, d), _BF16),
                            pltpu.SemaphoreType.DMA((n_e,)),
                            pltpu.SemaphoreType.DMA((n_rows // row_tile - min_blocks,)),
                            pltpu.VMEM((n_e, SEG_ALIGN, d), _BF16)]),
        out_shape=jax.ShapeDtypeStruct((n_rows, d), _BF16),
        compiler_params=pltpu.CompilerParams(
            dimension_semantics=("arbitrary",), vmem_limit_bytes=VMEM_LIMIT),
        name="moe_dispatch",
    )(w0_flat, seglen_flat, end_flat, n_valid, hn, pos_l)


def _ffn_grouped_kernel(be_ref, nv_ref, end_ref, x_ref, wg_hbm, wu_hbm, wd_hbm, y_ref,
                        wg_buf, wu_buf, wd_buf, stg_g, stg_u, stg_d, sems, state, a_scr,
                        *, n_ch, expert0):
    b = pl.program_id(0)
    n_valid = nv_ref[0]
    valid = b < n_valid
    row_tile = x_ref.shape[0]
    units = WEIGHT_UNITS
    gu_rows = wg_buf.shape[1] // units
    d_rows = wd_buf.shape[1] // units
    ST_SLOT, ST_RESIDENT, ST_NEXT, ST_DONE, ST_STAGE = range(5)

    def unit_copies(e, k, st):
        g0 = pl.multiple_of(k * gu_rows, gu_rows)
        d0 = pl.multiple_of(k * d_rows, SEG_ALIGN)
        return (pltpu.make_async_copy(wg_hbm.at[expert0 + e, pl.ds(g0, gu_rows)], stg_g.at[st],
                                      sems.at[st, 0]),
                pltpu.make_async_copy(wu_hbm.at[expert0 + e, pl.ds(g0, gu_rows)], stg_u.at[st],
                                      sems.at[st, 1]),
                pltpu.make_async_copy(wd_hbm.at[expert0 + e, pl.ds(d0, d_rows)], stg_d.at[st],
                                      sems.at[st, 2]))

    def start(e, k, st):
        for cp in unit_copies(e, k, st):
            cp.start()

    def wait(e, k, st):
        for cp in unit_copies(e, k, st):
            cp.wait()

    def cast_unit(slot, k, st):
        g0 = pl.multiple_of(k * gu_rows, gu_rows)
        d0 = pl.multiple_of(k * d_rows, SEG_ALIGN)
        wg_buf[slot, pl.ds(g0, gu_rows), :] = stg_g[st].astype(_BF16)
        wu_buf[slot, pl.ds(g0, gu_rows), :] = stg_u[st].astype(_BF16)
        wd_buf[slot, pl.ds(d0, d_rows), :] = stg_d[st].astype(_BF16)

    def expert_after(e):
        blk = end_ref[e * n_ch] // row_tile
        return jnp.where(blk < n_valid, be_ref[jnp.minimum(blk, n_valid - 1)], -1)

    def become_resident(slot, e):
        state[ST_SLOT] = slot
        state[ST_RESIDENT] = e
        nxt = expert_after(e)
        state[ST_NEXT] = nxt
        state[ST_DONE] = 0

        @pl.when(nxt >= 0)
        def _():
            start(nxt, 0, 0)
        state[ST_STAGE] = 0

    e_b = be_ref[jnp.minimum(b, n_valid - 1)]

    @pl.when(b == 0)
    def _():
        state[ST_SLOT] = 1
        state[ST_RESIDENT] = -1
        state[ST_NEXT] = e_b
        state[ST_DONE] = 0
        state[ST_STAGE] = 0
        start(e_b, 0, 0)

    s_slot = state[ST_SLOT]
    s_next = state[ST_NEXT]
    s_done = state[ST_DONE]
    s_stage = state[ST_STAGE]
    change = valid & (e_b != state[ST_RESIDENT])
    steady = valid & jnp.logical_not(change) & (s_next >= 0) & (s_done < units)

    @pl.when(change)
    def _():
        other = 1 - s_slot

        @pl.when(s_done < units)
        def _():
            wait(e_b, s_done, s_stage)
            cast_unit(other, s_done, s_stage)

            def body(k, carry):
                start(e_b, k, 0)
                wait(e_b, k, 0)
                cast_unit(other, k, 0)
                return carry
            lax.fori_loop(s_done + 1, units, body, 0)
        become_resident(other, e_b)

    def run_block(slot):
        y_ref[...] = _swiglu(x_ref[...], wg_buf, wu_buf, wd_buf, a_scr, slot).astype(_BF16)

    @pl.when(steady)
    def _():
        wait(s_next, s_done, s_stage)

        @pl.when(s_done + 1 < units)
        def _():
            start(s_next, s_done + 1, 1 - s_stage)
        state[ST_DONE] = s_done + 1
        state[ST_STAGE] = 1 - s_stage
        cast_unit(1 - s_slot, s_done, s_stage)
        run_block(s_slot)

    @pl.when(valid & jnp.logical_not(steady))
    def _():
        run_block(state[ST_SLOT])

    @pl.when(jnp.logical_not(valid))
    def _():
        y_ref[...] = jnp.zeros_like(y_ref)


def _ffn_grouped(x_sorted, wg, wu, wd, expert0, block_expert, n_valid, end_flat, n_ch, row_tile):
    rows, d = x_sorted.shape
    f = wg.shape[2]
    n_blocks = rows // row_tile
    hbm = pl.BlockSpec(memory_space=pl.ANY)
    return pl.pallas_call(
        functools.partial(_ffn_grouped_kernel, n_ch=n_ch, expert0=expert0),
        grid_spec=pltpu.PrefetchScalarGridSpec(
            num_scalar_prefetch=3,
            grid=(n_blocks,),
            in_specs=[pl.BlockSpec((row_tile, d),
                                   lambda b, be, nv, en: (jnp.minimum(b, nv[0] - 1), 0)),
                      hbm, hbm, hbm],
            out_specs=pl.BlockSpec((row_tile, d), lambda b, be, nv, en: (b, 0)),
            scratch_shapes=[pltpu.VMEM((2, d, f), _BF16),
                            pltpu.VMEM((2, d, f), _BF16),
                            pltpu.VMEM((2, f, d), _BF16),
                            pltpu.VMEM((2, d // WEIGHT_UNITS, f), _F32),
                            pltpu.VMEM((2, d // WEIGHT_UNITS, f), _F32),
                            pltpu.VMEM((2, f // WEIGHT_UNITS, d), _F32),
                            pltpu.SemaphoreType.DMA((2, 3)),
                            pltpu.SMEM((5,), _I32),
                            pltpu.VMEM((row_tile, f), _BF16)]),
        out_shape=jax.ShapeDtypeStruct((rows, d), _BF16),
        compiler_params=pltpu.CompilerParams(
            dimension_semantics=("arbitrary",), vmem_limit_bytes=VMEM_LIMIT),
        name="moe_experts",
    )(block_expert, n_valid, end_flat, x_sorted, wg, wu, wd)


def _combine_kernel(w0_ref, sl_ref, res_ref, tok_ref, gfin_ref, y_hbm,
                    out_ref, ybuf, sems, tmp_scr, acc_scr, *, n_e, n_ch):
    c = pl.program_id(0)
    ch = res_ref.shape[0]
    piece = WINDOW_PIECE
    n_pieces = ch // piece

    def piece_copy(cc, slot, e, k):
        row0 = pl.multiple_of(w0_ref[e * n_ch + cc] + k * piece, SEG_ALIGN)
        return pltpu.make_async_copy(
            y_hbm.at[pl.ds(row0, piece)],
            ybuf.at[slot, k, pl.ds(e * piece, piece)], sems.at[slot, e, k])

    def for_each_piece(cc, slot, fn):
        for e in range(n_e):
            fn(piece_copy(cc, slot, e, 0))
            for k in range(1, n_pieces):
                @pl.when(sl_ref[e * n_ch + cc] > k * piece)
                def _(e=e, k=k):
                    fn(piece_copy(cc, slot, e, k))

    @pl.when(c == 0)
    def _():
        ybuf[...] = jnp.zeros_like(ybuf)
        for_each_piece(0, 0, lambda cp: cp.start())

    @pl.when(c + 1 < n_ch)
    def _():
        for_each_piece(c + 1, (c + 1) % 2, lambda cp: cp.start())

    slot = c % 2
    for_each_piece(c, slot, lambda cp: cp.wait())

    per_dot = MXU_DIM // piece
    depth = per_dot * piece
    lane = lax.broadcasted_iota(_I32, (1, depth), 1)
    p1b = jnp.broadcast_to(tok_ref[:, 0:1], (ch, depth))
    p2b = jnp.broadcast_to(tok_ref[:, 1:2], (ch, depth))
    g1b = jnp.broadcast_to(tok_ref[:, 2:3], (ch, depth))
    g2b = jnp.broadcast_to(tok_ref[:, 3:4], (ch, depth))

    def scatter_back(m, k):
        tgt = jnp.full((1, depth), -1.0, _F32)
        for i in range(per_dot):
            e = m * per_dot + i
            row = lane - i * piece + k * piece
            mine = (lane >= i * piece) & (lane < (i + 1) * piece) & (row < sl_ref[e * n_ch + c])
            tgt = jnp.where(mine, (row + w0_ref[e * n_ch + c]).astype(_F32), tgt)
        q = (jnp.where(p1b == tgt, g1b, 0.0) + jnp.where(p2b == tgt, g2b, 0.0)).astype(_BF16)
        return _dot(q, ybuf[slot, k, m * depth:(m + 1) * depth, :])

    def any_reaches(m, k):
        hit = sl_ref[(m * per_dot) * n_ch + c] > k * piece
        for i in range(1, per_dot):
            hit = jnp.logical_or(hit, sl_ref[(m * per_dot + i) * n_ch + c] > k * piece)
        return hit

    acc = res_ref[...]
    for m in range(n_e // per_dot):
        acc = acc + scatter_back(m, 0)
    acc_scr[...] = acc
    for m in range(n_e // per_dot):
        for k in range(1, n_pieces):
            @pl.when(any_reaches(m, k))
            def _(m=m, k=k):
                acc_scr[...] += scatter_back(m, k)
    normed = _rmsnorm(acc_scr[...], gfin_ref[...])
    steps = ch // SUBLANES
    for k in range(tmp_scr.shape[0]):
        tmp_scr[k] = normed[:, k * LANES:(k + 1) * LANES]
    for bb in range(SUBLANES):
        for k in range(tmp_scr.shape[0]):
            out_ref[bb, :, k * LANES:(k + 1) * LANES] = (
                tmp_scr[k, pl.ds(bb, steps, stride=SUBLANES), :])


def _combine(res, tok_tab, gfin, y_sorted, w0_flat, seglen_flat, n_e, bsz, seq):
    t_rows, d = res.shape
    ch = TOKEN_CHUNK
    n_ch = t_rows // ch
    steps = ch // bsz
    return pl.pallas_call(
        functools.partial(_combine_kernel, n_e=n_e, n_ch=n_ch),
        grid_spec=pltpu.PrefetchScalarGridSpec(
            num_scalar_prefetch=2,
            grid=(n_ch,),
            in_specs=[pl.BlockSpec((ch, d), lambda c, w0, sl: (c, 0)),
                      pl.BlockSpec((ch, tok_tab.shape[1]), lambda c, w0, sl: (c, 0)),
                      pl.BlockSpec((1, d), lambda c, w0, sl: (0, 0)),
                      pl.BlockSpec(memory_space=pl.ANY)],
            out_specs=pl.BlockSpec((bsz, steps, d), lambda c, w0, sl: (0, c, 0)),
            scratch_shapes=[pltpu.VMEM((2, ch // WINDOW_PIECE, n_e * WINDOW_PIECE, d), _BF16),
                            pltpu.SemaphoreType.DMA((2, n_e, ch // WINDOW_PIECE)),
                            pltpu.VMEM((d // LANES, ch, LANES), _F32),
                            pltpu.VMEM((ch, d), _F32)]),
        out_shape=jax.ShapeDtypeStruct((bsz, seq, d), _F32),
        compiler_params=pltpu.CompilerParams(
            dimension_semantics=("arbitrary",), vmem_limit_bytes=VMEM_LIMIT),
        name="moe_combine",
    )(w0_flat, seglen_flat, res, tok_tab, gfin, y_sorted)


def _moe_layer(h, hn, logits_t, wg, wu, wd, expert0, gfin, bsz, seq):
    t_rows, d = h.shape
    n_e = logits_t.shape[0]
    tm = EXPERT_ROW_TILE
    n_ch = t_rows // TOKEN_CHUNK
    max_rows = 2 * t_rows + n_e * (SEG_ALIGN + TOKEN_CHUNK + tm)
    n_blocks = -(-max_rows // tm)
    n_blk_pad = -(-n_blocks // LANES) * LANES

    pos, gates, segtab, blktab = _route(logits_t, n_blk_pad, tm)
    pos_l = jnp.transpose(pos.reshape(2, n_ch, TOKEN_CHUNK), (1, 0, 2))
    w0_flat = segtab[0].reshape(-1)
    seglen_flat = segtab[1].reshape(-1)
    n_valid = blktab[1, :1]
    end_flat = segtab[2].reshape(-1)
    x_sorted = _dispatch(hn, pos_l, w0_flat, seglen_flat, end_flat, n_valid,
                         n_e, n_blocks * tm, tm)
    y_sorted = _ffn_grouped(x_sorted, wg, wu, wd, expert0, blktab[0], n_valid, end_flat,
                            n_ch, tm)
    tok_tab = jnp.transpose(jnp.concatenate([pos.astype(_F32), gates], axis=0))
    return _combine(h, tok_tab, gfin, y_sorted,
                    w0_flat, seglen_flat, n_e, bsz, seq)


def kernel(x, norm_mix_g, w_in, ssm_log_dt, ssm_a_re, ssm_a_im, ssm_b_re, ssm_b_im, ssm_c_re, ssm_c_im, ssm_d, ssm_w_glu, ssm_b_glu, pool_w, pool_scale, w_out, norm_ffn_g, ffn_w_gate, ffn_w_up, ffn_w_down, router_w, moe_w_gate, moe_w_up, moe_w_down, final_norm_g):
    bsz, seq, d = x.shape
    depth, n_heads, n_state, n_grp_ch = ssm_b_re.shape
    t_rows = bsz * seq
    assert bsz == SUBLANES and depth % 2 == 0
    groups = n_heads // HEADS_PER_GROUP

    nh = depth * n_heads
    per_head = lambda w: w.reshape((nh,) + w.shape[2:])
    lam, bw_all, cw_all = _discretize(
        per_head(ssm_log_dt), per_head(ssm_a_re), per_head(ssm_a_im),
        per_head(jnp.swapaxes(ssm_b_re, 2, 3)), per_head(jnp.swapaxes(ssm_b_im, 2, 3)),
        per_head(jnp.swapaxes(ssm_c_re, 2, 3)), per_head(jnp.swapaxes(ssm_c_im, 2, 3)))
    lam = lam.reshape(depth, 2 * groups, -1)
    bw = bw_all.reshape((depth, groups) + bw_all.shape[1:])
    cw = cw_all.reshape((depth, groups) + cw_all.shape[1:])
    vec = lambda v: v.reshape(v.shape[0], 1, v.shape[1])
    router_wt = jnp.swapaxes(router_w, 1, 2)

    h = x
    row = lambda v: v.reshape(1, -1)
    out = None
    for i in range(depth):
        is_moe = i % 2 == 1
        j = i // 2
        outs = _mix_layer(
            h, i, vec(norm_mix_g), w_in, bw, lam, cw, vec(ssm_d), ssm_w_glu, vec(ssm_b_glu),
            pool_w, vec(pool_scale), w_out, vec(norm_ffn_g), router_wt if is_moe else None, j)
        if is_moe:
            assert i == depth - 1
            h, hn, logits_t = outs
            n_e = moe_w_gate.shape[1]
            stack = lambda w: w.reshape((-1,) + w.shape[2:])
            out = _moe_layer(h, hn, logits_t, stack(moe_w_gate), stack(moe_w_up),
                             stack(moe_w_down), j * n_e, row(final_norm_g), bsz, seq)
        else:
            h, hn = outs
            h = _ffn(hn, ffn_w_gate, ffn_w_up, ffn_w_down, j, h)
    return out
```

```python
import functools
import math

import jax
import jax.numpy as jnp
from jax import lax
from jax.experimental import pallas as pl
from jax.experimental.pallas import tpu as pltpu

RMS_EPS = 1e-6
POOL_WINDOWS = (2, 4, 8, 16)
A_RE_MAX = -1e-4
GELU_C0 = math.sqrt(2.0 / math.pi)
GELU_C1 = 0.044715

SUBLANES = 8
LANES = 128
MXU_DIM = 256

HEADS_PER_GROUP = 16
TIME_TILE = 64
SCAN_COLS = 512
FFN_ROW_TILE = 512
EXPERT_ROW_TILE = 512
FFN_COL_CHUNK = 256
WEIGHT_UNITS = 8
TOKEN_CHUNK = MXU_DIM
SEG_ALIGN = 2 * SUBLANES
WINDOW_PIECE = 96
VMEM_LIMIT = 56 * 1024 * 1024

_F32 = jnp.float32
_BF16 = jnp.bfloat16
_I32 = jnp.int32


def _dot(a, b):
    return jnp.dot(a, b, preferred_element_type=_F32)


def _dot_nt(a, b):
    return lax.dot_general(a, b, (((1,), (1,)), ((), ())), preferred_element_type=_F32)


def _rmsnorm(x, g):
    inv = lax.rsqrt(jnp.mean(x * x, axis=-1, keepdims=True) + RMS_EPS)
    return x * inv * g


def _sigmoid(x):
    return 1.0 / (1.0 + jnp.exp(-x))


def _ceil_to(x, m):
    return jnp.floor((x + (m - 1.0)) * (1.0 / m)) * m


def _window_pieces(chunk):
    return -(-(chunk + SEG_ALIGN - 1) // WINDOW_PIECE)


def _discretize_kernel(log_dt_ref, a_re_ref, a_im_ref, b_re_ref, b_im_ref, c_re_ref, c_im_ref,
                       lam_ref, bw_ref, cw_ref):
    dt = jnp.exp(log_dt_ref[...])
    ar = jnp.minimum(a_re_ref[...], A_RE_MAX)
    ai = a_im_ref[...]
    mag = jnp.exp(ar * dt)
    lam_re = mag * jnp.cos(ai * dt)
    lam_im = mag * jnp.sin(ai * dt)
    den = ar * ar + ai * ai
    nr = lam_re - 1.0
    ni = lam_im
    coef_re = (nr * ar + ni * ai) / den
    coef_im = (ni * ar - nr * ai) / den
    br = b_re_ref[...]
    bi = b_im_ref[...]
    bb_re = (coef_re * br - coef_im * bi).astype(_BF16)
    bb_im = (coef_re * bi + coef_im * br).astype(_BF16)
    c_re = c_re_ref[...].astype(_BF16)
    c_im_neg = (-c_im_ref[...]).astype(_BF16)

    n, g, p = br.shape
    hpg = n // lam_ref.shape[0]
    half = hpg * p
    bw_ref[...] = jnp.zeros_like(bw_ref)
    cw_ref[...] = jnp.zeros_like(cw_ref)
    for h in range(n):
        q, hl = divmod(h, hpg)
        st = slice(hl * p, (hl + 1) * p)
        st_im = slice(half + hl * p, half + (hl + 1) * p)
        ch = slice(hl * g, (hl + 1) * g)
        lam_ref[q, 0:1, st] = lam_re[h]
        lam_ref[q, 1:2, st] = lam_im[h]
        bw_ref[q, ch, st] = bb_re[h]
        bw_ref[q, ch, st_im] = bb_im[h]
        cw_ref[q, st, ch] = c_re[h]
        cw_ref[q, st_im, ch] = c_im_neg[h]


def _discretize(log_dt, a_re, a_im, b_re_t, b_im_t, c_re_t, c_im_t):
    n, g, p = b_re_t.shape
    hpg = HEADS_PER_GROUP
    nq = n // hpg
    full3 = lambda s: pl.BlockSpec(s, lambda: (0, 0, 0))
    return pl.pallas_call(
        _discretize_kernel,
        out_shape=(jax.ShapeDtypeStruct((nq, 2, hpg * p), _F32),
                   jax.ShapeDtypeStruct((nq, hpg * g, 2 * hpg * p), _BF16),
                   jax.ShapeDtypeStruct((nq, 2 * hpg * p, hpg * g), _BF16)),
        in_specs=[full3((n, 1, 1)), full3((n, 1, p)), full3((n, 1, p)),
                  full3((n, g, p)), full3((n, g, p)), full3((n, p, g)), full3((n, p, g))],
        out_specs=(full3((nq, 2, hpg * p)), full3((nq, hpg * g, 2 * hpg * p)),
                   full3((nq, 2 * hpg * p, hpg * g))),
        name="ssm_discretize",
    )(log_dt.reshape(n, 1, 1), a_re.reshape(n, 1, p), a_im.reshape(n, 1, p),
      b_re_t, b_im_t, c_re_t, c_im_t)


def _mix_kernel(*refs, time_tile, with_router, batch_major_in):
    (h_ref, gmix_ref, win_f32, bw_ref, lam_ref, cw_ref, dskip_ref, wglu_f32,
     bglu_ref, poolw_f32, pscale_ref, wout_f32, gffn_ref) = refs[:13]
    refs = refs[13:]
    rw_ref = logits_ref = xin_scr = None
    if with_router:
        rw_ref, hout_ref, hn_ref, logits_ref = refs[:4]
        refs = refs[4:]
    else:
        hout_ref, hn_ref = refs[:2]
        refs = refs[2:]
    (s_scr, state_scr, ext_scr, mixed_scr, u_save, h_save,
     win_ref, wglu_ref, poolw_ref, wout_ref) = refs[:10]
    if batch_major_in:
        xin_scr = refs[10]

    step = pl.program_id(0)
    rows = time_tile * SUBLANES
    d_ssm = dskip_ref.shape[1]
    n_groups = bw_ref.shape[0]
    gin = bw_ref.shape[1]
    gstate = bw_ref.shape[2] // 2
    hist = ext_scr.shape[0] - rows

    @pl.when(step == 0)
    def _():
        state_scr[...] = jnp.zeros_like(state_scr)
        ext_scr[0:hist, :] = jnp.zeros((hist, ext_scr.shape[1]), _F32)
        win_ref[...] = win_f32[...].astype(_BF16)
        wglu_ref[...] = wglu_f32[...].astype(_BF16)
        poolw_ref[...] = poolw_f32[...].astype(_BF16)
        wout_ref[...] = wout_f32[...].astype(_BF16)

    if batch_major_in:
        for bb in range(SUBLANES):
            for k in range(xin_scr.shape[0]):
                xin_scr[k, pl.ds(bb, time_tile, stride=SUBLANES), :] = (
                    h_ref[bb, :, k * LANES:(k + 1) * LANES])
        h = jnp.concatenate([xin_scr[k] for k in range(xin_scr.shape[0])], axis=1)
    else:
        h = h_ref[...]
    h_save[...] = h
    hn = _rmsnorm(h, gmix_ref[...]).astype(_BF16)
    proj = _dot(hn, win_ref[...])
    u_ssm = proj[:, :d_ssm]
    u_save[...] = u_ssm
    ext_scr[hist:, :] = proj[:, d_ssm:]

    def drive(q):
        ug = u_ssm[:, q * gin:(q + 1) * gin].astype(_BF16)
        s_scr[q] = _dot(ug, bw_ref[q])

    def scan(q, c):
        re0 = c * SCAN_COLS
        im0 = gstate + c * SCAN_COLS
        lr = jnp.broadcast_to(lam_ref[2 * q:2 * q + 1, re0:re0 + SCAN_COLS],
                              (SUBLANES, SCAN_COLS))
        li = jnp.broadcast_to(lam_ref[2 * q + 1:2 * q + 2, re0:re0 + SCAN_COLS],
                              (SUBLANES, SCAN_COLS))
        sre = state_scr[q, :, re0:re0 + SCAN_COLS]
        sim = state_scr[q, :, im0:im0 + SCAN_COLS]
        for t in range(time_tile):
            r0 = t * SUBLANES
            bre = s_scr[q, r0:r0 + SUBLANES, re0:re0 + SCAN_COLS]
            bim = s_scr[q, r0:r0 + SUBLANES, im0:im0 + SCAN_COLS]
            sre, sim = (lr * sre - li * sim + bre, lr * sim + li * sre + bim)
            s_scr[q, r0:r0 + SUBLANES, re0:re0 + SCAN_COLS] = sre
            s_scr[q, r0:r0 + SUBLANES, im0:im0 + SCAN_COLS] = sim
        state_scr[q, :, re0:re0 + SCAN_COLS] = sre
        state_scr[q, :, im0:im0 + SCAN_COLS] = sim

    def pool():
        n_ext = rows + hist
        pos = step * time_tile + (lax.broadcasted_iota(_I32, (rows, LANES), 0) // SUBLANES)
        for g, w in enumerate(POOL_WINDOWS):
            c0 = g * LANES
            e = ext_scr[:, c0:c0 + LANES]
            acc = e
            n_acc = n_ext
            span = 1
            while span < w:
                sh = span * SUBLANES
                acc = acc[sh:, :] + acc[:n_acc - sh, :]
                n_acc -= sh
                span *= 2
            wsum = acc[n_acc - rows:, :]
            cnt = jnp.minimum(pos + 1, w).astype(_F32)
            pooled = wsum / cnt - e[hist:, :]
            mg = _dot(pooled.astype(_BF16), poolw_ref[g]) * pscale_ref[:, c0:c0 + LANES]
            mixed_scr[:, d_ssm + c0:d_ssm + c0 + LANES] = mg.astype(_BF16)
        ext_scr[0:hist, :] = ext_scr[rows:rows + hist, :]

    def readout(q):
        cols = slice(q * gin, (q + 1) * gin)
        yq = _dot(s_scr[q].astype(_BF16), cw_ref[q])
        yq = yq + dskip_ref[:, cols] * u_save[:, cols]
        hq = 0.5 * yq * (1.0 + jnp.tanh(GELU_C0 * (yq + GELU_C1 * (yq * yq * yq))))
        mixed_scr[:, cols] = hq.astype(_BF16)

    def glu_gate():
        hg = mixed_scr[:, :d_ssm]
        gate = _sigmoid(_dot(hg, wglu_ref[...]) + bglu_ref[...])
        mixed_scr[:, :d_ssm] = (hg.astype(_F32) * gate).astype(_BF16)

    def project_out():
        hout = h_save[...] + _dot(mixed_scr[...], wout_ref[...])
        hout_ref[...] = hout
        hn2 = _rmsnorm(hout, gffn_ref[...])
        hn_hi = hn2.astype(_BF16)
        hn_ref[...] = hn_hi
        if with_router:
            hn_lo = (hn2 - hn_hi.astype(_F32)).astype(_BF16)
            rw = rw_ref[...]
            rw_hi = rw.astype(_BF16)
            rw_lo = (rw - rw_hi.astype(_F32)).astype(_BF16)
            logits_ref[...] = (_dot_nt(rw_hi, hn_hi) + _dot_nt(rw_hi, hn_lo)
                               + _dot_nt(rw_lo, hn_hi))

    for q in range(n_groups):
        drive(q)
    for c in range(gstate // SCAN_COLS):
        for q in range(n_groups):
            scan(q, c)
        if c == 0:
            pool()
    for q in range(n_groups):
        readout(q)
    glu_gate()
    project_out()


def _mix_layer(h, layer, gmix, w_in, bw, lam, cw, dskip, wglu, bglu, poolw, pscale, wout,
               gffn, router_wt, moe_layer):
    batch_major_in = h.ndim == 3
    d = h.shape[-1]
    t_rows = h.size // d
    rows = TIME_TILE * SUBLANES
    n_steps = t_rows // rows
    d_ssm = dskip.shape[2]
    d_pool = pscale.shape[2]
    n_groups, _, two_gstate = bw.shape[1:]
    hist = max(POOL_WINDOWS) * SUBLANES
    with_router = router_wt is not None

    def const(a, idx):
        nd = a.ndim
        return pl.BlockSpec((None,) + a.shape[1:], lambda i, nd=nd, idx=idx: (idx,) + (0,) * (nd - 1),
                            pipeline_mode=pl.Buffered(1))

    row_blk = lambda: pl.BlockSpec((rows, d), lambda i: (i, 0))
    ins = [h, gmix, w_in, bw, lam, cw, dskip, wglu, bglu, poolw, pscale, wout, gffn]
    h_spec = (pl.BlockSpec((h.shape[0], TIME_TILE, d), lambda i: (0, i, 0))
              if batch_major_in else row_blk())
    in_specs = [h_spec] + [const(a, layer) for a in ins[1:]]
    out_shape = [jax.ShapeDtypeStruct((t_rows, d), _F32),
                 jax.ShapeDtypeStruct((t_rows, d), _BF16)]
    out_specs = [row_blk(), row_blk()]
    if with_router:
        n_e = router_wt.shape[1]
        ins.append(router_wt)
        in_specs.append(const(router_wt, moe_layer))
        out_shape.append(jax.ShapeDtypeStruct((n_e, t_rows), _F32))
        out_specs.append(pl.BlockSpec((n_e, rows), lambda i: (0, i)))
    scratch = [
        pltpu.VMEM((n_groups, rows, two_gstate), _F32),
        pltpu.VMEM((n_groups, SUBLANES, two_gstate), _F32),
        pltpu.VMEM((rows + hist, d_pool), _F32),
        pltpu.VMEM((rows, d_ssm + d_pool), _BF16),
        pltpu.VMEM((rows, d_ssm), _F32),
        pltpu.VMEM((rows, d), _F32),
        pltpu.VMEM(w_in.shape[1:], _BF16),
        pltpu.VMEM(wglu.shape[1:], _BF16),
        pltpu.VMEM(poolw.shape[1:], _BF16),
        pltpu.VMEM(wout.shape[1:], _BF16),
    ]
    if batch_major_in:
        scratch.append(pltpu.VMEM((d // LANES, rows, LANES), _F32))
    return pl.pallas_call(
        functools.partial(_mix_kernel, time_tile=TIME_TILE, with_router=with_router,
                          batch_major_in=batch_major_in),
        grid=(n_steps,),
        in_specs=in_specs,
        out_specs=out_specs,
        out_shape=out_shape,
        scratch_shapes=scratch,
        compiler_params=pltpu.CompilerParams(
            dimension_semantics=("arbitrary",), vmem_limit_bytes=VMEM_LIMIT),
        name="mix_router" if with_router else "mix",
    )(*ins)


def _swiglu(x, wg_ref, wu_ref, wd_ref, a_scr, slot):
    tf = wg_ref.shape[2]
    c0 = 0
    while c0 < tf:
        cw = min(FFN_COL_CHUNK, tf - c0)
        g = _dot(x, wg_ref[slot, :, c0:c0 + cw])
        u = _dot(x, wu_ref[slot, :, c0:c0 + cw])
        a_scr[:, c0:c0 + cw] = (g * _sigmoid(g) * u).astype(_BF16)
        c0 += cw
    return _dot(a_scr[...], wd_ref[slot])


def _ffn_kernel(x_ref, wg_hbm, wu_hbm, wd_hbm, res_ref, out_ref,
                wg_buf, wu_buf, wd_buf, stg_g, stg_u, stg_d, sems, a_scr, *, layer):
    units = WEIGHT_UNITS
    gu_rows = wg_buf.shape[1] // units
    d_rows = wd_buf.shape[1] // units

    @pl.when(pl.program_id(0) == 0)
    def _():
        def copies(k):
            st = k % 2
            return (pltpu.make_async_copy(wg_hbm.at[layer, pl.ds(k * gu_rows, gu_rows)],
                                          stg_g.at[st], sems.at[st, 0]),
                    pltpu.make_async_copy(wu_hbm.at[layer, pl.ds(k * gu_rows, gu_rows)],
                                          stg_u.at[st], sems.at[st, 1]),
                    pltpu.make_async_copy(wd_hbm.at[layer, pl.ds(k * d_rows, d_rows)],
                                          stg_d.at[st], sems.at[st, 2]))

        for cp in copies(0):
            cp.start()
        for k in range(units):
            if k + 1 < units:
                for cp in copies(k + 1):
                    cp.start()
            for cp in copies(k):
                cp.wait()
            st = k % 2
            wg_buf[0, k * gu_rows:(k + 1) * gu_rows, :] = stg_g[st].astype(_BF16)
            wu_buf[0, k * gu_rows:(k + 1) * gu_rows, :] = stg_u[st].astype(_BF16)
            wd_buf[0, k * d_rows:(k + 1) * d_rows, :] = stg_d[st].astype(_BF16)

    out_ref[...] = res_ref[...] + _swiglu(x_ref[...], wg_buf, wu_buf, wd_buf, a_scr, 0)


def _ffn(x, wg, wu, wd, layer, res):
    t_rows, d = x.shape
    f = wg.shape[2]
    tm = FFN_ROW_TILE
    hbm = pl.BlockSpec(memory_space=pl.ANY)
    return pl.pallas_call(
        functools.partial(_ffn_kernel, layer=layer),
        grid=(t_rows // tm,),
        in_specs=[pl.BlockSpec((tm, d), lambda i: (i, 0)), hbm, hbm, hbm,
                  pl.BlockSpec((tm, d), lambda i: (i, 0))],
        out_specs=pl.BlockSpec((tm, d), lambda i: (i, 0)),
        out_shape=jax.ShapeDtypeStruct((t_rows, d), _F32),
        scratch_shapes=[pltpu.VMEM((1, d, f), _BF16),
                        pltpu.VMEM((1, d, f), _BF16),
                        pltpu.VMEM((1, f, d), _BF16),
                        pltpu.VMEM((2, d // WEIGHT_UNITS, f), _F32),
                        pltpu.VMEM((2, d // WEIGHT_UNITS, f), _F32),
                        pltpu.VMEM((2, f // WEIGHT_UNITS, d), _F32),
                        pltpu.SemaphoreType.DMA((2, 3)),
                        pltpu.VMEM((tm, f), _BF16)],
        compiler_params=pltpu.CompilerParams(
            dimension_semantics=("arbitrary",), vmem_limit_bytes=VMEM_LIMIT),
        name="ffn_dense",
    )(x, wg, wu, wd, res)


def _route_kernel(lt_ref, pos_ref, gates_ref, segtab_ref, blktab_ref, *, n_blk, row_tile):
    l = lt_ref[...]
    n_e, t_rows = l.shape
    ch = TOKEN_CHUNK
    n_ch = t_rows // ch
    ie = lax.broadcasted_iota(_I32, l.shape, 0)
    m1 = jnp.max(l, axis=0, keepdims=True)
    i1 = jnp.min(jnp.where(l == m1, ie, n_e), axis=0, keepdims=True)
    l2 = jnp.where(ie == i1, -jnp.inf, l)
    m2 = jnp.max(l2, axis=0, keepdims=True)
    i2 = jnp.min(jnp.where(l2 == m2, ie, n_e), axis=0, keepdims=True)
    e2 = jnp.exp(m2 - m1)
    den = 1.0 + e2
    gates_ref[0:1, :] = 1.0 / den
    gates_ref[1:2, :] = e2 / den
    sel1 = ie == i1
    sel2 = ie == i2
    chosen = jnp.where(sel1 | sel2, 1.0, 0.0).astype(_BF16)

    chunk_of_t = lax.broadcasted_iota(_I32, (n_ch, t_rows), 1) // ch
    in_chunk = jnp.where(chunk_of_t == lax.broadcasted_iota(_I32, (n_ch, t_rows), 0),
                         1.0, 0.0).astype(_BF16)
    n_ec = _dot_nt(chosen, in_chunk)
    n_ce = _dot_nt(in_chunk, chosen)
    cc_r = lax.broadcasted_iota(_I32, (n_ch, n_ch), 0)
    cc_c = lax.broadcasted_iota(_I32, (n_ch, n_ch), 1)
    before = jnp.where(cc_r < cc_c, 1.0, 0.0).astype(_BF16)
    after = jnp.where(cc_c < cc_r, 1.0, 0.0).astype(_BF16)
    segoff_ec = _dot(n_ec.astype(_BF16), before)
    segoff_ce = _dot(after, n_ce.astype(_BF16))
    def window(segoff, n):
        start = jnp.floor(segoff * (1.0 / SEG_ALIGN)) * SEG_ALIGN
        used = segoff - start + n
        return start, used, start + _ceil_to(jnp.maximum(used, 1.0), float(WINDOW_PIECE))

    wstart_ec, used_ec, wend_ec = window(segoff_ec, n_ec)
    _, _, wend_ce = window(segoff_ce, n_ce)
    tot_col = _ceil_to(jnp.max(wend_ec, axis=1, keepdims=True), float(row_tile))
    tot_row = _ceil_to(jnp.max(wend_ce, axis=0, keepdims=True), float(row_tile))
    ee_r = lax.broadcasted_iota(_I32, (n_e, n_e), 0)
    ee_c = lax.broadcasted_iota(_I32, (n_e, n_e), 1)
    base_col = jnp.sum(jnp.where(ee_c < ee_r, tot_row, 0.0), axis=1, keepdims=True)
    w0_ec = base_col + segoff_ec
    end_col = base_col + tot_col
    segtab_ref[0] = (base_col + wstart_ec).astype(_I32)
    segtab_ref[1] = used_ec.astype(_I32)
    segtab_ref[2] = jnp.broadcast_to(end_col, (n_e, n_ch)).astype(_I32)

    tt_r = lax.broadcasted_iota(_I32, (ch, ch), 0)
    tt_c = lax.broadcasted_iota(_I32, (ch, ch), 1)
    earlier = jnp.where(tt_r < tt_c, 1.0, 0.0).astype(_BF16)
    for c in range(n_ch):
        cols = slice(c * ch, (c + 1) * ch)
        rank = _dot(chosen[:, cols], earlier)
        offs = w0_ec[:, c:c + 1] + rank
        p1 = jnp.sum(jnp.where(sel1[:, cols], offs, 0.0), axis=0, keepdims=True)
        p2 = jnp.sum(jnp.where(sel2[:, cols], offs, 0.0), axis=0, keepdims=True)
        pos_ref[0:1, cols] = p1.astype(_I32)
        pos_ref[1:2, cols] = p2.astype(_I32)

    start = (lax.broadcasted_iota(_I32, (1, n_blk), 1) * row_tile).astype(_F32)
    owner = jnp.sum(jnp.where(end_col <= start, 1.0, 0.0), axis=0, keepdims=True)
    n_valid = jnp.sum(tot_col, axis=0, keepdims=True) * (1.0 / row_tile)
    blktab_ref[0:1, :] = jnp.minimum(owner, n_e - 1.0).astype(_I32)
    blktab_ref[1:2, :] = jnp.broadcast_to(n_valid, (1, n_blk)).astype(_I32)


def _route(logits_t, n_blk, row_tile):
    n_e, t_rows = logits_t.shape
    n_ch = t_rows // TOKEN_CHUNK
    full = lambda s: pl.BlockSpec(s, lambda: (0,) * len(s))
    return pl.pallas_call(
        functools.partial(_route_kernel, n_blk=n_blk, row_tile=row_tile),
        in_specs=[full((n_e, t_rows))],
        out_specs=(full((2, t_rows)), full((2, t_rows)), full((3, n_e, n_ch)), full((2, n_blk))),
        out_shape=(jax.ShapeDtypeStruct((2, t_rows), _I32),
                   jax.ShapeDtypeStruct((2, t_rows), _F32),
                   jax.ShapeDtypeStruct((3, n_e, n_ch), _I32),
                   jax.ShapeDtypeStruct((2, n_blk), _I32)),
        compiler_params=pltpu.CompilerParams(vmem_limit_bytes=VMEM_LIMIT),
        name="route_sort",
    )(logits_t)


def _dispatch_kernel(w0_ref, sl_ref, end_ref, nv_ref, hn_ref, posl_ref, x_hbm,
                     stage, sems, zero_scr, zsems, tsems, carry, *, n_e, n_ch, row_tile, min_blocks):
    c = pl.program_id(0)
    ch = hn_ref.shape[0]
    piece = WINDOW_PIECE
    n_pieces = stage.shape[1]
    slot = c % 2
    n_blocks = x_hbm.shape[0] // row_tile

    def for_each_unused_block(fn):
        for b in range(min_blocks, n_blocks):
            @pl.when(b >= nv_ref[0])
            def _(b=b):
                fn(pltpu.make_async_copy(zero_scr, x_hbm.at[pl.ds(b * row_tile, row_tile)],
                                         tsems.at[b - min_blocks]))

    def piece_copy(cc, sl, e, k):
        row0 = pl.multiple_of(w0_ref[e * n_ch + cc] + k * piece, SEG_ALIGN)
        return pltpu.make_async_copy(
            stage.at[sl, k, pl.ds(e * piece, piece)],
            x_hbm.at[pl.ds(row0, piece)], sems.at[sl, e, k])

    def for_each_piece(cc, sl, fn):
        for e in range(n_e):
            fn(piece_copy(cc, sl, e, 0))
            used = sl_ref[e * n_ch + cc]

            def later(k, e=e, used=used):
                @pl.when(used > k * piece)
                def _():
                    fn(piece_copy(cc, sl, e, k))
                    if k + 1 < n_pieces:
                        later(k + 1)
            if n_pieces > 1:
                later(1)

    @pl.when(c == 0)
    def _():
        zero_scr[...] = jnp.zeros_like(zero_scr)
        carry[...] = jnp.zeros_like(carry)
        fills = [pltpu.make_async_copy(
            zero_scr,
            x_hbm.at[pl.ds(pl.multiple_of(end_ref[e * n_ch] - row_tile, SEG_ALIGN), row_tile)],
            zsems.at[e]) for e in range(n_e)]
        for cp in fills:
            cp.start()
        for_each_unused_block(lambda cp: cp.start())
        for cp in fills:
            cp.wait()

    pp = posl_ref[0]
    hn = hn_ref[...]

    def gather_piece(k):
        row_j = lax.broadcasted_iota(_I32, (piece, ch), 0) + k * piece
        parts = []
        for e in range(n_e):
            row_id = row_j + w0_ref[e * n_ch + c]
            hit = (pp[0:1, :] == row_id) | (pp[1:2, :] == row_id)
            parts.append(jnp.where(hit, 1.0, 0.0).astype(_BF16))
        onehot = jnp.concatenate(parts, axis=0)
        stage[slot, k] = _dot(onehot, hn).astype(_BF16)

    gather_piece(0)
    for e in range(n_e):
        head = slice(e * piece, e * piece + SEG_ALIGN)
        stage[slot, 0, head, :] = stage[slot, 0, head, :] + carry[e]
    longest = sl_ref[c]
    for e in range(1, n_e):
        longest = jnp.maximum(longest, sl_ref[e * n_ch + c])

    def later(k):
        @pl.when(longest > k * piece)
        def _():
            gather_piece(k)
            if k + 1 < n_pieces:
                later(k + 1)
    if n_pieces > 1:
        later(1)
    for e in range(n_e):
        used = sl_ref[e * n_ch + c]
        gathered = ((jnp.maximum(used, 1) + piece - 1) // piece) * piece
        off = (used // SEG_ALIGN) * SEG_ALIGN
        src = jnp.minimum(off, gathered - SEG_ALIGN)
        grp = stage[slot, src // piece,
                    pl.ds(pl.multiple_of(e * piece + src % piece, SEG_ALIGN), SEG_ALIGN), :]
        carry[e] = jnp.where(off < gathered, grp, jnp.zeros_like(grp))

    @pl.when(c > 0)
    def _():
        for_each_piece(c - 1, 1 - slot, lambda cp: cp.wait())

    for_each_piece(c, slot, lambda cp: cp.start())

    @pl.when(c == n_ch - 1)
    def _():
        for_each_piece(c, slot, lambda cp: cp.wait())
        for_each_unused_block(lambda cp: cp.wait())


def _dispatch(hn, pos_l, w0_flat, seglen_flat, end_flat, n_valid, n_e, n_rows, row_tile):
    t_rows, d = hn.shape
    ch = TOKEN_CHUNK
    n_ch = t_rows // ch
    n_pieces = _window_pieces(ch)
    min_blocks = 2 * t_rows // row_tile
    return pl.pallas_call(
        functools.partial(_dispatch_kernel, n_e=n_e, n_ch=n_ch, row_tile=row_tile,
                          min_blocks=min_blocks),
        grid_spec=pltpu.PrefetchScalarGridSpec(
            num_scalar_prefetch=4,
            grid=(n_ch,),
            in_specs=[pl.BlockSpec((ch, d), lambda c, *_: (c, 0)),
                      pl.BlockSpec((1, 2, ch), lambda c, *_: (c, 0, 0))],
            out_specs=pl.BlockSpec(memory_space=pl.ANY),
            scratch_shapes=[pltpu.VMEM((2, n_pieces, n_e * WINDOW_PIECE, d), _BF16),
                            pltpu.SemaphoreType.DMA((2, n_e, n_pieces)),
                            pltpu.VMEM((row_tile, d), _BF16),
                            pltpu.SemaphoreType.DMA((n_e,)),
                            pltpu.SemaphoreType.DMA((n_rows // row_tile - min_blocks,)),
                            pltpu.VMEM((n_e, SEG_ALIGN, d), _BF16)]),
        out_shape=jax.ShapeDtypeStruct((n_rows, d), _BF16),
        compiler_params=pltpu.CompilerParams(
            dimension_semantics=("arbitrary",), vmem_limit_bytes=VMEM_LIMIT),
        name="moe_dispatch",
    )(w0_flat, seglen_flat, end_flat, n_valid, hn, pos_l)


def _ffn_grouped_kernel(be_ref, nv_ref, end_ref, x_ref, wg_hbm, wu_hbm, wd_hbm, y_ref,
                        wg_buf, wu_buf, wd_buf, stg_g, stg_u, stg_d, sems, state, a_scr,
                        *, n_ch, expert0):
    b = pl.program_id(0)
    n_valid = nv_ref[0]
    valid = b < n_valid
    row_tile = x_ref.shape[0]
    units = WEIGHT_UNITS
    gu_rows = wg_buf.shape[1] // units
    d_rows = wd_buf.shape[1] // units
    ST_SLOT, ST_RESIDENT, ST_NEXT, ST_DONE, ST_STAGE = range(5)

    def unit_copies(e, k, st):
        g0 = pl.multiple_of(k * gu_rows, gu_rows)
        d0 = pl.multiple_of(k * d_rows, SEG_ALIGN)
        return (pltpu.make_async_copy(wg_hbm.at[expert0 + e, pl.ds(g0, gu_rows)], stg_g.at[st],
                                      sems.at[st, 0]),
                pltpu.make_async_copy(wu_hbm.at[expert0 + e, pl.ds(g0, gu_rows)], stg_u.at[st],
                                      sems.at[st, 1]),
                pltpu.make_async_copy(wd_hbm.at[expert0 + e, pl.ds(d0, d_rows)], stg_d.at[st],
                                      sems.at[st, 2]))

    def start(e, k, st):
        for cp in unit_copies(e, k, st):
            cp.start()

    def wait(e, k, st):
        for cp in unit_copies(e, k, st):
            cp.wait()

    def cast_unit(slot, k, st):
        g0 = pl.multiple_of(k * gu_rows, gu_rows)
        d0 = pl.multiple_of(k * d_rows, SEG_ALIGN)
        wg_buf[slot, pl.ds(g0, gu_rows), :] = stg_g[st].astype(_BF16)
        wu_buf[slot, pl.ds(g0, gu_rows), :] = stg_u[st].astype(_BF16)
        wd_buf[slot, pl.ds(d0, d_rows), :] = stg_d[st].astype(_BF16)

    def expert_after(e):
        blk = end_ref[e * n_ch] // row_tile
        return jnp.where(blk < n_valid, be_ref[jnp.minimum(blk, n_valid - 1)], -1)

    def become_resident(slot, e):
        state[ST_SLOT] = slot
        state[ST_RESIDENT] = e
        nxt = expert_after(e)
        state[ST_NEXT] = nxt
        state[ST_DONE] = 0

        @pl.when(nxt >= 0)
        def _():
            start(nxt, 0, 0)
        state[ST_STAGE] = 0

    e_b = be_ref[jnp.minimum(b, n_valid - 1)]

    @pl.when(b == 0)
    def _():
        state[ST_SLOT] = 1
        state[ST_RESIDENT] = -1
        state[ST_NEXT] = e_b
        state[ST_DONE] = 0
        state[ST_STAGE] = 0
        start(e_b, 0, 0)

    s_slot = state[ST_SLOT]
    s_next = state[ST_NEXT]
    s_done = state[ST_DONE]
    s_stage = state[ST_STAGE]
    change = valid & (e_b != state[ST_RESIDENT])
    steady = valid & jnp.logical_not(change) & (s_next >= 0) & (s_done < units)

    @pl.when(change)
    def _():
        other = 1 - s_slot

        @pl.when(s_done < units)
        def _():
            wait(e_b, s_done, s_stage)
            cast_unit(other, s_done, s_stage)

            def body(k, carry):
                start(e_b, k, 0)
                wait(e_b, k, 0)
                cast_unit(other, k, 0)
                return carry
            lax.fori_loop(s_done + 1, units, body, 0)
        become_resident(other, e_b)

    def run_block(slot):
        y_ref[...] = _swiglu(x_ref[...], wg_buf, wu_buf, wd_buf, a_scr, slot).astype(_BF16)

    @pl.when(steady)
    def _():
        wait(s_next, s_done, s_stage)

        @pl.when(s_done + 1 < units)
        def _():
            start(s_next, s_done + 1, 1 - s_stage)
        state[ST_DONE] = s_done + 1
        state[ST_STAGE] = 1 - s_stage
        cast_unit(1 - s_slot, s_done, s_stage)
        run_block(s_slot)

    @pl.when(valid & jnp.logical_not(steady))
    def _():
        run_block(state[ST_SLOT])

    @pl.when(jnp.logical_not(valid))
    def _():
        y_ref[...] = jnp.zeros_like(y_ref)


def _ffn_grouped(x_sorted, wg, wu, wd, expert0, block_expert, n_valid, end_flat, n_ch, row_tile):
    rows, d = x_sorted.shape
    f = wg.shape[2]
    n_blocks = rows // row_tile
    hbm = pl.BlockSpec(memory_space=pl.ANY)
    return pl.pallas_call(
        functools.partial(_ffn_grouped_kernel, n_ch=n_ch, expert0=expert0),
        grid_spec=pltpu.PrefetchScalarGridSpec(
            num_scalar_prefetch=3,
            grid=(n_blocks,),
            in_specs=[pl.BlockSpec((row_tile, d),
                                   lambda b, be, nv, en: (jnp.minimum(b, nv[0] - 1), 0)),
                      hbm, hbm, hbm],
            out_specs=pl.BlockSpec((row_tile, d), lambda b, be, nv, en: (b, 0)),
            scratch_shapes=[pltpu.VMEM((2, d, f), _BF16),
                            pltpu.VMEM((2, d, f), _BF16),
                            pltpu.VMEM((2, f, d), _BF16),
                            pltpu.VMEM((2, d // WEIGHT_UNITS, f), _F32),
                            pltpu.VMEM((2, d // WEIGHT_UNITS, f), _F32),
                            pltpu.VMEM((2, f // WEIGHT_UNITS, d), _F32),
                            pltpu.SemaphoreType.DMA((2, 3)),
                            pltpu.SMEM((5,), _I32),
                            pltpu.VMEM((row_tile, f), _BF16)]),
        out_shape=jax.ShapeDtypeStruct((rows, d), _BF16),
        compiler_params=pltpu.CompilerParams(
            dimension_semantics=("arbitrary",), vmem_limit_bytes=VMEM_LIMIT),
        name="moe_experts",
    )(block_expert, n_valid, end_flat, x_sorted, wg, wu, wd)


def _combine_kernel(w0_ref, sl_ref, res_ref, tok_ref, gfin_ref, y_hbm,
                    out_ref, ybuf, sems, tmp_scr, acc_scr, *, n_e, n_ch):
    c = pl.program_id(0)
    ch = res_ref.shape[0]
    piece = WINDOW_PIECE
    n_pieces = ybuf.shape[1]

    def piece_copy(cc, slot, e, k):
        row0 = pl.multiple_of(w0_ref[e * n_ch + cc] + k * piece, SEG_ALIGN)
        return pltpu.make_async_copy(
            y_hbm.at[pl.ds(row0, piece)],
            ybuf.at[slot, k, pl.ds(e * piece, piece)], sems.at[slot, e, k])

    def for_each_piece(cc, slot, fn):
        for e in range(n_e):
            fn(piece_copy(cc, slot, e, 0))
            used = sl_ref[e * n_ch + cc]

            def later(k, e=e, used=used):
                @pl.when(used > k * piece)
                def _():
                    fn(piece_copy(cc, slot, e, k))
                    if k + 1 < n_pieces:
                        later(k + 1)
            if n_pieces > 1:
                later(1)

    @pl.when(c == 0)
    def _():
        ybuf[...] = jnp.zeros_like(ybuf)
        for_each_piece(0, 0, lambda cp: cp.start())

    @pl.when(c + 1 < n_ch)
    def _():
        for_each_piece(c + 1, (c + 1) % 2, lambda cp: cp.start())

    slot = c % 2
    for_each_piece(c, slot, lambda cp: cp.wait())

    per_dot = MXU_DIM // piece
    depth = per_dot * piece
    lane = lax.broadcasted_iota(_I32, (1, depth), 1)
    p1b = jnp.broadcast_to(tok_ref[:, 0:1], (ch, depth))
    p2b = jnp.broadcast_to(tok_ref[:, 1:2], (ch, depth))
    g1b = jnp.broadcast_to(tok_ref[:, 2:3], (ch, depth))
    g2b = jnp.broadcast_to(tok_ref[:, 3:4], (ch, depth))

    def scatter_back(m, k):
        tgt = jnp.full((1, depth), -1.0, _F32)
        for i in range(per_dot):
            e = m * per_dot + i
            row = lane - i * piece + k * piece
            mine = (lane >= i * piece) & (lane < (i + 1) * piece) & (row < sl_ref[e * n_ch + c])
            tgt = jnp.where(mine, (row + w0_ref[e * n_ch + c]).astype(_F32), tgt)
        q = (jnp.where(p1b == tgt, g1b, 0.0) + jnp.where(p2b == tgt, g2b, 0.0)).astype(_BF16)
        return _dot(q, ybuf[slot, k, m * depth:(m + 1) * depth, :])

    def any_reaches(m, k):
        hit = sl_ref[(m * per_dot) * n_ch + c] > k * piece
        for i in range(1, per_dot):
            hit = jnp.logical_or(hit, sl_ref[(m * per_dot + i) * n_ch + c] > k * piece)
        return hit

    acc = res_ref[...]
    for m in range(n_e // per_dot):
        acc = acc + scatter_back(m, 0)
    acc_scr[...] = acc
    for m in range(n_e // per_dot):
        def later(k, m=m):
            @pl.when(any_reaches(m, k))
            def _():
                acc_scr[...] += scatter_back(m, k)
                if k + 1 < n_pieces:
                    later(k + 1)
        if n_pieces > 1:
            later(1)
    normed = _rmsnorm(acc_scr[...], gfin_ref[...])
    steps = ch // SUBLANES
    for k in range(tmp_scr.shape[0]):
        tmp_scr[k] = normed[:, k * LANES:(k + 1) * LANES]
    for bb in range(SUBLANES):
        for k in range(tmp_scr.shape[0]):
            out_ref[bb, :, k * LANES:(k + 1) * LANES] = (
                tmp_scr[k, pl.ds(bb, steps, stride=SUBLANES), :])


def _combine(res, tok_tab, gfin, y_sorted, w0_flat, seglen_flat, n_e, bsz, seq):
    t_rows, d = res.shape
    ch = TOKEN_CHUNK
    n_ch = t_rows // ch
    steps = ch // bsz
    return pl.pallas_call(
        functools.partial(_combine_kernel, n_e=n_e, n_ch=n_ch),
        grid_spec=pltpu.PrefetchScalarGridSpec(
            num_scalar_prefetch=2,
            grid=(n_ch,),
            in_specs=[pl.BlockSpec((ch, d), lambda c, w0, sl: (c, 0)),
                      pl.BlockSpec((ch, tok_tab.shape[1]), lambda c, w0, sl: (c, 0)),
                      pl.BlockSpec((1, d), lambda c, w0, sl: (0, 0)),
                      pl.BlockSpec(memory_space=pl.ANY)],
            out_specs=pl.BlockSpec((bsz, steps, d), lambda c, w0, sl: (0, c, 0)),
            scratch_shapes=[pltpu.VMEM((2, _window_pieces(ch), n_e * WINDOW_PIECE, d), _BF16),
                            pltpu.SemaphoreType.DMA((2, n_e, _window_pieces(ch))),
                            pltpu.VMEM((d // LANES, ch, LANES), _F32),
                            pltpu.VMEM((ch, d), _F32)]),
        out_shape=jax.ShapeDtypeStruct((bsz, seq, d), _F32),
        compiler_params=pltpu.CompilerParams(
            dimension_semantics=("arbitrary",), vmem_limit_bytes=VMEM_LIMIT),
        name="moe_combine",
    )(w0_flat, seglen_flat, res, tok_tab, gfin, y_sorted)


def _moe_layer(h, hn, logits_t, wg, wu, wd, expert0, gfin, bsz, seq):
    t_rows, d = h.shape
    n_e = logits_t.shape[0]
    tm = EXPERT_ROW_TILE
    n_ch = t_rows // TOKEN_CHUNK
    max_rows = 2 * t_rows + n_e * (SEG_ALIGN + TOKEN_CHUNK + tm)
    n_blocks = -(-max_rows // tm)
    n_blk_pad = -(-n_blocks // LANES) * LANES

    pos, gates, segtab, blktab = _route(logits_t, n_blk_pad, tm)
    pos_l = jnp.transpose(pos.reshape(2, n_ch, TOKEN_CHUNK), (1, 0, 2))
    w0_flat = segtab[0].reshape(-1)
    seglen_flat = segtab[1].reshape(-1)
    n_valid = blktab[1, :1]
    end_flat = segtab[2].reshape(-1)
    x_sorted = _dispatch(hn, pos_l, w0_flat, seglen_flat, end_flat, n_valid,
                         n_e, n_blocks * tm, tm)
    y_sorted = _ffn_grouped(x_sorted, wg, wu, wd, expert0, blktab[0], n_valid, end_flat,
                            n_ch, tm)
    tok_tab = jnp.transpose(jnp.concatenate([pos.astype(_F32), gates], axis=0))
    return _combine(h, tok_tab, gfin, y_sorted,
                    w0_flat, seglen_flat, n_e, bsz, seq)


def kernel(x, norm_mix_g, w_in, ssm_log_dt, ssm_a_re, ssm_a_im, ssm_b_re, ssm_b_im, ssm_c_re, ssm_c_im, ssm_d, ssm_w_glu, ssm_b_glu, pool_w, pool_scale, w_out, norm_ffn_g, ffn_w_gate, ffn_w_up, ffn_w_down, router_w, moe_w_gate, moe_w_up, moe_w_down, final_norm_g):
    bsz, seq, d = x.shape
    depth, n_heads, n_state, n_grp_ch = ssm_b_re.shape
    t_rows = bsz * seq
    assert bsz == SUBLANES and depth % 2 == 0
    groups = n_heads // HEADS_PER_GROUP

    nh = depth * n_heads
    per_head = lambda w: w.reshape((nh,) + w.shape[2:])
    lam, bw_all, cw_all = _discretize(
        per_head(ssm_log_dt), per_head(ssm_a_re), per_head(ssm_a_im),
        per_head(jnp.swapaxes(ssm_b_re, 2, 3)), per_head(jnp.swapaxes(ssm_b_im, 2, 3)),
        per_head(jnp.swapaxes(ssm_c_re, 2, 3)), per_head(jnp.swapaxes(ssm_c_im, 2, 3)))
    lam = lam.reshape(depth, 2 * groups, -1)
    bw = bw_all.reshape((depth, groups) + bw_all.shape[1:])
    cw = cw_all.reshape((depth, groups) + cw_all.shape[1:])
    vec = lambda v: v.reshape(v.shape[0], 1, v.shape[1])
    router_wt = jnp.swapaxes(router_w, 1, 2)

    h = x
    row = lambda v: v.reshape(1, -1)
    out = None
    for i in range(depth):
        is_moe = i % 2 == 1
        j = i // 2
        outs = _mix_layer(
            h, i, vec(norm_mix_g), w_in, bw, lam, cw, vec(ssm_d), ssm_w_glu, vec(ssm_b_glu),
            pool_w, vec(pool_scale), w_out, vec(norm_ffn_g), router_wt if is_moe else None, j)
        if is_moe:
            assert i == depth - 1
            h, hn, logits_t = outs
            n_e = moe_w_gate.shape[1]
            stack = lambda w: w.reshape((-1,) + w.shape[2:])
            out = _moe_layer(h, hn, logits_t, stack(moe_w_gate), stack(moe_w_up),
                             stack(moe_w_down), j * n_e, row(final_norm_g), bsz, seq)
        else:
            h, hn = outs
            h = _ffn(hn, ffn_w_gate, ffn_w_up, ffn_w_down, j, h)
    return out
```

```python
import functools
import math

import jax
import jax.numpy as jnp
from jax import lax
from jax.experimental import pallas as pl
from jax.experimental.pallas import tpu as pltpu

RMS_EPS = 1e-6
POOL_WINDOWS = (2, 4, 8, 16)
A_RE_MAX = -1e-4
GELU_C0 = math.sqrt(2.0 / math.pi)
GELU_C1 = 0.044715

SUBLANES = 8
LANES = 128
MXU_DIM = 256

HEADS_PER_GROUP = 16
TIME_TILE = 64
SCAN_COLS = 512
FFN_ROW_TILE = 512
EXPERT_ROW_TILE = 512
FFN_COL_CHUNK = 256
WEIGHT_UNITS = 8
TOKEN_CHUNK = MXU_DIM
SEG_ALIGN = 2 * SUBLANES
WINDOW_PIECE = 96
VMEM_LIMIT = 56 * 1024 * 1024
VEC_NORM_MIX, VEC_NORM_FFN, VEC_SKIP_BIAS, VEC_POOL_SCALE, VEC_LAMBDA = range(5)

_F32 = jnp.float32
_BF16 = jnp.bfloat16
_I32 = jnp.int32


def _dot(a, b):
    return jnp.dot(a, b, preferred_element_type=_F32)


def _dot_nt(a, b):
    return lax.dot_general(a, b, (((1,), (1,)), ((), ())), preferred_element_type=_F32)


def _rmsnorm(x, g):
    inv = lax.rsqrt(jnp.mean(x * x, axis=-1, keepdims=True) + RMS_EPS)
    return x * inv * g


def _sigmoid(x):
    return 1.0 / (1.0 + jnp.exp(-x))


def _ceil_to(x, m):
    return jnp.floor((x + (m - 1.0)) * (1.0 / m)) * m


def _window_pieces(chunk):
    return -(-(chunk + SEG_ALIGN - 1) // WINDOW_PIECE)


def _discretize_kernel(log_dt_ref, a_re_ref, a_im_ref, b_re_ref, b_im_ref, c_re_ref, c_im_ref,
                       lam_ref, bw_ref, cw_ref):
    dt = jnp.exp(log_dt_ref[...])
    ar = jnp.minimum(a_re_ref[...], A_RE_MAX)
    ai = a_im_ref[...]
    mag = jnp.exp(ar * dt)
    lam_re = mag * jnp.cos(ai * dt)
    lam_im = mag * jnp.sin(ai * dt)
    den = ar * ar + ai * ai
    nr = lam_re - 1.0
    ni = lam_im
    coef_re = (nr * ar + ni * ai) / den
    coef_im = (ni * ar - nr * ai) / den
    br = b_re_ref[...]
    bi = b_im_ref[...]
    bb_re = (coef_re * br - coef_im * bi).astype(_BF16)
    bb_im = (coef_re * bi + coef_im * br).astype(_BF16)
    c_re = c_re_ref[...].astype(_BF16)
    c_im_neg = (-c_im_ref[...]).astype(_BF16)

    n, g, p = br.shape
    hpg = n // lam_ref.shape[0]
    half = hpg * p
    bw_ref[...] = jnp.zeros_like(bw_ref)
    cw_ref[...] = jnp.zeros_like(cw_ref)
    for h in range(n):
        q, hl = divmod(h, hpg)
        st = slice(hl * p, (hl + 1) * p)
        st_im = slice(half + hl * p, half + (hl + 1) * p)
        ch = slice(hl * g, (hl + 1) * g)
        lam_ref[q, 0:1, st] = lam_re[h]
        lam_ref[q, 1:2, st] = lam_im[h]
        bw_ref[q, ch, st] = bb_re[h]
        bw_ref[q, ch, st_im] = bb_im[h]
        cw_ref[q, st, ch] = c_re[h]
        cw_ref[q, st_im, ch] = c_im_neg[h]


def _discretize(log_dt, a_re, a_im, b_re_t, b_im_t, c_re_t, c_im_t):
    n, g, p = b_re_t.shape
    hpg = HEADS_PER_GROUP
    nq = n // hpg
    full3 = lambda s: pl.BlockSpec(s, lambda: (0, 0, 0))
    return pl.pallas_call(
        _discretize_kernel,
        out_shape=(jax.ShapeDtypeStruct((nq, 2, hpg * p), _F32),
                   jax.ShapeDtypeStruct((nq, hpg * g, 2 * hpg * p), _BF16),
                   jax.ShapeDtypeStruct((nq, 2 * hpg * p, hpg * g), _BF16)),
        in_specs=[full3((n, 1, 1)), full3((n, 1, p)), full3((n, 1, p)),
                  full3((n, g, p)), full3((n, g, p)), full3((n, p, g)), full3((n, p, g))],
        out_specs=(full3((nq, 2, hpg * p)), full3((nq, hpg * g, 2 * hpg * p)),
                   full3((nq, 2 * hpg * p, hpg * g))),
        name="ssm_discretize",
    )(log_dt.reshape(n, 1, 1), a_re.reshape(n, 1, p), a_im.reshape(n, 1, p),
      b_re_t, b_im_t, c_re_t, c_im_t)


def _mix_kernel(*refs, time_tile, with_router, batch_major_in):
    (h_ref, vec_ref, win_f32, bw_ref, cw_ref, wglu_f32, poolw_f32, wout_f32) = refs[:8]
    refs = refs[8:]
    d_ssm = wglu_f32.shape[0]
    gmix_ref = vec_ref.at[VEC_NORM_MIX:VEC_NORM_MIX + 1]
    gffn_ref = vec_ref.at[VEC_NORM_FFN:VEC_NORM_FFN + 1]
    dskip_ref = vec_ref.at[VEC_SKIP_BIAS:VEC_SKIP_BIAS + 1, 0:d_ssm]
    bglu_ref = vec_ref.at[VEC_SKIP_BIAS:VEC_SKIP_BIAS + 1, d_ssm:2 * d_ssm]
    pscale_ref = vec_ref.at[VEC_POOL_SCALE:VEC_POOL_SCALE + 1]
    lam_ref = vec_ref.at[VEC_LAMBDA:]
    rw_ref = logits_ref = xin_scr = None
    if with_router:
        rw_ref, hout_ref, hn_ref, logits_ref = refs[:4]
        refs = refs[4:]
    else:
        hout_ref, hn_ref = refs[:2]
        refs = refs[2:]
    (s_scr, state_scr, ext_scr, mixed_scr, u_save, h_save,
     win_ref, wglu_ref, poolw_ref, wout_ref) = refs[:10]
    if batch_major_in:
        xin_scr = refs[10]

    step = pl.program_id(0)
    rows = time_tile * SUBLANES
    n_groups = bw_ref.shape[0]
    gin = bw_ref.shape[1]
    gstate = bw_ref.shape[2] // 2
    hist = ext_scr.shape[0] - rows

    @pl.when(step == 0)
    def _():
        state_scr[...] = jnp.zeros_like(state_scr)
        ext_scr[0:hist, :] = jnp.zeros((hist, ext_scr.shape[1]), _F32)
        win_ref[...] = win_f32[...].astype(_BF16)
        wglu_ref[...] = wglu_f32[...].astype(_BF16)
        poolw_ref[...] = poolw_f32[...].astype(_BF16)
        wout_ref[...] = wout_f32[...].astype(_BF16)

    if batch_major_in:
        for bb in range(SUBLANES):
            for k in range(xin_scr.shape[0]):
                xin_scr[k, pl.ds(bb, time_tile, stride=SUBLANES), :] = (
                    h_ref[bb, :, k * LANES:(k + 1) * LANES])
        h = jnp.concatenate([xin_scr[k] for k in range(xin_scr.shape[0])], axis=1)
    else:
        h = h_ref[...]
    h_save[...] = h
    hn = _rmsnorm(h, gmix_ref[...]).astype(_BF16)
    proj = _dot(hn, win_ref[...])
    u_ssm = proj[:, :d_ssm]
    u_save[...] = u_ssm
    ext_scr[hist:, :] = proj[:, d_ssm:]

    def drive(q):
        ug = u_ssm[:, q * gin:(q + 1) * gin].astype(_BF16)
        s_scr[q] = _dot(ug, bw_ref[q])

    def scan(q, c):
        re0 = c * SCAN_COLS
        im0 = gstate + c * SCAN_COLS
        lr = jnp.broadcast_to(lam_ref[2 * q:2 * q + 1, re0:re0 + SCAN_COLS],
                              (SUBLANES, SCAN_COLS))
        li = jnp.broadcast_to(lam_ref[2 * q + 1:2 * q + 2, re0:re0 + SCAN_COLS],
                              (SUBLANES, SCAN_COLS))
        sre = state_scr[q, :, re0:re0 + SCAN_COLS]
        sim = state_scr[q, :, im0:im0 + SCAN_COLS]
        for t in range(time_tile):
            r0 = t * SUBLANES
            bre = s_scr[q, r0:r0 + SUBLANES, re0:re0 + SCAN_COLS]
            bim = s_scr[q, r0:r0 + SUBLANES, im0:im0 + SCAN_COLS]
            sre, sim = (lr * sre - li * sim + bre, lr * sim + li * sre + bim)
            s_scr[q, r0:r0 + SUBLANES, re0:re0 + SCAN_COLS] = sre
            s_scr[q, r0:r0 + SUBLANES, im0:im0 + SCAN_COLS] = sim
        state_scr[q, :, re0:re0 + SCAN_COLS] = sre
        state_scr[q, :, im0:im0 + SCAN_COLS] = sim

    def pool():
        n_ext = rows + hist
        pos = step * time_tile + (lax.broadcasted_iota(_I32, (rows, LANES), 0) // SUBLANES)
        for g, w in enumerate(POOL_WINDOWS):
            c0 = g * LANES
            e = ext_scr[:, c0:c0 + LANES]
            acc = e
            n_acc = n_ext
            span = 1
            while span < w:
                sh = span * SUBLANES
                acc = acc[sh:, :] + acc[:n_acc - sh, :]
                n_acc -= sh
                span *= 2
            wsum = acc[n_acc - rows:, :]
            cnt = jnp.minimum(pos + 1, w).astype(_F32)
            pooled = wsum / cnt - e[hist:, :]
            mg = _dot(pooled.astype(_BF16), poolw_ref[g]) * pscale_ref[:, c0:c0 + LANES]
            mixed_scr[:, d_ssm + c0:d_ssm + c0 + LANES] = mg.astype(_BF16)
        ext_scr[0:hist, :] = ext_scr[rows:rows + hist, :]

    def readout(q):
        cols = slice(q * gin, (q + 1) * gin)
        yq = _dot(s_scr[q].astype(_BF16), cw_ref[q])
        yq = yq + dskip_ref[:, cols] * u_save[:, cols]
        hq = 0.5 * yq * (1.0 + jnp.tanh(GELU_C0 * (yq + GELU_C1 * (yq * yq * yq))))
        mixed_scr[:, cols] = hq.astype(_BF16)

    def glu_gate():
        hg = mixed_scr[:, :d_ssm]
        gate = _sigmoid(_dot(hg, wglu_ref[...]) + bglu_ref[...])
        mixed_scr[:, :d_ssm] = (hg.astype(_F32) * gate).astype(_BF16)

    def project_out():
        hout = h_save[...] + _dot(mixed_scr[...], wout_ref[...])
        hout_ref[...] = hout
        hn2 = _rmsnorm(hout, gffn_ref[...])
        hn_hi = hn2.astype(_BF16)
        hn_ref[...] = hn_hi
        if with_router:
            hn_lo = (hn2 - hn_hi.astype(_F32)).astype(_BF16)
            rw = rw_ref[...]
            rw_hi = rw.astype(_BF16)
            rw_lo = (rw - rw_hi.astype(_F32)).astype(_BF16)
            logits_ref[...] = (_dot_nt(rw_hi, hn_hi) + _dot_nt(rw_hi, hn_lo)
                               + _dot_nt(rw_lo, hn_hi))

    for q in range(n_groups):
        drive(q)
    for c in range(gstate // SCAN_COLS):
        for q in range(n_groups):
            scan(q, c)
        if c == 0:
            pool()
    for q in range(n_groups):
        readout(q)
    glu_gate()
    project_out()


def _pack_vectors(norm_mix_g, norm_ffn_g, ssm_d, ssm_b_glu, pool_scale, lam):
    depth, d = norm_mix_g.shape
    row = lambda v: jnp.pad(v, ((0, 0), (0, d - v.shape[1])))[:, None, :]
    return jnp.concatenate(
        [row(norm_mix_g), row(norm_ffn_g), row(jnp.concatenate([ssm_d, ssm_b_glu], axis=1)),
         row(pool_scale), lam], axis=1)


def _mix_layer(h, layer, vecs, w_in, bw, cw, wglu, poolw, wout, router_wt, moe_layer):
    batch_major_in = h.ndim == 3
    d = h.shape[-1]
    t_rows = h.size // d
    rows = TIME_TILE * SUBLANES
    n_steps = t_rows // rows
    d_ssm = wglu.shape[1]
    d_pool = d - d_ssm
    n_groups, _, two_gstate = bw.shape[1:]
    hist = max(POOL_WINDOWS) * SUBLANES
    with_router = router_wt is not None

    def const(a, idx):
        nd = a.ndim
        return pl.BlockSpec((None,) + a.shape[1:], lambda i, nd=nd, idx=idx: (idx,) + (0,) * (nd - 1),
                            pipeline_mode=pl.Buffered(1))

    row_blk = lambda: pl.BlockSpec((rows, d), lambda i: (i, 0))
    ins = [h, vecs, w_in, bw, cw, wglu, poolw, wout]
    h_spec =(pl.BlockSpec((h.shape[0], TIME_TILE, d), lambda i: (0, i, 0))
              if batch_major_in else row_blk())
    in_specs = [h_spec] + [const(a, layer) for a in ins[1:]]
    out_shape = [jax.ShapeDtypeStruct((t_rows, d), _F32),
                 jax.ShapeDtypeStruct((t_rows, d), _BF16)]
    out_specs = [row_blk(), row_blk()]
    if with_router:
        n_e = router_wt.shape[1]
        ins.append(router_wt)
        in_specs.append(const(router_wt, moe_layer))
        out_shape.append(jax.ShapeDtypeStruct((n_e, t_rows), _F32))
        out_specs.append(pl.BlockSpec((n_e, rows), lambda i: (0, i)))
    scratch = [
        pltpu.VMEM((n_groups, rows, two_gstate), _F32),
        pltpu.VMEM((n_groups, SUBLANES, two_gstate), _F32),
        pltpu.VMEM((rows + hist, d_pool), _F32),
        pltpu.VMEM((rows, d_ssm + d_pool), _BF16),
        pltpu.VMEM((rows, d_ssm), _F32),
        pltpu.VMEM((rows, d), _F32),
        pltpu.VMEM(w_in.shape[1:], _BF16),
        pltpu.VMEM(wglu.shape[1:], _BF16),
        pltpu.VMEM(poolw.shape[1:], _BF16),
        pltpu.VMEM(wout.shape[1:], _BF16),
    ]
    if batch_major_in:
        scratch.append(pltpu.VMEM((d // LANES, rows, LANES), _F32))
    return pl.pallas_call(
        functools.partial(_mix_kernel, time_tile=TIME_TILE, with_router=with_router,
                          batch_major_in=batch_major_in),
        grid=(n_steps,),
        in_specs=in_specs,
        out_specs=out_specs,
        out_shape=out_shape,
        scratch_shapes=scratch,
        compiler_params=pltpu.CompilerParams(
            dimension_semantics=("arbitrary",), vmem_limit_bytes=VMEM_LIMIT),
        name="mix_router" if with_router else "mix",
    )(*ins)


def _swiglu(x, wg_ref, wu_ref, wd_ref, a_scr, slot):
    tf = wg_ref.shape[2]
    c0 = 0
    while c0 < tf:
        cw = min(FFN_COL_CHUNK, tf - c0)
        g = _dot(x, wg_ref[slot, :, c0:c0 + cw])
        u = _dot(x, wu_ref[slot, :, c0:c0 + cw])
        a_scr[:, c0:c0 + cw] = (g * _sigmoid(g) * u).astype(_BF16)
        c0 += cw
    return _dot(a_scr[...], wd_ref[slot])


def _ffn_kernel(x_ref, wg_hbm, wu_hbm, wd_hbm, res_ref, out_ref,
                wg_buf, wu_buf, wd_buf, stg_g, stg_u, stg_d, sems, a_scr, *, layer):
    units = WEIGHT_UNITS
    gu_rows = wg_buf.shape[1] // units
    d_rows = wd_buf.shape[1] // units

    @pl.when(pl.program_id(0) == 0)
    def _():
        def copies(k):
            st = k % 2
            return (pltpu.make_async_copy(wg_hbm.at[layer, pl.ds(k * gu_rows, gu_rows)],
                                          stg_g.at[st], sems.at[st, 0]),
                    pltpu.make_async_copy(wu_hbm.at[layer, pl.ds(k * gu_rows, gu_rows)],
                                          stg_u.at[st], sems.at[st, 1]),
                    pltpu.make_async_copy(wd_hbm.at[layer, pl.ds(k * d_rows, d_rows)],
                                          stg_d.at[st], sems.at[st, 2]))

        for cp in copies(0):
            cp.start()
        for k in range(units):
            if k + 1 < units:
                for cp in copies(k + 1):
                    cp.start()
            for cp in copies(k):
                cp.wait()
            st = k % 2
            wg_buf[0, k * gu_rows:(k + 1) * gu_rows, :] = stg_g[st].astype(_BF16)
            wu_buf[0, k * gu_rows:(k + 1) * gu_rows, :] = stg_u[st].astype(_BF16)
            wd_buf[0, k * d_rows:(k + 1) * d_rows, :] = stg_d[st].astype(_BF16)

    out_ref[...] = res_ref[...] + _swiglu(x_ref[...], wg_buf, wu_buf, wd_buf, a_scr, 0)


def _ffn(x, wg, wu, wd, layer, res):
    t_rows, d = x.shape
    f = wg.shape[2]
    tm = FFN_ROW_TILE
    hbm = pl.BlockSpec(memory_space=pl.ANY)
    return pl.pallas_call(
        functools.partial(_ffn_kernel, layer=layer),
        grid=(t_rows // tm,),
        in_specs=[pl.BlockSpec((tm, d), lambda i: (i, 0)), hbm, hbm, hbm,
                  pl.BlockSpec((tm, d), lambda i: (i, 0))],
        out_specs=pl.BlockSpec((tm, d), lambda i: (i, 0)),
        out_shape=jax.ShapeDtypeStruct((t_rows, d), _F32),
        scratch_shapes=[pltpu.VMEM((1, d, f), _BF16),
                        pltpu.VMEM((1, d, f), _BF16),
                        pltpu.VMEM((1, f, d), _BF16),
                        pltpu.VMEM((2, d // WEIGHT_UNITS, f), _F32),
                        pltpu.VMEM((2, d // WEIGHT_UNITS, f), _F32),
                        pltpu.VMEM((2, f // WEIGHT_UNITS, d), _F32),
                        pltpu.SemaphoreType.DMA((2, 3)),
                        pltpu.VMEM((tm, f), _BF16)],
        compiler_params=pltpu.CompilerParams(
            dimension_semantics=("arbitrary",), vmem_limit_bytes=VMEM_LIMIT),
        name="ffn_dense",
    )(x, wg, wu, wd, res)


def _route_kernel(lt_ref, pos_ref, gates_ref, segtab_ref, blktab_ref, *, n_blk, row_tile):
    l = lt_ref[...]
    n_e, t_rows = l.shape
    ch = TOKEN_CHUNK
    n_ch = t_rows // ch
    ie = lax.broadcasted_iota(_I32, l.shape, 0)
    m1 = jnp.max(l, axis=0, keepdims=True)
    i1 = jnp.min(jnp.where(l == m1, ie, n_e), axis=0, keepdims=True)
    l2 = jnp.where(ie == i1, -jnp.inf, l)
    m2 = jnp.max(l2, axis=0, keepdims=True)
    i2 = jnp.min(jnp.where(l2 == m2, ie, n_e), axis=0, keepdims=True)
    e2 = jnp.exp(m2 - m1)
    den = 1.0 + e2
    gates_ref[0:1, :] = 1.0 / den
    gates_ref[1:2, :] = e2 / den
    sel1 = ie == i1
    sel2 = ie == i2
    chosen = jnp.where(sel1 | sel2, 1.0, 0.0).astype(_BF16)

    chunk_of_t = lax.broadcasted_iota(_I32, (n_ch, t_rows), 1) // ch
    in_chunk = jnp.where(chunk_of_t == lax.broadcasted_iota(_I32, (n_ch, t_rows), 0),
                         1.0, 0.0).astype(_BF16)
    n_ec = _dot_nt(chosen, in_chunk)
    n_ce = _dot_nt(in_chunk, chosen)
    cc_r = lax.broadcasted_iota(_I32, (n_ch, n_ch), 0)
    cc_c = lax.broadcasted_iota(_I32, (n_ch, n_ch), 1)
    before = jnp.where(cc_r < cc_c, 1.0, 0.0).astype(_BF16)
    after = jnp.where(cc_c < cc_r, 1.0, 0.0).astype(_BF16)
    segoff_ec = _dot(n_ec.astype(_BF16), before)
    segoff_ce = _dot(after, n_ce.astype(_BF16))
    def window(segoff, n):
        start = jnp.floor(segoff * (1.0 / SEG_ALIGN)) * SEG_ALIGN
        used = segoff - start + n
        return start, used, start + _ceil_to(jnp.maximum(used, 1.0), float(WINDOW_PIECE))

    wstart_ec, used_ec, wend_ec = window(segoff_ec, n_ec)
    _, _, wend_ce = window(segoff_ce, n_ce)
    tot_col = _ceil_to(jnp.max(wend_ec, axis=1, keepdims=True), float(row_tile))
    tot_row = _ceil_to(jnp.max(wend_ce, axis=0, keepdims=True), float(row_tile))
    ee_r = lax.broadcasted_iota(_I32, (n_e, n_e), 0)
    ee_c = lax.broadcasted_iota(_I32, (n_e, n_e), 1)
    base_col = jnp.sum(jnp.where(ee_c < ee_r, tot_row, 0.0), axis=1, keepdims=True)
    w0_ec = base_col + segoff_ec
    end_col = base_col + tot_col
    segtab_ref[0] = (base_col + wstart_ec).astype(_I32)
    segtab_ref[1] = used_ec.astype(_I32)
    segtab_ref[2] = jnp.broadcast_to(end_col, (n_e, n_ch)).astype(_I32)

    tt_r = lax.broadcasted_iota(_I32, (ch, ch), 0)
    tt_c = lax.broadcasted_iota(_I32, (ch, ch), 1)
    earlier = jnp.where(tt_r < tt_c, 1.0, 0.0).astype(_BF16)
    for c in range(n_ch):
        cols = slice(c * ch, (c + 1) * ch)
        rank = _dot(chosen[:, cols], earlier)
        offs = w0_ec[:, c:c + 1] + rank
        p1 = jnp.sum(jnp.where(sel1[:, cols], offs, 0.0), axis=0, keepdims=True)
        p2 = jnp.sum(jnp.where(sel2[:, cols], offs, 0.0), axis=0, keepdims=True)
        pos_ref[0:1, cols] = p1.astype(_I32)
        pos_ref[1:2, cols] = p2.astype(_I32)

    start = (lax.broadcasted_iota(_I32, (1, n_blk), 1) * row_tile).astype(_F32)
    owner = jnp.sum(jnp.where(end_col <= start, 1.0, 0.0), axis=0, keepdims=True)
    n_valid = jnp.sum(tot_col, axis=0, keepdims=True) * (1.0 / row_tile)
    blktab_ref[0:1, :] = jnp.minimum(owner, n_e - 1.0).astype(_I32)
    blktab_ref[1:2, :] = jnp.broadcast_to(n_valid, (1, n_blk)).astype(_I32)


def _route(logits_t, n_blk, row_tile):
    n_e, t_rows = logits_t.shape
    n_ch = t_rows // TOKEN_CHUNK
    full = lambda s: pl.BlockSpec(s, lambda: (0,) * len(s))
    return pl.pallas_call(
        functools.partial(_route_kernel, n_blk=n_blk, row_tile=row_tile),
        in_specs=[full((n_e, t_rows))],
        out_specs=(full((2, t_rows)), full((2, t_rows)), full((3, n_e, n_ch)), full((2, n_blk))),
        out_shape=(jax.ShapeDtypeStruct((2, t_rows), _I32),
                   jax.ShapeDtypeStruct((2, t_rows), _F32),
                   jax.ShapeDtypeStruct((3, n_e, n_ch), _I32),
                   jax.ShapeDtypeStruct((2, n_blk), _I32)),
        compiler_params=pltpu.CompilerParams(vmem_limit_bytes=VMEM_LIMIT),
        name="route_sort",
    )(logits_t)


def _dispatch_kernel(w0_ref, sl_ref, end_ref, nv_ref, hn_ref, posl_ref, x_hbm,
                     stage, sems, zero_scr, zsems, tsems, carry, *, n_e, n_ch, row_tile, min_blocks):
    c = pl.program_id(0)
    ch = hn_ref.shape[0]
    piece = WINDOW_PIECE
    n_pieces = stage.shape[1]
    slot = c % 2
    n_blocks = x_hbm.shape[0] // row_tile

    def for_each_unused_block(fn):
        for b in range(min_blocks, n_blocks):
            @pl.when(b >= nv_ref[0])
            def _(b=b):
                fn(pltpu.make_async_copy(zero_scr, x_hbm.at[pl.ds(b * row_tile, row_tile)],
                                         tsems.at[b - min_blocks]))

    def piece_copy(cc, sl, e, k):
        row0 = pl.multiple_of(w0_ref[e * n_ch + cc] + k * piece, SEG_ALIGN)
        return pltpu.make_async_copy(
            stage.at[sl, k, pl.ds(e * piece, piece)],
            x_hbm.at[pl.ds(row0, piece)], sems.at[sl, e, k])

    def for_each_piece(cc, sl, fn):
        for e in range(n_e):
            fn(piece_copy(cc, sl, e, 0))
            used = sl_ref[e * n_ch + cc]

            def later(k, e=e, used=used):
                @pl.when(used > k * piece)
                def _():
                    fn(piece_copy(cc, sl, e, k))
                    if k + 1 < n_pieces:
                        later(k + 1)
            if n_pieces > 1:
                later(1)

    @pl.when(c == 0)
    def _():
        zero_scr[...] = jnp.zeros_like(zero_scr)
        carry[...] = jnp.zeros_like(carry)
        fills = [pltpu.make_async_copy(
            zero_scr,
            x_hbm.at[pl.ds(pl.multiple_of(end_ref[e * n_ch] - row_tile, SEG_ALIGN), row_tile)],
            zsems.at[e]) for e in range(n_e)]
        for cp in fills:
            cp.start()
        for_each_unused_block(lambda cp: cp.start())
        for cp in fills:
            cp.wait()

    pp = posl_ref[0]
    hn = hn_ref[...]

    def gather_piece(k):
        row_j = lax.broadcasted_iota(_I32, (piece, ch), 0) + k * piece
        parts = []
        for e in range(n_e):
            row_id = row_j + w0_ref[e * n_ch + c]
            hit = (pp[0:1, :] == row_id) | (pp[1:2, :] == row_id)
            parts.append(jnp.where(hit, 1.0, 0.0).astype(_BF16))
        onehot = jnp.concatenate(parts, axis=0)
        stage[slot, k] = _dot(onehot, hn).astype(_BF16)

    gather_piece(0)
    for e in range(n_e):
        head = slice(e * piece, e * piece + SEG_ALIGN)
        stage[slot, 0, head, :] = stage[slot, 0, head, :] + carry[e]
    longest = sl_ref[c]
    for e in range(1, n_e):
        longest = jnp.maximum(longest, sl_ref[e * n_ch + c])

    def later(k):
        @pl.when(longest > k * piece)
        def _():
            gather_piece(k)
            if k + 1 < n_pieces:
                later(k + 1)
    if n_pieces > 1:
        later(1)
    for e in range(n_e):
        used = sl_ref[e * n_ch + c]
        gathered = ((jnp.maximum(used, 1) + piece - 1) // piece) * piece
        off = (used // SEG_ALIGN) * SEG_ALIGN
        src = jnp.minimum(off, gathered - SEG_ALIGN)
        grp = stage[slot, src // piece,
                    pl.ds(pl.multiple_of(e * piece + src % piece, SEG_ALIGN), SEG_ALIGN), :]
        carry[e] = jnp.where(off < gathered, grp, jnp.zeros_like(grp))

    @pl.when(c > 0)
    def _():
        for_each_piece(c - 1, 1 - slot, lambda cp: cp.wait())

    for_each_piece(c, slot, lambda cp: cp.start())

    @pl.when(c == n_ch - 1)
    def _():
        for_each_piece(c, slot, lambda cp: cp.wait())
        for_each_unused_block(lambda cp: cp.wait())


def _dispatch(hn, pos_l, w0_flat, seglen_flat, end_flat, n_valid, n_e, n_rows, row_tile):
    t_rows, d = hn.shape
    ch = TOKEN_CHUNK
    n_ch = t_rows // ch
    n_pieces = _window_pieces(ch)
    min_blocks = 2 * t_rows // row_tile
    return pl.pallas_call(
        functools.partial(_dispatch_kernel, n_e=n_e, n_ch=n_ch, row_tile=row_tile,
                          min_blocks=min_blocks),
        grid_spec=pltpu.PrefetchScalarGridSpec(
            num_scalar_prefetch=4,
            grid=(n_ch,),
            in_specs=[pl.BlockSpec((ch, d), lambda c, *_: (c, 0)),
                      pl.BlockSpec((1, 2, ch), lambda c, *_: (c, 0, 0))],
            out_specs=pl.BlockSpec(memory_space=pl.ANY),
            scratch_shapes=[pltpu.VMEM((2, n_pieces, n_e * WINDOW_PIECE, d), _BF16),
                            pltpu.SemaphoreType.DMA((2, n_e, n_pieces)),
                            pltpu.VMEM((row_tile, d), _BF16),
                            pltpu.SemaphoreType.DMA((n_e,)),
                            pltpu.SemaphoreType.DMA((n_rows // row_tile - min_blocks,)),
                            pltpu.VMEM((n_e, SEG_ALIGN, d), _BF16)]),
        out_shape=jax.ShapeDtypeStruct((n_rows, d), _BF16),
        compiler_params=pltpu.CompilerParams(
            dimension_semantics=("arbitrary",), vmem_limit_bytes=VMEM_LIMIT),
        name="moe_dispatch",
    )(w0_flat, seglen_flat, end_flat, n_valid, hn, pos_l)


def _ffn_grouped_kernel(be_ref, nv_ref, end_ref, x_ref, wg_hbm, wu_hbm, wd_hbm, y_ref,
                        wg_buf, wu_buf, wd_buf, stg_g, stg_u, stg_d, sems, state, a_scr,
                        *, n_ch, expert0):
    b = pl.program_id(0)
    n_valid = nv_ref[0]
    valid = b < n_valid
    row_tile = x_ref.shape[0]
    units = WEIGHT_UNITS
    gu_rows = wg_buf.shape[1] // units
    d_rows = wd_buf.shape[1] // units
    ST_SLOT, ST_RESIDENT, ST_NEXT, ST_DONE, ST_STAGE = range(5)

    def unit_copies(e, k, st):
        g0 = pl.multiple_of(k * gu_rows, gu_rows)
        d0 = pl.multiple_of(k * d_rows, SEG_ALIGN)
        return (pltpu.make_async_copy(wg_hbm.at[expert0 + e, pl.ds(g0, gu_rows)], stg_g.at[st],
                                      sems.at[st, 0]),
                pltpu.make_async_copy(wu_hbm.at[expert0 + e, pl.ds(g0, gu_rows)], stg_u.at[st],
                                      sems.at[st, 1]),
                pltpu.make_async_copy(wd_hbm.at[expert0 + e, pl.ds(d0, d_rows)], stg_d.at[st],
                                      sems.at[st, 2]))

    def start(e, k, st):
        for cp in unit_copies(e, k, st):
            cp.start()

    def wait(e, k, st):
        for cp in unit_copies(e, k, st):
            cp.wait()

    def cast_unit(slot, k, st):
        g0 = pl.multiple_of(k * gu_rows, gu_rows)
        d0 = pl.multiple_of(k * d_rows, SEG_ALIGN)
        wg_buf[slot, pl.ds(g0, gu_rows), :] = stg_g[st].astype(_BF16)
        wu_buf[slot, pl.ds(g0, gu_rows), :] = stg_u[st].astype(_BF16)
        wd_buf[slot, pl.ds(d0, d_rows), :] = stg_d[st].astype(_BF16)

    def expert_after(e):
        blk = end_ref[e * n_ch] // row_tile
        return jnp.where(blk < n_valid, be_ref[jnp.minimum(blk, n_valid - 1)], -1)

    def become_resident(slot, e):
        state[ST_SLOT] = slot
        state[ST_RESIDENT] = e
        nxt = expert_after(e)
        state[ST_NEXT] = nxt
        state[ST_DONE] = 0

        @pl.when(nxt >= 0)
        def _():
            start(nxt, 0, 0)
        state[ST_STAGE] = 0

    e_b = be_ref[jnp.minimum(b, n_valid - 1)]

    @pl.when(b == 0)
    def _():
        state[ST_SLOT] = 1
        state[ST_RESIDENT] = -1
        state[ST_NEXT] = e_b
        state[ST_DONE] = 0
        state[ST_STAGE] = 0
        start(e_b, 0, 0)

    s_slot = state[ST_SLOT]
    s_next = state[ST_NEXT]
    s_done = state[ST_DONE]
    s_stage = state[ST_STAGE]
    change = valid & (e_b != state[ST_RESIDENT])
    steady = valid & jnp.logical_not(change) & (s_next >= 0) & (s_done < units)

    @pl.when(change)
    def _():
        other = 1 - s_slot

        @pl.when(s_done < units)
        def _():
            wait(e_b, s_done, s_stage)
            cast_unit(other, s_done, s_stage)

            def body(k, carry):
                start(e_b, k, 0)
                wait(e_b, k, 0)
                cast_unit(other, k, 0)
                return carry
            lax.fori_loop(s_done + 1, units, body, 0)
        become_resident(other, e_b)

    def run_block(slot):
        y_ref[...] = _swiglu(x_ref[...], wg_buf, wu_buf, wd_buf, a_scr, slot).astype(_BF16)

    @pl.when(steady)
    def _():
        wait(s_next, s_done, s_stage)

        @pl.when(s_done + 1 < units)
        def _():
            start(s_next, s_done + 1, 1 - s_stage)
        state[ST_DONE] = s_done + 1
        state[ST_STAGE] = 1 - s_stage
        cast_unit(1 - s_slot, s_done, s_stage)
        run_block(s_slot)

    @pl.when(valid & jnp.logical_not(steady))
    def _():
        run_block(state[ST_SLOT])

    @pl.when(jnp.logical_not(valid))
    def _():
        y_ref[...] = jnp.zeros_like(y_ref)


def _ffn_grouped(x_sorted, wg, wu, wd, expert0, block_expert, n_valid, end_flat, n_ch, row_tile):
    rows, d = x_sorted.shape
    f = wg.shape[2]
    n_blocks = rows // row_tile
    hbm = pl.BlockSpec(memory_space=pl.ANY)
    return pl.pallas_call(
        functools.partial(_ffn_grouped_kernel, n_ch=n_ch, expert0=expert0),
        grid_spec=pltpu.PrefetchScalarGridSpec(
            num_scalar_prefetch=3,
            grid=(n_blocks,),
            in_specs=[pl.BlockSpec((row_tile, d),
                                   lambda b, be, nv, en: (jnp.minimum(b, nv[0] - 1), 0)),
                      hbm, hbm, hbm],
            out_specs=pl.BlockSpec((row_tile, d), lambda b, be, nv, en: (b, 0)),
            scratch_shapes=[pltpu.VMEM((2, d, f), _BF16),
                            pltpu.VMEM((2, d, f), _BF16),
                            pltpu.VMEM((2, f, d), _BF16),
                            pltpu.VMEM((2, d // WEIGHT_UNITS, f), _F32),
                            pltpu.VMEM((2, d // WEIGHT_UNITS, f), _F32),
                            pltpu.VMEM((2, f // WEIGHT_UNITS, d), _F32),
                            pltpu.SemaphoreType.DMA((2, 3)),
                            pltpu.SMEM((5,), _I32),
                            pltpu.VMEM((row_tile, f), _BF16)]),
        out_shape=jax.ShapeDtypeStruct((rows, d), _BF16),
        compiler_params=pltpu.CompilerParams(
            dimension_semantics=("arbitrary",), vmem_limit_bytes=VMEM_LIMIT),
        name="moe_experts",
    )(block_expert, n_valid, end_flat, x_sorted, wg, wu, wd)


def _combine_kernel(w0_ref, sl_ref, res_ref, tok_ref, gfin_ref, y_hbm,
                    out_ref, ybuf, sems, tmp_scr, acc_scr, *, n_e, n_ch):
    c = pl.program_id(0)
    ch = res_ref.shape[0]
    piece = WINDOW_PIECE
    n_pieces = ybuf.shape[1]

    def piece_copy(cc, slot, e, k):
        row0 = pl.multiple_of(w0_ref[e * n_ch + cc] + k * piece, SEG_ALIGN)
        return pltpu.make_async_copy(
            y_hbm.at[pl.ds(row0, piece)],
            ybuf.at[slot, k, pl.ds(e * piece, piece)], sems.at[slot, e, k])

    def for_each_piece(cc, slot, fn):
        for e in range(n_e):
            fn(piece_copy(cc, slot, e, 0))
            used = sl_ref[e * n_ch + cc]

            def later(k, e=e, used=used):
                @pl.when(used > k * piece)
                def _():
                    fn(piece_copy(cc, slot, e, k))
                    if k + 1 < n_pieces:
                        later(k + 1)
            if n_pieces > 1:
                later(1)

    @pl.when(c == 0)
    def _():
        ybuf[...] = jnp.zeros_like(ybuf)
        for_each_piece(0, 0, lambda cp: cp.start())

    @pl.when(c + 1 < n_ch)
    def _():
        for_each_piece(c + 1, (c + 1) % 2, lambda cp: cp.start())

    slot = c % 2
    for_each_piece(c, slot, lambda cp: cp.wait())

    per_dot = MXU_DIM // piece
    depth = per_dot * piece
    lane = lax.broadcasted_iota(_I32, (1, depth), 1)
    p1b = jnp.broadcast_to(tok_ref[:, 0:1], (ch, depth))
    p2b = jnp.broadcast_to(tok_ref[:, 1:2], (ch, depth))
    g1b = jnp.broadcast_to(tok_ref[:, 2:3], (ch, depth))
    g2b = jnp.broadcast_to(tok_ref[:, 3:4], (ch, depth))

    def scatter_back(m, k):
        tgt = jnp.full((1, depth), -1.0, _F32)
        for i in range(per_dot):
            e = m * per_dot + i
            row = lane - i * piece + k * piece
            mine = (lane >= i * piece) & (lane < (i + 1) * piece) & (row < sl_ref[e * n_ch + c])
            tgt = jnp.where(mine, (row + w0_ref[e * n_ch + c]).astype(_F32), tgt)
        q = (jnp.where(p1b == tgt, g1b, 0.0) + jnp.where(p2b == tgt, g2b, 0.0)).astype(_BF16)
        return _dot(q, ybuf[slot, k, m * depth:(m + 1) * depth, :])

    def any_reaches(m, k):
        hit = sl_ref[(m * per_dot) * n_ch + c] > k * piece
        for i in range(1, per_dot):
            hit = jnp.logical_or(hit, sl_ref[(m * per_dot + i) * n_ch + c] > k * piece)
        return hit

    acc = res_ref[...]
    for m in range(n_e // per_dot):
        acc = acc + scatter_back(m, 0)
    acc_scr[...] = acc
    for m in range(n_e // per_dot):
        def later(k, m=m):
            @pl.when(any_reaches(m, k))
            def _():
                acc_scr[...] += scatter_back(m, k)
                if k + 1 < n_pieces:
                    later(k + 1)
        if n_pieces > 1:
            later(1)
    normed = _rmsnorm(acc_scr[...], gfin_ref[...])
    steps = ch // SUBLANES
    for k in range(tmp_scr.shape[0]):
        tmp_scr[k] = normed[:, k * LANES:(k + 1) * LANES]
    for bb in range(SUBLANES):
        for k in range(tmp_scr.shape[0]):
            out_ref[bb, :, k * LANES:(k + 1) * LANES] = (
                tmp_scr[k, pl.ds(bb, steps, stride=SUBLANES), :])


def _combine(res, tok_tab, gfin, y_sorted, w0_flat, seglen_flat, n_e, bsz, seq):
    t_rows, d = res.shape
    ch = TOKEN_CHUNK
    n_ch = t_rows // ch
    steps = ch // bsz
    return pl.pallas_call(
        functools.partial(_combine_kernel, n_e=n_e, n_ch=n_ch),
        grid_spec=pltpu.PrefetchScalarGridSpec(
            num_scalar_prefetch=2,
            grid=(n_ch,),
            in_specs=[pl.BlockSpec((ch, d), lambda c, w0, sl: (c, 0)),
                      pl.BlockSpec((ch, tok_tab.shape[1]), lambda c, w0, sl: (c, 0)),
                      pl.BlockSpec((1, d), lambda c, w0, sl: (0, 0)),
                      pl.BlockSpec(memory_space=pl.ANY)],
            out_specs=pl.BlockSpec((bsz, steps, d), lambda c, w0, sl: (0, c, 0)),
            scratch_shapes=[pltpu.VMEM((2, _window_pieces(ch), n_e * WINDOW_PIECE, d), _BF16),
                            pltpu.SemaphoreType.DMA((2, n_e, _window_pieces(ch))),
                            pltpu.VMEM((d // LANES, ch, LANES), _F32),
                            pltpu.VMEM((ch, d), _F32)]),
        out_shape=jax.ShapeDtypeStruct((bsz, seq, d), _F32),
        compiler_params=pltpu.CompilerParams(
            dimension_semantics=("arbitrary",), vmem_limit_bytes=VMEM_LIMIT),
        name="moe_combine",
    )(w0_flat, seglen_flat, res, tok_tab, gfin, y_sorted)


def _moe_layer(h, hn, logits_t, wg, wu, wd, expert0, gfin, bsz, seq):
    t_rows, d = h.shape
    n_e = logits_t.shape[0]
    tm = EXPERT_ROW_TILE
    n_ch = t_rows // TOKEN_CHUNK
    max_rows = 2 * t_rows + n_e * (SEG_ALIGN + TOKEN_CHUNK + tm)
    n_blocks = -(-max_rows // tm)
    n_blk_pad = -(-n_blocks // LANES) * LANES

    pos, gates, segtab, blktab = _route(logits_t, n_blk_pad, tm)
    pos_l = jnp.transpose(pos.reshape(2, n_ch, TOKEN_CHUNK), (1, 0, 2))
    w0_flat = segtab[0].reshape(-1)
    seglen_flat = segtab[1].reshape(-1)
    n_valid = blktab[1, :1]
    end_flat = segtab[2].reshape(-1)
    x_sorted = _dispatch(hn, pos_l, w0_flat, seglen_flat, end_flat, n_valid,
                         n_e, n_blocks * tm, tm)
    y_sorted = _ffn_grouped(x_sorted, wg, wu, wd, expert0, blktab[0], n_valid, end_flat,
                            n_ch, tm)
    tok_tab = jnp.transpose(jnp.concatenate([pos.astype(_F32), gates], axis=0))
    return _combine(h, tok_tab, gfin, y_sorted,
                    w0_flat, seglen_flat, n_e, bsz, seq)


def kernel(x, norm_mix_g, w_in, ssm_log_dt, ssm_a_re, ssm_a_im, ssm_b_re, ssm_b_im, ssm_c_re, ssm_c_im, ssm_d, ssm_w_glu, ssm_b_glu, pool_w, pool_scale, w_out, norm_ffn_g, ffn_w_gate, ffn_w_up, ffn_w_down, router_w, moe_w_gate, moe_w_up, moe_w_down, final_norm_g):
    bsz, seq, d = x.shape
    depth, n_heads, n_state, n_grp_ch = ssm_b_re.shape
    t_rows = bsz * seq
    assert bsz == SUBLANES and depth % 2 == 0
    groups = n_heads // HEADS_PER_GROUP

    nh = depth * n_heads
    per_head = lambda w: w.reshape((nh,) + w.shape[2:])
    lam, bw_all, cw_all = _discretize(
        per_head(ssm_log_dt), per_head(ssm_a_re), per_head(ssm_a_im),
        per_head(jnp.swapaxes(ssm_b_re, 2, 3)), per_head(jnp.swapaxes(ssm_b_im, 2, 3)),
        per_head(jnp.swapaxes(ssm_c_re, 2, 3)), per_head(jnp.swapaxes(ssm_c_im, 2, 3)))
    lam = lam.reshape(depth, 2 * groups, -1)
    bw = bw_all.reshape((depth, groups) + bw_all.shape[1:])
    cw = cw_all.reshape((depth, groups) + cw_all.shape[1:])
    vecs = _pack_vectors(norm_mix_g, norm_ffn_g, ssm_d, ssm_b_glu, pool_scale, lam)
    router_wt = jnp.swapaxes(router_w, 1, 2)

    h = x
    row = lambda v: v.reshape(1, -1)
    out = None
    for i in range(depth):
        is_moe = i % 2 == 1
        j = i // 2
        outs = _mix_layer(h, i, vecs, w_in, bw, cw, ssm_w_glu, pool_w, w_out,
                          router_wt if is_moe else None, j)
        if is_moe:
            assert i == depth - 1
            h, hn, logits_t = outs
            n_e = moe_w_gate.shape[1]
            stack = lambda w: w.reshape((-1,) + w.shape[2:])
            out = _moe_layer(h, hn, logits_t, stack(moe_w_gate), stack(moe_w_up),
                             stack(moe_w_down), j * n_e, row(final_norm_g), bsz, seq)
        else:
            h, hn = outs
            h = _ffn(hn, ffn_w_gate, ffn_w_up, ffn_w_down, j, h)
    return out
```

```python
import functools
import math

import jax
import jax.numpy as jnp
from jax import lax
from jax.experimental import pallas as pl
from jax.experimental.pallas import tpu as pltpu

RMS_EPS = 1e-6
POOL_WINDOWS = (2, 4, 8, 16)
A_RE_MAX = -1e-4
GELU_C0 = math.sqrt(2.0 / math.pi)
GELU_C1 = 0.044715

SUBLANES = 8
LANES = 128
MXU_DIM = 256

HEADS_PER_GROUP = 16
TIME_TILE = 64
SCAN_COLS = 512
FFN_ROW_TILE = 512
EXPERT_ROW_TILE = 512
FFN_COL_CHUNK = 256
WEIGHT_UNITS = 8
TOKEN_CHUNK = MXU_DIM
SEG_ALIGN = 2 * SUBLANES
WINDOW_PIECE = 96
VMEM_LIMIT = 56 * 1024 * 1024
VEC_NORM_MIX, VEC_NORM_FFN, VEC_SKIP_BIAS, VEC_POOL_SCALE, VEC_LAMBDA = range(5)

_F32 = jnp.float32
_BF16 = jnp.bfloat16
_I32 = jnp.int32


def _dot(a, b):
    return jnp.dot(a, b, preferred_element_type=_F32)


def _dot_nt(a, b):
    return lax.dot_general(a, b, (((1,), (1,)), ((), ())), preferred_element_type=_F32)


def _rmsnorm(x, g):
    inv = lax.rsqrt(jnp.mean(x * x, axis=-1, keepdims=True) + RMS_EPS)
    return x * inv * g


def _sigmoid(x):
    return 1.0 / (1.0 + jnp.exp(-x))


def _ceil_to(x, m):
    return jnp.floor((x + (m - 1.0)) * (1.0 / m)) * m


def _window_pieces(chunk):
    return -(-(chunk + SEG_ALIGN - 1) // WINDOW_PIECE)


def _discretize_kernel(log_dt_ref, a_re_ref, a_im_ref, b_re_ref, b_im_ref, c_re_ref, c_im_ref,
                       lam_ref, bw_ref, cw_ref):
    dt = jnp.exp(log_dt_ref[...])
    ar = jnp.minimum(a_re_ref[...], A_RE_MAX)
    ai = a_im_ref[...]
    mag = jnp.exp(ar * dt)
    lam_re = mag * jnp.cos(ai * dt)
    lam_im = mag * jnp.sin(ai * dt)
    den = ar * ar + ai * ai
    nr = lam_re - 1.0
    ni = lam_im
    coef_re = (nr * ar + ni * ai) / den
    coef_im = (ni * ar - nr * ai) / den
    br = b_re_ref[...]
    bi = b_im_ref[...]
    bb_re = (coef_re * br - coef_im * bi).astype(_BF16)
    bb_im = (coef_re * bi + coef_im * br).astype(_BF16)
    c_re = c_re_ref[...].astype(_BF16)
    c_im_neg = (-c_im_ref[...]).astype(_BF16)

    n, g, p = br.shape
    hpg = n // lam_ref.shape[0]
    half = hpg * p
    bw_ref[...] = jnp.zeros_like(bw_ref)
    cw_ref[...] = jnp.zeros_like(cw_ref)
    for h in range(n):
        q, hl = divmod(h, hpg)
        st = slice(hl * p, (hl + 1) * p)
        st_im = slice(half + hl * p, half + (hl + 1) * p)
        ch = slice(hl * g, (hl + 1) * g)
        lam_ref[q, 0:1, st] = lam_re[h]
        lam_ref[q, 1:2, st] = lam_im[h]
        bw_ref[q, ch, st] = bb_re[h]
        bw_ref[q, ch, st_im] = bb_im[h]
        cw_ref[q, st, ch] = c_re[h]
        cw_ref[q, st_im, ch] = c_im_neg[h]


def _discretize(log_dt, a_re, a_im, b_re_t, b_im_t, c_re_t, c_im_t):
    n, g, p = b_re_t.shape
    hpg = HEADS_PER_GROUP
    nq = n // hpg
    full3 = lambda s: pl.BlockSpec(s, lambda: (0, 0, 0))
    return pl.pallas_call(
        _discretize_kernel,
        out_shape=(jax.ShapeDtypeStruct((nq, 2, hpg * p), _F32),
                   jax.ShapeDtypeStruct((nq, hpg * g, 2 * hpg * p), _BF16),
                   jax.ShapeDtypeStruct((nq, 2 * hpg * p, hpg * g), _BF16)),
        in_specs=[full3((n, 1, 1)), full3((n, 1, p)), full3((n, 1, p)),
                  full3((n, g, p)), full3((n, g, p)), full3((n, p, g)), full3((n, p, g))],
        out_specs=(full3((nq, 2, hpg * p)), full3((nq, hpg * g, 2 * hpg * p)),
                   full3((nq, 2 * hpg * p, hpg * g))),
        name="ssm_discretize",
    )(log_dt.reshape(n, 1, 1), a_re.reshape(n, 1, p), a_im.reshape(n, 1, p),
      b_re_t, b_im_t, c_re_t, c_im_t)


def _mix_kernel(*refs, time_tile, with_router, batch_major_in):
    (h_ref, vec_ref, win_f32, bw_ref, cw_ref, wglu_f32, poolw_f32, wout_f32) = refs[:8]
    refs = refs[8:]
    d_ssm = wglu_f32.shape[0]
    gmix_ref = vec_ref.at[VEC_NORM_MIX:VEC_NORM_MIX + 1]
    gffn_ref = vec_ref.at[VEC_NORM_FFN:VEC_NORM_FFN + 1]
    dskip_ref = vec_ref.at[VEC_SKIP_BIAS:VEC_SKIP_BIAS + 1, 0:d_ssm]
    bglu_ref = vec_ref.at[VEC_SKIP_BIAS:VEC_SKIP_BIAS + 1, d_ssm:2 * d_ssm]
    pscale_ref = vec_ref.at[VEC_POOL_SCALE:VEC_POOL_SCALE + 1]
    lam_ref = vec_ref.at[VEC_LAMBDA:]
    rw_ref = logits_ref = xin_scr = None
    if with_router:
        rw_ref, hout_ref, hn_ref, logits_ref = refs[:4]
        refs = refs[4:]
    else:
        hout_ref, hn_ref = refs[:2]
        refs = refs[2:]
    (s_scr, state_scr, ext_scr, mixed_scr, u_save, h_save,
     win_ref, wglu_ref, poolw_ref, wout_ref) = refs[:10]
    if batch_major_in:
        xin_scr = refs[10]

    step = pl.program_id(0)
    rows = time_tile * SUBLANES
    n_groups = bw_ref.shape[0]
    gin = bw_ref.shape[1]
    gstate = bw_ref.shape[2] // 2
    hist = ext_scr.shape[0] - rows

    @pl.when(step == 0)
    def _():
        state_scr[...] = jnp.zeros_like(state_scr)
        ext_scr[0:hist, :] = jnp.zeros((hist, ext_scr.shape[1]), _F32)
        win_ref[...] = win_f32[...].astype(_BF16)
        wglu_ref[...] = wglu_f32[...].astype(_BF16)
        poolw_ref[...] = poolw_f32[...].astype(_BF16)
        wout_ref[...] = wout_f32[...].astype(_BF16)

    if batch_major_in:
        for bb in range(SUBLANES):
            for k in range(xin_scr.shape[0]):
                xin_scr[k, pl.ds(bb, time_tile, stride=SUBLANES), :] = (
                    h_ref[bb, :, k * LANES:(k + 1) * LANES])
        h = jnp.concatenate([xin_scr[k] for k in range(xin_scr.shape[0])], axis=1)
    else:
        h = h_ref[...]
    h_save[...] = h
    hn = _rmsnorm(h, gmix_ref[...]).astype(_BF16)
    proj = _dot(hn, win_ref[...])
    u_ssm = proj[:, :d_ssm]
    u_save[...] = u_ssm
    ext_scr[hist:, :] = proj[:, d_ssm:]

    def drive(q):
        ug = u_ssm[:, q * gin:(q + 1) * gin].astype(_BF16)
        s_scr[q] = _dot(ug, bw_ref[q])

    def scan(q, c):
        re0 = c * SCAN_COLS
        im0 = gstate + c * SCAN_COLS
        lr = jnp.broadcast_to(lam_ref[2 * q:2 * q + 1, re0:re0 + SCAN_COLS],
                              (SUBLANES, SCAN_COLS))
        li = jnp.broadcast_to(lam_ref[2 * q + 1:2 * q + 2, re0:re0 + SCAN_COLS],
                              (SUBLANES, SCAN_COLS))
        sre = state_scr[q, :, re0:re0 + SCAN_COLS]
        sim = state_scr[q, :, im0:im0 + SCAN_COLS]
        for t in range(time_tile):
            r0 = t * SUBLANES
            bre = s_scr[q, r0:r0 + SUBLANES, re0:re0 + SCAN_COLS]
            bim = s_scr[q, r0:r0 + SUBLANES, im0:im0 + SCAN_COLS]
            sre, sim = (lr * sre - li * sim + bre, lr * sim + li * sre + bim)
            s_scr[q, r0:r0 + SUBLANES, re0:re0 + SCAN_COLS] = sre
            s_scr[q, r0:r0 + SUBLANES, im0:im0 + SCAN_COLS] = sim
        state_scr[q, :, re0:re0 + SCAN_COLS] = sre
        state_scr[q, :, im0:im0 + SCAN_COLS] = sim

    def pool():
        n_ext = rows + hist
        pos = step * time_tile + (lax.broadcasted_iota(_I32, (rows, LANES), 0) // SUBLANES)
        for g, w in enumerate(POOL_WINDOWS):
            c0 = g * LANES
            e = ext_scr[:, c0:c0 + LANES]
            acc = e
            n_acc = n_ext
            span = 1
            while span < w:
                sh = span * SUBLANES
                acc = acc[sh:, :] + acc[:n_acc - sh, :]
                n_acc -= sh
                span *= 2
            wsum = acc[n_acc - rows:, :]
            cnt = jnp.minimum(pos + 1, w).astype(_F32)
            pooled = wsum / cnt - e[hist:, :]
            mg = _dot(pooled.astype(_BF16), poolw_ref[g]) * pscale_ref[:, c0:c0 + LANES]
            mixed_scr[:, d_ssm + c0:d_ssm + c0 + LANES] = mg.astype(_BF16)
        ext_scr[0:hist, :] = ext_scr[rows:rows + hist, :]

    def readout(q):
        cols = slice(q * gin, (q + 1) * gin)
        yq = _dot(s_scr[q].astype(_BF16), cw_ref[q])
        yq = yq + dskip_ref[:, cols] * u_save[:, cols]
        hq = 0.5 * yq * (1.0 + jnp.tanh(GELU_C0 * (yq + GELU_C1 * (yq * yq * yq))))
        mixed_scr[:, cols] = hq.astype(_BF16)

    def glu_gate():
        hg = mixed_scr[:, :d_ssm]
        gate = _sigmoid(_dot(hg, wglu_ref[...]) + bglu_ref[...])
        mixed_scr[:, :d_ssm] = (hg.astype(_F32) * gate).astype(_BF16)

    def project_out():
        hout = h_save[...] + _dot(mixed_scr[...], wout_ref[...])
        hout_ref[...] = hout
        hn2 = _rmsnorm(hout, gffn_ref[...])
        hn_hi = hn2.astype(_BF16)
        hn_ref[...] = hn_hi
        if with_router:
            hn_lo = (hn2 - hn_hi.astype(_F32)).astype(_BF16)
            rw = rw_ref[...]
            rw_hi = rw.astype(_BF16)
            rw_lo = (rw - rw_hi.astype(_F32)).astype(_BF16)
            logits_ref[...] = (_dot_nt(rw_hi, hn_hi) + _dot_nt(rw_hi, hn_lo)
                               + _dot_nt(rw_lo, hn_hi))

    for q in range(n_groups):
        drive(q)
    for c in range(gstate // SCAN_COLS):
        for q in range(n_groups):
            scan(q, c)
        if c == 0:
            pool()
    for q in range(n_groups):
        readout(q)
    glu_gate()
    project_out()


def _pack_vectors(norm_mix_g, norm_ffn_g, ssm_d, ssm_b_glu, pool_scale, lam):
    depth, d = norm_mix_g.shape
    row = lambda v: jnp.pad(v, ((0, 0), (0, d - v.shape[1])))[:, None, :]
    return jnp.concatenate(
        [row(norm_mix_g), row(norm_ffn_g), row(jnp.concatenate([ssm_d, ssm_b_glu], axis=1)),
         row(pool_scale), lam], axis=1)


def _mix_layer(h, layer, vecs, w_in, bw, cw, wglu, poolw, wout, router_wt, moe_layer):
    batch_major_in = h.ndim == 3
    d = h.shape[-1]
    t_rows = h.size // d
    rows = TIME_TILE * SUBLANES
    n_steps = t_rows // rows
    d_ssm = wglu.shape[1]
    d_pool = d - d_ssm
    n_groups, _, two_gstate = bw.shape[1:]
    hist = max(POOL_WINDOWS) * SUBLANES
    with_router = router_wt is not None

    def const(a, idx):
        nd = a.ndim
        return pl.BlockSpec((None,) + a.shape[1:], lambda i, nd=nd, idx=idx: (idx,) + (0,) * (nd - 1),
                            pipeline_mode=pl.Buffered(1))

    row_blk = lambda: pl.BlockSpec((rows, d), lambda i: (i, 0))
    ins = [h, vecs, w_in, bw, cw, wglu, poolw, wout]
    h_spec =(pl.BlockSpec((h.shape[0], TIME_TILE, d), lambda i: (0, i, 0))
              if batch_major_in else row_blk())
    in_specs = [h_spec] + [const(a, layer) for a in ins[1:]]
    out_shape = [jax.ShapeDtypeStruct((t_rows, d), _F32),
                 jax.ShapeDtypeStruct((t_rows, d), _BF16)]
    out_specs = [row_blk(), row_blk()]
    if with_router:
        n_e = router_wt.shape[1]
        ins.append(router_wt)
        in_specs.append(const(router_wt, moe_layer))
        out_shape.append(jax.ShapeDtypeStruct((n_e, t_rows), _F32))
        out_specs.append(pl.BlockSpec((n_e, rows), lambda i: (0, i)))
    scratch = [
        pltpu.VMEM((n_groups, rows, two_gstate), _F32),
        pltpu.VMEM((n_groups, SUBLANES, two_gstate), _F32),
        pltpu.VMEM((rows + hist, d_pool), _F32),
        pltpu.VMEM((rows, d_ssm + d_pool), _BF16),
        pltpu.VMEM((rows, d_ssm), _F32),
        pltpu.VMEM((rows, d), _F32),
        pltpu.VMEM(w_in.shape[1:], _BF16),
        pltpu.VMEM(wglu.shape[1:], _BF16),
        pltpu.VMEM(poolw.shape[1:], _BF16),
        pltpu.VMEM(wout.shape[1:], _BF16),
    ]
    if batch_major_in:
        scratch.append(pltpu.VMEM((d // LANES, rows, LANES), _F32))
    return pl.pallas_call(
        functools.partial(_mix_kernel, time_tile=TIME_TILE, with_router=with_router,
                          batch_major_in=batch_major_in),
        grid=(n_steps,),
        in_specs=in_specs,
        out_specs=out_specs,
        out_shape=out_shape,
        scratch_shapes=scratch,
        compiler_params=pltpu.CompilerParams(
            dimension_semantics=("arbitrary",), vmem_limit_bytes=VMEM_LIMIT),
        name="mix_router" if with_router else "mix",
    )(*ins)


def _swiglu(x, wg_ref, wu_ref, wd_ref, a_scr, slot):
    tf = wg_ref.shape[2]
    c0 = 0
    while c0 < tf:
        cw = min(FFN_COL_CHUNK, tf - c0)
        g = _dot(x, wg_ref[slot, :, c0:c0 + cw])
        u = _dot(x, wu_ref[slot, :, c0:c0 + cw])
        a_scr[:, c0:c0 + cw] = (g * _sigmoid(g) * u).astype(_BF16)
        c0 += cw
    return _dot(a_scr[...], wd_ref[slot])


def _ffn_kernel(x_ref, wg_hbm, wu_hbm, wd_hbm, res_ref, out_ref,
                wg_buf, wu_buf, wd_buf, stg_g, stg_u, stg_d, sems, a_scr, *, layer):
    units = WEIGHT_UNITS
    gu_rows = wg_buf.shape[1] // units
    d_rows = wd_buf.shape[1] // units

    @pl.when(pl.program_id(0) == 0)
    def _():
        def copies(k):
            st = k % 2
            return (pltpu.make_async_copy(wg_hbm.at[layer, pl.ds(k * gu_rows, gu_rows)],
                                          stg_g.at[st], sems.at[st, 0]),
                    pltpu.make_async_copy(wu_hbm.at[layer, pl.ds(k * gu_rows, gu_rows)],
                                          stg_u.at[st], sems.at[st, 1]),
                    pltpu.make_async_copy(wd_hbm.at[layer, pl.ds(k * d_rows, d_rows)],
                                          stg_d.at[st], sems.at[st, 2]))

        for cp in copies(0):
            cp.start()
        for k in range(units):
            if k + 1 < units:
                for cp in copies(k + 1):
                    cp.start()
            for cp in copies(k):
                cp.wait()
            st = k % 2
            wg_buf[0, k * gu_rows:(k + 1) * gu_rows, :] = stg_g[st].astype(_BF16)
            wu_buf[0, k * gu_rows:(k + 1) * gu_rows, :] = stg_u[st].astype(_BF16)
            wd_buf[0, k * d_rows:(k + 1) * d_rows, :] = stg_d[st].astype(_BF16)

    out_ref[...] = res_ref[...] + _swiglu(x_ref[...], wg_buf, wu_buf, wd_buf, a_scr, 0)


def _ffn(x, wg, wu, wd, layer, res):
    t_rows, d = x.shape
    f = wg.shape[2]
    tm = FFN_ROW_TILE
    hbm = pl.BlockSpec(memory_space=pl.ANY)
    return pl.pallas_call(
        functools.partial(_ffn_kernel, layer=layer),
        grid=(t_rows // tm,),
        in_specs=[pl.BlockSpec((tm, d), lambda i: (i, 0)), hbm, hbm, hbm,
                  pl.BlockSpec((tm, d), lambda i: (i, 0))],
        out_specs=pl.BlockSpec((tm, d), lambda i: (i, 0)),
        out_shape=jax.ShapeDtypeStruct((t_rows, d), _F32),
        scratch_shapes=[pltpu.VMEM((1, d, f), _BF16),
                        pltpu.VMEM((1, d, f), _BF16),
                        pltpu.VMEM((1, f, d), _BF16),
                        pltpu.VMEM((2, d // WEIGHT_UNITS, f), _F32),
                        pltpu.VMEM((2, d // WEIGHT_UNITS, f), _F32),
                        pltpu.VMEM((2, f // WEIGHT_UNITS, d), _F32),
                        pltpu.SemaphoreType.DMA((2, 3)),
                        pltpu.VMEM((tm, f), _BF16)],
        compiler_params=pltpu.CompilerParams(
            dimension_semantics=("arbitrary",), vmem_limit_bytes=VMEM_LIMIT),
        name="ffn_dense",
    )(x, wg, wu, wd, res)


def _route_kernel(lt_ref, pos_ref, gates_ref, segtab_ref, blktab_ref, *, n_blk, row_tile):
    l = lt_ref[...]
    n_e, t_rows = l.shape
    ch = TOKEN_CHUNK
    n_ch = t_rows // ch
    ie = lax.broadcasted_iota(_I32, l.shape, 0)
    m1 = jnp.max(l, axis=0, keepdims=True)
    i1 = jnp.min(jnp.where(l == m1, ie, n_e), axis=0, keepdims=True)
    l2 = jnp.where(ie == i1, -jnp.inf, l)
    m2 = jnp.max(l2, axis=0, keepdims=True)
    i2 = jnp.min(jnp.where(l2 == m2, ie, n_e), axis=0, keepdims=True)
    e2 = jnp.exp(m2 - m1)
    den = 1.0 + e2
    gates_ref[0:1, :] = 1.0 / den
    gates_ref[1:2, :] = e2 / den
    sel1 = ie == i1
    sel2 = ie == i2
    chosen = jnp.where(sel1 | sel2, 1.0, 0.0).astype(_BF16)

    chunk_of_t = lax.broadcasted_iota(_I32, (n_ch, t_rows), 1) // ch
    in_chunk = jnp.where(chunk_of_t == lax.broadcasted_iota(_I32, (n_ch, t_rows), 0),
                         1.0, 0.0).astype(_BF16)
    n_ec = _dot_nt(chosen, in_chunk)
    n_ce = _dot_nt(in_chunk, chosen)
    cc_r = lax.broadcasted_iota(_I32, (n_ch, n_ch), 0)
    cc_c = lax.broadcasted_iota(_I32, (n_ch, n_ch), 1)
    before = jnp.where(cc_r < cc_c, 1.0, 0.0).astype(_BF16)
    after = jnp.where(cc_c < cc_r, 1.0, 0.0).astype(_BF16)
    segoff_ec = _dot(n_ec.astype(_BF16), before)
    segoff_ce = _dot(after, n_ce.astype(_BF16))
    def window(segoff, n):
        start = jnp.floor(segoff * (1.0 / SEG_ALIGN)) * SEG_ALIGN
        used = segoff - start + n
        return start, used, start + _ceil_to(jnp.maximum(used, 1.0), float(WINDOW_PIECE))

    wstart_ec, used_ec, wend_ec = window(segoff_ec, n_ec)
    _, _, wend_ce = window(segoff_ce, n_ce)
    tot_col = _ceil_to(jnp.max(wend_ec, axis=1, keepdims=True), float(row_tile))
    tot_row = _ceil_to(jnp.max(wend_ce, axis=0, keepdims=True), float(row_tile))
    ee_r = lax.broadcasted_iota(_I32, (n_e, n_e), 0)
    ee_c = lax.broadcasted_iota(_I32, (n_e, n_e), 1)
    base_col = jnp.sum(jnp.where(ee_c < ee_r, tot_row, 0.0), axis=1, keepdims=True)
    w0_ec = base_col + segoff_ec
    end_col = base_col + tot_col
    segtab_ref[0] = (base_col + wstart_ec).astype(_I32)
    segtab_ref[1] = used_ec.astype(_I32)
    segtab_ref[2] = jnp.broadcast_to(end_col, (n_e, n_ch)).astype(_I32)

    tt_r = lax.broadcasted_iota(_I32, (ch, ch), 0)
    tt_c = lax.broadcasted_iota(_I32, (ch, ch), 1)
    earlier = jnp.where(tt_r < tt_c, 1.0, 0.0).astype(_BF16)
    for c in range(n_ch):
        cols = slice(c * ch, (c + 1) * ch)
        rank = _dot(chosen[:, cols], earlier)
        offs = w0_ec[:, c:c + 1] + rank
        p1 = jnp.sum(jnp.where(sel1[:, cols], offs, 0.0), axis=0, keepdims=True)
        p2 = jnp.sum(jnp.where(sel2[:, cols], offs, 0.0), axis=0, keepdims=True)
        pos_ref[0:1, cols] = p1.astype(_I32)
        pos_ref[1:2, cols] = p2.astype(_I32)

    start = (lax.broadcasted_iota(_I32, (1, n_blk), 1) * row_tile).astype(_F32)
    owner = jnp.sum(jnp.where(end_col <= start, 1.0, 0.0), axis=0, keepdims=True)
    n_valid = jnp.sum(tot_col, axis=0, keepdims=True) * (1.0 / row_tile)
    blktab_ref[0:1, :] = jnp.minimum(owner, n_e - 1.0).astype(_I32)
    blktab_ref[1:2, :] = jnp.broadcast_to(n_valid, (1, n_blk)).astype(_I32)
    real_end_col = base_col + jnp.sum(n_ec, axis=1, keepdims=True)
    mine = lax.broadcasted_iota(_I32, (n_e, n_blk), 0).astype(_F32) == owner
    real = jnp.clip(real_end_col - start, 0.0, float(row_tile))
    blktab_ref[2:3, :] = jnp.sum(jnp.where(mine, real, 0.0), axis=0, keepdims=True).astype(_I32)


def _route(logits_t, n_blk, row_tile):
    n_e, t_rows = logits_t.shape
    n_ch = t_rows // TOKEN_CHUNK
    full = lambda s: pl.BlockSpec(s, lambda: (0,) * len(s))
    return pl.pallas_call(
        functools.partial(_route_kernel, n_blk=n_blk, row_tile=row_tile),
        in_specs=[full((n_e, t_rows))],
        out_specs=(full((2, t_rows)), full((2, t_rows)), full((3, n_e, n_ch)), full((3, n_blk))),
        out_shape=(jax.ShapeDtypeStruct((2, t_rows), _I32),
                   jax.ShapeDtypeStruct((2, t_rows), _F32),
                   jax.ShapeDtypeStruct((3, n_e, n_ch), _I32),
                   jax.ShapeDtypeStruct((3, n_blk), _I32)),
        compiler_params=pltpu.CompilerParams(vmem_limit_bytes=VMEM_LIMIT),
        name="route_sort",
    )(logits_t)


def _dispatch_kernel(w0_ref, sl_ref, end_ref, nv_ref, hn_ref, posl_ref, x_hbm,
                     stage, sems, zero_scr, zsems, tsems, carry, *, n_e, n_ch, row_tile, min_blocks):
    c = pl.program_id(0)
    ch = hn_ref.shape[0]
    piece = WINDOW_PIECE
    n_pieces = stage.shape[1]
    slot = c % 2
    n_blocks = x_hbm.shape[0] // row_tile

    def for_each_unused_block(fn):
        for b in range(min_blocks, n_blocks):
            @pl.when(b >= nv_ref[0])
            def _(b=b):
                fn(pltpu.make_async_copy(zero_scr, x_hbm.at[pl.ds(b * row_tile, row_tile)],
                                         tsems.at[b - min_blocks]))

    def piece_copy(cc, sl, e, k):
        row0 = pl.multiple_of(w0_ref[e * n_ch + cc] + k * piece, SEG_ALIGN)
        return pltpu.make_async_copy(
            stage.at[sl, k, pl.ds(e * piece, piece)],
            x_hbm.at[pl.ds(row0, piece)], sems.at[sl, e, k])

    def for_each_piece(cc, sl, fn):
        for e in range(n_e):
            fn(piece_copy(cc, sl, e, 0))
            used = sl_ref[e * n_ch + cc]

            def later(k, e=e, used=used):
                @pl.when(used > k * piece)
                def _():
                    fn(piece_copy(cc, sl, e, k))
                    if k + 1 < n_pieces:
                        later(k + 1)
            if n_pieces > 1:
                later(1)

    @pl.when(c == 0)
    def _():
        zero_scr[...] = jnp.zeros_like(zero_scr)
        carry[...] = jnp.zeros_like(carry)
        fills = [pltpu.make_async_copy(
            zero_scr,
            x_hbm.at[pl.ds(pl.multiple_of(end_ref[e * n_ch] - row_tile, SEG_ALIGN), row_tile)],
            zsems.at[e]) for e in range(n_e)]
        for cp in fills:
            cp.start()
        for_each_unused_block(lambda cp: cp.start())
        for cp in fills:
            cp.wait()

    pp = posl_ref[0]
    hn = hn_ref[...]

    def gather_piece(k):
        row_j = lax.broadcasted_iota(_I32, (piece, ch), 0) + k * piece
        parts = []
        for e in range(n_e):
            row_id = row_j + w0_ref[e * n_ch + c]
            hit = (pp[0:1, :] == row_id) | (pp[1:2, :] == row_id)
            parts.append(jnp.where(hit, 1.0, 0.0).astype(_BF16))
        onehot = jnp.concatenate(parts, axis=0)
        stage[slot, k] = _dot(onehot, hn).astype(_BF16)

    gather_piece(0)
    for e in range(n_e):
        head = slice(e * piece, e * piece + SEG_ALIGN)
        stage[slot, 0, head, :] = stage[slot, 0, head, :] + carry[e]
    longest = sl_ref[c]
    for e in range(1, n_e):
        longest = jnp.maximum(longest, sl_ref[e * n_ch + c])

    def later(k):
        @pl.when(longest > k * piece)
        def _():
            gather_piece(k)
            if k + 1 < n_pieces:
                later(k + 1)
    if n_pieces > 1:
        later(1)
    for e in range(n_e):
        used = sl_ref[e * n_ch + c]
        gathered = ((jnp.maximum(used, 1) + piece - 1) // piece) * piece
        off = (used // SEG_ALIGN) * SEG_ALIGN
        src = jnp.minimum(off, gathered - SEG_ALIGN)
        grp = stage[slot, src // piece,
                    pl.ds(pl.multiple_of(e * piece + src % piece, SEG_ALIGN), SEG_ALIGN), :]
        carry[e] = jnp.where(off < gathered, grp, jnp.zeros_like(grp))

    @pl.when(c > 0)
    def _():
        for_each_piece(c - 1, 1 - slot, lambda cp: cp.wait())

    for_each_piece(c, slot, lambda cp: cp.start())

    @pl.when(c == n_ch - 1)
    def _():
        for_each_piece(c, slot, lambda cp: cp.wait())
        for_each_unused_block(lambda cp: cp.wait())


def _dispatch(hn, pos_l, w0_flat, seglen_flat, end_flat, n_valid, n_e, n_rows, row_tile):
    t_rows, d = hn.shape
    ch = TOKEN_CHUNK
    n_ch = t_rows // ch
    n_pieces = _window_pieces(ch)
    min_blocks = 2 * t_rows // row_tile
    return pl.pallas_call(
        functools.partial(_dispatch_kernel, n_e=n_e, n_ch=n_ch, row_tile=row_tile,
                          min_blocks=min_blocks),
        grid_spec=pltpu.PrefetchScalarGridSpec(
            num_scalar_prefetch=4,
            grid=(n_ch,),
            in_specs=[pl.BlockSpec((ch, d), lambda c, *_: (c, 0)),
                      pl.BlockSpec((1, 2, ch), lambda c, *_: (c, 0, 0))],
            out_specs=pl.BlockSpec(memory_space=pl.ANY),
            scratch_shapes=[pltpu.VMEM((2, n_pieces, n_e * WINDOW_PIECE, d), _BF16),
                            pltpu.SemaphoreType.DMA((2, n_e, n_pieces)),
                            pltpu.VMEM((row_tile, d), _BF16),
                            pltpu.SemaphoreType.DMA((n_e,)),
                            pltpu.SemaphoreType.DMA((n_rows // row_tile - min_blocks,)),
                            pltpu.VMEM((n_e, SEG_ALIGN, d), _BF16)]),
        out_shape=jax.ShapeDtypeStruct((n_rows, d), _BF16),
        compiler_params=pltpu.CompilerParams(
            dimension_semantics=("arbitrary",), vmem_limit_bytes=VMEM_LIMIT),
        name="moe_dispatch",
    )(w0_flat, seglen_flat, end_flat, n_valid, hn, pos_l)


def _ffn_grouped_kernel(be_ref, nv_ref, end_ref, bu_ref, x_ref, wg_hbm, wu_hbm, wd_hbm, y_ref,
                        wg_buf, wu_buf, wd_buf, stg_g, stg_u, stg_d, sems, state, a_scr,
                        *, n_ch, expert0):
    b = pl.program_id(0)
    n_valid = nv_ref[0]
    valid = b < n_valid
    row_tile = x_ref.shape[0]
    units = WEIGHT_UNITS
    gu_rows = wg_buf.shape[1] // units
    d_rows = wd_buf.shape[1] // units
    ST_SLOT, ST_RESIDENT, ST_NEXT, ST_DONE, ST_STAGE = range(5)

    def unit_copies(e, k, st):
        g0 = pl.multiple_of(k * gu_rows, gu_rows)
        d0 = pl.multiple_of(k * d_rows, SEG_ALIGN)
        return (pltpu.make_async_copy(wg_hbm.at[expert0 + e, pl.ds(g0, gu_rows)], stg_g.at[st],
                                      sems.at[st, 0]),
                pltpu.make_async_copy(wu_hbm.at[expert0 + e, pl.ds(g0, gu_rows)], stg_u.at[st],
                                      sems.at[st, 1]),
                pltpu.make_async_copy(wd_hbm.at[expert0 + e, pl.ds(d0, d_rows)], stg_d.at[st],
                                      sems.at[st, 2]))

    def start(e, k, st):
        for cp in unit_copies(e, k, st):
            cp.start()

    def wait(e, k, st):
        for cp in unit_copies(e, k, st):
            cp.wait()

    def cast_unit(slot, k, st):
        g0 = pl.multiple_of(k * gu_rows, gu_rows)
        d0 = pl.multiple_of(k * d_rows, SEG_ALIGN)
        wg_buf[slot, pl.ds(g0, gu_rows), :] = stg_g[st].astype(_BF16)
        wu_buf[slot, pl.ds(g0, gu_rows), :] = stg_u[st].astype(_BF16)
        wd_buf[slot, pl.ds(d0, d_rows), :] = stg_d[st].astype(_BF16)

    def expert_after(e):
        blk = end_ref[e * n_ch] // row_tile
        return jnp.where(blk < n_valid, be_ref[jnp.minimum(blk, n_valid - 1)], -1)

    def become_resident(slot, e):
        state[ST_SLOT] = slot
        state[ST_RESIDENT] = e
        nxt = expert_after(e)
        state[ST_NEXT] = nxt
        state[ST_DONE] = 0

        @pl.when(nxt >= 0)
        def _():
            start(nxt, 0, 0)
        state[ST_STAGE] = 0

    e_b = be_ref[jnp.minimum(b, n_valid - 1)]

    @pl.when(b == 0)
    def _():
        state[ST_SLOT] = 1
        state[ST_RESIDENT] = -1
        state[ST_NEXT] = e_b
        state[ST_DONE] = 0
        state[ST_STAGE] = 0
        start(e_b, 0, 0)

    s_slot = state[ST_SLOT]
    s_next = state[ST_NEXT]
    s_done = state[ST_DONE]
    s_stage = state[ST_STAGE]
    change = valid & (e_b != state[ST_RESIDENT])
    steady = valid & jnp.logical_not(change) & (s_next >= 0) & (s_done < units)

    @pl.when(change)
    def _():
        other = 1 - s_slot

        @pl.when(s_done < units)
        def _():
            wait(e_b, s_done, s_stage)
            cast_unit(other, s_done, s_stage)

            def body(k, carry):
                start(e_b, k, 0)
                wait(e_b, k, 0)
                cast_unit(other, k, 0)
                return carry
            lax.fori_loop(s_done + 1, units, body, 0)
        become_resident(other, e_b)

    half = row_tile // 2
    rows_used = jnp.where(valid, bu_ref[jnp.minimum(b, n_valid - 1)], 0)
    whole = rows_used > half
    part = (rows_used > 0) & jnp.logical_not(whole)

    def run_block(slot):
        y_ref[...] = _swiglu(x_ref[...], wg_buf, wu_buf, wd_buf, a_scr, slot).astype(_BF16)

    @pl.when(steady)
    def _():
        wait(s_next, s_done, s_stage)

        @pl.when(s_done + 1 < units)
        def _():
            start(s_next, s_done + 1, 1 - s_stage)
        state[ST_DONE] = s_done + 1
        state[ST_STAGE] = 1 - s_stage

    @pl.when(steady & whole)
    def _():
        cast_unit(1 - s_slot, s_done, s_stage)
        run_block(s_slot)

    @pl.when(steady & jnp.logical_not(whole))
    def _():
        cast_unit(1 - s_slot, s_done, s_stage)

    @pl.when(jnp.logical_not(steady) & whole)
    def _():
        run_block(state[ST_SLOT])

    @pl.when(part)
    def _():
        slot = state[ST_SLOT]
        y_ref[0:half, :] = _swiglu(x_ref[0:half, :], wg_buf, wu_buf, wd_buf,
                                   a_scr.at[0:half], slot).astype(_BF16)
        y_ref[half:, :] = jnp.zeros((row_tile - half, y_ref.shape[1]), _BF16)

    @pl.when(rows_used == 0)
    def _():
        y_ref[...] = jnp.zeros_like(y_ref)


def _ffn_grouped(x_sorted, wg, wu, wd, expert0, block_expert, n_valid, end_flat, block_rows,
                 n_ch, row_tile):
    rows, d = x_sorted.shape
    f = wg.shape[2]
    n_blocks = rows // row_tile
    hbm = pl.BlockSpec(memory_space=pl.ANY)
    return pl.pallas_call(
        functools.partial(_ffn_grouped_kernel, n_ch=n_ch, expert0=expert0),
        grid_spec=pltpu.PrefetchScalarGridSpec(
            num_scalar_prefetch=4,
            grid=(n_blocks,),
            in_specs=[pl.BlockSpec((row_tile, d),
                                   lambda b, be, nv, en, bu: (jnp.minimum(b, nv[0] - 1), 0)),
                      hbm, hbm, hbm],
            out_specs=pl.BlockSpec((row_tile, d), lambda b, be, nv, en, bu: (b, 0)),
            scratch_shapes=[pltpu.VMEM((2, d, f), _BF16),
                            pltpu.VMEM((2, d, f), _BF16),
                            pltpu.VMEM((2, f, d), _BF16),
                            pltpu.VMEM((2, d // WEIGHT_UNITS, f), _F32),
                            pltpu.VMEM((2, d // WEIGHT_UNITS, f), _F32),
                            pltpu.VMEM((2, f // WEIGHT_UNITS, d), _F32),
                            pltpu.SemaphoreType.DMA((2, 3)),
                            pltpu.SMEM((5,), _I32),
                            pltpu.VMEM((row_tile, f), _BF16)]),
        out_shape=jax.ShapeDtypeStruct((rows, d), _BF16),
        compiler_params=pltpu.CompilerParams(
            dimension_semantics=("arbitrary",), vmem_limit_bytes=VMEM_LIMIT),
        name="moe_experts",
    )(block_expert, n_valid, end_flat, block_rows, x_sorted, wg, wu, wd)


def _combine_kernel(w0_ref, sl_ref, res_ref, tok_ref, gfin_ref, y_hbm,
                    out_ref, ybuf, sems, tmp_scr, acc_scr, *, n_e, n_ch):
    c = pl.program_id(0)
    ch = res_ref.shape[0]
    piece = WINDOW_PIECE
    n_pieces = ybuf.shape[1]

    def piece_copy(cc, slot, e, k):
        row0 = pl.multiple_of(w0_ref[e * n_ch + cc] + k * piece, SEG_ALIGN)
        return pltpu.make_async_copy(
            y_hbm.at[pl.ds(row0, piece)],
            ybuf.at[slot, k, pl.ds(e * piece, piece)], sems.at[slot, e, k])

    def for_each_piece(cc, slot, fn):
        for e in range(n_e):
            fn(piece_copy(cc, slot, e, 0))
            used = sl_ref[e * n_ch + cc]

            def later(k, e=e, used=used):
                @pl.when(used > k * piece)
                def _():
                    fn(piece_copy(cc, slot, e, k))
                    if k + 1 < n_pieces:
                        later(k + 1)
            if n_pieces > 1:
                later(1)

    @pl.when(c == 0)
    def _():
        ybuf[...] = jnp.zeros_like(ybuf)
        for_each_piece(0, 0, lambda cp: cp.start())

    @pl.when(c + 1 < n_ch)
    def _():
        for_each_piece(c + 1, (c + 1) % 2, lambda cp: cp.start())

    slot = c % 2
    for_each_piece(c, slot, lambda cp: cp.wait())

    per_dot = MXU_DIM // piece
    depth = per_dot * piece
    lane = lax.broadcasted_iota(_I32, (1, depth), 1)
    p1b = jnp.broadcast_to(tok_ref[:, 0:1], (ch, depth))
    p2b = jnp.broadcast_to(tok_ref[:, 1:2], (ch, depth))
    g1b = jnp.broadcast_to(tok_ref[:, 2:3], (ch, depth))
    g2b = jnp.broadcast_to(tok_ref[:, 3:4], (ch, depth))

    def scatter_back(m, k):
        tgt = jnp.full((1, depth), -1.0, _F32)
        for i in range(per_dot):
            e = m * per_dot + i
            row = lane - i * piece + k * piece
            mine = (lane >= i * piece) & (lane < (i + 1) * piece) & (row < sl_ref[e * n_ch + c])
            tgt = jnp.where(mine, (row + w0_ref[e * n_ch + c]).astype(_F32), tgt)
        q = (jnp.where(p1b == tgt, g1b, 0.0) + jnp.where(p2b == tgt, g2b, 0.0)).astype(_BF16)
        return _dot(q, ybuf[slot, k, m * depth:(m + 1) * depth, :])

    def any_reaches(m, k):
        hit = sl_ref[(m * per_dot) * n_ch + c] > k * piece
        for i in range(1, per_dot):
            hit = jnp.logical_or(hit, sl_ref[(m * per_dot + i) * n_ch + c] > k * piece)
        return hit

    acc = res_ref[...]
    for m in range(n_e // per_dot):
        acc = acc + scatter_back(m, 0)
    acc_scr[...] = acc
    for m in range(n_e // per_dot):
        def later(k, m=m):
            @pl.when(any_reaches(m, k))
            def _():
                acc_scr[...] += scatter_back(m, k)
                if k + 1 < n_pieces:
                    later(k + 1)
        if n_pieces > 1:
            later(1)
    normed = _rmsnorm(acc_scr[...], gfin_ref[...])
    steps = ch // SUBLANES
    for k in range(tmp_scr.shape[0]):
        tmp_scr[k] = normed[:, k * LANES:(k + 1) * LANES]
    for bb in range(SUBLANES):
        for k in range(tmp_scr.shape[0]):
            out_ref[bb, :, k * LANES:(k + 1) * LANES] = (
                tmp_scr[k, pl.ds(bb, steps, stride=SUBLANES), :])


def _combine(res, tok_tab, gfin, y_sorted, w0_flat, seglen_flat, n_e, bsz, seq):
    t_rows, d = res.shape
    ch = TOKEN_CHUNK
    n_ch = t_rows // ch
    steps = ch // bsz
    return pl.pallas_call(
        functools.partial(_combine_kernel, n_e=n_e, n_ch=n_ch),
        grid_spec=pltpu.PrefetchScalarGridSpec(
            num_scalar_prefetch=2,
            grid=(n_ch,),
            in_specs=[pl.BlockSpec((ch, d), lambda c, w0, sl: (c, 0)),
                      pl.BlockSpec((ch, tok_tab.shape[1]), lambda c, w0, sl: (c, 0)),
                      pl.BlockSpec((1, d), lambda c, w0, sl: (0, 0)),
                      pl.BlockSpec(memory_space=pl.ANY)],
            out_specs=pl.BlockSpec((bsz, steps, d), lambda c, w0, sl: (0, c, 0)),
            scratch_shapes=[pltpu.VMEM((2, _window_pieces(ch), n_e * WINDOW_PIECE, d), _BF16),
                            pltpu.SemaphoreType.DMA((2, n_e, _window_pieces(ch))),
                            pltpu.VMEM((d // LANES, ch, LANES), _F32),
                            pltpu.VMEM((ch, d), _F32)]),
        out_shape=jax.ShapeDtypeStruct((bsz, seq, d), _F32),
        compiler_params=pltpu.CompilerParams(
            dimension_semantics=("arbitrary",), vmem_limit_bytes=VMEM_LIMIT),
        name="moe_combine",
    )(w0_flat, seglen_flat, res, tok_tab, gfin, y_sorted)


def _moe_layer(h, hn, logits_t, wg, wu, wd, expert0, gfin, bsz, seq):
    t_rows, d = h.shape
    n_e = logits_t.shape[0]
    tm = EXPERT_ROW_TILE
    n_ch = t_rows // TOKEN_CHUNK
    max_rows = 2 * t_rows + n_e * (SEG_ALIGN + TOKEN_CHUNK + tm)
    n_blocks = -(-max_rows // tm)
    n_blk_pad = -(-n_blocks // LANES) * LANES

    pos, gates, segtab, blktab = _route(logits_t, n_blk_pad, tm)
    pos_l = jnp.transpose(pos.reshape(2, n_ch, TOKEN_CHUNK), (1, 0, 2))
    w0_flat = segtab[0].reshape(-1)
    seglen_flat = segtab[1].reshape(-1)
    n_valid = blktab[1, :1]
    end_flat = segtab[2].reshape(-1)
    x_sorted = _dispatch(hn, pos_l, w0_flat, seglen_flat, end_flat, n_valid,
                         n_e, n_blocks * tm, tm)
    y_sorted = _ffn_grouped(x_sorted, wg, wu, wd, expert0, blktab[0], n_valid, end_flat,
                            blktab[2], n_ch, tm)
    tok_tab = jnp.transpose(jnp.concatenate([pos.astype(_F32), gates], axis=0))
    return _combine(h, tok_tab, gfin, y_sorted,
                    w0_flat, seglen_flat, n_e, bsz, seq)


def kernel(x, norm_mix_g, w_in, ssm_log_dt, ssm_a_re, ssm_a_im, ssm_b_re, ssm_b_im, ssm_c_re, ssm_c_im, ssm_d, ssm_w_glu, ssm_b_glu, pool_w, pool_scale, w_out, norm_ffn_g, ffn_w_gate, ffn_w_up, ffn_w_down, router_w, moe_w_gate, moe_w_up, moe_w_down, final_norm_g):
    bsz, seq, d = x.shape
    depth, n_heads, n_state, n_grp_ch = ssm_b_re.shape
    t_rows = bsz * seq
    assert bsz == SUBLANES and depth % 2 == 0
    groups = n_heads // HEADS_PER_GROUP

    nh = depth * n_heads
    per_head = lambda w: w.reshape((nh,) + w.shape[2:])
    lam, bw_all, cw_all = _discretize(
        per_head(ssm_log_dt), per_head(ssm_a_re), per_head(ssm_a_im),
        per_head(jnp.swapaxes(ssm_b_re, 2, 3)), per_head(jnp.swapaxes(ssm_b_im, 2, 3)),
        per_head(jnp.swapaxes(ssm_c_re, 2, 3)), per_head(jnp.swapaxes(ssm_c_im, 2, 3)))
    lam = lam.reshape(depth, 2 * groups, -1)
    bw = bw_all.reshape((depth, groups) + bw_all.shape[1:])
    cw = cw_all.reshape((depth, groups) + cw_all.shape[1:])
    vecs = _pack_vectors(norm_mix_g, norm_ffn_g, ssm_d, ssm_b_glu, pool_scale, lam)
    router_wt = jnp.swapaxes(router_w, 1, 2)

    h = x
    row = lambda v: v.reshape(1, -1)
    out = None
    for i in range(depth):
        is_moe = i % 2 == 1
        j = i // 2
        outs = _mix_layer(h, i, vecs, w_in, bw, cw, ssm_w_glu, pool_w, w_out,
                          router_wt if is_moe else None, j)
        if is_moe:
            assert i == depth - 1
            h, hn, logits_t = outs
            n_e = moe_w_gate.shape[1]
            stack = lambda w: w.reshape((-1,) + w.shape[2:])
            out = _moe_layer(h, hn, logits_t, stack(moe_w_gate), stack(moe_w_up),
                             stack(moe_w_down), j * n_e, row(final_norm_g), bsz, seq)
        else:
            h, hn = outs
            h = _ffn(hn, ffn_w_gate, ffn_w_up, ffn_w_down, j, h)
    return out
```

```python
import functools
import math

import jax
import jax.numpy as jnp
from jax import lax
from jax.experimental import pallas as pl
from jax.experimental.pallas import tpu as pltpu

RMS_EPS = 1e-6
POOL_WINDOWS = (2, 4, 8, 16)
A_RE_MAX = -1e-4
GELU_C0 = math.sqrt(2.0 / math.pi)
GELU_C1 = 0.044715

SUBLANES = 8
LANES = 128
MXU_DIM = 256

HEADS_PER_GROUP = 16
TIME_TILE = 64
SCAN_COLS = 512
FFN_ROW_TILE = 512
EXPERT_ROW_TILE = 512
FFN_COL_CHUNK = 256
WEIGHT_UNITS = 8
TOKEN_CHUNK = MXU_DIM
SEG_ALIGN = 2 * SUBLANES
WINDOW_PIECE = 96
COMBINE_CHUNKS = 2
VMEM_LIMIT = 56 * 1024 * 1024
VEC_NORM_MIX, VEC_NORM_FFN, VEC_SKIP_BIAS, VEC_POOL_SCALE, VEC_LAMBDA = range(5)

_F32 = jnp.float32
_BF16 = jnp.bfloat16
_I32 = jnp.int32


def _dot(a, b):
    return jnp.dot(a, b, preferred_element_type=_F32)


def _dot_nt(a, b):
    return lax.dot_general(a, b, (((1,), (1,)), ((), ())), preferred_element_type=_F32)


def _rmsnorm(x, g):
    inv = lax.rsqrt(jnp.mean(x * x, axis=-1, keepdims=True) + RMS_EPS)
    return x * inv * g


def _sigmoid(x):
    return 1.0 / (1.0 + jnp.exp(-x))


def _ceil_to(x, m):
    return jnp.floor((x + (m - 1.0)) * (1.0 / m)) * m


def _window_pieces(chunk):
    return -(-(chunk + SEG_ALIGN - 1) // WINDOW_PIECE)


def _discretize_kernel(log_dt_ref, a_re_ref, a_im_ref, b_re_ref, b_im_ref, c_re_ref, c_im_ref,
                       lam_ref, bw_ref, cw_ref):
    dt = jnp.exp(log_dt_ref[...])
    ar = jnp.minimum(a_re_ref[...], A_RE_MAX)
    ai = a_im_ref[...]
    mag = jnp.exp(ar * dt)
    lam_re = mag * jnp.cos(ai * dt)
    lam_im = mag * jnp.sin(ai * dt)
    den = ar * ar + ai * ai
    nr = lam_re - 1.0
    ni = lam_im
    coef_re = (nr * ar + ni * ai) / den
    coef_im = (ni * ar - nr * ai) / den
    br = b_re_ref[...]
    bi = b_im_ref[...]
    bb_re = (coef_re * br - coef_im * bi).astype(_BF16)
    bb_im = (coef_re * bi + coef_im * br).astype(_BF16)
    c_re = c_re_ref[...].astype(_BF16)
    c_im_neg = (-c_im_ref[...]).astype(_BF16)

    n, g, p = br.shape
    hpg = n // lam_ref.shape[0]
    half = hpg * p
    bw_ref[...] = jnp.zeros_like(bw_ref)
    cw_ref[...] = jnp.zeros_like(cw_ref)
    for h in range(n):
        q, hl = divmod(h, hpg)
        st = slice(hl * p, (hl + 1) * p)
        st_im = slice(half + hl * p, half + (hl + 1) * p)
        ch = slice(hl * g, (hl + 1) * g)
        lam_ref[q, 0:1, st] = lam_re[h]
        lam_ref[q, 1:2, st] = lam_im[h]
        bw_ref[q, ch, st] = bb_re[h]
        bw_ref[q, ch, st_im] = bb_im[h]
        cw_ref[q, st, ch] = c_re[h]
        cw_ref[q, st_im, ch] = c_im_neg[h]


def _discretize(log_dt, a_re, a_im, b_re_t, b_im_t, c_re_t, c_im_t):
    n, g, p = b_re_t.shape
    hpg = HEADS_PER_GROUP
    nq = n // hpg
    full3 = lambda s: pl.BlockSpec(s, lambda: (0, 0, 0))
    return pl.pallas_call(
        _discretize_kernel,
        out_shape=(jax.ShapeDtypeStruct((nq, 2, hpg * p), _F32),
                   jax.ShapeDtypeStruct((nq, hpg * g, 2 * hpg * p), _BF16),
                   jax.ShapeDtypeStruct((nq, 2 * hpg * p, hpg * g), _BF16)),
        in_specs=[full3((n, 1, 1)), full3((n, 1, p)), full3((n, 1, p)),
                  full3((n, g, p)), full3((n, g, p)), full3((n, p, g)), full3((n, p, g))],
        out_specs=(full3((nq, 2, hpg * p)), full3((nq, hpg * g, 2 * hpg * p)),
                   full3((nq, 2 * hpg * p, hpg * g))),
        name="ssm_discretize",
    )(log_dt.reshape(n, 1, 1), a_re.reshape(n, 1, p), a_im.reshape(n, 1, p),
      b_re_t, b_im_t, c_re_t, c_im_t)


def _mix_kernel(*refs, time_tile, with_router, batch_major_in):
    (h_ref, vec_ref, win_f32, bw_ref, cw_ref, wglu_f32, poolw_f32, wout_f32) = refs[:8]
    refs = refs[8:]
    d_ssm = wglu_f32.shape[0]
    gmix_ref = vec_ref.at[VEC_NORM_MIX:VEC_NORM_MIX + 1]
    gffn_ref = vec_ref.at[VEC_NORM_FFN:VEC_NORM_FFN + 1]
    dskip_ref = vec_ref.at[VEC_SKIP_BIAS:VEC_SKIP_BIAS + 1, 0:d_ssm]
    bglu_ref = vec_ref.at[VEC_SKIP_BIAS:VEC_SKIP_BIAS + 1, d_ssm:2 * d_ssm]
    pscale_ref = vec_ref.at[VEC_POOL_SCALE:VEC_POOL_SCALE + 1]
    lam_ref = vec_ref.at[VEC_LAMBDA:]
    rw_ref = logits_ref = xin_scr = None
    if with_router:
        rw_ref, hout_ref, hn_ref, logits_ref = refs[:4]
        refs = refs[4:]
    else:
        hout_ref, hn_ref = refs[:2]
        refs = refs[2:]
    (s_scr, state_scr, ext_scr, mixed_scr, u_save, h_save,
     win_ref, wglu_ref, poolw_ref, wout_ref) = refs[:10]
    if batch_major_in:
        xin_scr = refs[10]

    step = pl.program_id(0)
    rows = time_tile * SUBLANES
    n_groups = bw_ref.shape[0]
    gin = bw_ref.shape[1]
    gstate = bw_ref.shape[2] // 2
    hist = ext_scr.shape[0] - rows

    @pl.when(step == 0)
    def _():
        state_scr[...] = jnp.zeros_like(state_scr)
        ext_scr[0:hist, :] = jnp.zeros((hist, ext_scr.shape[1]), _F32)
        win_ref[...] = win_f32[...].astype(_BF16)
        wglu_ref[...] = wglu_f32[...].astype(_BF16)
        poolw_ref[...] = poolw_f32[...].astype(_BF16)
        wout_ref[...] = wout_f32[...].astype(_BF16)

    if batch_major_in:
        for bb in range(SUBLANES):
            for k in range(xin_scr.shape[0]):
                xin_scr[k, pl.ds(bb, time_tile, stride=SUBLANES), :] = (
                    h_ref[bb, :, k * LANES:(k + 1) * LANES])
        h = jnp.concatenate([xin_scr[k] for k in range(xin_scr.shape[0])], axis=1)
    else:
        h = h_ref[...]
    h_save[...] = h
    hn = _rmsnorm(h, gmix_ref[...]).astype(_BF16)
    proj = _dot(hn, win_ref[...])
    u_ssm = proj[:, :d_ssm]
    u_save[...] = u_ssm
    ext_scr[hist:, :] = proj[:, d_ssm:]

    def drive(q):
        ug = u_ssm[:, q * gin:(q + 1) * gin].astype(_BF16)
        s_scr[q] = _dot(ug, bw_ref[q])

    def scan(q, c):
        re0 = c * SCAN_COLS
        im0 = gstate + c * SCAN_COLS
        lr = jnp.broadcast_to(lam_ref[2 * q:2 * q + 1, re0:re0 + SCAN_COLS],
                              (SUBLANES, SCAN_COLS))
        li = jnp.broadcast_to(lam_ref[2 * q + 1:2 * q + 2, re0:re0 + SCAN_COLS],
                              (SUBLANES, SCAN_COLS))
        sre = state_scr[q, :, re0:re0 + SCAN_COLS]
        sim = state_scr[q, :, im0:im0 + SCAN_COLS]
        for t in range(time_tile):
            r0 = t * SUBLANES
            bre = s_scr[q, r0:r0 + SUBLANES, re0:re0 + SCAN_COLS]
            bim = s_scr[q, r0:r0 + SUBLANES, im0:im0 + SCAN_COLS]
            sre, sim = (lr * sre - li * sim + bre, lr * sim + li * sre + bim)
            s_scr[q, r0:r0 + SUBLANES, re0:re0 + SCAN_COLS] = sre
            s_scr[q, r0:r0 + SUBLANES, im0:im0 + SCAN_COLS] = sim
        state_scr[q, :, re0:re0 + SCAN_COLS] = sre
        state_scr[q, :, im0:im0 + SCAN_COLS] = sim

    def pool():
        n_ext = rows + hist
        pos = step * time_tile + (lax.broadcasted_iota(_I32, (rows, LANES), 0) // SUBLANES)
        for g, w in enumerate(POOL_WINDOWS):
            c0 = g * LANES
            e = ext_scr[:, c0:c0 + LANES]
            acc = e
            n_acc = n_ext
            span = 1
            while span < w:
                sh = span * SUBLANES
                acc = acc[sh:, :] + acc[:n_acc - sh, :]
                n_acc -= sh
                span *= 2
            wsum = acc[n_acc - rows:, :]
            cnt = jnp.minimum(pos + 1, w).astype(_F32)
            pooled = wsum / cnt - e[hist:, :]
            mg = _dot(pooled.astype(_BF16), poolw_ref[g]) * pscale_ref[:, c0:c0 + LANES]
            mixed_scr[:, d_ssm + c0:d_ssm + c0 + LANES] = mg.astype(_BF16)
        ext_scr[0:hist, :] = ext_scr[rows:rows + hist, :]

    def readout(q):
        cols = slice(q * gin, (q + 1) * gin)
        yq = _dot(s_scr[q].astype(_BF16), cw_ref[q])
        yq = yq + dskip_ref[:, cols] * u_save[:, cols]
        hq = 0.5 * yq * (1.0 + jnp.tanh(GELU_C0 * (yq + GELU_C1 * (yq * yq * yq))))
        mixed_scr[:, cols] = hq.astype(_BF16)

    def glu_gate():
        hg = mixed_scr[:, :d_ssm]
        gate = _sigmoid(_dot(hg, wglu_ref[...]) + bglu_ref[...])
        mixed_scr[:, :d_ssm] = (hg.astype(_F32) * gate).astype(_BF16)

    def project_out():
        hout = h_save[...] + _dot(mixed_scr[...], wout_ref[...])
        hout_ref[...] = hout
        hn2 = _rmsnorm(hout, gffn_ref[...])
        hn_hi = hn2.astype(_BF16)
        hn_ref[...] = hn_hi
        if with_router:
            hn_lo = (hn2 - hn_hi.astype(_F32)).astype(_BF16)
            rw = rw_ref[...]
            rw_hi = rw.astype(_BF16)
            rw_lo = (rw - rw_hi.astype(_F32)).astype(_BF16)
            logits_ref[...] = (_dot_nt(rw_hi, hn_hi) + _dot_nt(rw_hi, hn_lo)
                               + _dot_nt(rw_lo, hn_hi))

    for q in range(n_groups):
        drive(q)
    for c in range(gstate // SCAN_COLS):
        for q in range(n_groups):
            scan(q, c)
        if c == 0:
            pool()
    for q in range(n_groups):
        readout(q)
    glu_gate()
    project_out()


def _pack_vectors(norm_mix_g, norm_ffn_g, ssm_d, ssm_b_glu, pool_scale, lam):
    depth, d = norm_mix_g.shape
    row = lambda v: jnp.pad(v, ((0, 0), (0, d - v.shape[1])))[:, None, :]
    return jnp.concatenate(
        [row(norm_mix_g), row(norm_ffn_g), row(jnp.concatenate([ssm_d, ssm_b_glu], axis=1)),
         row(pool_scale), lam], axis=1)


def _mix_layer(h, layer, vecs, w_in, bw, cw, wglu, poolw, wout, router_wt, moe_layer):
    batch_major_in = h.ndim == 3
    d = h.shape[-1]
    t_rows = h.size // d
    rows = TIME_TILE * SUBLANES
    n_steps = t_rows // rows
    d_ssm = wglu.shape[1]
    d_pool = d - d_ssm
    n_groups, _, two_gstate = bw.shape[1:]
    hist = max(POOL_WINDOWS) * SUBLANES
    with_router = router_wt is not None

    def const(a, idx):
        nd = a.ndim
        return pl.BlockSpec((None,) + a.shape[1:], lambda i, nd=nd, idx=idx: (idx,) + (0,) * (nd - 1),
                            pipeline_mode=pl.Buffered(1))

    row_blk = lambda: pl.BlockSpec((rows, d), lambda i: (i, 0))
    ins = [h, vecs, w_in, bw, cw, wglu, poolw, wout]
    h_spec =(pl.BlockSpec((h.shape[0], TIME_TILE, d), lambda i: (0, i, 0))
              if batch_major_in else row_blk())
    in_specs = [h_spec] + [const(a, layer) for a in ins[1:]]
    out_shape = [jax.ShapeDtypeStruct((t_rows, d), _F32),
                 jax.ShapeDtypeStruct((t_rows, d), _BF16)]
    out_specs = [row_blk(), row_blk()]
    if with_router:
        n_e = router_wt.shape[1]
        ins.append(router_wt)
        in_specs.append(const(router_wt, moe_layer))
        out_shape.append(jax.ShapeDtypeStruct((n_e, t_rows), _F32))
        out_specs.append(pl.BlockSpec((n_e, rows), lambda i: (0, i)))
    scratch = [
        pltpu.VMEM((n_groups, rows, two_gstate), _F32),
        pltpu.VMEM((n_groups, SUBLANES, two_gstate), _F32),
        pltpu.VMEM((rows + hist, d_pool), _F32),
        pltpu.VMEM((rows, d_ssm + d_pool), _BF16),
        pltpu.VMEM((rows, d_ssm), _F32),
        pltpu.VMEM((rows, d), _F32),
        pltpu.VMEM(w_in.shape[1:], _BF16),
        pltpu.VMEM(wglu.shape[1:], _BF16),
        pltpu.VMEM(poolw.shape[1:], _BF16),
        pltpu.VMEM(wout.shape[1:], _BF16),
    ]
    if batch_major_in:
        scratch.append(pltpu.VMEM((d // LANES, rows, LANES), _F32))
    return pl.pallas_call(
        functools.partial(_mix_kernel, time_tile=TIME_TILE, with_router=with_router,
                          batch_major_in=batch_major_in),
        grid=(n_steps,),
        in_specs=in_specs,
        out_specs=out_specs,
        out_shape=out_shape,
        scratch_shapes=scratch,
        compiler_params=pltpu.CompilerParams(
            dimension_semantics=("arbitrary",), vmem_limit_bytes=VMEM_LIMIT),
        name="mix_router" if with_router else "mix",
    )(*ins)


def _swiglu(x, wg_ref, wu_ref, wd_ref, a_scr, slot):
    tf = wg_ref.shape[2]
    c0 = 0
    while c0 < tf:
        cw = min(FFN_COL_CHUNK, tf - c0)
        g = _dot(x, wg_ref[slot, :, c0:c0 + cw])
        u = _dot(x, wu_ref[slot, :, c0:c0 + cw])
        a_scr[:, c0:c0 + cw] = (g * _sigmoid(g) * u).astype(_BF16)
        c0 += cw
    return _dot(a_scr[...], wd_ref[slot])


def _ffn_kernel(x_ref, wg_hbm, wu_hbm, wd_hbm, res_ref, out_ref,
                wg_buf, wu_buf, wd_buf, stg_g, stg_u, stg_d, sems, a_scr, *, layer):
    units = WEIGHT_UNITS
    gu_rows = wg_buf.shape[1] // units
    d_rows = wd_buf.shape[1] // units

    @pl.when(pl.program_id(0) == 0)
    def _():
        def copies(k):
            st = k % 2
            return (pltpu.make_async_copy(wg_hbm.at[layer, pl.ds(k * gu_rows, gu_rows)],
                                          stg_g.at[st], sems.at[st, 0]),
                    pltpu.make_async_copy(wu_hbm.at[layer, pl.ds(k * gu_rows, gu_rows)],
                                          stg_u.at[st], sems.at[st, 1]),
                    pltpu.make_async_copy(wd_hbm.at[layer, pl.ds(k * d_rows, d_rows)],
                                          stg_d.at[st], sems.at[st, 2]))

        for cp in copies(0):
            cp.start()
        for k in range(units):
            if k + 1 < units:
                for cp in copies(k + 1):
                    cp.start()
            for cp in copies(k):
                cp.wait()
            st = k % 2
            wg_buf[0, k * gu_rows:(k + 1) * gu_rows, :] = stg_g[st].astype(_BF16)
            wu_buf[0, k * gu_rows:(k + 1) * gu_rows, :] = stg_u[st].astype(_BF16)
            wd_buf[0, k * d_rows:(k + 1) * d_rows, :] = stg_d[st].astype(_BF16)

    out_ref[...] = res_ref[...] + _swiglu(x_ref[...], wg_buf, wu_buf, wd_buf, a_scr, 0)


def _ffn(x, wg, wu, wd, layer, res):
    t_rows, d = x.shape
    f = wg.shape[2]
    tm = FFN_ROW_TILE
    hbm = pl.BlockSpec(memory_space=pl.ANY)
    return pl.pallas_call(
        functools.partial(_ffn_kernel, layer=layer),
        grid=(t_rows // tm,),
        in_specs=[pl.BlockSpec((tm, d), lambda i: (i, 0)), hbm, hbm, hbm,
                  pl.BlockSpec((tm, d), lambda i: (i, 0))],
        out_specs=pl.BlockSpec((tm, d), lambda i: (i, 0)),
        out_shape=jax.ShapeDtypeStruct((t_rows, d), _F32),
        scratch_shapes=[pltpu.VMEM((1, d, f), _BF16),
                        pltpu.VMEM((1, d, f), _BF16),
                        pltpu.VMEM((1, f, d), _BF16),
                        pltpu.VMEM((2, d // WEIGHT_UNITS, f), _F32),
                        pltpu.VMEM((2, d // WEIGHT_UNITS, f), _F32),
                        pltpu.VMEM((2, f // WEIGHT_UNITS, d), _F32),
                        pltpu.SemaphoreType.DMA((2, 3)),
                        pltpu.VMEM((tm, f), _BF16)],
        compiler_params=pltpu.CompilerParams(
            dimension_semantics=("arbitrary",), vmem_limit_bytes=VMEM_LIMIT),
        name="ffn_dense",
    )(x, wg, wu, wd, res)


def _route_kernel(lt_ref, pos_ref, gates_ref, segtab_ref, blktab_ref, *, n_blk, row_tile):
    l = lt_ref[...]
    n_e, t_rows = l.shape
    ch = TOKEN_CHUNK
    n_ch = t_rows // ch
    ie = lax.broadcasted_iota(_I32, l.shape, 0)
    m1 = jnp.max(l, axis=0, keepdims=True)
    i1 = jnp.min(jnp.where(l == m1, ie, n_e), axis=0, keepdims=True)
    l2 = jnp.where(ie == i1, -jnp.inf, l)
    m2 = jnp.max(l2, axis=0, keepdims=True)
    i2 = jnp.min(jnp.where(l2 == m2, ie, n_e), axis=0, keepdims=True)
    e2 = jnp.exp(m2 - m1)
    den = 1.0 + e2
    gates_ref[0:1, :] = 1.0 / den
    gates_ref[1:2, :] = e2 / den
    sel1 = ie == i1
    sel2 = ie == i2
    chosen = jnp.where(sel1 | sel2, 1.0, 0.0).astype(_BF16)

    chunk_of_t = lax.broadcasted_iota(_I32, (n_ch, t_rows), 1) // ch
    in_chunk = jnp.where(chunk_of_t == lax.broadcasted_iota(_I32, (n_ch, t_rows), 0),
                         1.0, 0.0).astype(_BF16)
    n_ec = _dot_nt(chosen, in_chunk)
    n_ce = _dot_nt(in_chunk, chosen)
    cc_r = lax.broadcasted_iota(_I32, (n_ch, n_ch), 0)
    cc_c = lax.broadcasted_iota(_I32, (n_ch, n_ch), 1)
    before = jnp.where(cc_r < cc_c, 1.0, 0.0).astype(_BF16)
    after = jnp.where(cc_c < cc_r, 1.0, 0.0).astype(_BF16)
    segoff_ec = _dot(n_ec.astype(_BF16), before)
    segoff_ce = _dot(after, n_ce.astype(_BF16))
    def window(segoff, n):
        start = jnp.floor(segoff * (1.0 / SEG_ALIGN)) * SEG_ALIGN
        used = segoff - start + n
        return start, used, start + _ceil_to(jnp.maximum(used, 1.0), float(WINDOW_PIECE))

    wstart_ec, used_ec, wend_ec = window(segoff_ec, n_ec)
    _, _, wend_ce = window(segoff_ce, n_ce)
    tot_col = _ceil_to(jnp.max(wend_ec, axis=1, keepdims=True), float(row_tile))
    tot_row = _ceil_to(jnp.max(wend_ce, axis=0, keepdims=True), float(row_tile))
    ee_r = lax.broadcasted_iota(_I32, (n_e, n_e), 0)
    ee_c = lax.broadcasted_iota(_I32, (n_e, n_e), 1)
    base_col = jnp.sum(jnp.where(ee_c < ee_r, tot_row, 0.0), axis=1, keepdims=True)
    w0_ec = base_col + segoff_ec
    end_col = base_col + tot_col
    segtab_ref[0] = (base_col + wstart_ec).astype(_I32)
    segtab_ref[1] = used_ec.astype(_I32)
    segtab_ref[2] = jnp.broadcast_to(end_col, (n_e, n_ch)).astype(_I32)

    tt_r = lax.broadcasted_iota(_I32, (ch, ch), 0)
    tt_c = lax.broadcasted_iota(_I32, (ch, ch), 1)
    earlier = jnp.where(tt_r < tt_c, 1.0, 0.0).astype(_BF16)
    for c in range(n_ch):
        cols = slice(c * ch, (c + 1) * ch)
        rank = _dot(chosen[:, cols], earlier)
        offs = w0_ec[:, c:c + 1] + rank
        p1 = jnp.sum(jnp.where(sel1[:, cols], offs, 0.0), axis=0, keepdims=True)
        p2 = jnp.sum(jnp.where(sel2[:, cols], offs, 0.0), axis=0, keepdims=True)
        pos_ref[0:1, cols] = p1.astype(_I32)
        pos_ref[1:2, cols] = p2.astype(_I32)

    start = (lax.broadcasted_iota(_I32, (1, n_blk), 1) * row_tile).astype(_F32)
    owner = jnp.sum(jnp.where(end_col <= start, 1.0, 0.0), axis=0, keepdims=True)
    n_valid = jnp.sum(tot_col, axis=0, keepdims=True) * (1.0 / row_tile)
    blktab_ref[0:1, :] = jnp.minimum(owner, n_e - 1.0).astype(_I32)
    blktab_ref[1:2, :] = jnp.broadcast_to(n_valid, (1, n_blk)).astype(_I32)
    real_end_col = base_col + jnp.sum(n_ec, axis=1, keepdims=True)
    mine = lax.broadcasted_iota(_I32, (n_e, n_blk), 0).astype(_F32) == owner
    real = jnp.clip(real_end_col - start, 0.0, float(row_tile))
    blktab_ref[2:3, :] = jnp.sum(jnp.where(mine, real, 0.0), axis=0, keepdims=True).astype(_I32)


def _route(logits_t, n_blk, row_tile):
    n_e, t_rows = logits_t.shape
    n_ch = t_rows // TOKEN_CHUNK
    full = lambda s: pl.BlockSpec(s, lambda: (0,) * len(s))
    return pl.pallas_call(
        functools.partial(_route_kernel, n_blk=n_blk, row_tile=row_tile),
        in_specs=[full((n_e, t_rows))],
        out_specs=(full((2, t_rows)), full((2, t_rows)), full((3, n_e, n_ch)), full((3, n_blk))),
        out_shape=(jax.ShapeDtypeStruct((2, t_rows), _I32),
                   jax.ShapeDtypeStruct((2, t_rows), _F32),
                   jax.ShapeDtypeStruct((3, n_e, n_ch), _I32),
                   jax.ShapeDtypeStruct((3, n_blk), _I32)),
        compiler_params=pltpu.CompilerParams(vmem_limit_bytes=VMEM_LIMIT),
        name="route_sort",
    )(logits_t)


def _dispatch_kernel(w0_ref, sl_ref, end_ref, nv_ref, hn_ref, posl_ref, x_hbm,
                     stage, sems, zero_scr, zsems, tsems, carry, *, n_e, n_ch, row_tile, min_blocks):
    c = pl.program_id(0)
    ch = hn_ref.shape[0]
    piece = WINDOW_PIECE
    n_pieces = stage.shape[1]
    slot = c % 2
    n_blocks = x_hbm.shape[0] // row_tile

    def for_each_unused_block(fn):
        for b in range(min_blocks, n_blocks):
            @pl.when(b >= nv_ref[0])
            def _(b=b):
                fn(pltpu.make_async_copy(zero_scr, x_hbm.at[pl.ds(b * row_tile, row_tile)],
                                         tsems.at[b - min_blocks]))

    def piece_copy(cc, sl, e, k):
        row0 = pl.multiple_of(w0_ref[e * n_ch + cc] + k * piece, SEG_ALIGN)
        return pltpu.make_async_copy(
            stage.at[sl, k, pl.ds(e * piece, piece)],
            x_hbm.at[pl.ds(row0, piece)], sems.at[sl, e, k])

    def for_each_piece(cc, sl, fn):
        for e in range(n_e):
            fn(piece_copy(cc, sl, e, 0))
            used = sl_ref[e * n_ch + cc]

            def later(k, e=e, used=used):
                @pl.when(used > k * piece)
                def _():
                    fn(piece_copy(cc, sl, e, k))
                    if k + 1 < n_pieces:
                        later(k + 1)
            if n_pieces > 1:
                later(1)

    @pl.when(c == 0)
    def _():
        zero_scr[...] = jnp.zeros_like(zero_scr)
        carry[...] = jnp.zeros_like(carry)
        fills = [pltpu.make_async_copy(
            zero_scr,
            x_hbm.at[pl.ds(pl.multiple_of(end_ref[e * n_ch] - row_tile, SEG_ALIGN), row_tile)],
            zsems.at[e]) for e in range(n_e)]
        for cp in fills:
            cp.start()
        for_each_unused_block(lambda cp: cp.start())
        for cp in fills:
            cp.wait()

    pp = posl_ref[0]
    hn = hn_ref[...]

    def gather_piece(k):
        row_j = lax.broadcasted_iota(_I32, (piece, ch), 0) + k * piece
        parts = []
        for e in range(n_e):
            row_id = row_j + w0_ref[e * n_ch + c]
            hit = (pp[0:1, :] == row_id) | (pp[1:2, :] == row_id)
            parts.append(jnp.where(hit, 1.0, 0.0).astype(_BF16))
        onehot = jnp.concatenate(parts, axis=0)
        stage[slot, k] = _dot(onehot, hn).astype(_BF16)

    gather_piece(0)
    for e in range(n_e):
        head = slice(e * piece, e * piece + SEG_ALIGN)
        stage[slot, 0, head, :] = stage[slot, 0, head, :] + carry[e]
    longest = sl_ref[c]
    for e in range(1, n_e):
        longest = jnp.maximum(longest, sl_ref[e * n_ch + c])

    def later(k):
        @pl.when(longest > k * piece)
        def _():
            gather_piece(k)
            if k + 1 < n_pieces:
                later(k + 1)
    if n_pieces > 1:
        later(1)
    for e in range(n_e):
        used = sl_ref[e * n_ch + c]
        gathered = ((jnp.maximum(used, 1) + piece - 1) // piece) * piece
        off = (used // SEG_ALIGN) * SEG_ALIGN
        src = jnp.minimum(off, gathered - SEG_ALIGN)
        grp = stage[slot, src // piece,
                    pl.ds(pl.multiple_of(e * piece + src % piece, SEG_ALIGN), SEG_ALIGN), :]
        carry[e] = jnp.where(off < gathered, grp, jnp.zeros_like(grp))

    @pl.when(c > 0)
    def _():
        for_each_piece(c - 1, 1 - slot, lambda cp: cp.wait())

    for_each_piece(c, slot, lambda cp: cp.start())

    @pl.when(c == n_ch - 1)
    def _():
        for_each_piece(c, slot, lambda cp: cp.wait())
        for_each_unused_block(lambda cp: cp.wait())


def _dispatch(hn, pos_l, w0_flat, seglen_flat, end_flat, n_valid, n_e, n_rows, row_tile):
    t_rows, d = hn.shape
    ch = TOKEN_CHUNK
    n_ch = t_rows // ch
    n_pieces = _window_pieces(ch)
    min_blocks = 2 * t_rows // row_tile
    return pl.pallas_call(
        functools.partial(_dispatch_kernel, n_e=n_e, n_ch=n_ch, row_tile=row_tile,
                          min_blocks=min_blocks),
        grid_spec=pltpu.PrefetchScalarGridSpec(
            num_scalar_prefetch=4,
            grid=(n_ch,),
            in_specs=[pl.BlockSpec((ch, d), lambda c, *_: (c, 0)),
                      pl.BlockSpec((1, 2, ch), lambda c, *_: (c, 0, 0))],
            out_specs=pl.BlockSpec(memory_space=pl.ANY),
            scratch_shapes=[pltpu.VMEM((2, n_pieces, n_e * WINDOW_PIECE, d), _BF16),
                            pltpu.SemaphoreType.DMA((2, n_e, n_pieces)),
                            pltpu.VMEM((row_tile, d), _BF16),
                            pltpu.SemaphoreType.DMA((n_e,)),
                            pltpu.SemaphoreType.DMA((n_rows // row_tile - min_blocks,)),
                            pltpu.VMEM((n_e, SEG_ALIGN, d), _BF16)]),
        out_shape=jax.ShapeDtypeStruct((n_rows, d), _BF16),
        compiler_params=pltpu.CompilerParams(
            dimension_semantics=("arbitrary",), vmem_limit_bytes=VMEM_LIMIT),
        name="moe_dispatch",
    )(w0_flat, seglen_flat, end_flat, n_valid, hn, pos_l)


def _ffn_grouped_kernel(be_ref, nv_ref, end_ref, bu_ref, x_ref, wg_hbm, wu_hbm, wd_hbm, y_ref,
                        wg_buf, wu_buf, wd_buf, stg_g, stg_u, stg_d, sems, state, a_scr,
                        *, n_ch, expert0):
    b = pl.program_id(0)
    n_valid = nv_ref[0]
    valid = b < n_valid
    row_tile = x_ref.shape[0]
    units = WEIGHT_UNITS
    gu_rows = wg_buf.shape[1] // units
    d_rows = wd_buf.shape[1] // units
    ST_SLOT, ST_RESIDENT, ST_NEXT, ST_DONE, ST_STAGE = range(5)

    def unit_copies(e, k, st):
        g0 = pl.multiple_of(k * gu_rows, gu_rows)
        d0 = pl.multiple_of(k * d_rows, SEG_ALIGN)
        return (pltpu.make_async_copy(wg_hbm.at[expert0 + e, pl.ds(g0, gu_rows)], stg_g.at[st],
                                      sems.at[st, 0]),
                pltpu.make_async_copy(wu_hbm.at[expert0 + e, pl.ds(g0, gu_rows)], stg_u.at[st],
                                      sems.at[st, 1]),
                pltpu.make_async_copy(wd_hbm.at[expert0 + e, pl.ds(d0, d_rows)], stg_d.at[st],
                                      sems.at[st, 2]))

    def start(e, k, st):
        for cp in unit_copies(e, k, st):
            cp.start()

    def wait(e, k, st):
        for cp in unit_copies(e, k, st):
            cp.wait()

    def cast_unit(slot, k, st):
        g0 = pl.multiple_of(k * gu_rows, gu_rows)
        d0 = pl.multiple_of(k * d_rows, SEG_ALIGN)
        wg_buf[slot, pl.ds(g0, gu_rows), :] = stg_g[st].astype(_BF16)
        wu_buf[slot, pl.ds(g0, gu_rows), :] = stg_u[st].astype(_BF16)
        wd_buf[slot, pl.ds(d0, d_rows), :] = stg_d[st].astype(_BF16)

    def expert_after(e):
        blk = end_ref[e * n_ch] // row_tile
        return jnp.where(blk < n_valid, be_ref[jnp.minimum(blk, n_valid - 1)], -1)

    def become_resident(slot, e):
        state[ST_SLOT] = slot
        state[ST_RESIDENT] = e
        nxt = expert_after(e)
        state[ST_NEXT] = nxt
        state[ST_DONE] = 0

        @pl.when(nxt >= 0)
        def _():
            start(nxt, 0, 0)
        state[ST_STAGE] = 0

    e_b = be_ref[jnp.minimum(b, n_valid - 1)]

    @pl.when(b == 0)
    def _():
        state[ST_SLOT] = 1
        state[ST_RESIDENT] = -1
        state[ST_NEXT] = e_b
        state[ST_DONE] = 0
        state[ST_STAGE] = 0
        start(e_b, 0, 0)

    s_slot = state[ST_SLOT]
    s_next = state[ST_NEXT]
    s_done = state[ST_DONE]
    s_stage = state[ST_STAGE]
    change = valid & (e_b != state[ST_RESIDENT])
    steady = valid & jnp.logical_not(change) & (s_next >= 0) & (s_done < units)

    @pl.when(change)
    def _():
        other = 1 - s_slot

        @pl.when(s_done < units)
        def _():
            wait(e_b, s_done, s_stage)
            cast_unit(other, s_done, s_stage)

            def body(k, carry):
                start(e_b, k, 0)
                wait(e_b, k, 0)
                cast_unit(other, k, 0)
                return carry
            lax.fori_loop(s_done + 1, units, body, 0)
        become_resident(other, e_b)

    half = row_tile // 2
    rows_used = jnp.where(valid, bu_ref[jnp.minimum(b, n_valid - 1)], 0)
    whole = rows_used > half
    part = (rows_used > 0) & jnp.logical_not(whole)

    def run_block(slot):
        y_ref[...] = _swiglu(x_ref[...], wg_buf, wu_buf, wd_buf, a_scr, slot).astype(_BF16)

    @pl.when(steady)
    def _():
        wait(s_next, s_done, s_stage)

        @pl.when(s_done + 1 < units)
        def _():
            start(s_next, s_done + 1, 1 - s_stage)
        state[ST_DONE] = s_done + 1
        state[ST_STAGE] = 1 - s_stage

    @pl.when(steady & whole)
    def _():
        cast_unit(1 - s_slot, s_done, s_stage)
        run_block(s_slot)

    @pl.when(steady & jnp.logical_not(whole))
    def _():
        cast_unit(1 - s_slot, s_done, s_stage)

    @pl.when(jnp.logical_not(steady) & whole)
    def _():
        run_block(state[ST_SLOT])

    @pl.when(part)
    def _():
        slot = state[ST_SLOT]
        y_ref[0:half, :] = _swiglu(x_ref[0:half, :], wg_buf, wu_buf, wd_buf,
                                   a_scr.at[0:half], slot).astype(_BF16)
        y_ref[half:, :] = jnp.zeros((row_tile - half, y_ref.shape[1]), _BF16)

    @pl.when(rows_used == 0)
    def _():
        y_ref[...] = jnp.zeros_like(y_ref)


def _ffn_grouped(x_sorted, wg, wu, wd, expert0, block_expert, n_valid, end_flat, block_rows,
                 n_ch, row_tile):
    rows, d = x_sorted.shape
    f = wg.shape[2]
    n_blocks = rows // row_tile
    hbm = pl.BlockSpec(memory_space=pl.ANY)
    return pl.pallas_call(
        functools.partial(_ffn_grouped_kernel, n_ch=n_ch, expert0=expert0),
        grid_spec=pltpu.PrefetchScalarGridSpec(
            num_scalar_prefetch=4,
            grid=(n_blocks,),
            in_specs=[pl.BlockSpec((row_tile, d),
                                   lambda b, be, nv, en, bu: (jnp.minimum(b, nv[0] - 1), 0)),
                      hbm, hbm, hbm],
            out_specs=pl.BlockSpec((row_tile, d), lambda b, be, nv, en, bu: (b, 0)),
            scratch_shapes=[pltpu.VMEM((2, d, f), _BF16),
                            pltpu.VMEM((2, d, f), _BF16),
                            pltpu.VMEM((2, f, d), _BF16),
                            pltpu.VMEM((2, d // WEIGHT_UNITS, f), _F32),
                            pltpu.VMEM((2, d // WEIGHT_UNITS, f), _F32),
                            pltpu.VMEM((2, f // WEIGHT_UNITS, d), _F32),
                            pltpu.SemaphoreType.DMA((2, 3)),
                            pltpu.SMEM((5,), _I32),
                            pltpu.VMEM((row_tile, f), _BF16)]),
        out_shape=jax.ShapeDtypeStruct((rows, d), _BF16),
        compiler_params=pltpu.CompilerParams(
            dimension_semantics=("arbitrary",), vmem_limit_bytes=VMEM_LIMIT),
        name="moe_experts",
    )(block_expert, n_valid, end_flat, block_rows, x_sorted, wg, wu, wd)


def _combine_kernel(w0_ref, sl_ref, res_ref, tok_ref, gfin_ref, y_hbm,
                    out_ref, ybuf, sems, tmp_scr, acc_scr, *, n_e, n_ch):
    step = pl.program_id(0)
    per_step = ybuf.shape[1]
    n_steps = n_ch // per_step
    ch = res_ref.shape[0] // per_step
    piece = WINDOW_PIECE
    n_pieces = ybuf.shape[2]

    def piece_copy(st, slot, j, e, k):
        row0 = pl.multiple_of(w0_ref[e * n_ch + st * per_step + j] + k * piece, SEG_ALIGN)
        return pltpu.make_async_copy(
            y_hbm.at[pl.ds(row0, piece)],
            ybuf.at[slot, j, k, pl.ds(e * piece, piece)], sems.at[slot, j, e, k])

    def for_each_piece(st, slot, fn):
        for j in range(per_step):
            for e in range(n_e):
                fn(piece_copy(st, slot, j, e, 0))
                used = sl_ref[e * n_ch + st * per_step + j]

                def later(k, j=j, e=e, used=used):
                    @pl.when(used > k * piece)
                    def _():
                        fn(piece_copy(st, slot, j, e, k))
                        if k + 1 < n_pieces:
                            later(k + 1)
                if n_pieces > 1:
                    later(1)

    @pl.when(step == 0)
    def _():
        ybuf[...] = jnp.zeros_like(ybuf)
        for_each_piece(0, 0, lambda cp: cp.start())

    @pl.when(step + 1 < n_steps)
    def _():
        for_each_piece(step + 1, (step + 1) % 2, lambda cp: cp.start())

    slot = step % 2
    for_each_piece(step, slot, lambda cp: cp.wait())

    per_dot = MXU_DIM // piece
    depth = per_dot * piece
    lane = lax.broadcasted_iota(_I32, (1, depth), 1)
    steps = ch // SUBLANES

    def combine_chunk(j):
        c = step * per_step + j
        rows = slice(j * ch, (j + 1) * ch)
        p1b = jnp.broadcast_to(tok_ref[rows, 0:1], (ch, depth))
        p2b = jnp.broadcast_to(tok_ref[rows, 1:2], (ch, depth))
        g1b = jnp.broadcast_to(tok_ref[rows, 2:3], (ch, depth))
        g2b = jnp.broadcast_to(tok_ref[rows, 3:4], (ch, depth))

        def scatter_back(m, k):
            tgt = jnp.full((1, depth), -1.0, _F32)
            for i in range(per_dot):
                e = m * per_dot + i
                row = lane - i * piece + k * piece
                mine = ((lane >= i * piece) & (lane < (i + 1) * piece)
                        & (row < sl_ref[e * n_ch + c]))
                tgt = jnp.where(mine, (row + w0_ref[e * n_ch + c]).astype(_F32), tgt)
            q = (jnp.where(p1b == tgt, g1b, 0.0) + jnp.where(p2b == tgt, g2b, 0.0)).astype(_BF16)
            return _dot(q, ybuf[slot, j, k, m * depth:(m + 1) * depth, :])

        def any_reaches(m, k):
            hit = sl_ref[(m * per_dot) * n_ch + c] > k * piece
            for i in range(1, per_dot):
                hit = jnp.logical_or(hit, sl_ref[(m * per_dot + i) * n_ch + c] > k * piece)
            return hit

        acc = res_ref[rows, :]
        for m in range(n_e // per_dot):
            acc = acc + scatter_back(m, 0)
        acc_scr[j] = acc
        for m in range(n_e // per_dot):
            def later(k, m=m):
                @pl.when(any_reaches(m, k))
                def _():
                    acc_scr[j] += scatter_back(m, k)
                    if k + 1 < n_pieces:
                        later(k + 1)
            if n_pieces > 1:
                later(1)
        normed = _rmsnorm(acc_scr[j], gfin_ref[...])
        for k in range(tmp_scr.shape[1]):
            tmp_scr[j, k] = normed[:, k * LANES:(k + 1) * LANES]
        for bb in range(SUBLANES):
            for k in range(tmp_scr.shape[1]):
                out_ref[bb, j * steps:(j + 1) * steps, k * LANES:(k + 1) * LANES] = (
                    tmp_scr[j, k, pl.ds(bb, steps, stride=SUBLANES), :])

    for j in range(per_step):
        combine_chunk(j)


def _combine(res, tok_tab, gfin, y_sorted, w0_flat, seglen_flat, n_e, bsz, seq):
    t_rows, d = res.shape
    ch = TOKEN_CHUNK
    n_ch = t_rows // ch
    per_step = COMBINE_CHUNKS
    rows = per_step * ch
    return pl.pallas_call(
        functools.partial(_combine_kernel, n_e=n_e, n_ch=n_ch),
        grid_spec=pltpu.PrefetchScalarGridSpec(
            num_scalar_prefetch=2,
            grid=(n_ch // per_step,),
            in_specs=[pl.BlockSpec((rows, d), lambda s, w0, sl: (s, 0)),
                      pl.BlockSpec((rows, tok_tab.shape[1]), lambda s, w0, sl: (s, 0)),
                      pl.BlockSpec((1, d), lambda s, w0, sl: (0, 0)),
                      pl.BlockSpec(memory_space=pl.ANY)],
            out_specs=pl.BlockSpec((bsz, rows // bsz, d), lambda s, w0, sl: (0, s, 0)),
            scratch_shapes=[pltpu.VMEM((2, per_step, _window_pieces(ch), n_e * WINDOW_PIECE, d),
                                       _BF16),
                            pltpu.SemaphoreType.DMA((2, per_step, n_e, _window_pieces(ch))),
                            pltpu.VMEM((per_step, d // LANES, ch, LANES), _F32),
                            pltpu.VMEM((per_step, ch, d), _F32)]),
        out_shape=jax.ShapeDtypeStruct((bsz, seq, d), _F32),
        compiler_params=pltpu.CompilerParams(
            dimension_semantics=("arbitrary",), vmem_limit_bytes=VMEM_LIMIT),
        name="moe_combine",
    )(w0_flat, seglen_flat, res, tok_tab, gfin, y_sorted)


def _moe_layer(h, hn, logits_t, wg, wu, wd, expert0, gfin, bsz, seq):
    t_rows, d = h.shape
    n_e = logits_t.shape[0]
    tm = EXPERT_ROW_TILE
    n_ch = t_rows // TOKEN_CHUNK
    max_rows = 2 * t_rows + n_e * (SEG_ALIGN + TOKEN_CHUNK + tm)
    n_blocks = -(-max_rows // tm)
    n_blk_pad = -(-n_blocks // LANES) * LANES

    pos, gates, segtab, blktab = _route(logits_t, n_blk_pad, tm)
    pos_l = jnp.transpose(pos.reshape(2, n_ch, TOKEN_CHUNK), (1, 0, 2))
    w0_flat = segtab[0].reshape(-1)
    seglen_flat = segtab[1].reshape(-1)
    n_valid = blktab[1, :1]
    end_flat = segtab[2].reshape(-1)
    x_sorted = _dispatch(hn, pos_l, w0_flat, seglen_flat, end_flat, n_valid,
                         n_e, n_blocks * tm, tm)
    y_sorted = _ffn_grouped(x_sorted, wg, wu, wd, expert0, blktab[0], n_valid, end_flat,
                            blktab[2], n_ch, tm)
    tok_tab = jnp.transpose(jnp.concatenate([pos.astype(_F32), gates], axis=0))
    return _combine(h, tok_tab, gfin, y_sorted,
                    w0_flat, seglen_flat, n_e, bsz, seq)


def kernel(x, norm_mix_g, w_in, ssm_log_dt, ssm_a_re, ssm_a_im, ssm_b_re, ssm_b_im, ssm_c_re, ssm_c_im, ssm_d, ssm_w_glu, ssm_b_glu, pool_w, pool_scale, w_out, norm_ffn_g, ffn_w_gate, ffn_w_up, ffn_w_down, router_w, moe_w_gate, moe_w_up, moe_w_down, final_norm_g):
    bsz, seq, d = x.shape
    depth, n_heads, n_state, n_grp_ch = ssm_b_re.shape
    t_rows = bsz * seq
    assert bsz == SUBLANES and depth % 2 == 0
    groups = n_heads // HEADS_PER_GROUP

    nh = depth * n_heads
    per_head = lambda w: w.reshape((nh,) + w.shape[2:])
    lam, bw_all, cw_all = _discretize(
        per_head(ssm_log_dt), per_head(ssm_a_re), per_head(ssm_a_im),
        per_head(jnp.swapaxes(ssm_b_re, 2, 3)), per_head(jnp.swapaxes(ssm_b_im, 2, 3)),
        per_head(jnp.swapaxes(ssm_c_re, 2, 3)), per_head(jnp.swapaxes(ssm_c_im, 2, 3)))
    lam = lam.reshape(depth, 2 * groups, -1)
    bw = bw_all.reshape((depth, groups) + bw_all.shape[1:])
    cw = cw_all.reshape((depth, groups) + cw_all.shape[1:])
    vecs = _pack_vectors(norm_mix_g, norm_ffn_g, ssm_d, ssm_b_glu, pool_scale, lam)
    router_wt = jnp.swapaxes(router_w, 1, 2)

    h = x
    row = lambda v: v.reshape(1, -1)
    out = None
    for i in range(depth):
        is_moe = i % 2 == 1
        j = i // 2
        outs = _mix_layer(h, i, vecs, w_in, bw, cw, ssm_w_glu, pool_w, w_out,
                          router_wt if is_moe else None, j)
        if is_moe:
            assert i == depth - 1
            h, hn, logits_t = outs
            n_e = moe_w_gate.shape[1]
            stack = lambda w: w.reshape((-1,) + w.shape[2:])
            out = _moe_layer(h, hn, logits_t, stack(moe_w_gate), stack(moe_w_up),
                             stack(moe_w_down), j * n_e, row(final_norm_g), bsz, seq)
        else:
            h, hn = outs
            h = _ffn(hn, ffn_w_gate, ffn_w_up, ffn_w_down, j, h)
    return out
```

```python
import functools
import math

import jax
import jax.numpy as jnp
from jax import lax
from jax.experimental import pallas as pl
from jax.experimental.pallas import tpu as pltpu

RMS_EPS = 1e-6
POOL_WINDOWS = (2, 4, 8, 16)
A_RE_MAX = -1e-4
GELU_C0 = math.sqrt(2.0 / math.pi)
GELU_C1 = 0.044715

SUBLANES = 8
LANES = 128
MXU_DIM = 256

HEADS_PER_GROUP = 16
TIME_TILE = 64
SCAN_COLS = 512
FFN_ROW_TILE = 512
EXPERT_ROW_TILE = 512
FFN_COL_CHUNK = 256
WEIGHT_UNITS = 8
TOKEN_CHUNK = MXU_DIM
SEG_ALIGN = 2 * SUBLANES
WINDOW_PIECE = 96
CHUNKS_PER_STEP = 2
VMEM_LIMIT = 56 * 1024 * 1024
VEC_NORM_MIX, VEC_NORM_FFN, VEC_SKIP_BIAS, VEC_POOL_SCALE, VEC_LAMBDA = range(5)

_F32 = jnp.float32
_BF16 = jnp.bfloat16
_I32 = jnp.int32


def _dot(a, b):
    return jnp.dot(a, b, preferred_element_type=_F32)


def _dot_nt(a, b):
    return lax.dot_general(a, b, (((1,), (1,)), ((), ())), preferred_element_type=_F32)


def _rmsnorm(x, g):
    inv = lax.rsqrt(jnp.mean(x * x, axis=-1, keepdims=True) + RMS_EPS)
    return x * inv * g


def _sigmoid(x):
    return 1.0 / (1.0 + jnp.exp(-x))


def _ceil_to(x, m):
    return jnp.floor((x + (m - 1.0)) * (1.0 / m)) * m


def _window_pieces(chunk):
    return -(-(chunk + SEG_ALIGN - 1) // WINDOW_PIECE)


def _discretize_kernel(log_dt_ref, a_re_ref, a_im_ref, b_re_ref, b_im_ref, c_re_ref, c_im_ref,
                       lam_ref, bw_ref, cw_ref):
    dt = jnp.exp(log_dt_ref[...])
    ar = jnp.minimum(a_re_ref[...], A_RE_MAX)
    ai = a_im_ref[...]
    mag = jnp.exp(ar * dt)
    lam_re = mag * jnp.cos(ai * dt)
    lam_im = mag * jnp.sin(ai * dt)
    den = ar * ar + ai * ai
    nr = lam_re - 1.0
    ni = lam_im
    coef_re = (nr * ar + ni * ai) / den
    coef_im = (ni * ar - nr * ai) / den
    br = b_re_ref[...]
    bi = b_im_ref[...]
    bb_re = (coef_re * br - coef_im * bi).astype(_BF16)
    bb_im = (coef_re * bi + coef_im * br).astype(_BF16)
    c_re = c_re_ref[...].astype(_BF16)
    c_im_neg = (-c_im_ref[...]).astype(_BF16)

    n, g, p = br.shape
    hpg = n // lam_ref.shape[0]
    half = hpg * p
    bw_ref[...] = jnp.zeros_like(bw_ref)
    cw_ref[...] = jnp.zeros_like(cw_ref)
    for h in range(n):
        q, hl = divmod(h, hpg)
        st = slice(hl * p, (hl + 1) * p)
        st_im = slice(half + hl * p, half + (hl + 1) * p)
        ch = slice(hl * g, (hl + 1) * g)
        lam_ref[q, 0:1, st] = lam_re[h]
        lam_ref[q, 1:2, st] = lam_im[h]
        bw_ref[q, ch, st] = bb_re[h]
        bw_ref[q, ch, st_im] = bb_im[h]
        cw_ref[q, st, ch] = c_re[h]
        cw_ref[q, st_im, ch] = c_im_neg[h]


def _discretize(log_dt, a_re, a_im, b_re_t, b_im_t, c_re_t, c_im_t):
    n, g, p = b_re_t.shape
    hpg = HEADS_PER_GROUP
    nq = n // hpg
    full3 = lambda s: pl.BlockSpec(s, lambda: (0, 0, 0))
    return pl.pallas_call(
        _discretize_kernel,
        out_shape=(jax.ShapeDtypeStruct((nq, 2, hpg * p), _F32),
                   jax.ShapeDtypeStruct((nq, hpg * g, 2 * hpg * p), _BF16),
                   jax.ShapeDtypeStruct((nq, 2 * hpg * p, hpg * g), _BF16)),
        in_specs=[full3((n, 1, 1)), full3((n, 1, p)), full3((n, 1, p)),
                  full3((n, g, p)), full3((n, g, p)), full3((n, p, g)), full3((n, p, g))],
        out_specs=(full3((nq, 2, hpg * p)), full3((nq, hpg * g, 2 * hpg * p)),
                   full3((nq, 2 * hpg * p, hpg * g))),
        name="ssm_discretize",
    )(log_dt.reshape(n, 1, 1), a_re.reshape(n, 1, p), a_im.reshape(n, 1, p),
      b_re_t, b_im_t, c_re_t, c_im_t)


def _mix_kernel(*refs, time_tile, with_router, batch_major_in):
    (h_ref, vec_ref, win_f32, bw_ref, cw_ref, wglu_f32, poolw_f32, wout_f32) = refs[:8]
    refs = refs[8:]
    d_ssm = wglu_f32.shape[0]
    gmix_ref = vec_ref.at[VEC_NORM_MIX:VEC_NORM_MIX + 1]
    gffn_ref = vec_ref.at[VEC_NORM_FFN:VEC_NORM_FFN + 1]
    dskip_ref = vec_ref.at[VEC_SKIP_BIAS:VEC_SKIP_BIAS + 1, 0:d_ssm]
    bglu_ref = vec_ref.at[VEC_SKIP_BIAS:VEC_SKIP_BIAS + 1, d_ssm:2 * d_ssm]
    pscale_ref = vec_ref.at[VEC_POOL_SCALE:VEC_POOL_SCALE + 1]
    lam_ref = vec_ref.at[VEC_LAMBDA:]
    rw_ref = logits_ref = xin_scr = None
    if with_router:
        rw_ref, hout_ref, hn_ref, logits_ref = refs[:4]
        refs = refs[4:]
    else:
        hout_ref, hn_ref = refs[:2]
        refs = refs[2:]
    (s_scr, state_scr, ext_scr, mixed_scr, u_save, h_save,
     win_ref, wglu_ref, poolw_ref, wout_ref) = refs[:10]
    if batch_major_in:
        xin_scr = refs[10]

    step = pl.program_id(0)
    rows = time_tile * SUBLANES
    n_groups = bw_ref.shape[0]
    gin = bw_ref.shape[1]
    gstate = bw_ref.shape[2] // 2
    hist = ext_scr.shape[0] - rows

    @pl.when(step == 0)
    def _():
        state_scr[...] = jnp.zeros_like(state_scr)
        ext_scr[0:hist, :] = jnp.zeros((hist, ext_scr.shape[1]), _F32)
        win_ref[...] = win_f32[...].astype(_BF16)
        wglu_ref[...] = wglu_f32[...].astype(_BF16)
        poolw_ref[...] = poolw_f32[...].astype(_BF16)
        wout_ref[...] = wout_f32[...].astype(_BF16)

    if batch_major_in:
        for bb in range(SUBLANES):
            for k in range(xin_scr.shape[0]):
                xin_scr[k, pl.ds(bb, time_tile, stride=SUBLANES), :] = (
                    h_ref[bb, :, k * LANES:(k + 1) * LANES])
        h = jnp.concatenate([xin_scr[k] for k in range(xin_scr.shape[0])], axis=1)
    else:
        h = h_ref[...]
    h_save[...] = h
    hn = _rmsnorm(h, gmix_ref[...]).astype(_BF16)
    proj = _dot(hn, win_ref[...])
    u_ssm = proj[:, :d_ssm]
    u_save[...] = u_ssm
    ext_scr[hist:, :] = proj[:, d_ssm:]

    def drive(q):
        ug = u_ssm[:, q * gin:(q + 1) * gin].astype(_BF16)
        s_scr[q] = _dot(ug, bw_ref[q])

    def scan(q, c):
        re0 = c * SCAN_COLS
        im0 = gstate + c * SCAN_COLS
        lr = jnp.broadcast_to(lam_ref[2 * q:2 * q + 1, re0:re0 + SCAN_COLS],
                              (SUBLANES, SCAN_COLS))
        li = jnp.broadcast_to(lam_ref[2 * q + 1:2 * q + 2, re0:re0 + SCAN_COLS],
                              (SUBLANES, SCAN_COLS))
        sre = state_scr[q, :, re0:re0 + SCAN_COLS]
        sim = state_scr[q, :, im0:im0 + SCAN_COLS]
        for t in range(time_tile):
            r0 = t * SUBLANES
            bre = s_scr[q, r0:r0 + SUBLANES, re0:re0 + SCAN_COLS]
            bim = s_scr[q, r0:r0 + SUBLANES, im0:im0 + SCAN_COLS]
            sre, sim = (lr * sre - li * sim + bre, lr * sim + li * sre + bim)
            s_scr[q, r0:r0 + SUBLANES, re0:re0 + SCAN_COLS] = sre
            s_scr[q, r0:r0 + SUBLANES, im0:im0 + SCAN_COLS] = sim
        state_scr[q, :, re0:re0 + SCAN_COLS] = sre
        state_scr[q, :, im0:im0 + SCAN_COLS] = sim

    def pool():
        n_ext = rows + hist
        pos = step * time_tile + (lax.broadcasted_iota(_I32, (rows, LANES), 0) // SUBLANES)
        for g, w in enumerate(POOL_WINDOWS):
            c0 = g * LANES
            e = ext_scr[:, c0:c0 + LANES]
            acc = e
            n_acc = n_ext
            span = 1
            while span < w:
                sh = span * SUBLANES
                acc = acc[sh:, :] + acc[:n_acc - sh, :]
                n_acc -= sh
                span *= 2
            wsum = acc[n_acc - rows:, :]
            cnt = jnp.minimum(pos + 1, w).astype(_F32)
            pooled = wsum / cnt - e[hist:, :]
            mg = _dot(pooled.astype(_BF16), poolw_ref[g]) * pscale_ref[:, c0:c0 + LANES]
            mixed_scr[:, d_ssm + c0:d_ssm + c0 + LANES] = mg.astype(_BF16)
        ext_scr[0:hist, :] = ext_scr[rows:rows + hist, :]

    def readout(q):
        cols = slice(q * gin, (q + 1) * gin)
        yq = _dot(s_scr[q].astype(_BF16), cw_ref[q])
        yq = yq + dskip_ref[:, cols] * u_save[:, cols]
        hq = 0.5 * yq * (1.0 + jnp.tanh(GELU_C0 * (yq + GELU_C1 * (yq * yq * yq))))
        mixed_scr[:, cols] = hq.astype(_BF16)

    def glu_gate():
        hg = mixed_scr[:, :d_ssm]
        gate = _sigmoid(_dot(hg, wglu_ref[...]) + bglu_ref[...])
        mixed_scr[:, :d_ssm] = (hg.astype(_F32) * gate).astype(_BF16)

    def project_out():
        hout = h_save[...] + _dot(mixed_scr[...], wout_ref[...])
        hout_ref[...] = hout
        hn2 = _rmsnorm(hout, gffn_ref[...])
        hn_hi = hn2.astype(_BF16)
        hn_ref[...] = hn_hi
        if with_router:
            hn_lo = (hn2 - hn_hi.astype(_F32)).astype(_BF16)
            rw = rw_ref[...]
            rw_hi = rw.astype(_BF16)
            rw_lo = (rw - rw_hi.astype(_F32)).astype(_BF16)
            logits_ref[...] = (_dot_nt(rw_hi, hn_hi) + _dot_nt(rw_hi, hn_lo)
                               + _dot_nt(rw_lo, hn_hi))

    for q in range(n_groups):
        drive(q)
    for c in range(gstate // SCAN_COLS):
        for q in range(n_groups):
            scan(q, c)
        if c == 0:
            pool()
    for q in range(n_groups):
        readout(q)
    glu_gate()
    project_out()


def _pack_vectors(norm_mix_g, norm_ffn_g, ssm_d, ssm_b_glu, pool_scale, lam):
    depth, d = norm_mix_g.shape
    row = lambda v: jnp.pad(v, ((0, 0), (0, d - v.shape[1])))[:, None, :]
    return jnp.concatenate(
        [row(norm_mix_g), row(norm_ffn_g), row(jnp.concatenate([ssm_d, ssm_b_glu], axis=1)),
         row(pool_scale), lam], axis=1)


def _mix_layer(h, layer, vecs, w_in, bw, cw, wglu, poolw, wout, router_wt, moe_layer):
    batch_major_in = h.ndim == 3
    d = h.shape[-1]
    t_rows = h.size // d
    rows = TIME_TILE * SUBLANES
    n_steps = t_rows // rows
    d_ssm = wglu.shape[1]
    d_pool = d - d_ssm
    n_groups, _, two_gstate = bw.shape[1:]
    hist = max(POOL_WINDOWS) * SUBLANES
    with_router = router_wt is not None

    def const(a, idx):
        nd = a.ndim
        return pl.BlockSpec((None,) + a.shape[1:], lambda i, nd=nd, idx=idx: (idx,) + (0,) * (nd - 1),
                            pipeline_mode=pl.Buffered(1))

    row_blk = lambda: pl.BlockSpec((rows, d), lambda i: (i, 0))
    ins = [h, vecs, w_in, bw, cw, wglu, poolw, wout]
    h_spec =(pl.BlockSpec((h.shape[0], TIME_TILE, d), lambda i: (0, i, 0))
              if batch_major_in else row_blk())
    in_specs = [h_spec] + [const(a, layer) for a in ins[1:]]
    out_shape = [jax.ShapeDtypeStruct((t_rows, d), _F32),
                 jax.ShapeDtypeStruct((t_rows, d), _BF16)]
    out_specs = [row_blk(), row_blk()]
    if with_router:
        n_e = router_wt.shape[1]
        ins.append(router_wt)
        in_specs.append(const(router_wt, moe_layer))
        out_shape.append(jax.ShapeDtypeStruct((n_e, t_rows), _F32))
        out_specs.append(pl.BlockSpec((n_e, rows), lambda i: (0, i)))
    scratch = [
        pltpu.VMEM((n_groups, rows, two_gstate), _F32),
        pltpu.VMEM((n_groups, SUBLANES, two_gstate), _F32),
        pltpu.VMEM((rows + hist, d_pool), _F32),
        pltpu.VMEM((rows, d_ssm + d_pool), _BF16),
        pltpu.VMEM((rows, d_ssm), _F32),
        pltpu.VMEM((rows, d), _F32),
        pltpu.VMEM(w_in.shape[1:], _BF16),
        pltpu.VMEM(wglu.shape[1:], _BF16),
        pltpu.VMEM(poolw.shape[1:], _BF16),
        pltpu.VMEM(wout.shape[1:], _BF16),
    ]
    if batch_major_in:
        scratch.append(pltpu.VMEM((d // LANES, rows, LANES), _F32))
    return pl.pallas_call(
        functools.partial(_mix_kernel, time_tile=TIME_TILE, with_router=with_router,
                          batch_major_in=batch_major_in),
        grid=(n_steps,),
        in_specs=in_specs,
        out_specs=out_specs,
        out_shape=out_shape,
        scratch_shapes=scratch,
        compiler_params=pltpu.CompilerParams(
            dimension_semantics=("arbitrary",), vmem_limit_bytes=VMEM_LIMIT),
        name="mix_router" if with_router else "mix",
    )(*ins)


def _swiglu(x, wg_ref, wu_ref, wd_ref, a_scr, slot):
    tf = wg_ref.shape[2]
    c0 = 0
    while c0 < tf:
        cw = min(FFN_COL_CHUNK, tf - c0)
        g = _dot(x, wg_ref[slot, :, c0:c0 + cw])
        u = _dot(x, wu_ref[slot, :, c0:c0 + cw])
        a_scr[:, c0:c0 + cw] = (g * _sigmoid(g) * u).astype(_BF16)
        c0 += cw
    return _dot(a_scr[...], wd_ref[slot])


def _ffn_kernel(x_ref, wg_hbm, wu_hbm, wd_hbm, res_ref, out_ref,
                wg_buf, wu_buf, wd_buf, stg_g, stg_u, stg_d, sems, a_scr, *, layer):
    units = WEIGHT_UNITS
    gu_rows = wg_buf.shape[1] // units
    d_rows = wd_buf.shape[1] // units

    @pl.when(pl.program_id(0) == 0)
    def _():
        def copies(k):
            st = k % 2
            return (pltpu.make_async_copy(wg_hbm.at[layer, pl.ds(k * gu_rows, gu_rows)],
                                          stg_g.at[st], sems.at[st, 0]),
                    pltpu.make_async_copy(wu_hbm.at[layer, pl.ds(k * gu_rows, gu_rows)],
                                          stg_u.at[st], sems.at[st, 1]),
                    pltpu.make_async_copy(wd_hbm.at[layer, pl.ds(k * d_rows, d_rows)],
                                          stg_d.at[st], sems.at[st, 2]))

        for cp in copies(0):
            cp.start()
        for k in range(units):
            if k + 1 < units:
                for cp in copies(k + 1):
                    cp.start()
            for cp in copies(k):
                cp.wait()
            st = k % 2
            wg_buf[0, k * gu_rows:(k + 1) * gu_rows, :] = stg_g[st].astype(_BF16)
            wu_buf[0, k * gu_rows:(k + 1) * gu_rows, :] = stg_u[st].astype(_BF16)
            wd_buf[0, k * d_rows:(k + 1) * d_rows, :] = stg_d[st].astype(_BF16)

    out_ref[...] = res_ref[...] + _swiglu(x_ref[...], wg_buf, wu_buf, wd_buf, a_scr, 0)


def _ffn(x, wg, wu, wd, layer, res):
    t_rows, d = x.shape
    f = wg.shape[2]
    tm = FFN_ROW_TILE
    hbm = pl.BlockSpec(memory_space=pl.ANY)
    return pl.pallas_call(
        functools.partial(_ffn_kernel, layer=layer),
        grid=(t_rows // tm,),
        in_specs=[pl.BlockSpec((tm, d), lambda i: (i, 0)), hbm, hbm, hbm,
                  pl.BlockSpec((tm, d), lambda i: (i, 0))],
        out_specs=pl.BlockSpec((tm, d), lambda i: (i, 0)),
        out_shape=jax.ShapeDtypeStruct((t_rows, d), _F32),
        scratch_shapes=[pltpu.VMEM((1, d, f), _BF16),
                        pltpu.VMEM((1, d, f), _BF16),
                        pltpu.VMEM((1, f, d), _BF16),
                        pltpu.VMEM((2, d // WEIGHT_UNITS, f), _F32),
                        pltpu.VMEM((2, d // WEIGHT_UNITS, f), _F32),
                        pltpu.VMEM((2, f // WEIGHT_UNITS, d), _F32),
                        pltpu.SemaphoreType.DMA((2, 3)),
                        pltpu.VMEM((tm, f), _BF16)],
        compiler_params=pltpu.CompilerParams(
            dimension_semantics=("arbitrary",), vmem_limit_bytes=VMEM_LIMIT),
        name="ffn_dense",
    )(x, wg, wu, wd, res)


def _route_kernel(lt_ref, pos_ref, gates_ref, segtab_ref, blktab_ref, *, n_blk, row_tile):
    l = lt_ref[...]
    n_e, t_rows = l.shape
    ch = TOKEN_CHUNK
    n_ch = t_rows // ch
    ie = lax.broadcasted_iota(_I32, l.shape, 0)
    m1 = jnp.max(l, axis=0, keepdims=True)
    i1 = jnp.min(jnp.where(l == m1, ie, n_e), axis=0, keepdims=True)
    l2 = jnp.where(ie == i1, -jnp.inf, l)
    m2 = jnp.max(l2, axis=0, keepdims=True)
    i2 = jnp.min(jnp.where(l2 == m2, ie, n_e), axis=0, keepdims=True)
    e2 = jnp.exp(m2 - m1)
    den = 1.0 + e2
    gates_ref[0:1, :] = 1.0 / den
    gates_ref[1:2, :] = e2 / den
    sel1 = ie == i1
    sel2 = ie == i2
    chosen = jnp.where(sel1 | sel2, 1.0, 0.0).astype(_BF16)

    chunk_of_t = lax.broadcasted_iota(_I32, (n_ch, t_rows), 1) // ch
    in_chunk = jnp.where(chunk_of_t == lax.broadcasted_iota(_I32, (n_ch, t_rows), 0),
                         1.0, 0.0).astype(_BF16)
    n_ec = _dot_nt(chosen, in_chunk)
    n_ce = _dot_nt(in_chunk, chosen)
    cc_r = lax.broadcasted_iota(_I32, (n_ch, n_ch), 0)
    cc_c = lax.broadcasted_iota(_I32, (n_ch, n_ch), 1)
    before = jnp.where(cc_r < cc_c, 1.0, 0.0).astype(_BF16)
    after = jnp.where(cc_c < cc_r, 1.0, 0.0).astype(_BF16)
    segoff_ec = _dot(n_ec.astype(_BF16), before)
    segoff_ce = _dot(after, n_ce.astype(_BF16))
    def window(segoff, n):
        start = jnp.floor(segoff * (1.0 / SEG_ALIGN)) * SEG_ALIGN
        used = segoff - start + n
        return start, used, start + _ceil_to(jnp.maximum(used, 1.0), float(WINDOW_PIECE))

    wstart_ec, used_ec, wend_ec = window(segoff_ec, n_ec)
    _, _, wend_ce = window(segoff_ce, n_ce)
    tot_col = _ceil_to(jnp.max(wend_ec, axis=1, keepdims=True), float(row_tile))
    tot_row = _ceil_to(jnp.max(wend_ce, axis=0, keepdims=True), float(row_tile))
    ee_r = lax.broadcasted_iota(_I32, (n_e, n_e), 0)
    ee_c = lax.broadcasted_iota(_I32, (n_e, n_e), 1)
    base_col = jnp.sum(jnp.where(ee_c < ee_r, tot_row, 0.0), axis=1, keepdims=True)
    w0_ec = base_col + segoff_ec
    end_col = base_col + tot_col
    segtab_ref[0] = (base_col + wstart_ec).astype(_I32)
    segtab_ref[1] = used_ec.astype(_I32)
    segtab_ref[2] = jnp.broadcast_to(end_col, (n_e, n_ch)).astype(_I32)

    tt_r = lax.broadcasted_iota(_I32, (ch, ch), 0)
    tt_c = lax.broadcasted_iota(_I32, (ch, ch), 1)
    earlier = jnp.where(tt_r < tt_c, 1.0, 0.0).astype(_BF16)
    for c in range(n_ch):
        cols = slice(c * ch, (c + 1) * ch)
        rank = _dot(chosen[:, cols], earlier)
        offs = w0_ec[:, c:c + 1] + rank
        p1 = jnp.sum(jnp.where(sel1[:, cols], offs, 0.0), axis=0, keepdims=True)
        p2 = jnp.sum(jnp.where(sel2[:, cols], offs, 0.0), axis=0, keepdims=True)
        pos_ref[0:1, cols] = p1.astype(_I32)
        pos_ref[1:2, cols] = p2.astype(_I32)

    start = (lax.broadcasted_iota(_I32, (1, n_blk), 1) * row_tile).astype(_F32)
    owner = jnp.sum(jnp.where(end_col <= start, 1.0, 0.0), axis=0, keepdims=True)
    n_valid = jnp.sum(tot_col, axis=0, keepdims=True) * (1.0 / row_tile)
    blktab_ref[0:1, :] = jnp.minimum(owner, n_e - 1.0).astype(_I32)
    blktab_ref[1:2, :] = jnp.broadcast_to(n_valid, (1, n_blk)).astype(_I32)
    real_end_col = base_col + jnp.sum(n_ec, axis=1, keepdims=True)
    mine = lax.broadcasted_iota(_I32, (n_e, n_blk), 0).astype(_F32) == owner
    real = jnp.clip(real_end_col - start, 0.0, float(row_tile))
    blktab_ref[2:3, :] = jnp.sum(jnp.where(mine, real, 0.0), axis=0, keepdims=True).astype(_I32)


def _route(logits_t, n_blk, row_tile):
    n_e, t_rows = logits_t.shape
    n_ch = t_rows // TOKEN_CHUNK
    full = lambda s: pl.BlockSpec(s, lambda: (0,) * len(s))
    return pl.pallas_call(
        functools.partial(_route_kernel, n_blk=n_blk, row_tile=row_tile),
        in_specs=[full((n_e, t_rows))],
        out_specs=(full((2, t_rows)), full((2, t_rows)), full((3, n_e, n_ch)), full((3, n_blk))),
        out_shape=(jax.ShapeDtypeStruct((2, t_rows), _I32),
                   jax.ShapeDtypeStruct((2, t_rows), _F32),
                   jax.ShapeDtypeStruct((3, n_e, n_ch), _I32),
                   jax.ShapeDtypeStruct((3, n_blk), _I32)),
        compiler_params=pltpu.CompilerParams(vmem_limit_bytes=VMEM_LIMIT),
        name="route_sort",
    )(logits_t)


def _dispatch_kernel(w0_ref, sl_ref, end_ref, nv_ref, hn_ref, posl_ref, x_hbm,
                     stage, sems, zero_scr, zsems, tsems, carry, *, n_e, n_ch, row_tile, min_blocks):
    step = pl.program_id(0)
    per_step = posl_ref.shape[0]
    ch = hn_ref.shape[0] // per_step
    piece = WINDOW_PIECE
    n_pieces = stage.shape[1]
    n_blocks = x_hbm.shape[0] // row_tile

    def for_each_unused_block(fn):
        for b in range(min_blocks, n_blocks):
            @pl.when(b >= nv_ref[0])
            def _(b=b):
                fn(pltpu.make_async_copy(zero_scr, x_hbm.at[pl.ds(b * row_tile, row_tile)],
                                         tsems.at[b - min_blocks]))

    def piece_copy(cc, sl, e, k):
        row0 = pl.multiple_of(w0_ref[e * n_ch + cc] + k * piece, SEG_ALIGN)
        return pltpu.make_async_copy(
            stage.at[sl, k, pl.ds(e * piece, piece)],
            x_hbm.at[pl.ds(row0, piece)], sems.at[sl, e, k])

    def for_each_piece(cc, sl, fn):
        for e in range(n_e):
            fn(piece_copy(cc, sl, e, 0))
            used = sl_ref[e * n_ch + cc]

            def later(k, e=e, used=used):
                @pl.when(used > k * piece)
                def _():
                    fn(piece_copy(cc, sl, e, k))
                    if k + 1 < n_pieces:
                        later(k + 1)
            if n_pieces > 1:
                later(1)

    @pl.when(step == 0)
    def _():
        zero_scr[...] = jnp.zeros_like(zero_scr)
        carry[...] = jnp.zeros_like(carry)
        fills = [pltpu.make_async_copy(
            zero_scr,
            x_hbm.at[pl.ds(pl.multiple_of(end_ref[e * n_ch] - row_tile, SEG_ALIGN), row_tile)],
            zsems.at[e]) for e in range(n_e)]
        for cp in fills:
            cp.start()
        for_each_unused_block(lambda cp: cp.start())
        for cp in fills:
            cp.wait()

    def dispatch_chunk(j):
        c = step * per_step + j
        slot = j % 2
        pp = posl_ref[j]
        hn = hn_ref[j * ch:(j + 1) * ch, :]

        def gather_piece(k):
            row_j = lax.broadcasted_iota(_I32, (piece, ch), 0) + k * piece
            parts = []
            for e in range(n_e):
                row_id = row_j + w0_ref[e * n_ch + c]
                hit = (pp[0:1, :] == row_id) | (pp[1:2, :] == row_id)
                parts.append(jnp.where(hit, 1.0, 0.0).astype(_BF16))
            onehot = jnp.concatenate(parts, axis=0)
            stage[slot, k] = _dot(onehot, hn).astype(_BF16)

        gather_piece(0)
        for e in range(n_e):
            head = slice(e * piece, e * piece + SEG_ALIGN)
            stage[slot, 0, head, :] = stage[slot, 0, head, :] + carry[e]
        longest = sl_ref[c]
        for e in range(1, n_e):
            longest = jnp.maximum(longest, sl_ref[e * n_ch + c])

        def later(k):
            @pl.when(longest > k * piece)
            def _():
                gather_piece(k)
                if k + 1 < n_pieces:
                    later(k + 1)
        if n_pieces > 1:
            later(1)
        for e in range(n_e):
            used = sl_ref[e * n_ch + c]
            gathered = ((jnp.maximum(used, 1) + piece - 1) // piece) * piece
            off = (used // SEG_ALIGN) * SEG_ALIGN
            src = jnp.minimum(off, gathered - SEG_ALIGN)
            grp = stage[slot, src // piece,
                        pl.ds(pl.multiple_of(e * piece + src % piece, SEG_ALIGN), SEG_ALIGN), :]
            carry[e] = jnp.where(off < gathered, grp, jnp.zeros_like(grp))

        if j > 0:
            for_each_piece(c - 1, 1 - slot, lambda cp: cp.wait())
        else:
            @pl.when(step > 0)
            def _():
                for_each_piece(c - 1, 1 - slot, lambda cp: cp.wait())

        for_each_piece(c, slot, lambda cp: cp.start())

        if j == per_step - 1:
            @pl.when(c == n_ch - 1)
            def _():
                for_each_piece(c, slot, lambda cp: cp.wait())
                for_each_unused_block(lambda cp: cp.wait())

    for j in range(per_step):
        dispatch_chunk(j)


def _dispatch(hn, pos_l, w0_flat, seglen_flat, end_flat, n_valid, n_e, n_rows, row_tile):
    t_rows, d = hn.shape
    ch = TOKEN_CHUNK
    n_ch = t_rows // ch
    n_pieces = _window_pieces(ch)
    per_step = CHUNKS_PER_STEP
    min_blocks = 2 * t_rows // row_tile
    return pl.pallas_call(
        functools.partial(_dispatch_kernel, n_e=n_e, n_ch=n_ch, row_tile=row_tile,
                          min_blocks=min_blocks),
        grid_spec=pltpu.PrefetchScalarGridSpec(
            num_scalar_prefetch=4,
            grid=(n_ch // per_step,),
            in_specs=[pl.BlockSpec((per_step * ch, d), lambda s, *_: (s, 0)),
                      pl.BlockSpec((per_step, 2, ch), lambda s, *_: (s, 0, 0))],
            out_specs=pl.BlockSpec(memory_space=pl.ANY),
            scratch_shapes=[pltpu.VMEM((2, n_pieces, n_e * WINDOW_PIECE, d), _BF16),
                            pltpu.SemaphoreType.DMA((2, n_e, n_pieces)),
                            pltpu.VMEM((row_tile, d), _BF16),
                            pltpu.SemaphoreType.DMA((n_e,)),
                            pltpu.SemaphoreType.DMA((n_rows // row_tile - min_blocks,)),
                            pltpu.VMEM((n_e, SEG_ALIGN, d), _BF16)]),
        out_shape=jax.ShapeDtypeStruct((n_rows, d), _BF16),
        compiler_params=pltpu.CompilerParams(
            dimension_semantics=("arbitrary",), vmem_limit_bytes=VMEM_LIMIT),
        name="moe_dispatch",
    )(w0_flat, seglen_flat, end_flat, n_valid, hn, pos_l)


def _ffn_grouped_kernel(be_ref, nv_ref, end_ref, bu_ref, x_ref, wg_hbm, wu_hbm, wd_hbm, y_ref,
                        wg_buf, wu_buf, wd_buf, stg_g, stg_u, stg_d, sems, state, a_scr,
                        *, n_ch, expert0):
    b = pl.program_id(0)
    n_valid = nv_ref[0]
    valid = b < n_valid
    row_tile = x_ref.shape[0]
    units = WEIGHT_UNITS
    gu_rows = wg_buf.shape[1] // units
    d_rows = wd_buf.shape[1] // units
    ST_SLOT, ST_RESIDENT, ST_NEXT, ST_DONE, ST_STAGE = range(5)

    def unit_copies(e, k, st):
        g0 = pl.multiple_of(k * gu_rows, gu_rows)
        d0 = pl.multiple_of(k * d_rows, SEG_ALIGN)
        return (pltpu.make_async_copy(wg_hbm.at[expert0 + e, pl.ds(g0, gu_rows)], stg_g.at[st],
                                      sems.at[st, 0]),
                pltpu.make_async_copy(wu_hbm.at[expert0 + e, pl.ds(g0, gu_rows)], stg_u.at[st],
                                      sems.at[st, 1]),
                pltpu.make_async_copy(wd_hbm.at[expert0 + e, pl.ds(d0, d_rows)], stg_d.at[st],
                                      sems.at[st, 2]))

    def start(e, k, st):
        for cp in unit_copies(e, k, st):
            cp.start()

    def wait(e, k, st):
        for cp in unit_copies(e, k, st):
            cp.wait()

    def cast_unit(slot, k, st):
        g0 = pl.multiple_of(k * gu_rows, gu_rows)
        d0 = pl.multiple_of(k * d_rows, SEG_ALIGN)
        wg_buf[slot, pl.ds(g0, gu_rows), :] = stg_g[st].astype(_BF16)
        wu_buf[slot, pl.ds(g0, gu_rows), :] = stg_u[st].astype(_BF16)
        wd_buf[slot, pl.ds(d0, d_rows), :] = stg_d[st].astype(_BF16)

    def expert_after(e):
        blk = end_ref[e * n_ch] // row_tile
        return jnp.where(blk < n_valid, be_ref[jnp.minimum(blk, n_valid - 1)], -1)

    def become_resident(slot, e):
        state[ST_SLOT] = slot
        state[ST_RESIDENT] = e
        nxt = expert_after(e)
        state[ST_NEXT] = nxt
        state[ST_DONE] = 0

        @pl.when(nxt >= 0)
        def _():
            start(nxt, 0, 0)
        state[ST_STAGE] = 0

    e_b = be_ref[jnp.minimum(b, n_valid - 1)]

    @pl.when(b == 0)
    def _():
        state[ST_SLOT] = 1
        state[ST_RESIDENT] = -1
        state[ST_NEXT] = e_b
        state[ST_DONE] = 0
        state[ST_STAGE] = 0
        start(e_b, 0, 0)

    s_slot = state[ST_SLOT]
    s_next = state[ST_NEXT]
    s_done = state[ST_DONE]
    s_stage = state[ST_STAGE]
    change = valid & (e_b != state[ST_RESIDENT])
    steady = valid & jnp.logical_not(change) & (s_next >= 0) & (s_done < units)

    @pl.when(change)
    def _():
        other = 1 - s_slot

        @pl.when(s_done < units)
        def _():
            wait(e_b, s_done, s_stage)
            cast_unit(other, s_done, s_stage)

            def body(k, carry):
                start(e_b, k, 0)
                wait(e_b, k, 0)
                cast_unit(other, k, 0)
                return carry
            lax.fori_loop(s_done + 1, units, body, 0)
        become_resident(other, e_b)

    half = row_tile // 2
    rows_used = jnp.where(valid, bu_ref[jnp.minimum(b, n_valid - 1)], 0)
    whole = rows_used > half
    part = (rows_used > 0) & jnp.logical_not(whole)

    def run_block(slot):
        y_ref[...] = _swiglu(x_ref[...], wg_buf, wu_buf, wd_buf, a_scr, slot).astype(_BF16)

    @pl.when(steady)
    def _():
        wait(s_next, s_done, s_stage)

        @pl.when(s_done + 1 < units)
        def _():
            start(s_next, s_done + 1, 1 - s_stage)
        state[ST_DONE] = s_done + 1
        state[ST_STAGE] = 1 - s_stage

    @pl.when(steady & whole)
    def _():
        cast_unit(1 - s_slot, s_done, s_stage)
        run_block(s_slot)

    @pl.when(steady & jnp.logical_not(whole))
    def _():
        cast_unit(1 - s_slot, s_done, s_stage)

    @pl.when(jnp.logical_not(steady) & whole)
    def _():
        run_block(state[ST_SLOT])

    @pl.when(part)
    def _():
        slot = state[ST_SLOT]
        y_ref[0:half, :] = _swiglu(x_ref[0:half, :], wg_buf, wu_buf, wd_buf,
                                   a_scr.at[0:half], slot).astype(_BF16)
        y_ref[half:, :] = jnp.zeros((row_tile - half, y_ref.shape[1]), _BF16)

    @pl.when(rows_used == 0)
    def _():
        y_ref[...] = jnp.zeros_like(y_ref)


def _ffn_grouped(x_sorted, wg, wu, wd, expert0, block_expert, n_valid, end_flat, block_rows,
                 n_ch, row_tile):
    rows, d = x_sorted.shape
    f = wg.shape[2]
    n_blocks = rows // row_tile
    hbm = pl.BlockSpec(memory_space=pl.ANY)
    return pl.pallas_call(
        functools.partial(_ffn_grouped_kernel, n_ch=n_ch, expert0=expert0),
        grid_spec=pltpu.PrefetchScalarGridSpec(
            num_scalar_prefetch=4,
            grid=(n_blocks,),
            in_specs=[pl.BlockSpec((row_tile, d),
                                   lambda b, be, nv, en, bu: (jnp.minimum(b, nv[0] - 1), 0)),
                      hbm, hbm, hbm],
            out_specs=pl.BlockSpec((row_tile, d), lambda b, be, nv, en, bu: (b, 0)),
            scratch_shapes=[pltpu.VMEM((2, d, f), _BF16),
                            pltpu.VMEM((2, d, f), _BF16),
                            pltpu.VMEM((2, f, d), _BF16),
                            pltpu.VMEM((2, d // WEIGHT_UNITS, f), _F32),
                            pltpu.VMEM((2, d // WEIGHT_UNITS, f), _F32),
                            pltpu.VMEM((2, f // WEIGHT_UNITS, d), _F32),
                            pltpu.SemaphoreType.DMA((2, 3)),
                            pltpu.SMEM((5,), _I32),
                            pltpu.VMEM((row_tile, f), _BF16)]),
        out_shape=jax.ShapeDtypeStruct((rows, d), _BF16),
        compiler_params=pltpu.CompilerParams(
            dimension_semantics=("arbitrary",), vmem_limit_bytes=VMEM_LIMIT),
        name="moe_experts",
    )(block_expert, n_valid, end_flat, block_rows, x_sorted, wg, wu, wd)


def _combine_kernel(w0_ref, sl_ref, res_ref, tok_ref, gfin_ref, y_hbm,
                    out_ref, ybuf, sems, tmp_scr, acc_scr, *, n_e, n_ch):
    step = pl.program_id(0)
    per_step = ybuf.shape[1]
    n_steps = n_ch // per_step
    ch = res_ref.shape[0] // per_step
    piece = WINDOW_PIECE
    n_pieces = ybuf.shape[2]

    def piece_copy(st, slot, j, e, k):
        row0 = pl.multiple_of(w0_ref[e * n_ch + st * per_step + j] + k * piece, SEG_ALIGN)
        return pltpu.make_async_copy(
            y_hbm.at[pl.ds(row0, piece)],
            ybuf.at[slot, j, k, pl.ds(e * piece, piece)], sems.at[slot, j, e, k])

    def for_each_piece(st, slot, fn):
        for j in range(per_step):
            for e in range(n_e):
                fn(piece_copy(st, slot, j, e, 0))
                used = sl_ref[e * n_ch + st * per_step + j]

                def later(k, j=j, e=e, used=used):
                    @pl.when(used > k * piece)
                    def _():
                        fn(piece_copy(st, slot, j, e, k))
                        if k + 1 < n_pieces:
                            later(k + 1)
                if n_pieces > 1:
                    later(1)

    @pl.when(step == 0)
    def _():
        ybuf[...] = jnp.zeros_like(ybuf)
        for_each_piece(0, 0, lambda cp: cp.start())

    @pl.when(step + 1 < n_steps)
    def _():
        for_each_piece(step + 1, (step + 1) % 2, lambda cp: cp.start())

    slot = step % 2
    for_each_piece(step, slot, lambda cp: cp.wait())

    per_dot = MXU_DIM // piece
    depth = per_dot * piece
    lane = lax.broadcasted_iota(_I32, (1, depth), 1)
    steps = ch // SUBLANES

    def combine_chunk(j):
        c = step * per_step + j
        rows = slice(j * ch, (j + 1) * ch)
        p1b = jnp.broadcast_to(tok_ref[rows, 0:1], (ch, depth))
        p2b = jnp.broadcast_to(tok_ref[rows, 1:2], (ch, depth))
        g1b = jnp.broadcast_to(tok_ref[rows, 2:3], (ch, depth))
        g2b = jnp.broadcast_to(tok_ref[rows, 3:4], (ch, depth))

        def scatter_back(m, k):
            tgt = jnp.full((1, depth), -1.0, _F32)
            for i in range(per_dot):
                e = m * per_dot + i
                row = lane - i * piece + k * piece
                mine = ((lane >= i * piece) & (lane < (i + 1) * piece)
                        & (row < sl_ref[e * n_ch + c]))
                tgt = jnp.where(mine, (row + w0_ref[e * n_ch + c]).astype(_F32), tgt)
            q = (jnp.where(p1b == tgt, g1b, 0.0) + jnp.where(p2b == tgt, g2b, 0.0)).astype(_BF16)
            return _dot(q, ybuf[slot, j, k, m * depth:(m + 1) * depth, :])

        def any_reaches(m, k):
            hit = sl_ref[(m * per_dot) * n_ch + c] > k * piece
            for i in range(1, per_dot):
                hit = jnp.logical_or(hit, sl_ref[(m * per_dot + i) * n_ch + c] > k * piece)
            return hit

        acc = res_ref[rows, :]
        for m in range(n_e // per_dot):
            acc = acc + scatter_back(m, 0)
        acc_scr[j] = acc
        for m in range(n_e // per_dot):
            def later(k, m=m):
                @pl.when(any_reaches(m, k))
                def _():
                    acc_scr[j] += scatter_back(m, k)
                    if k + 1 < n_pieces:
                        later(k + 1)
            if n_pieces > 1:
                later(1)
        normed = _rmsnorm(acc_scr[j], gfin_ref[...])
        for k in range(tmp_scr.shape[1]):
            tmp_scr[j, k] = normed[:, k * LANES:(k + 1) * LANES]
        for bb in range(SUBLANES):
            for k in range(tmp_scr.shape[1]):
                out_ref[bb, j * steps:(j + 1) * steps, k * LANES:(k + 1) * LANES] = (
                    tmp_scr[j, k, pl.ds(bb, steps, stride=SUBLANES), :])

    for j in range(per_step):
        combine_chunk(j)


def _combine(res, tok_tab, gfin, y_sorted, w0_flat, seglen_flat, n_e, bsz, seq):
    t_rows, d = res.shape
    ch = TOKEN_CHUNK
    n_ch = t_rows // ch
    per_step = CHUNKS_PER_STEP
    rows = per_step * ch
    return pl.pallas_call(
        functools.partial(_combine_kernel, n_e=n_e, n_ch=n_ch),
        grid_spec=pltpu.PrefetchScalarGridSpec(
            num_scalar_prefetch=2,
            grid=(n_ch // per_step,),
            in_specs=[pl.BlockSpec((rows, d), lambda s, w0, sl: (s, 0)),
                      pl.BlockSpec((rows, tok_tab.shape[1]), lambda s, w0, sl: (s, 0)),
                      pl.BlockSpec((1, d), lambda s, w0, sl: (0, 0)),
                      pl.BlockSpec(memory_space=pl.ANY)],
            out_specs=pl.BlockSpec((bsz, rows // bsz, d), lambda s, w0, sl: (0, s, 0)),
            scratch_shapes=[pltpu.VMEM((2, per_step, _window_pieces(ch), n_e * WINDOW_PIECE, d),
                                       _BF16),
                            pltpu.SemaphoreType.DMA((2, per_step, n_e, _window_pieces(ch))),
                            pltpu.VMEM((per_step, d // LANES, ch, LANES), _F32),
                            pltpu.VMEM((per_step, ch, d), _F32)]),
        out_shape=jax.ShapeDtypeStruct((bsz, seq, d), _F32),
        compiler_params=pltpu.CompilerParams(
            dimension_semantics=("arbitrary",), vmem_limit_bytes=VMEM_LIMIT),
        name="moe_combine",
    )(w0_flat, seglen_flat, res, tok_tab, gfin, y_sorted)


def _moe_layer(h, hn, logits_t, wg, wu, wd, expert0, gfin, bsz, seq):
    t_rows, d = h.shape
    n_e = logits_t.shape[0]
    tm = EXPERT_ROW_TILE
    n_ch = t_rows // TOKEN_CHUNK
    max_rows = 2 * t_rows + n_e * (SEG_ALIGN + TOKEN_CHUNK + tm)
    n_blocks = -(-max_rows // tm)
    n_blk_pad = -(-n_blocks // LANES) * LANES

    pos, gates, segtab, blktab = _route(logits_t, n_blk_pad, tm)
    pos_l = jnp.transpose(pos.reshape(2, n_ch, TOKEN_CHUNK), (1, 0, 2))
    w0_flat = segtab[0].reshape(-1)
    seglen_flat = segtab[1].reshape(-1)
    n_valid = blktab[1, :1]
    end_flat = segtab[2].reshape(-1)
    x_sorted = _dispatch(hn, pos_l, w0_flat, seglen_flat, end_flat, n_valid,
                         n_e, n_blocks * tm, tm)
    y_sorted = _ffn_grouped(x_sorted, wg, wu, wd, expert0, blktab[0], n_valid, end_flat,
                            blktab[2], n_ch, tm)
    tok_tab = jnp.transpose(jnp.concatenate([pos.astype(_F32), gates], axis=0))
    return _combine(h, tok_tab, gfin, y_sorted,
                    w0_flat, seglen_flat, n_e, bsz, seq)


def kernel(x, norm_mix_g, w_in, ssm_log_dt, ssm_a_re, ssm_a_im, ssm_b_re, ssm_b_im, ssm_c_re, ssm_c_im, ssm_d, ssm_w_glu, ssm_b_glu, pool_w, pool_scale, w_out, norm_ffn_g, ffn_w_gate, ffn_w_up, ffn_w_down, router_w, moe_w_gate, moe_w_up, moe_w_down, final_norm_g):
    bsz, seq, d = x.shape
    depth, n_heads, n_state, n_grp_ch = ssm_b_re.shape
    t_rows = bsz * seq
    assert bsz == SUBLANES and depth % 2 == 0
    groups = n_heads // HEADS_PER_GROUP

    nh = depth * n_heads
    per_head = lambda w: w.reshape((nh,) + w.shape[2:])
    lam, bw_all, cw_all = _discretize(
        per_head(ssm_log_dt), per_head(ssm_a_re), per_head(ssm_a_im),
        per_head(jnp.swapaxes(ssm_b_re, 2, 3)), per_head(jnp.swapaxes(ssm_b_im, 2, 3)),
        per_head(jnp.swapaxes(ssm_c_re, 2, 3)), per_head(jnp.swapaxes(ssm_c_im, 2, 3)))
    lam = lam.reshape(depth, 2 * groups, -1)
    bw = bw_all.reshape((depth, groups) + bw_all.shape[1:])
    cw = cw_all.reshape((depth, groups) + cw_all.shape[1:])
    vecs = _pack_vectors(norm_mix_g, norm_ffn_g, ssm_d, ssm_b_glu, pool_scale, lam)
    router_wt = jnp.swapaxes(router_w, 1, 2)

    h = x
    row = lambda v: v.reshape(1, -1)
    out = None
    for i in range(depth):
        is_moe = i % 2 == 1
        j = i // 2
        outs = _mix_layer(h, i, vecs, w_in, bw, cw, ssm_w_glu, pool_w, w_out,
                          router_wt if is_moe else None, j)
        if is_moe:
            assert i == depth - 1
            h, hn, logits_t = outs
            n_e = moe_w_gate.shape[1]
            stack = lambda w: w.reshape((-1,) + w.shape[2:])
            out = _moe_layer(h, hn, logits_t, stack(moe_w_gate), stack(moe_w_up),
                             stack(moe_w_down), j * n_e, row(final_norm_g), bsz, seq)
        else:
            h, hn = outs
            h = _ffn(hn, ffn_w_gate, ffn_w_up, ffn_w_down, j, h)
    return out
```

```python
import functools
import math

import jax
import jax.numpy as jnp
from jax import lax
from jax.experimental import pallas as pl
from jax.experimental.pallas import tpu as pltpu

RMS_EPS = 1e-6
POOL_WINDOWS = (2, 4, 8, 16)
A_RE_MAX = -1e-4
GELU_C0 = math.sqrt(2.0 / math.pi)
GELU_C1 = 0.044715

SUBLANES = 8
LANES = 128
MXU_DIM = 256

HEADS_PER_GROUP = 16
TIME_TILE = 64
SCAN_COLS = 512
FFN_ROW_TILE = 512
EXPERT_ROW_TILE = 512
FFN_COL_CHUNK = 256
WEIGHT_UNITS = 8
TOKEN_CHUNK = MXU_DIM
SEG_ALIGN = 2 * SUBLANES
WINDOW_PIECE = 96
CHUNKS_PER_STEP = 2
VMEM_LIMIT = 56 * 1024 * 1024
VEC_NORM_MIX, VEC_NORM_FFN, VEC_SKIP_BIAS, VEC_POOL_SCALE, VEC_LAMBDA = range(5)

_F32 = jnp.float32
_BF16 = jnp.bfloat16
_I32 = jnp.int32


def _dot(a, b):
    return jnp.dot(a, b, preferred_element_type=_F32)


def _dot_nt(a, b):
    return lax.dot_general(a, b, (((1,), (1,)), ((), ())), preferred_element_type=_F32)


def _rmsnorm(x, g):
    inv = lax.rsqrt(jnp.mean(x * x, axis=-1, keepdims=True) + RMS_EPS)
    return x * inv * g


def _sigmoid(x):
    return 1.0 / (1.0 + jnp.exp(-x))


def _ceil_to(x, m):
    return jnp.floor((x + (m - 1.0)) * (1.0 / m)) * m


def _window_pieces(chunk):
    return -(-(chunk + SEG_ALIGN - 1) // WINDOW_PIECE)


def _discretize_kernel(log_dt_ref, a_re_ref, a_im_ref, b_re_ref, b_im_ref, c_re_ref, c_im_ref,
                       lam_ref, bw_ref, cw_ref):
    dt = jnp.exp(log_dt_ref[...])
    ar = jnp.minimum(a_re_ref[...], A_RE_MAX)
    ai = a_im_ref[...]
    mag = jnp.exp(ar * dt)
    lam_re = mag * jnp.cos(ai * dt)
    lam_im = mag * jnp.sin(ai * dt)
    den = ar * ar + ai * ai
    nr = lam_re - 1.0
    ni = lam_im
    coef_re = (nr * ar + ni * ai) / den
    coef_im = (ni * ar - nr * ai) / den
    br = jnp.swapaxes(b_re_ref[...], 1, 2)
    bi = jnp.swapaxes(b_im_ref[...], 1, 2)
    bb_re = (coef_re * br - coef_im * bi).astype(_BF16)
    bb_im = (coef_re * bi + coef_im * br).astype(_BF16)
    c_re = jnp.swapaxes(c_re_ref[...], 1, 2).astype(_BF16)
    c_im_neg = (-jnp.swapaxes(c_im_ref[...], 1, 2)).astype(_BF16)

    n, g, p = br.shape
    hpg = n // lam_ref.shape[0]
    half = hpg * p
    bw_ref[...] = jnp.zeros_like(bw_ref)
    cw_ref[...] = jnp.zeros_like(cw_ref)
    for h in range(n):
        q, hl = divmod(h, hpg)
        st = slice(hl * p, (hl + 1) * p)
        st_im = slice(half + hl * p, half + (hl + 1) * p)
        ch = slice(hl * g, (hl + 1) * g)
        lam_ref[q, 0:1, st] = lam_re[h]
        lam_ref[q, 1:2, st] = lam_im[h]
        bw_ref[q, ch, st] = bb_re[h]
        bw_ref[q, ch, st_im] = bb_im[h]
        cw_ref[q, st, ch] = c_re[h]
        cw_ref[q, st_im, ch] = c_im_neg[h]


def _discretize(log_dt, a_re, a_im, b_re, b_im, c_re, c_im):
    n, p, g = b_re.shape
    hpg = HEADS_PER_GROUP
    nq = n // hpg
    full3 = lambda s: pl.BlockSpec(s, lambda: (0, 0, 0))
    return pl.pallas_call(
        _discretize_kernel,
        out_shape=(jax.ShapeDtypeStruct((nq, 2, hpg * p), _F32),
                   jax.ShapeDtypeStruct((nq, hpg * g, 2 * hpg * p), _BF16),
                   jax.ShapeDtypeStruct((nq, 2 * hpg * p, hpg * g), _BF16)),
        in_specs=[full3((n, 1, 1)), full3((n, 1, p)), full3((n, 1, p)),
                  full3((n, p, g)), full3((n, p, g)), full3((n, g, p)), full3((n, g, p))],
        out_specs=(full3((nq, 2, hpg * p)), full3((nq, hpg * g, 2 * hpg * p)),
                   full3((nq, 2 * hpg * p, hpg * g))),
        name="ssm_discretize",
    )(log_dt.reshape(n, 1, 1), a_re.reshape(n, 1, p), a_im.reshape(n, 1, p),
      b_re, b_im, c_re, c_im)


def _mix_kernel(*refs, time_tile, with_router, batch_major_in):
    (h_ref, vec_ref, win_f32, bw_ref, cw_ref, wglu_f32, poolw_f32, wout_f32) = refs[:8]
    refs = refs[8:]
    d_ssm = wglu_f32.shape[0]
    gmix_ref = vec_ref.at[VEC_NORM_MIX:VEC_NORM_MIX + 1]
    gffn_ref = vec_ref.at[VEC_NORM_FFN:VEC_NORM_FFN + 1]
    dskip_ref = vec_ref.at[VEC_SKIP_BIAS:VEC_SKIP_BIAS + 1, 0:d_ssm]
    bglu_ref = vec_ref.at[VEC_SKIP_BIAS:VEC_SKIP_BIAS + 1, d_ssm:2 * d_ssm]
    pscale_ref = vec_ref.at[VEC_POOL_SCALE:VEC_POOL_SCALE + 1]
    lam_ref = vec_ref.at[VEC_LAMBDA:]
    rw_ref = logits_ref = xin_scr = None
    if with_router:
        rw_ref, hout_ref, hn_ref, logits_ref = refs[:4]
        refs = refs[4:]
    else:
        hout_ref, hn_ref = refs[:2]
        refs = refs[2:]
    (s_scr, state_scr, ext_scr, mixed_scr, u_save, h_save,
     win_ref, wglu_ref, poolw_ref, wout_ref) = refs[:10]
    if batch_major_in:
        xin_scr = refs[10]

    step = pl.program_id(0)
    rows = time_tile * SUBLANES
    n_groups = bw_ref.shape[0]
    gin = bw_ref.shape[1]
    gstate = bw_ref.shape[2] // 2
    hist = ext_scr.shape[0] - rows

    @pl.when(step == 0)
    def _():
        state_scr[...] = jnp.zeros_like(state_scr)
        ext_scr[0:hist, :] = jnp.zeros((hist, ext_scr.shape[1]), _F32)
        win_ref[...] = win_f32[...].astype(_BF16)
        wglu_ref[...] = wglu_f32[...].astype(_BF16)
        poolw_ref[...] = poolw_f32[...].astype(_BF16)
        wout_ref[...] = wout_f32[...].astype(_BF16)

    if batch_major_in:
        for bb in range(SUBLANES):
            for k in range(xin_scr.shape[0]):
                xin_scr[k, pl.ds(bb, time_tile, stride=SUBLANES), :] = (
                    h_ref[bb, :, k * LANES:(k + 1) * LANES])
        h = jnp.concatenate([xin_scr[k] for k in range(xin_scr.shape[0])], axis=1)
    else:
        h = h_ref[...]
    h_save[...] = h
    hn = _rmsnorm(h, gmix_ref[...]).astype(_BF16)
    proj = _dot(hn, win_ref[...])
    u_ssm = proj[:, :d_ssm]
    u_save[...] = u_ssm
    ext_scr[hist:, :] = proj[:, d_ssm:]

    def drive(q):
        ug = u_ssm[:, q * gin:(q + 1) * gin].astype(_BF16)
        s_scr[q] = _dot(ug, bw_ref[q])

    def scan(q, c):
        re0 = c * SCAN_COLS
        im0 = gstate + c * SCAN_COLS
        lr = jnp.broadcast_to(lam_ref[2 * q:2 * q + 1, re0:re0 + SCAN_COLS],
                              (SUBLANES, SCAN_COLS))
        li = jnp.broadcast_to(lam_ref[2 * q + 1:2 * q + 2, re0:re0 + SCAN_COLS],
                              (SUBLANES, SCAN_COLS))
        sre = state_scr[q, :, re0:re0 + SCAN_COLS]
        sim = state_scr[q, :, im0:im0 + SCAN_COLS]
        for t in range(time_tile):
            r0 = t * SUBLANES
            bre = s_scr[q, r0:r0 + SUBLANES, re0:re0 + SCAN_COLS]
            bim = s_scr[q, r0:r0 + SUBLANES, im0:im0 + SCAN_COLS]
            sre, sim = (lr * sre - li * sim + bre, lr * sim + li * sre + bim)
            s_scr[q, r0:r0 + SUBLANES, re0:re0 + SCAN_COLS] = sre
            s_scr[q, r0:r0 + SUBLANES, im0:im0 + SCAN_COLS] = sim
        state_scr[q, :, re0:re0 + SCAN_COLS] = sre
        state_scr[q, :, im0:im0 + SCAN_COLS] = sim

    def pool():
        n_ext = rows + hist
        pos = step * time_tile + (lax.broadcasted_iota(_I32, (rows, LANES), 0) // SUBLANES)
        for g, w in enumerate(POOL_WINDOWS):
            c0 = g * LANES
            e = ext_scr[:, c0:c0 + LANES]
            acc = e
            n_acc = n_ext
            span = 1
            while span < w:
                sh = span * SUBLANES
                acc = acc[sh:, :] + acc[:n_acc - sh, :]
                n_acc -= sh
                span *= 2
            wsum = acc[n_acc - rows:, :]
            cnt = jnp.minimum(pos + 1, w).astype(_F32)
            pooled = wsum / cnt - e[hist:, :]
            mg = _dot(pooled.astype(_BF16), poolw_ref[g]) * pscale_ref[:, c0:c0 + LANES]
            mixed_scr[:, d_ssm + c0:d_ssm + c0 + LANES] = mg.astype(_BF16)
        ext_scr[0:hist, :] = ext_scr[rows:rows + hist, :]

    def readout(q):
        cols = slice(q * gin, (q + 1) * gin)
        yq = _dot(s_scr[q].astype(_BF16), cw_ref[q])
        yq = yq + dskip_ref[:, cols] * u_save[:, cols]
        hq = 0.5 * yq * (1.0 + jnp.tanh(GELU_C0 * (yq + GELU_C1 * (yq * yq * yq))))
        mixed_scr[:, cols] = hq.astype(_BF16)

    def glu_gate():
        hg = mixed_scr[:, :d_ssm]
        gate = _sigmoid(_dot(hg, wglu_ref[...]) + bglu_ref[...])
        mixed_scr[:, :d_ssm] = (hg.astype(_F32) * gate).astype(_BF16)

    def project_out():
        hout = h_save[...] + _dot(mixed_scr[...], wout_ref[...])
        hout_ref[...] = hout
        hn2 = _rmsnorm(hout, gffn_ref[...])
        hn_hi = hn2.astype(_BF16)
        hn_ref[...] = hn_hi
        if with_router:
            hn_lo = (hn2 - hn_hi.astype(_F32)).astype(_BF16)
            rw = rw_ref[...]
            rw_hi = rw.astype(_BF16)
            rw_lo = (rw - rw_hi.astype(_F32)).astype(_BF16)
            logits_ref[...] = (_dot_nt(rw_hi, hn_hi) + _dot_nt(rw_hi, hn_lo)
                               + _dot_nt(rw_lo, hn_hi))

    for q in range(n_groups):
        drive(q)
    for c in range(gstate // SCAN_COLS):
        for q in range(n_groups):
            scan(q, c)
        if c == 0:
            pool()
    for q in range(n_groups):
        readout(q)
    glu_gate()
    project_out()


def _pack_vectors(norm_mix_g, norm_ffn_g, ssm_d, ssm_b_glu, pool_scale, lam):
    depth, d = norm_mix_g.shape
    row = lambda v: jnp.pad(v, ((0, 0), (0, d - v.shape[1])))[:, None, :]
    return jnp.concatenate(
        [row(norm_mix_g), row(norm_ffn_g), row(jnp.concatenate([ssm_d, ssm_b_glu], axis=1)),
         row(pool_scale), lam], axis=1)


def _mix_layer(h, layer, vecs, w_in, bw, cw, wglu, poolw, wout, router_wt, moe_layer):
    batch_major_in = h.ndim == 3
    d = h.shape[-1]
    t_rows = h.size // d
    rows = TIME_TILE * SUBLANES
    n_steps = t_rows // rows
    d_ssm = wglu.shape[1]
    d_pool = d - d_ssm
    n_groups, _, two_gstate = bw.shape[1:]
    hist = max(POOL_WINDOWS) * SUBLANES
    with_router = router_wt is not None

    def const(a, idx):
        nd = a.ndim
        return pl.BlockSpec((None,) + a.shape[1:], lambda i, nd=nd, idx=idx: (idx,) + (0,) * (nd - 1),
                            pipeline_mode=pl.Buffered(1))

    row_blk = lambda: pl.BlockSpec((rows, d), lambda i: (i, 0))
    ins = [h, vecs, w_in, bw, cw, wglu, poolw, wout]
    h_spec =(pl.BlockSpec((h.shape[0], TIME_TILE, d), lambda i: (0, i, 0))
              if batch_major_in else row_blk())
    in_specs = [h_spec] + [const(a, layer) for a in ins[1:]]
    out_shape = [jax.ShapeDtypeStruct((t_rows, d), _F32),
                 jax.ShapeDtypeStruct((t_rows, d), _BF16)]
    out_specs = [row_blk(), row_blk()]
    if with_router:
        n_e = router_wt.shape[1]
        ins.append(router_wt)
        in_specs.append(const(router_wt, moe_layer))
        out_shape.append(jax.ShapeDtypeStruct((n_e, t_rows), _F32))
        out_specs.append(pl.BlockSpec((n_e, rows), lambda i: (0, i)))
    scratch = [
        pltpu.VMEM((n_groups, rows, two_gstate), _F32),
        pltpu.VMEM((n_groups, SUBLANES, two_gstate), _F32),
        pltpu.VMEM((rows + hist, d_pool), _F32),
        pltpu.VMEM((rows, d_ssm + d_pool), _BF16),
        pltpu.VMEM((rows, d_ssm), _F32),
        pltpu.VMEM((rows, d), _F32),
        pltpu.VMEM(w_in.shape[1:], _BF16),
        pltpu.VMEM(wglu.shape[1:], _BF16),
        pltpu.VMEM(poolw.shape[1:], _BF16),
        pltpu.VMEM(wout.shape[1:], _BF16),
    ]
    if batch_major_in:
        scratch.append(pltpu.VMEM((d // LANES, rows, LANES), _F32))
    return pl.pallas_call(
        functools.partial(_mix_kernel, time_tile=TIME_TILE, with_router=with_router,
                          batch_major_in=batch_major_in),
        grid=(n_steps,),
        in_specs=in_specs,
        out_specs=out_specs,
        out_shape=out_shape,
        scratch_shapes=scratch,
        compiler_params=pltpu.CompilerParams(
            dimension_semantics=("arbitrary",), vmem_limit_bytes=VMEM_LIMIT),
        name="mix_router" if with_router else "mix",
    )(*ins)


def _swiglu(x, wg_ref, wu_ref, wd_ref, a_scr, slot):
    tf = wg_ref.shape[2]
    c0 = 0
    while c0 < tf:
        cw = min(FFN_COL_CHUNK, tf - c0)
        g = _dot(x, wg_ref[slot, :, c0:c0 + cw])
        u = _dot(x, wu_ref[slot, :, c0:c0 + cw])
        a_scr[:, c0:c0 + cw] = (g * _sigmoid(g) * u).astype(_BF16)
        c0 += cw
    return _dot(a_scr[...], wd_ref[slot])


def _ffn_kernel(x_ref, wg_hbm, wu_hbm, wd_hbm, res_ref, out_ref,
                wg_buf, wu_buf, wd_buf, stg_g, stg_u, stg_d, sems, a_scr, *, layer):
    units = WEIGHT_UNITS
    gu_rows = wg_buf.shape[1] // units
    d_rows = wd_buf.shape[1] // units

    @pl.when(pl.program_id(0) == 0)
    def _():
        def copies(k):
            st = k % 2
            return (pltpu.make_async_copy(wg_hbm.at[layer, pl.ds(k * gu_rows, gu_rows)],
                                          stg_g.at[st], sems.at[st, 0]),
                    pltpu.make_async_copy(wu_hbm.at[layer, pl.ds(k * gu_rows, gu_rows)],
                                          stg_u.at[st], sems.at[st, 1]),
                    pltpu.make_async_copy(wd_hbm.at[layer, pl.ds(k * d_rows, d_rows)],
                                          stg_d.at[st], sems.at[st, 2]))

        for cp in copies(0):
            cp.start()
        for k in range(units):
            if k + 1 < units:
                for cp in copies(k + 1):
                    cp.start()
            for cp in copies(k):
                cp.wait()
            st = k % 2
            wg_buf[0, k * gu_rows:(k + 1) * gu_rows, :] = stg_g[st].astype(_BF16)
            wu_buf[0, k * gu_rows:(k + 1) * gu_rows, :] = stg_u[st].astype(_BF16)
            wd_buf[0, k * d_rows:(k + 1) * d_rows, :] = stg_d[st].astype(_BF16)

    out_ref[...] = res_ref[...] + _swiglu(x_ref[...], wg_buf, wu_buf, wd_buf, a_scr, 0)


def _ffn(x, wg, wu, wd, layer, res):
    t_rows, d = x.shape
    f = wg.shape[2]
    tm = FFN_ROW_TILE
    hbm = pl.BlockSpec(memory_space=pl.ANY)
    return pl.pallas_call(
        functools.partial(_ffn_kernel, layer=layer),
        grid=(t_rows // tm,),
        in_specs=[pl.BlockSpec((tm, d), lambda i: (i, 0)), hbm, hbm, hbm,
                  pl.BlockSpec((tm, d), lambda i: (i, 0))],
        out_specs=pl.BlockSpec((tm, d), lambda i: (i, 0)),
        out_shape=jax.ShapeDtypeStruct((t_rows, d), _F32),
        scratch_shapes=[pltpu.VMEM((1, d, f), _BF16),
                        pltpu.VMEM((1, d, f), _BF16),
                        pltpu.VMEM((1, f, d), _BF16),
                        pltpu.VMEM((2, d // WEIGHT_UNITS, f), _F32),
                        pltpu.VMEM((2, d // WEIGHT_UNITS, f), _F32),
                        pltpu.VMEM((2, f // WEIGHT_UNITS, d), _F32),
                        pltpu.SemaphoreType.DMA((2, 3)),
                        pltpu.VMEM((tm, f), _BF16)],
        compiler_params=pltpu.CompilerParams(
            dimension_semantics=("arbitrary",), vmem_limit_bytes=VMEM_LIMIT),
        name="ffn_dense",
    )(x, wg, wu, wd, res)


def _route_kernel(lt_ref, pos_ref, gates_ref, segtab_ref, blktab_ref, *, n_blk, row_tile):
    l = lt_ref[...]
    n_e, t_rows = l.shape
    ch = TOKEN_CHUNK
    n_ch = t_rows // ch
    ie = lax.broadcasted_iota(_I32, l.shape, 0)
    m1 = jnp.max(l, axis=0, keepdims=True)
    i1 = jnp.min(jnp.where(l == m1, ie, n_e), axis=0, keepdims=True)
    l2 = jnp.where(ie == i1, -jnp.inf, l)
    m2 = jnp.max(l2, axis=0, keepdims=True)
    i2 = jnp.min(jnp.where(l2 == m2, ie, n_e), axis=0, keepdims=True)
    e2 = jnp.exp(m2 - m1)
    den = 1.0 + e2
    gates_ref[0:1, :] = 1.0 / den
    gates_ref[1:2, :] = e2 / den
    sel1 = ie == i1
    sel2 = ie == i2
    chosen = jnp.where(sel1 | sel2, 1.0, 0.0).astype(_BF16)

    chunk_of_t = lax.broadcasted_iota(_I32, (n_ch, t_rows), 1) // ch
    in_chunk = jnp.where(chunk_of_t == lax.broadcasted_iota(_I32, (n_ch, t_rows), 0),
                         1.0, 0.0).astype(_BF16)
    n_ec = _dot_nt(chosen, in_chunk)
    n_ce = _dot_nt(in_chunk, chosen)
    cc_r = lax.broadcasted_iota(_I32, (n_ch, n_ch), 0)
    cc_c = lax.broadcasted_iota(_I32, (n_ch, n_ch), 1)
    before = jnp.where(cc_r < cc_c, 1.0, 0.0).astype(_BF16)
    after = jnp.where(cc_c < cc_r, 1.0, 0.0).astype(_BF16)
    segoff_ec = _dot(n_ec.astype(_BF16), before)
    segoff_ce = _dot(after, n_ce.astype(_BF16))
    def window(segoff, n):
        start = jnp.floor(segoff * (1.0 / SEG_ALIGN)) * SEG_ALIGN
        used = segoff - start + n
        return start, used, start + _ceil_to(jnp.maximum(used, 1.0), float(WINDOW_PIECE))

    wstart_ec, used_ec, wend_ec = window(segoff_ec, n_ec)
    _, _, wend_ce = window(segoff_ce, n_ce)
    tot_col = _ceil_to(jnp.max(wend_ec, axis=1, keepdims=True), float(row_tile))
    tot_row = _ceil_to(jnp.max(wend_ce, axis=0, keepdims=True), float(row_tile))
    ee_r = lax.broadcasted_iota(_I32, (n_e, n_e), 0)
    ee_c = lax.broadcasted_iota(_I32, (n_e, n_e), 1)
    base_col = jnp.sum(jnp.where(ee_c < ee_r, tot_row, 0.0), axis=1, keepdims=True)
    w0_ec = base_col + segoff_ec
    end_col = base_col + tot_col
    segtab_ref[0] = (base_col + wstart_ec).astype(_I32)
    segtab_ref[1] = used_ec.astype(_I32)
    segtab_ref[2] = jnp.broadcast_to(end_col, (n_e, n_ch)).astype(_I32)

    tt_r = lax.broadcasted_iota(_I32, (ch, ch), 0)
    tt_c = lax.broadcasted_iota(_I32, (ch, ch), 1)
    earlier = jnp.where(tt_r < tt_c, 1.0, 0.0).astype(_BF16)
    for c in range(n_ch):
        cols = slice(c * ch, (c + 1) * ch)
        rank = _dot(chosen[:, cols], earlier)
        offs = w0_ec[:, c:c + 1] + rank
        p1 = jnp.sum(jnp.where(sel1[:, cols], offs, 0.0), axis=0, keepdims=True)
        p2 = jnp.sum(jnp.where(sel2[:, cols], offs, 0.0), axis=0, keepdims=True)
        pos_ref[0:1, cols] = p1.astype(_I32)
        pos_ref[1:2, cols] = p2.astype(_I32)

    start = (lax.broadcasted_iota(_I32, (1, n_blk), 1) * row_tile).astype(_F32)
    owner = jnp.sum(jnp.where(end_col <= start, 1.0, 0.0), axis=0, keepdims=True)
    n_valid = jnp.sum(tot_col, axis=0, keepdims=True) * (1.0 / row_tile)
    blktab_ref[0:1, :] = jnp.minimum(owner, n_e - 1.0).astype(_I32)
    blktab_ref[1:2, :] = jnp.broadcast_to(n_valid, (1, n_blk)).astype(_I32)
    real_end_col = base_col + jnp.sum(n_ec, axis=1, keepdims=True)
    mine = lax.broadcasted_iota(_I32, (n_e, n_blk), 0).astype(_F32) == owner
    real = jnp.clip(real_end_col - start, 0.0, float(row_tile))
    blktab_ref[2:3, :] = jnp.sum(jnp.where(mine, real, 0.0), axis=0, keepdims=True).astype(_I32)


def _route(logits_t, n_blk, row_tile):
    n_e, t_rows = logits_t.shape
    n_ch = t_rows // TOKEN_CHUNK
    full = lambda s: pl.BlockSpec(s, lambda: (0,) * len(s))
    return pl.pallas_call(
        functools.partial(_route_kernel, n_blk=n_blk, row_tile=row_tile),
        in_specs=[full((n_e, t_rows))],
        out_specs=(full((2, t_rows)), full((2, t_rows)), full((3, n_e, n_ch)), full((3, n_blk))),
        out_shape=(jax.ShapeDtypeStruct((2, t_rows), _I32),
                   jax.ShapeDtypeStruct((2, t_rows), _F32),
                   jax.ShapeDtypeStruct((3, n_e, n_ch), _I32),
                   jax.ShapeDtypeStruct((3, n_blk), _I32)),
        compiler_params=pltpu.CompilerParams(vmem_limit_bytes=VMEM_LIMIT),
        name="route_sort",
    )(logits_t)


def _dispatch_kernel(w0_ref, sl_ref, end_ref, nv_ref, hn_ref, posl_ref, x_hbm,
                     stage, sems, zero_scr, zsems, tsems, carry, *, n_e, n_ch, row_tile, min_blocks):
    step = pl.program_id(0)
    per_step = posl_ref.shape[0]
    ch = hn_ref.shape[0] // per_step
    piece = WINDOW_PIECE
    n_pieces = stage.shape[1]
    n_blocks = x_hbm.shape[0] // row_tile

    def for_each_unused_block(fn):
        for b in range(min_blocks, n_blocks):
            @pl.when(b >= nv_ref[0])
            def _(b=b):
                fn(pltpu.make_async_copy(zero_scr, x_hbm.at[pl.ds(b * row_tile, row_tile)],
                                         tsems.at[b - min_blocks]))

    def piece_copy(cc, sl, e, k):
        row0 = pl.multiple_of(w0_ref[e * n_ch + cc] + k * piece, SEG_ALIGN)
        return pltpu.make_async_copy(
            stage.at[sl, k, pl.ds(e * piece, piece)],
            x_hbm.at[pl.ds(row0, piece)], sems.at[sl, e, k])

    def for_each_piece(cc, sl, fn):
        for e in range(n_e):
            fn(piece_copy(cc, sl, e, 0))
            used = sl_ref[e * n_ch + cc]

            def later(k, e=e, used=used):
                @pl.when(used > k * piece)
                def _():
                    fn(piece_copy(cc, sl, e, k))
                    if k + 1 < n_pieces:
                        later(k + 1)
            if n_pieces > 1:
                later(1)

    @pl.when(step == 0)
    def _():
        zero_scr[...] = jnp.zeros_like(zero_scr)
        carry[...] = jnp.zeros_like(carry)
        fills = [pltpu.make_async_copy(
            zero_scr,
            x_hbm.at[pl.ds(pl.multiple_of(end_ref[e * n_ch] - row_tile, SEG_ALIGN), row_tile)],
            zsems.at[e]) for e in range(n_e)]
        for cp in fills:
            cp.start()
        for_each_unused_block(lambda cp: cp.start())
        for cp in fills:
            cp.wait()

    def dispatch_chunk(j):
        c = step * per_step + j
        slot = j % 2
        pp = posl_ref[j]
        hn = hn_ref[j * ch:(j + 1) * ch, :]

        def gather_piece(k):
            row_j = lax.broadcasted_iota(_I32, (piece, ch), 0) + k * piece
            parts = []
            for e in range(n_e):
                row_id = row_j + w0_ref[e * n_ch + c]
                hit = (pp[0:1, :] == row_id) | (pp[1:2, :] == row_id)
                parts.append(jnp.where(hit, 1.0, 0.0).astype(_BF16))
            onehot = jnp.concatenate(parts, axis=0)
            stage[slot, k] = _dot(onehot, hn).astype(_BF16)

        gather_piece(0)
        for e in range(n_e):
            head = slice(e * piece, e * piece + SEG_ALIGN)
            stage[slot, 0, head, :] = stage[slot, 0, head, :] + carry[e]
        longest = sl_ref[c]
        for e in range(1, n_e):
            longest = jnp.maximum(longest, sl_ref[e * n_ch + c])

        def later(k):
            @pl.when(longest > k * piece)
            def _():
                gather_piece(k)
                if k + 1 < n_pieces:
                    later(k + 1)
        if n_pieces > 1:
            later(1)
        for e in range(n_e):
            used = sl_ref[e * n_ch + c]
            gathered = ((jnp.maximum(used, 1) + piece - 1) // piece) * piece
            off = (used // SEG_ALIGN) * SEG_ALIGN
            src = jnp.minimum(off, gathered - SEG_ALIGN)
            grp = stage[slot, src // piece,
                        pl.ds(pl.multiple_of(e * piece + src % piece, SEG_ALIGN), SEG_ALIGN), :]
            carry[e] = jnp.where(off < gathered, grp, jnp.zeros_like(grp))

        if j > 0:
            for_each_piece(c - 1, 1 - slot, lambda cp: cp.wait())
        else:
            @pl.when(step > 0)
            def _():
                for_each_piece(c - 1, 1 - slot, lambda cp: cp.wait())

        for_each_piece(c, slot, lambda cp: cp.start())

        if j == per_step - 1:
            @pl.when(c == n_ch - 1)
            def _():
                for_each_piece(c, slot, lambda cp: cp.wait())
                for_each_unused_block(lambda cp: cp.wait())

    for j in range(per_step):
        dispatch_chunk(j)


def _dispatch(hn, pos_l, w0_flat, seglen_flat, end_flat, n_valid, n_e, n_rows, row_tile):
    t_rows, d = hn.shape
    ch = TOKEN_CHUNK
    n_ch = t_rows // ch
    n_pieces = _window_pieces(ch)
    per_step = CHUNKS_PER_STEP
    min_blocks = 2 * t_rows // row_tile
    return pl.pallas_call(
        functools.partial(_dispatch_kernel, n_e=n_e, n_ch=n_ch, row_tile=row_tile,
                          min_blocks=min_blocks),
        grid_spec=pltpu.PrefetchScalarGridSpec(
            num_scalar_prefetch=4,
            grid=(n_ch // per_step,),
            in_specs=[pl.BlockSpec((per_step * ch, d), lambda s, *_: (s, 0)),
                      pl.BlockSpec((per_step, 2, ch), lambda s, *_: (s, 0, 0))],
            out_specs=pl.BlockSpec(memory_space=pl.ANY),
            scratch_shapes=[pltpu.VMEM((2, n_pieces, n_e * WINDOW_PIECE, d), _BF16),
                            pltpu.SemaphoreType.DMA((2, n_e, n_pieces)),
                            pltpu.VMEM((row_tile, d), _BF16),
                            pltpu.SemaphoreType.DMA((n_e,)),
                            pltpu.SemaphoreType.DMA((n_rows // row_tile - min_blocks,)),
                            pltpu.VMEM((n_e, SEG_ALIGN, d), _BF16)]),
        out_shape=jax.ShapeDtypeStruct((n_rows, d), _BF16),
        compiler_params=pltpu.CompilerParams(
            dimension_semantics=("arbitrary",), vmem_limit_bytes=VMEM_LIMIT),
        name="moe_dispatch",
    )(w0_flat, seglen_flat, end_flat, n_valid, hn, pos_l)


def _ffn_grouped_kernel(be_ref, nv_ref, end_ref, bu_ref, x_ref, wg_hbm, wu_hbm, wd_hbm, y_ref,
                        wg_buf, wu_buf, wd_buf, stg_g, stg_u, stg_d, sems, state, a_scr,
                        *, n_ch, expert0):
    b = pl.program_id(0)
    n_valid = nv_ref[0]
    valid = b < n_valid
    row_tile = x_ref.shape[0]
    units = WEIGHT_UNITS
    gu_rows = wg_buf.shape[1] // units
    d_rows = wd_buf.shape[1] // units
    ST_SLOT, ST_RESIDENT, ST_NEXT, ST_DONE, ST_STAGE = range(5)

    def unit_copies(e, k, st):
        g0 = pl.multiple_of(k * gu_rows, gu_rows)
        d0 = pl.multiple_of(k * d_rows, SEG_ALIGN)
        return (pltpu.make_async_copy(wg_hbm.at[expert0 + e, pl.ds(g0, gu_rows)], stg_g.at[st],
                                      sems.at[st, 0]),
                pltpu.make_async_copy(wu_hbm.at[expert0 + e, pl.ds(g0, gu_rows)], stg_u.at[st],
                                      sems.at[st, 1]),
                pltpu.make_async_copy(wd_hbm.at[expert0 + e, pl.ds(d0, d_rows)], stg_d.at[st],
                                      sems.at[st, 2]))

    def start(e, k, st):
        for cp in unit_copies(e, k, st):
            cp.start()

    def wait(e, k, st):
        for cp in unit_copies(e, k, st):
            cp.wait()

    def cast_unit(slot, k, st):
        g0 = pl.multiple_of(k * gu_rows, gu_rows)
        d0 = pl.multiple_of(k * d_rows, SEG_ALIGN)
        wg_buf[slot, pl.ds(g0, gu_rows), :] = stg_g[st].astype(_BF16)
        wu_buf[slot, pl.ds(g0, gu_rows), :] = stg_u[st].astype(_BF16)
        wd_buf[slot, pl.ds(d0, d_rows), :] = stg_d[st].astype(_BF16)

    def expert_after(e):
        blk = end_ref[e * n_ch] // row_tile
        return jnp.where(blk < n_valid, be_ref[jnp.minimum(blk, n_valid - 1)], -1)

    def become_resident(slot, e):
        state[ST_SLOT] = slot
        state[ST_RESIDENT] = e
        nxt = expert_after(e)
        state[ST_NEXT] = nxt
        state[ST_DONE] = 0

        @pl.when(nxt >= 0)
        def _():
            start(nxt, 0, 0)
        state[ST_STAGE] = 0

    e_b = be_ref[jnp.minimum(b, n_valid - 1)]

    @pl.when(b == 0)
    def _():
        state[ST_SLOT] = 1
        state[ST_RESIDENT] = -1
        state[ST_NEXT] = e_b
        state[ST_DONE] = 0
        state[ST_STAGE] = 0
        start(e_b, 0, 0)

    s_slot = state[ST_SLOT]
    s_next = state[ST_NEXT]
    s_done = state[ST_DONE]
    s_stage = state[ST_STAGE]
    change = valid & (e_b != state[ST_RESIDENT])
    steady = valid & jnp.logical_not(change) & (s_next >= 0) & (s_done < units)

    @pl.when(change)
    def _():
        other = 1 - s_slot

        @pl.when(s_done < units)
        def _():
            wait(e_b, s_done, s_stage)
            cast_unit(other, s_done, s_stage)

            def body(k, carry):
                start(e_b, k, 0)
                wait(e_b, k, 0)
                cast_unit(other, k, 0)
                return carry
            lax.fori_loop(s_done + 1, units, body, 0)
        become_resident(other, e_b)

    half = row_tile // 2
    rows_used = jnp.where(valid, bu_ref[jnp.minimum(b, n_valid - 1)], 0)
    whole = rows_used > half
    part = (rows_used > 0) & jnp.logical_not(whole)

    def run_block(slot):
        y_ref[...] = _swiglu(x_ref[...], wg_buf, wu_buf, wd_buf, a_scr, slot).astype(_BF16)

    @pl.when(steady)
    def _():
        wait(s_next, s_done, s_stage)

        @pl.when(s_done + 1 < units)
        def _():
            start(s_next, s_done + 1, 1 - s_stage)
        state[ST_DONE] = s_done + 1
        state[ST_STAGE] = 1 - s_stage

    @pl.when(steady & whole)
    def _():
        cast_unit(1 - s_slot, s_done, s_stage)
        run_block(s_slot)

    @pl.when(steady & jnp.logical_not(whole))
    def _():
        cast_unit(1 - s_slot, s_done, s_stage)

    @pl.when(jnp.logical_not(steady) & whole)
    def _():
        run_block(state[ST_SLOT])

    @pl.when(part)
    def _():
        slot = state[ST_SLOT]
        y_ref[0:half, :] = _swiglu(x_ref[0:half, :], wg_buf, wu_buf, wd_buf,
                                   a_scr.at[0:half], slot).astype(_BF16)
        y_ref[half:, :] = jnp.zeros((row_tile - half, y_ref.shape[1]), _BF16)

    @pl.when(rows_used == 0)
    def _():
        y_ref[...] = jnp.zeros_like(y_ref)


def _ffn_grouped(x_sorted, wg, wu, wd, expert0, block_expert, n_valid, end_flat, block_rows,
                 n_ch, row_tile):
    rows, d = x_sorted.shape
    f = wg.shape[2]
    n_blocks = rows // row_tile
    hbm = pl.BlockSpec(memory_space=pl.ANY)
    return pl.pallas_call(
        functools.partial(_ffn_grouped_kernel, n_ch=n_ch, expert0=expert0),
        grid_spec=pltpu.PrefetchScalarGridSpec(
            num_scalar_prefetch=4,
            grid=(n_blocks,),
            in_specs=[pl.BlockSpec((row_tile, d),
                                   lambda b, be, nv, en, bu: (jnp.minimum(b, nv[0] - 1), 0)),
                      hbm, hbm, hbm],
            out_specs=pl.BlockSpec((row_tile, d), lambda b, be, nv, en, bu: (b, 0)),
            scratch_shapes=[pltpu.VMEM((2, d, f), _BF16),
                            pltpu.VMEM((2, d, f), _BF16),
                            pltpu.VMEM((2, f, d), _BF16),
                            pltpu.VMEM((2, d // WEIGHT_UNITS, f), _F32),
                            pltpu.VMEM((2, d // WEIGHT_UNITS, f), _F32),
                            pltpu.VMEM((2, f // WEIGHT_UNITS, d), _F32),
                            pltpu.SemaphoreType.DMA((2, 3)),
                            pltpu.SMEM((5,), _I32),
                            pltpu.VMEM((row_tile, f), _BF16)]),
        out_shape=jax.ShapeDtypeStruct((rows, d), _BF16),
        compiler_params=pltpu.CompilerParams(
            dimension_semantics=("arbitrary",), vmem_limit_bytes=VMEM_LIMIT),
        name="moe_experts",
    )(block_expert, n_valid, end_flat, block_rows, x_sorted, wg, wu, wd)


def _combine_kernel(w0_ref, sl_ref, res_ref, tok_ref, gfin_ref, y_hbm,
                    out_ref, ybuf, sems, tmp_scr, acc_scr, *, n_e, n_ch):
    step = pl.program_id(0)
    per_step = ybuf.shape[1]
    n_steps = n_ch // per_step
    ch = res_ref.shape[0] // per_step
    piece = WINDOW_PIECE
    n_pieces = ybuf.shape[2]

    def piece_copy(st, slot, j, e, k):
        row0 = pl.multiple_of(w0_ref[e * n_ch + st * per_step + j] + k * piece, SEG_ALIGN)
        return pltpu.make_async_copy(
            y_hbm.at[pl.ds(row0, piece)],
            ybuf.at[slot, j, k, pl.ds(e * piece, piece)], sems.at[slot, j, e, k])

    def for_each_piece(st, slot, fn):
        for j in range(per_step):
            for e in range(n_e):
                fn(piece_copy(st, slot, j, e, 0))
                used = sl_ref[e * n_ch + st * per_step + j]

                def later(k, j=j, e=e, used=used):
                    @pl.when(used > k * piece)
                    def _():
                        fn(piece_copy(st, slot, j, e, k))
                        if k + 1 < n_pieces:
                            later(k + 1)
                if n_pieces > 1:
                    later(1)

    @pl.when(step == 0)
    def _():
        ybuf[...] = jnp.zeros_like(ybuf)
        for_each_piece(0, 0, lambda cp: cp.start())

    @pl.when(step + 1 < n_steps)
    def _():
        for_each_piece(step + 1, (step + 1) % 2, lambda cp: cp.start())

    slot = step % 2
    for_each_piece(step, slot, lambda cp: cp.wait())

    per_dot = MXU_DIM // piece
    depth = per_dot * piece
    lane = lax.broadcasted_iota(_I32, (1, depth), 1)
    steps = ch // SUBLANES

    def combine_chunk(j):
        c = step * per_step + j
        rows = slice(j * ch, (j + 1) * ch)
        p1b = jnp.broadcast_to(tok_ref[rows, 0:1], (ch, depth))
        p2b = jnp.broadcast_to(tok_ref[rows, 1:2], (ch, depth))
        g1b = jnp.broadcast_to(tok_ref[rows, 2:3], (ch, depth))
        g2b = jnp.broadcast_to(tok_ref[rows, 3:4], (ch, depth))

        def scatter_back(m, k):
            tgt = jnp.full((1, depth), -1.0, _F32)
            for i in range(per_dot):
                e = m * per_dot + i
                row = lane - i * piece + k * piece
                mine = ((lane >= i * piece) & (lane < (i + 1) * piece)
                        & (row < sl_ref[e * n_ch + c]))
                tgt = jnp.where(mine, (row + w0_ref[e * n_ch + c]).astype(_F32), tgt)
            q = (jnp.where(p1b == tgt, g1b, 0.0) + jnp.where(p2b == tgt, g2b, 0.0)).astype(_BF16)
            return _dot(q, ybuf[slot, j, k, m * depth:(m + 1) * depth, :])

        def any_reaches(m, k):
            hit = sl_ref[(m * per_dot) * n_ch + c] > k * piece
            for i in range(1, per_dot):
                hit = jnp.logical_or(hit, sl_ref[(m * per_dot + i) * n_ch + c] > k * piece)
            return hit

        acc = res_ref[rows, :]
        for m in range(n_e // per_dot):
            acc = acc + scatter_back(m, 0)
        acc_scr[j] = acc
        for m in range(n_e // per_dot):
            def later(k, m=m):
                @pl.when(any_reaches(m, k))
                def _():
                    acc_scr[j] += scatter_back(m, k)
                    if k + 1 < n_pieces:
                        later(k + 1)
            if n_pieces > 1:
                later(1)
        normed = _rmsnorm(acc_scr[j], gfin_ref[...])
        for k in range(tmp_scr.shape[1]):
            tmp_scr[j, k] = normed[:, k * LANES:(k + 1) * LANES]
        for bb in range(SUBLANES):
            for k in range(tmp_scr.shape[1]):
                out_ref[bb, j * steps:(j + 1) * steps, k * LANES:(k + 1) * LANES] = (
                    tmp_scr[j, k, pl.ds(bb, steps, stride=SUBLANES), :])

    for j in range(per_step):
        combine_chunk(j)


def _combine(res, tok_tab, gfin, y_sorted, w0_flat, seglen_flat, n_e, bsz, seq):
    t_rows, d = res.shape
    ch = TOKEN_CHUNK
    n_ch = t_rows // ch
    per_step = CHUNKS_PER_STEP
    rows = per_step * ch
    return pl.pallas_call(
        functools.partial(_combine_kernel, n_e=n_e, n_ch=n_ch),
        grid_spec=pltpu.PrefetchScalarGridSpec(
            num_scalar_prefetch=2,
            grid=(n_ch // per_step,),
            in_specs=[pl.BlockSpec((rows, d), lambda s, w0, sl: (s, 0)),
                      pl.BlockSpec((rows, tok_tab.shape[1]), lambda s, w0, sl: (s, 0)),
                      pl.BlockSpec((1, d), lambda s, w0, sl: (0, 0)),
                      pl.BlockSpec(memory_space=pl.ANY)],
            out_specs=pl.BlockSpec((bsz, rows // bsz, d), lambda s, w0, sl: (0, s, 0)),
            scratch_shapes=[pltpu.VMEM((2, per_step, _window_pieces(ch), n_e * WINDOW_PIECE, d),
                                       _BF16),
                            pltpu.SemaphoreType.DMA((2, per_step, n_e, _window_pieces(ch))),
                            pltpu.VMEM((per_step, d // LANES, ch, LANES), _F32),
                            pltpu.VMEM((per_step, ch, d), _F32)]),
        out_shape=jax.ShapeDtypeStruct((bsz, seq, d), _F32),
        compiler_params=pltpu.CompilerParams(
            dimension_semantics=("arbitrary",), vmem_limit_bytes=VMEM_LIMIT),
        name="moe_combine",
    )(w0_flat, seglen_flat, res, tok_tab, gfin, y_sorted)


def _moe_layer(h, hn, logits_t, wg, wu, wd, expert0, gfin, bsz, seq):
    t_rows, d = h.shape
    n_e = logits_t.shape[0]
    tm = EXPERT_ROW_TILE
    n_ch = t_rows // TOKEN_CHUNK
    max_rows = 2 * t_rows + n_e * (SEG_ALIGN + TOKEN_CHUNK + tm)
    n_blocks = -(-max_rows // tm)
    n_blk_pad = -(-n_blocks // LANES) * LANES

    pos, gates, segtab, blktab = _route(logits_t, n_blk_pad, tm)
    pos_l = jnp.transpose(pos.reshape(2, n_ch, TOKEN_CHUNK), (1, 0, 2))
    w0_flat = segtab[0].reshape(-1)
    seglen_flat = segtab[1].reshape(-1)
    n_valid = blktab[1, :1]
    end_flat = segtab[2].reshape(-1)
    x_sorted = _dispatch(hn, pos_l, w0_flat, seglen_flat, end_flat, n_valid,
                         n_e, n_blocks * tm, tm)
    y_sorted = _ffn_grouped(x_sorted, wg, wu, wd, expert0, blktab[0], n_valid, end_flat,
                            blktab[2], n_ch, tm)
    tok_tab = jnp.transpose(jnp.concatenate([pos.astype(_F32), gates], axis=0))
    return _combine(h, tok_tab, gfin, y_sorted,
                    w0_flat, seglen_flat, n_e, bsz, seq)


def kernel(x, norm_mix_g, w_in, ssm_log_dt, ssm_a_re, ssm_a_im, ssm_b_re, ssm_b_im, ssm_c_re, ssm_c_im, ssm_d, ssm_w_glu, ssm_b_glu, pool_w, pool_scale, w_out, norm_ffn_g, ffn_w_gate, ffn_w_up, ffn_w_down, router_w, moe_w_gate, moe_w_up, moe_w_down, final_norm_g):
    bsz, seq, d = x.shape
    depth, n_heads, n_state, n_grp_ch = ssm_b_re.shape
    t_rows = bsz * seq
    assert bsz == SUBLANES and depth % 2 == 0
    groups = n_heads // HEADS_PER_GROUP

    nh = depth * n_heads
    per_head = lambda w: w.reshape((nh,) + w.shape[2:])
    lam, bw_all, cw_all = _discretize(
        per_head(ssm_log_dt), per_head(ssm_a_re), per_head(ssm_a_im),
        per_head(ssm_b_re), per_head(ssm_b_im), per_head(ssm_c_re), per_head(ssm_c_im))
    lam = lam.reshape(depth, 2 * groups, -1)
    bw = bw_all.reshape((depth, groups) + bw_all.shape[1:])
    cw = cw_all.reshape((depth, groups) + cw_all.shape[1:])
    vecs = _pack_vectors(norm_mix_g, norm_ffn_g, ssm_d, ssm_b_glu, pool_scale, lam)
    router_wt = jnp.swapaxes(router_w, 1, 2)

    h = x
    row = lambda v: v.reshape(1, -1)
    out = None
    for i in range(depth):
        is_moe = i % 2 == 1
        j = i // 2
        outs = _mix_layer(h, i, vecs, w_in, bw, cw, ssm_w_glu, pool_w, w_out,
                          router_wt if is_moe else None, j)
        if is_moe:
            assert i == depth - 1
            h, hn, logits_t = outs
            n_e = moe_w_gate.shape[1]
            stack = lambda w: w.reshape((-1,) + w.shape[2:])
            out = _moe_layer(h, hn, logits_t, stack(moe_w_gate), stack(moe_w_up),
                             stack(moe_w_down), j * n_e, row(final_norm_g), bsz, seq)
        else:
            h, hn = outs
            h = _ffn(hn, ffn_w_gate, ffn_w_up, ffn_w_down, j, h)
    return out
```
